```python
import math
import jax, jax.numpy as jnp
from jax import lax
import numpy as np

D_MODEL = 2048
BATCH = 4
SEQ = 2048
DEPTH = 1
DEC_BATCH = 32
DEC_SEQ = 1
PAST_LEN = 16384
PAGE_SIZE = 128

HEAD_DIM = 64
ATT_WIDTH = D_MODEL // 2
N_Q_HEADS = ATT_WIDTH // HEAD_DIM
N_KV_HEADS = 4
GQA_GROUP = N_Q_HEADS // N_KV_HEADS
KV_WIDTH = N_KV_HEADS * HEAD_DIM
RWKV_WIDTH = D_MODEL - ATT_WIDTH
RWKV_HEADS = RWKV_WIDTH // HEAD_DIM
WINDOW = 128
BLOCK = 128
N_BUCKETS = 32
MAX_DISTANCE = 128
DECAY_LORA = 64
AAA_LORA = 64
GATE_LORA = 128
D_FF = 5504
ATT_COLS = ATT_WIDTH + 2 * KV_WIDTH
SHIFT_COLS = 3 * RWKV_WIDTH + DECAY_LORA + AAA_LORA + GATE_LORA
IN_COLS = ATT_COLS + SHIFT_COLS
RMS_EPS = 1e-5
GN_EPS = 64e-5
FFN_RES = 0.5

kernel_name = "hymba_swa_sink_rwkv7_macaron_step"


def rmsnorm(x, g):
    xf = x.astype(jnp.float32)
    y = xf * lax.rsqrt(jnp.mean(xf * xf, axis=-1, keepdims=True) + RMS_EPS)
    return (y * g.astype(jnp.float32)).astype(x.dtype)


def swiglu(x, wg, wu, wd):
    return (jax.nn.silu(x @ wg) * (x @ wu)) @ wd


def t5_bucket(dist):
    n = jnp.maximum(dist, 0)
    max_exact = N_BUCKETS // 2
    nf = jnp.maximum(n, 1).astype(jnp.float32)
    large = max_exact + (jnp.log(nf / max_exact) / math.log(MAX_DISTANCE / max_exact)
                         * (N_BUCKETS - max_exact)).astype(jnp.int32)
    large = jnp.minimum(large, N_BUCKETS - 1)
    return jnp.where(n < max_exact, n, large)


def sink_attention(q, k, v, dist, valid, rel_bias, sinks):
    bias = rel_bias.astype(jnp.float32)[t5_bucket(dist)]
    bias = jnp.transpose(bias, (2, 0, 1)).reshape(N_KV_HEADS, GQA_GROUP, *dist.shape)
    s = jnp.einsum('...qhgd,...khd->...hgqk', q, k).astype(jnp.float32) * (HEAD_DIM ** -0.5) + bias
    s = jnp.where(valid[..., None, None, :, :], s, -jnp.inf)
    sink = sinks.astype(jnp.float32).reshape(N_KV_HEADS, GQA_GROUP)[:, :, None, None]
    m = jnp.maximum(jnp.max(s, axis=-1, keepdims=True), sink)
    p = jnp.exp(s - m)
    denom = jnp.sum(p, axis=-1, keepdims=True) + jnp.exp(sink - m)
    return jnp.einsum('...hgqk,...khd->...qhgd', (p / denom).astype(v.dtype), v)


def swa_prompt(q, k, v, rel_bias, sinks):
    B, S = q.shape[:2]
    nb = S // BLOCK
    qb = q.reshape(B, nb, BLOCK, N_KV_HEADS, GQA_GROUP, HEAD_DIM)
    pad = ((0, 0), (BLOCK, 0), (0, 0), (0, 0))
    kp = jnp.pad(k, pad).reshape(B, nb + 1, BLOCK, N_KV_HEADS, HEAD_DIM)
    vp = jnp.pad(v, pad).reshape(B, nb + 1, BLOCK, N_KV_HEADS, HEAD_DIM)
    kc = jnp.concatenate([kp[:, :-1], kp[:, 1:]], axis=2)
    vc = jnp.concatenate([vp[:, :-1], vp[:, 1:]], axis=2)
    qi = jnp.arange(BLOCK)[:, None]
    kj = jnp.arange(2 * BLOCK)[None, :]
    dist = BLOCK + qi - kj
    blk = jnp.arange(nb)[:, None, None]
    valid = (dist >= 0) & (dist <= WINDOW) & ((blk > 0) | (kj >= BLOCK))
    o = sink_attention(qb, kc, vc, dist, valid, rel_bias, sinks)
    L = min(WINDOW, S)
    return o.reshape(B, S, ATT_WIDTH), k[:, S - L:], v[:, S - L:]


def swa_decode(q, k, v, k_buf, v_buf, rel_bias, sinks):
    B, T = q.shape[:2]
    L = k_buf.shape[1]
    kc = jnp.concatenate([k_buf.astype(k.dtype), k], axis=1)
    vc = jnp.concatenate([v_buf.astype(v.dtype), v], axis=1)
    qi = jnp.arange(T)[:, None]
    kj = jnp.arange(L + T)[None, :]
    dist = L + qi - kj
    valid = (dist >= 0) & (dist <= WINDOW)
    o = sink_attention(q.reshape(B, T, N_KV_HEADS, GQA_GROUP, HEAD_DIM), kc, vc, dist, valid, rel_bias, sinks)
    return o.reshape(B, T, ATT_WIDTH), kc[:, -L:], vc[:, -L:]


def wkv_scan(S0, r, logw, k, v, kk, a):
    xs = tuple(jnp.swapaxes(t, 0, 1) for t in (r, logw, k, v, kk, a))

    def step(S, inp):
        r_t, lw_t, k_t, v_t, kk_t, a_t = inp
        sa = jnp.einsum('bhij,bhj->bhi', S, -kk_t)
        S = (S * jnp.exp(lw_t)[:, :, None, :] + sa[..., None] * (kk_t * a_t)[:, :, None, :]
             + v_t[..., None] * k_t[:, :, None, :])
        return S, jnp.einsum('bhij,bhj->bhi', S, r_t)

    S, ys = lax.scan(step, S0.astype(jnp.float32), xs)
    return S, jnp.swapaxes(ys, 0, 1)


def rwkv_time_mix(proj, shift0, wkv0, mu, w0, w2, a0, a2, g2, k_k, k_a, r_k, ln_w, ln_b):
    f32 = jnp.float32
    B, T = proj.shape[:2]
    prev = jnp.concatenate([shift0[:, None].astype(proj.dtype), proj[:, :-1]], axis=1)
    xm = (proj + mu * (prev - proj)).astype(f32)
    o1, o2, o3 = RWKV_WIDTH, 2 * RWKV_WIDTH, 3 * RWKV_WIDTH
    o4, o5 = o3 + DECAY_LORA, o3 + DECAY_LORA + AAA_LORA
    r, k, v = xm[..., :o1], xm[..., o1:o2], xm[..., o2:o3]
    wl, al, gl = xm[..., o3:o4], xm[..., o4:o5], xm[..., o5:]
    w = -jax.nn.softplus(-(w0.astype(f32) + jnp.tanh(wl) @ w2.astype(f32))) - 0.5
    logw = -jnp.exp(w)
    a = jax.nn.sigmoid(a0.astype(f32) + al @ a2.astype(f32))
    g = jax.nn.sigmoid(gl) @ g2.astype(f32)
    kk = k * k_k.astype(f32)
    k = k * (1.0 + (a - 1.0) * k_a.astype(f32))
    hs = lambda t: t.reshape(B, T, RWKV_HEADS, HEAD_DIM)
    r, logw, k, v, kk, a = hs(r), hs(logw), hs(k), hs(v), hs(kk), hs(a)
    kk = kk / jnp.maximum(jnp.sqrt(jnp.sum(kk * kk, axis=-1, keepdims=True)), 1e-12)
    S, y = wkv_scan(wkv0, r, logw, k, v, kk, a)
    mean = jnp.mean(y, axis=-1, keepdims=True)
    var = jnp.mean((y - mean) ** 2, axis=-1, keepdims=True)
    yn = ((y - mean) * lax.rsqrt(var + GN_EPS)).reshape(B, T, RWKV_WIDTH) * ln_w.astype(f32) + ln_b.astype(f32)
    bonus = jnp.sum(r * k * r_k.astype(f32), axis=-1, keepdims=True) * v
    out = (yn + bonus.reshape(B, T, RWKV_WIDTH)) * g
    return out, proj[:, -1], S


def trunk_layer(x, attend, shift0, wkv0, lp):
    (n1, f1g, f1u, f1d, nm, w_in, sinks, mu, w0, w2, a0, a2, g2,
     k_k, k_a, r_k, ln_w, ln_b, w_out, n2, f2g, f2u, f2d) = lp
    B, T = x.shape[:2]
    x = x + FFN_RES * swiglu(rmsnorm(x, n1), f1g, f1u, f1d)
    h = rmsnorm(x, nm)
    proj = h @ w_in
    q = proj[..., :ATT_WIDTH].reshape(B, T, N_Q_HEADS, HEAD_DIM)
    k = proj[..., ATT_WIDTH:ATT_WIDTH + KV_WIDTH].reshape(B, T, N_KV_HEADS, HEAD_DIM)
    v = proj[..., ATT_WIDTH + KV_WIDTH:ATT_COLS].reshape(B, T, N_KV_HEADS, HEAD_DIM)
    o_att, k_buf, v_buf = attend(q, k, v, sinks)
    o_rwkv, shift, S = rwkv_time_mix(proj[..., ATT_COLS:], shift0, wkv0, mu, w0, w2, a0, a2, g2,
                                     k_k, k_a, r_k, ln_w, ln_b)
    mixed = jnp.concatenate([o_att, o_rwkv.astype(o_att.dtype)], axis=-1)
    x = x + mixed @ w_out
    x = x + FFN_RES * swiglu(rmsnorm(x, n2), f2g, f2u, f2d)
    return x, k_buf, v_buf, S, shift


def setup_inputs(seed: int = 0) -> dict:
    key = jax.random.key(seed)
    ks = jax.random.split(key, 32)
    f32 = jnp.float32
    nrm = lambda k, shape, s: s * jax.random.normal(k, shape, f32)
    L_win = min(WINDOW, PAST_LEN)
    ramp = (jnp.arange(RWKV_WIDTH, dtype=f32) / (RWKV_WIDTH - 1)) ** 0.85
    return {
        "x_prompt": nrm(ks[0], (BATCH, SEQ, D_MODEL), 1.0),
        "x_sample": nrm(ks[1], (DEC_BATCH, DEC_SEQ, D_MODEL), 1.0),
        "cache_k": nrm(ks[2], (DEPTH, DEC_BATCH, L_win, N_KV_HEADS, HEAD_DIM), 1.0),
        "cache_v": nrm(ks[3], (DEPTH, DEC_BATCH, L_win, N_KV_HEADS, HEAD_DIM), 1.0),
        "state_wkv": nrm(ks[4], (DEPTH, DEC_BATCH, RWKV_HEADS, HEAD_DIM, HEAD_DIM), 0.5),
        "state_shift": nrm(ks[5], (DEPTH, DEC_BATCH, SHIFT_COLS), 1.0),
        "rel_bias": nrm(ks[6], (N_BUCKETS, N_Q_HEADS), 0.5),
        "ffn1_norm": 1.0 + nrm(ks[7], (DEPTH, D_MODEL), 0.02),
        "ffn1_w_gate": nrm(ks[8], (DEPTH, D_MODEL, D_FF), D_MODEL ** -0.5),
        "ffn1_w_up": nrm(ks[9], (DEPTH, D_MODEL, D_FF), D_MODEL ** -0.5),
        "ffn1_w_down": nrm(ks[10], (DEPTH, D_FF, D_MODEL), D_FF ** -0.5),
        "mix_norm": 1.0 + nrm(ks[11], (DEPTH, D_MODEL), 0.02),
        "w_in": nrm(ks[12], (DEPTH, D_MODEL, IN_COLS), D_MODEL ** -0.5),
        "attn_sinks": nrm(ks[13], (DEPTH, N_Q_HEADS), 0.5),
        "shift_mu": jax.random.uniform(ks[14], (DEPTH, SHIFT_COLS), f32),
        "decay_w0": -6.0 + 5.0 * ramp[None, :] + nrm(ks[15], (DEPTH, RWKV_WIDTH), 0.05),
        "decay_w2": nrm(ks[16], (DEPTH, DECAY_LORA, RWKV_WIDTH), 0.1 * DECAY_LORA ** -0.5),
        "aaa_a0": nrm(ks[17], (DEPTH, RWKV_WIDTH), 0.1),
        "aaa_a2": nrm(ks[18], (DEPTH, AAA_LORA, RWKV_WIDTH), AAA_LORA ** -0.5),
        "gate_g2": nrm(ks[19], (DEPTH, GATE_LORA, RWKV_WIDTH), GATE_LORA ** -0.5),
        "key_k": 0.85 + nrm(ks[20], (DEPTH, RWKV_WIDTH), 0.02),
        "key_a": 1.0 + nrm(ks[21], (DEPTH, RWKV_WIDTH), 0.02),
        "bonus_r_k": nrm(ks[22], (DEPTH, RWKV_HEADS, HEAD_DIM), 0.1),
        "ln_x_w": 1.0 + nrm(ks[23], (DEPTH, RWKV_WIDTH), 0.02),
        "ln_x_b": nrm(ks[24], (DEPTH, RWKV_WIDTH), 0.02),
        "w_out": nrm(ks[25], (DEPTH, D_MODEL, D_MODEL), D_MODEL ** -0.5),
        "ffn2_norm": 1.0 + nrm(ks[26], (DEPTH, D_MODEL), 0.02),
        "ffn2_w_gate": nrm(ks[27], (DEPTH, D_MODEL, D_FF), D_MODEL ** -0.5),
        "ffn2_w_up": nrm(ks[28], (DEPTH, D_MODEL, D_FF), D_MODEL ** -0.5),
        "ffn2_w_down": nrm(ks[29], (DEPTH, D_FF, D_MODEL), D_FF ** -0.5),
        "final_norm": 1.0 + nrm(ks[30], (D_MODEL,), 0.02),
    }


def reference(x_prompt, x_sample, cache_k, cache_v, state_wkv, state_shift, rel_bias,
              ffn1_norm, ffn1_w_gate, ffn1_w_up, ffn1_w_down, mix_norm, w_in, attn_sinks,
              shift_mu, decay_w0, decay_w2, aaa_a0, aaa_a2, gate_g2, key_k, key_a, bonus_r_k,
              ln_x_w, ln_x_b, w_out, ffn2_norm, ffn2_w_gate, ffn2_w_up, ffn2_w_down, final_norm):
    xp, xs = x_prompt, x_sample
    B = x_prompt.shape[0]
    kp_l, vp_l, sp_l, shp_l = [], [], [], []
    ks_l, vs_l, ss_l, shs_l = [], [], [], []
    for l in range(DEPTH):
        lp = (ffn1_norm[l], ffn1_w_gate[l], ffn1_w_up[l], ffn1_w_down[l], mix_norm[l], w_in[l],
              attn_sinks[l], shift_mu[l], decay_w0[l], decay_w2[l], aaa_a0[l], aaa_a2[l], gate_g2[l],
              key_k[l], key_a[l], bonus_r_k[l], ln_x_w[l], ln_x_b[l], w_out[l],
              ffn2_norm[l], ffn2_w_gate[l], ffn2_w_up[l], ffn2_w_down[l])
        shift0 = jnp.zeros((B, SHIFT_COLS), x_prompt.dtype)
        wkv0 = jnp.zeros((B, RWKV_HEADS, HEAD_DIM, HEAD_DIM), jnp.float32)
        xp, kp, vp, sp, shp = trunk_layer(
            xp, lambda q, k, v, s: swa_prompt(q, k, v, rel_bias, s), shift0, wkv0, lp)
        xs, kd, vd, sd, shd = trunk_layer(
            xs, lambda q, k, v, s, l=l: swa_decode(q, k, v, cache_k[l], cache_v[l], rel_bias, s),
            state_shift[l], state_wkv[l], lp)
        kp_l.append(kp); vp_l.append(vp); sp_l.append(sp); shp_l.append(shp)
        ks_l.append(kd); vs_l.append(vd); ss_l.append(sd); shs_l.append(shd)
    y_prompt = rmsnorm(xp, final_norm)
    y_sample = rmsnorm(xs, final_norm)
    new_k_prompt = jnp.stack(kp_l, 0)
    new_v_prompt = jnp.stack(vp_l, 0)
    new_wkv_prompt = jnp.stack(sp_l, 0)
    new_shift_prompt = jnp.stack(shp_l, 0)
    new_k_sample = jnp.stack(ks_l, 0)
    new_v_sample = jnp.stack(vs_l, 0)
    new_wkv_sample = jnp.stack(ss_l, 0)
    new_shift_sample = jnp.stack(shs_l, 0)
    return (y_prompt, y_sample, new_k_prompt, new_v_prompt, new_wkv_prompt, new_shift_prompt,
            new_k_sample, new_v_sample, new_wkv_sample, new_shift_sample)
```

```python
import functools
import math

import numpy as np
import jax
import jax.numpy as jnp
from jax import lax
from jax.experimental import pallas as pl
from jax.experimental.pallas import tpu as pltpu

F32 = jnp.float32
BF16 = jnp.bfloat16

D_MODEL = 2048
HEAD_DIM = 64
ATT_WIDTH = 1024
N_Q_HEADS = 16
N_KV_HEADS = 4
GQA_GROUP = 4
KV_WIDTH = 256
RWKV_WIDTH = 1024
RWKV_HEADS = 16
WINDOW = 128
BLOCK = 128
N_BUCKETS = 32
MAX_DISTANCE = 128
DECAY_LORA = 64
AAA_LORA = 64
GATE_LORA = 128
D_FF = 5504
ATT_COLS = ATT_WIDTH + 2 * KV_WIDTH
SHIFT_COLS = 3 * RWKV_WIDTH + DECAY_LORA + AAA_LORA + GATE_LORA
IN_COLS = ATT_COLS + SHIFT_COLS
RMS_EPS = 1e-5
GN_EPS = 64e-5
FFN_RES = 0.5

LANES = 128
VMEM_LIMIT_BYTES = 56 * 1024 * 1024

FF_TILE = 512
D_FF_PAD = 5632
CHUNK = 64
HEAD_PAIRS = RWKV_HEADS // 2
NEG_BIG = -1e30

_NN = (((1,), (0,)), ((), ()))
_NT = (((1,), (1,)), ((), ()))
_TN = (((0,), (0,)), ((), ()))


def _dg(a, b, dims=_NN):
    return lax.dot_general(a, b, dims, preferred_element_type=F32)


def _mm(a, b, dims=_NN):
    return _dg(a.astype(BF16), b.astype(BF16), dims)


def _split2(x):
    hi = x.astype(BF16)
    lo = (x - hi.astype(F32)).astype(BF16)
    return hi, lo


def _mm3(a, b, dims=_NN):
    ah, al = _split2(a)
    bh, bl = _split2(b)
    return _dg(ah, bh, dims) + _dg(ah, bl, dims) + _dg(al, bh, dims)


def _mm2(a, b_bf16, dims=_NN):
    ah, al = _split2(a)
    return _dg(ah, b_bf16, dims) + _dg(al, b_bf16, dims)


def _mm2r(a_bf16, b, dims=_NN):
    bh, bl = _split2(b)
    return _dg(a_bf16, bh, dims) + _dg(a_bf16, bl, dims)


def _mm_w(a, w_hi, w_lo):
    ah, al = _split2(a)
    return _dg(ah, w_hi) + _dg(ah, w_lo) + _dg(al, w_hi)


def _sigmoid(x):
    return 1.0 / (1.0 + jnp.exp(-x))


def _softplus(x):
    return jnp.maximum(x, 0.0) + jnp.log(1.0 + jnp.exp(-jnp.abs(x)))


def _rms(x, g):
    ms = jnp.mean(x * x, axis=-1, keepdims=True)
    return x * lax.rsqrt(ms + RMS_EPS) * g


def _cparams(sem):
    return pltpu.CompilerParams(dimension_semantics=sem, vmem_limit_bytes=VMEM_LIMIT_BYTES)


def _ffn_kernel(x_ref, g_ref, wg_ref, wu_ref, wd_ref, g2_ref, *rest, emit_x, n_dtype):
    if emit_x:
        ox_ref, on_ref, xn_scr, acc_scr = rest
    else:
        on_ref, xn_scr, acc_scr = rest
    j = pl.program_id(1)

    @pl.when(j == 0)
    def _():
        xn_scr[...] = _rms(x_ref[...], g_ref[...]).astype(BF16)
        acc_scr[...] = jnp.zeros_like(acc_scr)

    xn = xn_scr[...]
    gate = jnp.dot(xn, wg_ref[...], preferred_element_type=F32)
    up = jnp.dot(xn, wu_ref[...], preferred_element_type=F32)
    h = (gate * _sigmoid(gate) * up).astype(BF16)
    acc_scr[...] += jnp.dot(h, wd_ref[...], preferred_element_type=F32)

    @pl.when(j == pl.num_programs(1) - 1)
    def _():
        y = x_ref[...] + FFN_RES * acc_scr[...]
        if emit_x:
            ox_ref[...] = y
        on_ref[...] = _rms(y, g2_ref[...]).astype(n_dtype)


def _ffn(x, g, wg, wu, wd, g2, *, tm, emit_x, n_dtype):
    m = x.shape[0]
    grid = (m // tm, D_FF_PAD // FF_TILE)
    row = pl.BlockSpec((tm, D_MODEL), lambda i, j: (i, 0))
    vec = pl.BlockSpec((1, D_MODEL), lambda i, j: (0, 0))
    out_shape = [jax.ShapeDtypeStruct((m, D_MODEL), n_dtype)]
    out_specs = [row]
    if emit_x:
        out_shape = [jax.ShapeDtypeStruct((m, D_MODEL), F32)] + out_shape
        out_specs = [row, row]
    return pl.pallas_call(
        functools.partial(_ffn_kernel, emit_x=emit_x, n_dtype=n_dtype),
        grid=grid,
        in_specs=[row, vec,
                  pl.BlockSpec((D_MODEL, FF_TILE), lambda i, j: (0, j)),
                  pl.BlockSpec((D_MODEL, FF_TILE), lambda i, j: (0, j)),
                  pl.BlockSpec((FF_TILE, D_MODEL), lambda i, j: (j, 0)),
                  vec],
        out_specs=out_specs,
        out_shape=out_shape,
        scratch_shapes=[pltpu.VMEM((tm, D_MODEL), BF16), pltpu.VMEM((tm, D_MODEL), F32)],
        compiler_params=_cparams(("arbitrary", "arbitrary")),
        name="ffn",
    )(x, g, wg, wu, wd, g2)


def _proj_kernel(h_ref, w_ref, q_ref, kv_ref, rw_ref):
    h = h_ref[...]
    q = jnp.dot(h, w_ref[:, 0:ATT_WIDTH], preferred_element_type=F32)
    q_ref[...] = (q * (HEAD_DIM ** -0.5)).astype(BF16)
    kv_ref[...] = jnp.dot(h, w_ref[:, ATT_WIDTH:ATT_COLS], preferred_element_type=F32)
    rw_ref[...] = jnp.dot(h, w_ref[:, ATT_COLS:IN_COLS], preferred_element_type=F32)


def _proj(h, w, *, tm):
    m = h.shape[0]
    return pl.pallas_call(
        _proj_kernel,
        grid=(m // tm,),
        in_specs=[pl.BlockSpec((tm, D_MODEL), lambda i: (i, 0)),
                  pl.BlockSpec((D_MODEL, IN_COLS), lambda i: (0, 0), pipeline_mode=pl.Buffered(1))],
        out_specs=[pl.BlockSpec((tm, ATT_WIDTH), lambda i: (i, 0)),
                   pl.BlockSpec((tm, 2 * KV_WIDTH), lambda i: (i, 0)),
                   pl.BlockSpec((tm, SHIFT_COLS), lambda i: (i, 0))],
        out_shape=[jax.ShapeDtypeStruct((m, ATT_WIDTH), BF16),
                   jax.ShapeDtypeStruct((m, 2 * KV_WIDTH), F32),
                   jax.ShapeDtypeStruct((m, SHIFT_COLS), F32)],
        compiler_params=_cparams(("arbitrary",)),
        name="in_proj",
    )(h, w)


def _outproj_kernel(x_ref, oa_ref, orw_ref, w_ref, o_ref):
    acc = jnp.dot(oa_ref[...], w_ref[0:ATT_WIDTH, :], preferred_element_type=F32)
    acc += jnp.dot(orw_ref[...], w_ref[ATT_WIDTH:D_MODEL, :], preferred_element_type=F32)
    o_ref[...] = x_ref[...] + acc


def _outproj(x, oa, orw, w, *, tm):
    m = x.shape[0]
    row = pl.BlockSpec((tm, D_MODEL), lambda i: (i, 0))
    half = pl.BlockSpec((tm, ATT_WIDTH), lambda i: (i, 0))
    return pl.pallas_call(
        _outproj_kernel,
        grid=(m // tm,),
        in_specs=[row, half, half,
                  pl.BlockSpec((D_MODEL, D_MODEL), lambda i: (0, 0), pipeline_mode=pl.Buffered(1))],
        out_specs=row,
        out_shape=jax.ShapeDtypeStruct((m, D_MODEL), F32),
        compiler_params=_cparams(("arbitrary",)),
        name="out_proj",
    )(x, oa, orw, w)


def _bucket_table():
    qi = np.arange(BLOCK)[:, None]
    kj = np.arange(2 * BLOCK)[None, :]
    dist = BLOCK + qi - kj
    n = np.maximum(dist, 0)
    max_exact = N_BUCKETS // 2
    nf = np.maximum(n, 1).astype(np.float32)
    large = max_exact + (np.log(nf / np.float32(max_exact)) / np.float32(math.log(MAX_DISTANCE / max_exact))
                         * np.float32(N_BUCKETS - max_exact)).astype(np.int32)
    large = np.minimum(large, N_BUCKETS - 1)
    bucket = np.where(n < max_exact, n, large).astype(np.int32)
    valid = (dist >= 0) & (dist <= WINDOW)
    return np.where(valid, bucket, -1).astype(np.int32)


def _bias_kernel(bucket_ref, rb_ref, o_ref):
    bucket = bucket_ref[...]
    for h in range(N_Q_HEADS):
        acc = jnp.full((BLOCK, 2 * BLOCK), NEG_BIG, F32)
        for n in range(N_BUCKETS):
            acc = jnp.where(bucket == n, rb_ref[n, h], acc)
        o_ref[h] = acc


def _bias_table(rel_bias):
    bucket = jnp.asarray(_bucket_table())
    return pl.pallas_call(
        _bias_kernel,
        in_specs=[pl.BlockSpec(memory_space=pltpu.VMEM), pl.BlockSpec(memory_space=pltpu.SMEM)],
        out_specs=pl.BlockSpec(memory_space=pltpu.VMEM),
        out_shape=jax.ShapeDtypeStruct((N_Q_HEADS, BLOCK, 2 * BLOCK), F32),
        name="bias_table",
    )(bucket, rel_bias)


def _attn_kernel(sink_ref, q_ref, kvp_ref, kvc_ref, bias_ref, o_ref):
    j = pl.program_id(1)
    kj = lax.broadcasted_iota(jnp.int32, (BLOCK, 2 * BLOCK), 1)
    blk_ok = jnp.logical_or(kj >= BLOCK, j > 0)
    kvp = kvp_ref[...]
    kvc = kvc_ref[...]
    for g in range(N_KV_HEADS):
        ks = slice(g * HEAD_DIM, (g + 1) * HEAD_DIM)
        vs = slice(KV_WIDTH + g * HEAD_DIM, KV_WIDTH + (g + 1) * HEAD_DIM)
        kcat = jnp.concatenate([kvp[:, ks], kvc[:, ks]], axis=0).astype(BF16)
        vcat = jnp.concatenate([kvp[:, vs], kvc[:, vs]], axis=0).astype(BF16)
        for u in range(GQA_GROUP):
            h = g * GQA_GROUP + u
            hs = slice(h * HEAD_DIM, (h + 1) * HEAD_DIM)
            s = _dg(q_ref[:, hs], kcat, _NT)
            s = jnp.where(blk_ok, s + bias_ref[h], NEG_BIG)
            sink = sink_ref[h]
            m = jnp.maximum(jnp.max(s, axis=-1, keepdims=True), sink)
            p = jnp.exp(s - m)
            denom = jnp.sum(p, axis=-1, keepdims=True) + jnp.exp(sink - m)
            o = jnp.dot(p.astype(BF16), vcat, preferred_element_type=F32)
            o_ref[:, hs] = (o / denom).astype(BF16)


def _attn_prompt(q, kv, bias, sinks, *, batch, seq):
    nb = seq // BLOCK
    q3 = q.reshape(batch, seq, ATT_WIDTH)
    kv3 = kv.reshape(batch, seq, 2 * KV_WIDTH)
    out = pl.pallas_call(
        _attn_kernel,
        grid=(batch, nb),
        in_specs=[pl.BlockSpec(memory_space=pltpu.SMEM),
                  pl.BlockSpec((None, BLOCK, ATT_WIDTH), lambda b, j: (b, j, 0)),
                  pl.BlockSpec((None, BLOCK, 2 * KV_WIDTH), lambda b, j: (b, jnp.maximum(j - 1, 0), 0)),
                  pl.BlockSpec((None, BLOCK, 2 * KV_WIDTH), lambda b, j: (b, j, 0)),
                  pl.BlockSpec((N_Q_HEADS, BLOCK, 2 * BLOCK), lambda b, j: (0, 0, 0))],
        out_specs=pl.BlockSpec((None, BLOCK, ATT_WIDTH), lambda b, j: (b, j, 0)),
        out_shape=jax.ShapeDtypeStruct((batch, seq, ATT_WIDTH), BF16),
        compiler_params=_cparams(("arbitrary", "arbitrary")),
        name="attn_prompt",
    )(sinks, q3, kv3, kv3, bias)
    return out.reshape(batch * seq, ATT_WIDTH)


def _attn_decode_kernel(sink_ref, q_ref, kvn_ref, ck_ref, cv_ref, bias_ref, o_ref):
    q = q_ref[...].astype(F32)
    kvn = kvn_ref[...]
    for g in range(N_KV_HEADS):
        ks = slice(g * HEAD_DIM, (g + 1) * HEAD_DIM)
        vs = slice(KV_WIDTH + g * HEAD_DIM, KV_WIDTH + (g + 1) * HEAD_DIM)
        hs = slice(g * GQA_GROUP, (g + 1) * GQA_GROUP)
        qg = q[:, hs, :]
        kc = ck_ref[:, :, ks]
        vc = cv_ref[:, :, ks]
        kn = kvn[:, ks].astype(BF16).astype(F32)[:, None, :]
        vn = kvn[:, vs].astype(BF16).astype(F32)[:, None, :]
        s = jnp.einsum("bqd,bkd->bqk", qg.astype(BF16), kc.astype(BF16), preferred_element_type=F32)
        s = s + bias_ref[hs, 0:WINDOW][None]
        s_new = jnp.sum(qg * kn, axis=-1, keepdims=True) + bias_ref[hs, WINDOW:WINDOW + 1][None]
        sink = sink_ref[hs, 0:1][None]
        m = jnp.maximum(jnp.maximum(jnp.max(s, axis=-1, keepdims=True), s_new), sink)
        p = jnp.exp(s - m)
        p_new = jnp.exp(s_new - m)
        denom = jnp.sum(p, axis=-1, keepdims=True) + p_new + jnp.exp(sink - m)
        o = jnp.einsum("bqk,bkd->bqd", p.astype(BF16), vc.astype(BF16), preferred_element_type=F32)
        o = o + p_new * vn
        o_ref[:, hs, :] = o / denom


def _attn_decode(q, kvn, cache_k, cache_v, bias_row, sinks_b):
    b = q.shape[0]
    lw = cache_k.shape[1]
    vm = pl.BlockSpec(memory_space=pltpu.VMEM)
    out = pl.pallas_call(
        _attn_decode_kernel,
        in_specs=[vm, vm, vm, vm, vm, vm],
        out_specs=vm,
        out_shape=jax.ShapeDtypeStruct((b, N_Q_HEADS, HEAD_DIM), F32),
        compiler_params=pltpu.CompilerParams(vmem_limit_bytes=VMEM_LIMIT_BYTES),
        name="attn_decode",
    )(sinks_b, q.reshape(b, N_Q_HEADS, HEAD_DIM), kvn,
      cache_k.reshape(b, lw, KV_WIDTH), cache_v.reshape(b, lw, KV_WIDTH), bias_row)
    return out.reshape(b, ATT_WIDTH)


def _seg_expand(x, seg_ref, segt_ref):
    s = _mm2(x, seg_ref[...])
    return _mm2(s, segt_ref[...])


def _rwkv_front(x, prev, p):
    (mu_ref, w0_ref, a0_ref, kk_ref, ka_ref, lh_ref, ll_ref, gh_ref, gl_ref, seg_ref, segt_ref) = p
    xm = x + mu_ref[...] * (prev - x)
    o3 = 3 * RWKV_WIDTH
    r = xm[:, 0:RWKV_WIDTH]
    k = xm[:, RWKV_WIDTH:2 * RWKV_WIDTH]
    v = xm[:, 2 * RWKV_WIDTH:o3]
    wa = xm[:, o3:o3 + LANES]
    lane = lax.broadcasted_iota(jnp.int32, wa.shape, 1)
    z = jnp.where(lane < DECAY_LORA, jnp.tanh(wa), wa)
    lora = _mm_w(z, lh_ref[...], ll_ref[...])
    w = -_softplus(-(w0_ref[...] + lora[:, 0:RWKV_WIDTH])) - 0.5
    logw = -jnp.exp(w)
    a = _sigmoid(a0_ref[...] + lora[:, RWKV_WIDTH:2 * RWKV_WIDTH])
    g = _mm_w(_sigmoid(xm[:, o3 + LANES:SHIFT_COLS]), gh_ref[...], gl_ref[...])
    kk = k * kk_ref[...]
    kmod = k * (1.0 + (a - 1.0) * ka_ref[...])
    nrm = jnp.sqrt(_seg_expand(kk * kk, seg_ref, segt_ref))
    kk = kk / jnp.maximum(nrm, 1e-12)
    return r, logw, kmod, v, kk, kk * a, g


def _rwkv_back(y, r, kmod, v, g, q):
    rk_ref, lnw_ref, lnb_ref, seg_ref, segt_ref = q
    mean = _seg_expand(y, seg_ref, segt_ref) * (1.0 / HEAD_DIM)
    d = y - mean
    var = _seg_expand(d * d, seg_ref, segt_ref) * (1.0 / HEAD_DIM)
    yn = d * lax.rsqrt(var + GN_EPS) * lnw_ref[...] + lnb_ref[...]
    bonus = _seg_expand(r * kmod * rk_ref[...], seg_ref, segt_ref) * v
    return (yn + bonus) * g


def _rwkv_chunk_kernel(rw_ref, mu_ref, w0_ref, a0_ref, kk_ref, ka_ref, lh_ref, ll_ref, gh_ref, gl_ref,
                       seg_ref, segt_ref, rk_ref, lnw_ref, lnb_ref,
                       o_ref, s_out_ref,
                       s_scr, prev_scr, y_scr, ea_scr, er_scr, eb_scr, ek_scr, be_scr, ke_scr, v_scr):
    c = pl.program_id(1)
    C = CHUNK

    @pl.when(c == 0)
    def _():
        s_scr[...] = jnp.zeros_like(s_scr)
        prev_scr[...] = jnp.zeros_like(prev_scr)

    x = rw_ref[...]
    row = lax.broadcasted_iota(jnp.int32, x.shape, 0)
    prev = jnp.where(row == 0, prev_scr[...], pltpu.roll(x, 1, axis=0))
    prev_scr[...] = x[C - 1:C, :]
    front = (mu_ref, w0_ref, a0_ref, kk_ref, ka_ref, lh_ref, ll_ref, gh_ref, gl_ref, seg_ref, segt_ref)
    r, logw, kmod, v, kk, bb, g = _rwkv_front(x, prev, front)

    ti = lax.broadcasted_iota(jnp.int32, (C, C), 0)
    si = lax.broadcasted_iota(jnp.int32, (C, C), 1)
    incl = si <= ti
    strict = si < ti
    cum = _mm2r(jnp.where(incl, 1.0, 0.0).astype(BF16), logw)
    cum_end = cum[C - 1:C, :]
    e_in = jnp.exp(cum)
    e_neg = jnp.exp(-cum)
    e_end = jnp.exp(cum_end - cum)
    ea_scr[...] = kk * jnp.exp(cum - logw)
    er_scr[...] = r * e_in
    eb_scr[...] = bb * e_neg
    ek_scr[...] = kmod * e_neg
    be_scr[...] = bb * e_end
    ke_scr[...] = kmod * e_end
    v_scr[...] = v
    w_end = jnp.exp(cum_end)

    lane = lax.broadcasted_iota(jnp.int32, (C, LANES), 1)
    lo_half = lane < HEAD_DIM
    lane2 = lax.broadcasted_iota(jnp.int32, (2 * C, LANES), 1)
    ri = lax.broadcasted_iota(jnp.int32, (LANES, LANES), 0)
    ci = lax.broadcasted_iota(jnp.int32, (LANES, LANES), 1)
    same_head = jnp.right_shift(ri, 6) == jnp.right_shift(ci, 6)
    eye = ri == ci
    lane_w = lax.broadcasted_iota(jnp.int32, (C, 2 * LANES), 1)
    lo_half_w = jnp.bitwise_and(lane_w, LANES - 1) < HEAD_DIM

    for p in range(HEAD_PAIRS):
        ps = slice(p * LANES, (p + 1) * LANES)
        a_p = ea_scr[:, ps]
        r_p = er_scr[:, ps]
        b_p = eb_scr[:, ps]
        k_p = ek_scr[:, ps]
        v_p = v_scr[:, ps]
        ar_p = jnp.concatenate([a_p, r_p], axis=0)
        at_h, ul_h, lrb_h, lrk_h = [], [], [], []
        for hh in range(2):
            in_head = (lane2 < HEAD_DIM) if hh == 0 else (lane2 >= HEAD_DIM)
            ar_h = jnp.where(in_head, ar_p, 0.0)
            m_b = _mm(ar_h, b_p, _NT)
            m_k = _mm(ar_h, k_p, _NT)
            l_ab = jnp.where(strict, m_b[0:C], 0.0)
            l_ak = jnp.where(strict, m_k[0:C], 0.0)
            lrb_h.append(jnp.where(incl, m_b[C:2 * C], 0.0))
            lrk_h.append(jnp.where(incl, m_k[C:2 * C], 0.0))
            x1 = a_p
            x2 = -_mm(l_ak, v_p)
            pw = -l_ab
            for it in range(6):
                if it < 5:
                    px = _mm(pw, jnp.concatenate([x1, x2, pw], axis=1))
                    pw = px[:, 2 * LANES:2 * LANES + C]
                else:
                    px = _mm(pw, jnp.concatenate([x1, x2], axis=1))
                x1 = x1 + px[:, 0:LANES]
                x2 = x2 + px[:, LANES:2 * LANES]
            at_h.append(x1)
            ul_h.append(x2)
        at_p = jnp.where(lo_half, at_h[0], at_h[1])
        ul_p = jnp.where(lo_half, ul_h[0], ul_h[1])
        au_p = jnp.concatenate([at_p, ul_p], axis=1)
        lv0 = _mm(lrb_h[0], au_p)
        lv1 = _mm(lrb_h[1], au_p)
        lb = jnp.where(lo_half_w, lv0, lv1)
        rt_p = r_p - lb[:, 0:LANES]
        yl_p = lb[:, LANES:2 * LANES] + jnp.where(lo_half, _mm(lrk_h[0], v_p), _mm(lrk_h[1], v_p))
        be_p = be_scr[:, ps]
        ke_p = ke_scr[:, ps]
        d_p = jnp.where(eye, w_end[:, ps], 0.0) - jnp.where(same_head, _mm(at_p, be_p, _TN), 0.0)
        g_p = jnp.where(same_head, _mm(ul_p, be_p, _TN) + _mm(v_p, ke_p, _TN), 0.0)
        s0 = s_scr[p]
        y_scr[:, ps] = yl_p + _mm3(rt_p, s0, _NT)
        s_scr[p] = _mm3(s0, d_p) + g_p

    back = (rk_ref, lnw_ref, lnb_ref, seg_ref, segt_ref)
    o_ref[...] = _rwkv_back(y_scr[...], r, kmod, v, g, back).astype(BF16)

    @pl.when(c == pl.num_programs(1) - 1)
    def _():
        for p in range(HEAD_PAIRS):
            s_out_ref[2 * p] = s_scr[p, 0:HEAD_DIM, 0:HEAD_DIM]
            s_out_ref[2 * p + 1] = s_scr[p, HEAD_DIM:LANES, HEAD_DIM:LANES]


def _rwkv_prompt(rw, params, *, batch, seq):
    rw3 = rw.reshape(batch, seq, SHIFT_COLS)
    nc = seq // CHUNK
    const = lambda shape: pl.BlockSpec(shape, lambda b, c: tuple(0 for _ in shape))
    in_specs = [pl.BlockSpec((None, CHUNK, SHIFT_COLS), lambda b, c: (b, c, 0))]
    in_specs += [const(p.shape) for p in params]
    wide = pltpu.VMEM((CHUNK, RWKV_WIDTH), F32)
    o, s_out = pl.pallas_call(
        _rwkv_chunk_kernel,
        grid=(batch, nc),
        in_specs=in_specs,
        out_specs=[pl.BlockSpec((None, CHUNK, RWKV_WIDTH), lambda b, c: (b, c, 0)),
                   pl.BlockSpec((None, RWKV_HEADS, HEAD_DIM, HEAD_DIM), lambda b, c: (b, 0, 0, 0))],
        out_shape=[jax.ShapeDtypeStruct((batch, seq, RWKV_WIDTH), BF16),
                   jax.ShapeDtypeStruct((batch, RWKV_HEADS, HEAD_DIM, HEAD_DIM), F32)],
        scratch_shapes=[pltpu.VMEM((HEAD_PAIRS, LANES, LANES), F32),
                        pltpu.VMEM((1, SHIFT_COLS), F32),
                        wide, wide, wide, wide, wide, wide, wide, wide],
        compiler_params=_cparams(("arbitrary", "arbitrary")),
        name="rwkv_prompt",
    )(rw3, *params)
    return o.reshape(batch * seq, RWKV_WIDTH), s_out


def _rwkv_dec_front_kernel(rw_ref, prev_ref, mu_ref, w0_ref, a0_ref, kk_ref, ka_ref, lh_ref, ll_ref,
                           gh_ref, gl_ref, seg_ref, segt_ref,
                           r_ref, w_ref, k_ref, v_ref, kkn_ref, b_ref, g_ref):
    front = (mu_ref, w0_ref, a0_ref, kk_ref, ka_ref, lh_ref, ll_ref, gh_ref, gl_ref, seg_ref, segt_ref)
    r, logw, kmod, v, kk, bb, g = _rwkv_front(rw_ref[...], prev_ref[...], front)
    r_ref[...] = r
    w_ref[...] = jnp.exp(logw)
    k_ref[...] = kmod
    v_ref[...] = v
    kkn_ref[...] = kk
    b_ref[...] = bb
    g_ref[...] = g


def _rwkv_dec_state_kernel(r_ref, w_ref, k_ref, v_ref, kk_ref, b_ref, s_ref, y_ref, so_ref):
    ri = lax.broadcasted_iota(jnp.int32, (HEAD_DIM, HEAD_DIM), 0)
    ci = lax.broadcasted_iota(jnp.int32, (HEAD_DIM, HEAD_DIM), 1)
    eye = ri == ci
    for h in range(RWKV_HEADS):
        hs = slice(h * HEAD_DIM, (h + 1) * HEAD_DIM)
        s = s_ref[h]
        kk = kk_ref[:, hs]
        sa = -jnp.sum(s * kk, axis=1, keepdims=True)
        v_col = jnp.sum(jnp.where(eye, v_ref[:, hs], 0.0), axis=1, keepdims=True)
        s_new = s * w_ref[:, hs] + sa * b_ref[:, hs] + v_col * k_ref[:, hs]
        so_ref[h] = s_new
        y_col = jnp.sum(s_new * r_ref[:, hs], axis=1, keepdims=True)
        y_ref[:, hs] = jnp.sum(jnp.where(eye, y_col, 0.0), axis=0, keepdims=True)


def _rwkv_dec_back_kernel(y_ref, r_ref, k_ref, v_ref, g_ref, rk_ref, lnw_ref, lnb_ref, seg_ref, segt_ref, o_ref):
    back = (rk_ref, lnw_ref, lnb_ref, seg_ref, segt_ref)
    o_ref[...] = _rwkv_back(y_ref[...], r_ref[...], k_ref[...], v_ref[...], g_ref[...], back).astype(BF16)


def _rwkv_decode(rw, shift0, wkv0, front_params, back_params):
    b = rw.shape[0]
    vm = pl.BlockSpec(memory_space=pltpu.VMEM)
    wide = jax.ShapeDtypeStruct((b, RWKV_WIDTH), F32)
    r, w, k, v, kk, bb, g = pl.pallas_call(
        _rwkv_dec_front_kernel,
        in_specs=[vm] * (2 + len(front_params)),
        out_specs=[vm] * 7,
        out_shape=[wide] * 7,
        name="rwkv_dec_front",
    )(rw, shift0, *front_params)
    rowspec = pl.BlockSpec((None, 1, RWKV_WIDTH), lambda i: (i, 0, 0))
    stspec = pl.BlockSpec((None, RWKV_HEADS, HEAD_DIM, HEAD_DIM), lambda i: (i, 0, 0, 0))
    as_rows = lambda t: t.reshape(b, 1, RWKV_WIDTH)
    y, s_new = pl.pallas_call(
        _rwkv_dec_state_kernel,
        grid=(b,),
        in_specs=[rowspec] * 6 + [stspec],
        out_specs=[rowspec, stspec],
        out_shape=[jax.ShapeDtypeStruct((b, 1, RWKV_WIDTH), F32),
                   jax.ShapeDtypeStruct((b, RWKV_HEADS, HEAD_DIM, HEAD_DIM), F32)],
        compiler_params=_cparams(("arbitrary",)),
        name="rwkv_dec_state",
    )(as_rows(r), as_rows(w), as_rows(k), as_rows(v), as_rows(kk), as_rows(bb), wkv0)
    o = pl.pallas_call(
        _rwkv_dec_back_kernel,
        in_specs=[vm] * (5 + len(back_params)),
        out_specs=vm,
        out_shape=jax.ShapeDtypeStruct((b, RWKV_WIDTH), BF16),
        name="rwkv_dec_back",
    )(y.reshape(b, RWKV_WIDTH), r, k, v, g, *back_params)
    return o, s_new


def _split_w(w):
    hi = w.astype(BF16)
    return hi, (w - hi.astype(F32)).astype(BF16)


def _pad_ff(w, axis):
    pad = [(0, 0), (0, 0)]
    pad[axis] = (0, D_FF_PAD - D_FF)
    return jnp.pad(w.astype(BF16), pad)


def kernel(x_prompt, x_sample, cache_k, cache_v, state_wkv, state_shift, rel_bias, ffn1_norm, ffn1_w_gate, ffn1_w_up, ffn1_w_down, mix_norm, w_in, attn_sinks, shift_mu, decay_w0, decay_w2, aaa_a0, aaa_a2, gate_g2, key_k, key_a, bonus_r_k, ln_x_w, ln_x_b, w_out, ffn2_norm, ffn2_w_gate, ffn2_w_up, ffn2_w_down, final_norm):
    batch, seq, _ = x_prompt.shape
    dec_b = x_sample.shape[0]
    lw = cache_k.shape[2]
    l = 0
    row = lambda t: t.reshape(1, -1)

    f1 = (_pad_ff(ffn1_w_gate[l], 1), _pad_ff(ffn1_w_up[l], 1), _pad_ff(ffn1_w_down[l], 0))
    f2 = (_pad_ff(ffn2_w_gate[l], 1), _pad_ff(ffn2_w_up[l], 1), _pad_ff(ffn2_w_down[l], 0))
    w_in_b = w_in[l].astype(BF16)
    w_out_b = w_out[l].astype(BF16)
    lora_w = jnp.zeros((LANES, 2 * RWKV_WIDTH), F32)
    lora_w = lora_w.at[0:DECAY_LORA, 0:RWKV_WIDTH].set(decay_w2[l])
    lora_w = lora_w.at[DECAY_LORA:LANES, RWKV_WIDTH:].set(aaa_a2[l])
    lh, ll = _split_w(lora_w)
    gh, gl = _split_w(gate_g2[l])
    seg_np = (np.arange(RWKV_WIDTH)[:, None] // HEAD_DIM == np.arange(LANES)[None, :]).astype(np.float32)
    seg = jnp.asarray(seg_np, BF16)
    segt = jnp.asarray(seg_np.T, BF16)
    front_params = (row(shift_mu[l]), row(decay_w0[l]), row(aaa_a0[l]), row(key_k[l]), row(key_a[l]),
                    lh, ll, gh, gl, seg, segt)
    back_params = (row(bonus_r_k[l]), row(ln_x_w[l]), row(ln_x_b[l]), seg, segt)

    bias = _bias_table(rel_bias)
    sinks = attn_sinks[l]

    xp = x_prompt.reshape(batch * seq, D_MODEL)
    x1, h1 = _ffn(xp, row(ffn1_norm[l]), *f1, row(mix_norm[l]), tm=512, emit_x=True, n_dtype=BF16)
    q, kv, rw = _proj(h1, w_in_b, tm=256)
    o_att = _attn_prompt(q, kv, bias, sinks, batch=batch, seq=seq)
    o_rw, s_p = _rwkv_prompt(rw, front_params + back_params[:3], batch=batch, seq=seq)
    x2 = _outproj(x1, o_att, o_rw, w_out_b, tm=512)
    (y_p,) = _ffn(x2, row(ffn2_norm[l]), *f2, row(final_norm), tm=512, emit_x=False, n_dtype=F32)

    xs = x_sample.reshape(dec_b, D_MODEL)
    x1s, h1s = _ffn(xs, row(ffn1_norm[l]), *f1, row(mix_norm[l]), tm=dec_b, emit_x=True, n_dtype=BF16)
    qs, kvs, rws = _proj(h1s, w_in_b, tm=dec_b)
    sinks_b = jnp.broadcast_to(sinks[:, None], (N_Q_HEADS, LANES))
    o_att_s = _attn_decode(qs, kvs, cache_k[l], cache_v[l], bias[:, 0, :], sinks_b)
    o_rw_s, s_s = _rwkv_decode(rws, state_shift[l], state_wkv[l], front_params, back_params)
    x2s = _outproj(x1s, o_att_s.astype(BF16), o_rw_s, w_out_b, tm=dec_b)
    (y_s,) = _ffn(x2s, row(ffn2_norm[l]), *f2, row(final_norm), tm=dec_b, emit_x=False, n_dtype=F32)

    kv3 = kv.reshape(batch, seq, 2 * KV_WIDTH)
    lp = min(WINDOW, seq)
    new_k_p = kv3[:, seq - lp:, 0:KV_WIDTH].reshape(1, batch, lp, N_KV_HEADS, HEAD_DIM)
    new_v_p = kv3[:, seq - lp:, KV_WIDTH:].reshape(1, batch, lp, N_KV_HEADS, HEAD_DIM)
    new_shift_p = rw.reshape(batch, seq, SHIFT_COLS)[:, seq - 1][None]
    k_new = kvs[:, 0:KV_WIDTH].reshape(dec_b, 1, N_KV_HEADS, HEAD_DIM)
    v_new = kvs[:, KV_WIDTH:].reshape(dec_b, 1, N_KV_HEADS, HEAD_DIM)
    new_k_s = jnp.concatenate([cache_k[l], k_new], axis=1)[:, -lw:][None]
    new_v_s = jnp.concatenate([cache_v[l], v_new], axis=1)[:, -lw:][None]
    return (y_p.reshape(batch, seq, D_MODEL), y_s.reshape(dec_b, 1, D_MODEL),
            new_k_p, new_v_p, s_p[None], new_shift_p,
            new_k_s, new_v_s, s_s[None], rws[None])
```

```python
import functools
import math

import numpy as np
import jax
import jax.numpy as jnp
from jax import lax
from jax.experimental import pallas as pl
from jax.experimental.pallas import tpu as pltpu

F32 = jnp.float32
BF16 = jnp.bfloat16

D_MODEL = 2048
HEAD_DIM = 64
ATT_WIDTH = 1024
N_Q_HEADS = 16
N_KV_HEADS = 4
GQA_GROUP = 4
KV_WIDTH = 256
RWKV_WIDTH = 1024
RWKV_HEADS = 16
WINDOW = 128
BLOCK = 128
N_BUCKETS = 32
MAX_DISTANCE = 128
DECAY_LORA = 64
AAA_LORA = 64
GATE_LORA = 128
D_FF = 5504
ATT_COLS = ATT_WIDTH + 2 * KV_WIDTH
SHIFT_COLS = 3 * RWKV_WIDTH + DECAY_LORA + AAA_LORA + GATE_LORA
IN_COLS = ATT_COLS + SHIFT_COLS
RMS_EPS = 1e-5
GN_EPS = 64e-5
FFN_RES = 0.5

LANES = 128
VMEM_LIMIT_BYTES = 56 * 1024 * 1024

FF_TILE = 512
D_FF_PAD = 5632
CHUNK = 64
HEAD_PAIRS = RWKV_HEADS // 2
NEG_BIG = -1e30

_NN = (((1,), (0,)), ((), ()))
_NT = (((1,), (1,)), ((), ()))
_TN = (((0,), (0,)), ((), ()))


def _dg(a, b, dims=_NN):
    return lax.dot_general(a, b, dims, preferred_element_type=F32)


def _mm(a, b, dims=_NN):
    return _dg(a.astype(BF16), b.astype(BF16), dims)


def _split2(x):
    hi = x.astype(BF16)
    lo = (x - hi.astype(F32)).astype(BF16)
    return hi, lo


def _mm3(a, b, dims=_NN):
    ah, al = _split2(a)
    bh, bl = _split2(b)
    return _dg(ah, bh, dims) + _dg(ah, bl, dims) + _dg(al, bh, dims)


def _mm2(a, b_bf16, dims=_NN):
    ah, al = _split2(a)
    return _dg(ah, b_bf16, dims) + _dg(al, b_bf16, dims)


def _mm2r(a_bf16, b, dims=_NN):
    bh, bl = _split2(b)
    return _dg(a_bf16, bh, dims) + _dg(a_bf16, bl, dims)


def _mm_w(a, w_hi, w_lo):
    ah, al = _split2(a)
    return _dg(ah, w_hi) + _dg(ah, w_lo) + _dg(al, w_hi)


def _sigmoid(x):
    return 1.0 / (1.0 + jnp.exp(-x))


def _softplus(x):
    return jnp.maximum(x, 0.0) + jnp.log(1.0 + jnp.exp(-jnp.abs(x)))


def _rms(x, g):
    ms = jnp.mean(x * x, axis=-1, keepdims=True)
    return x * lax.rsqrt(ms + RMS_EPS) * g


def _cparams(sem):
    return pltpu.CompilerParams(dimension_semantics=sem, vmem_limit_bytes=VMEM_LIMIT_BYTES)


def _ffn_kernel(x_ref, g_ref, wg_ref, wu_ref, wd_ref, g2_ref, *rest, emit_x, n_dtype):
    if emit_x:
        ox_ref, on_ref, xn_scr, acc_scr = rest
    else:
        on_ref, xn_scr, acc_scr = rest
    j = pl.program_id(1)

    @pl.when(j == 0)
    def _():
        xn_scr[...] = _rms(x_ref[...], g_ref[...]).astype(BF16)
        acc_scr[...] = jnp.zeros_like(acc_scr)

    xn = xn_scr[...]
    gate = jnp.dot(xn, wg_ref[...], preferred_element_type=F32)
    up = jnp.dot(xn, wu_ref[...], preferred_element_type=F32)
    h = (gate * _sigmoid(gate) * up).astype(BF16)
    acc_scr[...] += jnp.dot(h, wd_ref[...], preferred_element_type=F32)

    @pl.when(j == pl.num_programs(1) - 1)
    def _():
        y = x_ref[...] + FFN_RES * acc_scr[...]
        if emit_x:
            ox_ref[...] = y
        on_ref[...] = _rms(y, g2_ref[...]).astype(n_dtype)


def _ffn(x, g, wg, wu, wd, g2, *, tm, emit_x, n_dtype):
    m = x.shape[0]
    grid = (m // tm, D_FF_PAD // FF_TILE)
    row = pl.BlockSpec((tm, D_MODEL), lambda i, j: (i, 0))
    vec = pl.BlockSpec((1, D_MODEL), lambda i, j: (0, 0))
    out_shape = [jax.ShapeDtypeStruct((m, D_MODEL), n_dtype)]
    out_specs = [row]
    if emit_x:
        out_shape = [jax.ShapeDtypeStruct((m, D_MODEL), F32)] + out_shape
        out_specs = [row, row]
    return pl.pallas_call(
        functools.partial(_ffn_kernel, emit_x=emit_x, n_dtype=n_dtype),
        grid=grid,
        in_specs=[row, vec,
                  pl.BlockSpec((D_MODEL, FF_TILE), lambda i, j: (0, j)),
                  pl.BlockSpec((D_MODEL, FF_TILE), lambda i, j: (0, j)),
                  pl.BlockSpec((FF_TILE, D_MODEL), lambda i, j: (j, 0)),
                  vec],
        out_specs=out_specs,
        out_shape=out_shape,
        scratch_shapes=[pltpu.VMEM((tm, D_MODEL), BF16), pltpu.VMEM((tm, D_MODEL), F32)],
        compiler_params=_cparams(("arbitrary", "arbitrary")),
        name="ffn",
    )(x, g, wg, wu, wd, g2)


def _proj_kernel(h_ref, w_ref, q_ref, kv_ref, rw_ref):
    h = h_ref[...]
    q = jnp.dot(h, w_ref[:, 0:ATT_WIDTH], preferred_element_type=F32)
    q_ref[...] = (q * (HEAD_DIM ** -0.5)).astype(BF16)
    kv_ref[...] = jnp.dot(h, w_ref[:, ATT_WIDTH:ATT_COLS], preferred_element_type=F32)
    rw_ref[...] = jnp.dot(h, w_ref[:, ATT_COLS:IN_COLS], preferred_element_type=F32)


def _proj(h, w, *, tm):
    m = h.shape[0]
    return pl.pallas_call(
        _proj_kernel,
        grid=(m // tm,),
        in_specs=[pl.BlockSpec((tm, D_MODEL), lambda i: (i, 0)),
                  pl.BlockSpec((D_MODEL, IN_COLS), lambda i: (0, 0), pipeline_mode=pl.Buffered(1))],
        out_specs=[pl.BlockSpec((tm, ATT_WIDTH), lambda i: (i, 0)),
                   pl.BlockSpec((tm, 2 * KV_WIDTH), lambda i: (i, 0)),
                   pl.BlockSpec((tm, SHIFT_COLS), lambda i: (i, 0))],
        out_shape=[jax.ShapeDtypeStruct((m, ATT_WIDTH), BF16),
                   jax.ShapeDtypeStruct((m, 2 * KV_WIDTH), F32),
                   jax.ShapeDtypeStruct((m, SHIFT_COLS), F32)],
        compiler_params=_cparams(("arbitrary",)),
        name="in_proj",
    )(h, w)


def _outproj_kernel(x_ref, oa_ref, orw_ref, w_ref, o_ref):
    acc = jnp.dot(oa_ref[...], w_ref[0:ATT_WIDTH, :], preferred_element_type=F32)
    acc += jnp.dot(orw_ref[...], w_ref[ATT_WIDTH:D_MODEL, :], preferred_element_type=F32)
    o_ref[...] = x_ref[...] + acc


def _outproj(x, oa, orw, w, *, tm):
    m = x.shape[0]
    row = pl.BlockSpec((tm, D_MODEL), lambda i: (i, 0))
    half = pl.BlockSpec((tm, ATT_WIDTH), lambda i: (i, 0))
    return pl.pallas_call(
        _outproj_kernel,
        grid=(m // tm,),
        in_specs=[row, half, half,
                  pl.BlockSpec((D_MODEL, D_MODEL), lambda i: (0, 0), pipeline_mode=pl.Buffered(1))],
        out_specs=row,
        out_shape=jax.ShapeDtypeStruct((m, D_MODEL), F32),
        compiler_params=_cparams(("arbitrary",)),
        name="out_proj",
    )(x, oa, orw, w)


def _bucket_table():
    qi = np.arange(BLOCK)[:, None]
    kj = np.arange(2 * BLOCK)[None, :]
    dist = BLOCK + qi - kj
    n = np.maximum(dist, 0)
    max_exact = N_BUCKETS // 2
    nf = np.maximum(n, 1).astype(np.float32)
    large = max_exact + (np.log(nf / np.float32(max_exact)) / np.float32(math.log(MAX_DISTANCE / max_exact))
                         * np.float32(N_BUCKETS - max_exact)).astype(np.int32)
    large = np.minimum(large, N_BUCKETS - 1)
    bucket = np.where(n < max_exact, n, large).astype(np.int32)
    valid = (dist >= 0) & (dist <= WINDOW)
    return np.where(valid, bucket, -1).astype(np.int32)


def _bias_kernel(bucket_ref, rb_ref, o_ref):
    bucket = bucket_ref[...]
    for h in range(N_Q_HEADS):
        acc = jnp.full((BLOCK, 2 * BLOCK), NEG_BIG, F32)
        for n in range(N_BUCKETS):
            acc = jnp.where(bucket == n, rb_ref[n, h], acc)
        o_ref[h] = acc


def _bias_table(rel_bias):
    bucket = jnp.asarray(_bucket_table())
    return pl.pallas_call(
        _bias_kernel,
        in_specs=[pl.BlockSpec(memory_space=pltpu.VMEM), pl.BlockSpec(memory_space=pltpu.SMEM)],
        out_specs=pl.BlockSpec(memory_space=pltpu.VMEM),
        out_shape=jax.ShapeDtypeStruct((N_Q_HEADS, BLOCK, 2 * BLOCK), F32),
        name="bias_table",
    )(bucket, rel_bias)


def _attn_kernel(sink_ref, q_ref, kvp_ref, kvc_ref, bias_ref, o_ref):
    j = pl.program_id(1)
    kj = lax.broadcasted_iota(jnp.int32, (BLOCK, 2 * BLOCK), 1)
    blk_ok = jnp.logical_or(kj >= BLOCK, j > 0)
    kvp = kvp_ref[...]
    kvc = kvc_ref[...]
    for g in range(N_KV_HEADS):
        ks = slice(g * HEAD_DIM, (g + 1) * HEAD_DIM)
        vs = slice(KV_WIDTH + g * HEAD_DIM, KV_WIDTH + (g + 1) * HEAD_DIM)
        kcat = jnp.concatenate([kvp[:, ks], kvc[:, ks]], axis=0).astype(BF16)
        vcat = jnp.concatenate([kvp[:, vs], kvc[:, vs]], axis=0).astype(BF16)
        for u in range(GQA_GROUP):
            h = g * GQA_GROUP + u
            hs = slice(h * HEAD_DIM, (h + 1) * HEAD_DIM)
            s = _dg(q_ref[:, hs], kcat, _NT)
            s = jnp.where(blk_ok, s + bias_ref[h], NEG_BIG)
            sink = sink_ref[h]
            m = jnp.maximum(jnp.max(s, axis=-1, keepdims=True), sink)
            p = jnp.exp(s - m)
            denom = jnp.sum(p, axis=-1, keepdims=True) + jnp.exp(sink - m)
            o = jnp.dot(p.astype(BF16), vcat, preferred_element_type=F32)
            o_ref[:, hs] = (o / denom).astype(BF16)


def _attn_prompt(q, kv, bias, sinks, *, batch, seq):
    nb = seq // BLOCK
    q3 = q.reshape(batch, seq, ATT_WIDTH)
    kv3 = kv.reshape(batch, seq, 2 * KV_WIDTH)
    out = pl.pallas_call(
        _attn_kernel,
        grid=(batch, nb),
        in_specs=[pl.BlockSpec(memory_space=pltpu.SMEM),
                  pl.BlockSpec((None, BLOCK, ATT_WIDTH), lambda b, j: (b, j, 0)),
                  pl.BlockSpec((None, BLOCK, 2 * KV_WIDTH), lambda b, j: (b, jnp.maximum(j - 1, 0), 0)),
                  pl.BlockSpec((None, BLOCK, 2 * KV_WIDTH), lambda b, j: (b, j, 0)),
                  pl.BlockSpec((N_Q_HEADS, BLOCK, 2 * BLOCK), lambda b, j: (0, 0, 0))],
        out_specs=pl.BlockSpec((None, BLOCK, ATT_WIDTH), lambda b, j: (b, j, 0)),
        out_shape=jax.ShapeDtypeStruct((batch, seq, ATT_WIDTH), BF16),
        compiler_params=_cparams(("arbitrary", "arbitrary")),
        name="attn_prompt",
    )(sinks, q3, kv3, kv3, bias)
    return out.reshape(batch * seq, ATT_WIDTH)


def _attn_decode_kernel(sink_ref, q_ref, kvn_ref, ck_ref, cv_ref, bias_ref, o_ref):
    q = q_ref[...].astype(F32)
    kvn = kvn_ref[...]
    for g in range(N_KV_HEADS):
        ks = slice(g * HEAD_DIM, (g + 1) * HEAD_DIM)
        vs = slice(KV_WIDTH + g * HEAD_DIM, KV_WIDTH + (g + 1) * HEAD_DIM)
        hs = slice(g * GQA_GROUP, (g + 1) * GQA_GROUP)
        qg = q[:, hs, :]
        kc = ck_ref[:, :, ks]
        vc = cv_ref[:, :, ks]
        kn = kvn[:, ks].astype(BF16).astype(F32)[:, None, :]
        vn = kvn[:, vs].astype(BF16).astype(F32)[:, None, :]
        s = jnp.einsum("bqd,bkd->bqk", qg.astype(BF16), kc.astype(BF16), preferred_element_type=F32)
        s = s + bias_ref[hs, 0:WINDOW][None]
        s_new = jnp.sum(qg * kn, axis=-1, keepdims=True) + bias_ref[hs, WINDOW:WINDOW + 1][None]
        sink = sink_ref[hs, 0:1][None]
        m = jnp.maximum(jnp.maximum(jnp.max(s, axis=-1, keepdims=True), s_new), sink)
        p = jnp.exp(s - m)
        p_new = jnp.exp(s_new - m)
        denom = jnp.sum(p, axis=-1, keepdims=True) + p_new + jnp.exp(sink - m)
        o = jnp.einsum("bqk,bkd->bqd", p.astype(BF16), vc.astype(BF16), preferred_element_type=F32)
        o = o + p_new * vn
        o_ref[:, hs, :] = o / denom


def _attn_decode(q, kvn, cache_k, cache_v, bias_row, sinks_b):
    b = q.shape[0]
    lw = cache_k.shape[1]
    vm = pl.BlockSpec(memory_space=pltpu.VMEM)
    out = pl.pallas_call(
        _attn_decode_kernel,
        in_specs=[vm, vm, vm, vm, vm, vm],
        out_specs=vm,
        out_shape=jax.ShapeDtypeStruct((b, N_Q_HEADS, HEAD_DIM), F32),
        compiler_params=pltpu.CompilerParams(vmem_limit_bytes=VMEM_LIMIT_BYTES),
        name="attn_decode",
    )(sinks_b, q.reshape(b, N_Q_HEADS, HEAD_DIM), kvn,
      cache_k.reshape(b, lw, KV_WIDTH), cache_v.reshape(b, lw, KV_WIDTH), bias_row)
    return out.reshape(b, ATT_WIDTH)


def _seg_expand(x, seg_ref, segt_ref):
    s = _mm2(x, seg_ref[...])
    return _mm2(s, segt_ref[...])


def _rwkv_front(x, prev, p):
    (mu_ref, w0_ref, a0_ref, kk_ref, ka_ref, lh_ref, ll_ref, gh_ref, gl_ref, seg_ref, segt_ref) = p
    xm = x + mu_ref[...] * (prev - x)
    o3 = 3 * RWKV_WIDTH
    r = xm[:, 0:RWKV_WIDTH]
    k = xm[:, RWKV_WIDTH:2 * RWKV_WIDTH]
    v = xm[:, 2 * RWKV_WIDTH:o3]
    wa = xm[:, o3:o3 + LANES]
    lane = lax.broadcasted_iota(jnp.int32, wa.shape, 1)
    z = jnp.where(lane < DECAY_LORA, jnp.tanh(wa), wa)
    lora = _mm_w(z, lh_ref[...], ll_ref[...])
    w = -_softplus(-(w0_ref[...] + lora[:, 0:RWKV_WIDTH])) - 0.5
    logw = -jnp.exp(w)
    a = _sigmoid(a0_ref[...] + lora[:, RWKV_WIDTH:2 * RWKV_WIDTH])
    g = _mm_w(_sigmoid(xm[:, o3 + LANES:SHIFT_COLS]), gh_ref[...], gl_ref[...])
    kk = k * kk_ref[...]
    kmod = k * (1.0 + (a - 1.0) * ka_ref[...])
    nrm = jnp.sqrt(_seg_expand(kk * kk, seg_ref, segt_ref))
    kk = kk / jnp.maximum(nrm, 1e-12)
    return r, logw, kmod, v, kk, kk * a, g


def _rwkv_back(y, r, kmod, v, g, q):
    rk_ref, lnw_ref, lnb_ref, seg_ref, segt_ref = q
    mean = _seg_expand(y, seg_ref, segt_ref) * (1.0 / HEAD_DIM)
    d = y - mean
    var = _seg_expand(d * d, seg_ref, segt_ref) * (1.0 / HEAD_DIM)
    yn = d * lax.rsqrt(var + GN_EPS) * lnw_ref[...] + lnb_ref[...]
    bonus = _seg_expand(r * kmod * rk_ref[...], seg_ref, segt_ref) * v
    return (yn + bonus) * g


def _rwkv_chunk_kernel(rw_ref, mu_ref, w0_ref, a0_ref, kk_ref, ka_ref, lh_ref, ll_ref, gh_ref, gl_ref,
                       seg_ref, segt_ref, rk_ref, lnw_ref, lnb_ref,
                       o_ref, s_out_ref,
                       s_scr, prev_scr, y_scr, ea_scr, er_scr, eb_scr, ek_scr, be_scr, ke_scr, v_scr):
    c = pl.program_id(1)
    C = CHUNK

    @pl.when(c == 0)
    def _():
        s_scr[...] = jnp.zeros_like(s_scr)
        prev_scr[...] = jnp.zeros_like(prev_scr)

    x = rw_ref[...]
    row = lax.broadcasted_iota(jnp.int32, x.shape, 0)
    prev = jnp.where(row == 0, prev_scr[...], pltpu.roll(x, 1, axis=0))
    prev_scr[...] = x[C - 1:C, :]
    front = (mu_ref, w0_ref, a0_ref, kk_ref, ka_ref, lh_ref, ll_ref, gh_ref, gl_ref, seg_ref, segt_ref)
    r, logw, kmod, v, kk, bb, g = _rwkv_front(x, prev, front)

    ti = lax.broadcasted_iota(jnp.int32, (C, C), 0)
    si = lax.broadcasted_iota(jnp.int32, (C, C), 1)
    cum = _mm2r(jnp.where(si <= ti, 1.0, 0.0).astype(BF16), logw)
    cum_end = cum[C - 1:C, :]
    e_in = jnp.exp(cum)
    e_neg = jnp.exp(-cum)
    e_end = jnp.exp(cum_end - cum)
    ea_scr[...] = kk * jnp.exp(cum - logw)
    er_scr[...] = r * e_in
    eb_scr[...] = bb * e_neg
    ek_scr[...] = kmod * e_neg
    be_scr[...] = bb * e_end
    ke_scr[...] = kmod * e_end
    v_scr[...] = v
    w_end = jnp.exp(cum_end)

    lane = lax.broadcasted_iota(jnp.int32, (C, LANES), 1)
    lo_half = lane < HEAD_DIM
    ri = lax.broadcasted_iota(jnp.int32, (LANES, LANES), 0)
    ci = lax.broadcasted_iota(jnp.int32, (LANES, LANES), 1)
    same_head = jnp.right_shift(ri, 6) == jnp.right_shift(ci, 6)
    eye = ri == ci
    t_loc = jnp.bitwise_and(ri, C - 1)
    s_loc = jnp.bitwise_and(ci, C - 1)
    bd_strict = jnp.logical_and(same_head, s_loc < t_loc)
    bd_incl = jnp.logical_and(same_head, s_loc <= t_loc)
    zero_blk = jnp.zeros((2 * C, LANES), F32)

    def stack2(t):
        return jnp.concatenate([jnp.where(lo_half, t, 0.0), jnp.where(lo_half, 0.0, t)], axis=0)

    def fold2(t):
        return t[0:C] + t[C:2 * C]

    pairs = range(HEAD_PAIRS)
    sl = [slice(p * LANES, (p + 1) * LANES) for p in pairs]
    v2, x1, x2, pw, lrbk = [], [], [], [], []
    for p in pairs:
        a2 = stack2(ea_scr[:, sl[p]])
        r2 = stack2(er_scr[:, sl[p]])
        b_p = eb_scr[:, sl[p]]
        k_p = ek_scr[:, sl[p]]
        m = _mm(jnp.concatenate([a2, r2], axis=0), jnp.concatenate([b_p, b_p, k_p, k_p], axis=0), _NT)
        l_ab = jnp.where(bd_strict, m[0:2 * C, 0:LANES], 0.0)
        l_ak = jnp.where(bd_strict, m[0:2 * C, LANES:2 * LANES], 0.0)
        l_rb = jnp.where(bd_incl, m[2 * C:4 * C, 0:LANES], 0.0)
        l_rk = jnp.where(bd_incl, m[2 * C:4 * C, LANES:2 * LANES], 0.0)
        v2.append(stack2(v_scr[:, sl[p]]))
        x1.append(a2)
        pw.append(-l_ab)
        lrbk.append(jnp.concatenate([l_rb, l_rk], axis=1))
        x2.append(l_ak)
    for p in pairs:
        x2[p] = -_mm(x2[p], v2[p])
    for it in range(6):
        for p in pairs:
            if it < 5:
                px = _mm(pw[p], jnp.concatenate([x1[p], x2[p], pw[p]], axis=1))
                pw[p] = px[:, 2 * LANES:3 * LANES]
            else:
                px = _mm(pw[p], jnp.concatenate([x1[p], x2[p]], axis=1))
            x1[p] = x1[p] + px[:, 0:LANES]
            x2[p] = x2[p] + px[:, LANES:2 * LANES]
    at, ul, rt, yl = [], [], [], []
    for p in pairs:
        rhs = jnp.concatenate([jnp.concatenate([x1[p], x2[p]], axis=1),
                               jnp.concatenate([zero_blk, v2[p]], axis=1)], axis=0)
        lb = fold2(_mm(lrbk[p], rhs))
        at.append(fold2(x1[p]))
        ul.append(fold2(x2[p]))
        rt.append(er_scr[:, sl[p]] - lb[:, 0:LANES])
        yl.append(lb[:, LANES:2 * LANES])
    for p in pairs:
        be_p = be_scr[:, sl[p]]
        d_p = jnp.where(eye, w_end[:, sl[p]], 0.0) - jnp.where(same_head, _mm(at[p], be_p, _TN), 0.0)
        g_p = jnp.where(same_head, _mm(ul[p], be_p, _TN) + _mm(v_scr[:, sl[p]], ke_scr[:, sl[p]], _TN), 0.0)
        s0 = s_scr[p]
        y_scr[:, sl[p]] = yl[p] + _mm3(rt[p], s0, _NT)
        s_scr[p] = _mm3(s0, d_p) + g_p

    back = (rk_ref, lnw_ref, lnb_ref, seg_ref, segt_ref)
    o_ref[...] = _rwkv_back(y_scr[...], r, kmod, v, g, back).astype(BF16)

    @pl.when(c == pl.num_programs(1) - 1)
    def _():
        for p in range(HEAD_PAIRS):
            s_out_ref[2 * p] = s_scr[p, 0:HEAD_DIM, 0:HEAD_DIM]
            s_out_ref[2 * p + 1] = s_scr[p, HEAD_DIM:LANES, HEAD_DIM:LANES]


def _rwkv_prompt(rw, params, *, batch, seq):
    rw3 = rw.reshape(batch, seq, SHIFT_COLS)
    nc = seq // CHUNK
    const = lambda shape: pl.BlockSpec(shape, lambda b, c: tuple(0 for _ in shape))
    in_specs = [pl.BlockSpec((None, CHUNK, SHIFT_COLS), lambda b, c: (b, c, 0))]
    in_specs += [const(p.shape) for p in params]
    wide = pltpu.VMEM((CHUNK, RWKV_WIDTH), F32)
    o, s_out = pl.pallas_call(
        _rwkv_chunk_kernel,
        grid=(batch, nc),
        in_specs=in_specs,
        out_specs=[pl.BlockSpec((None, CHUNK, RWKV_WIDTH), lambda b, c: (b, c, 0)),
                   pl.BlockSpec((None, RWKV_HEADS, HEAD_DIM, HEAD_DIM), lambda b, c: (b, 0, 0, 0))],
        out_shape=[jax.ShapeDtypeStruct((batch, seq, RWKV_WIDTH), BF16),
                   jax.ShapeDtypeStruct((batch, RWKV_HEADS, HEAD_DIM, HEAD_DIM), F32)],
        scratch_shapes=[pltpu.VMEM((HEAD_PAIRS, LANES, LANES), F32),
                        pltpu.VMEM((1, SHIFT_COLS), F32),
                        wide, wide, wide, wide, wide, wide, wide, wide],
        compiler_params=_cparams(("arbitrary", "arbitrary")),
        name="rwkv_prompt",
    )(rw3, *params)
    return o.reshape(batch * seq, RWKV_WIDTH), s_out


def _rwkv_dec_front_kernel(rw_ref, prev_ref, mu_ref, w0_ref, a0_ref, kk_ref, ka_ref, lh_ref, ll_ref,
                           gh_ref, gl_ref, seg_ref, segt_ref,
                           r_ref, w_ref, k_ref, v_ref, kkn_ref, b_ref, g_ref):
    front = (mu_ref, w0_ref, a0_ref, kk_ref, ka_ref, lh_ref, ll_ref, gh_ref, gl_ref, seg_ref, segt_ref)
    r, logw, kmod, v, kk, bb, g = _rwkv_front(rw_ref[...], prev_ref[...], front)
    r_ref[...] = r
    w_ref[...] = jnp.exp(logw)
    k_ref[...] = kmod
    v_ref[...] = v
    kkn_ref[...] = kk
    b_ref[...] = bb
    g_ref[...] = g


def _rwkv_dec_state_kernel(r_ref, w_ref, k_ref, v_ref, kk_ref, b_ref, s_ref, y_ref, so_ref):
    ri = lax.broadcasted_iota(jnp.int32, (HEAD_DIM, HEAD_DIM), 0)
    ci = lax.broadcasted_iota(jnp.int32, (HEAD_DIM, HEAD_DIM), 1)
    eye = ri == ci
    for h in range(RWKV_HEADS):
        hs = slice(h * HEAD_DIM, (h + 1) * HEAD_DIM)
        s = s_ref[h]
        kk = kk_ref[:, hs]
        sa = -jnp.sum(s * kk, axis=1, keepdims=True)
        v_col = jnp.sum(jnp.where(eye, v_ref[:, hs], 0.0), axis=1, keepdims=True)
        s_new = s * w_ref[:, hs] + sa * b_ref[:, hs] + v_col * k_ref[:, hs]
        so_ref[h] = s_new
        y_col = jnp.sum(s_new * r_ref[:, hs], axis=1, keepdims=True)
        y_ref[:, hs] = jnp.sum(jnp.where(eye, y_col, 0.0), axis=0, keepdims=True)


def _rwkv_dec_back_kernel(y_ref, r_ref, k_ref, v_ref, g_ref, rk_ref, lnw_ref, lnb_ref, seg_ref, segt_ref, o_ref):
    back = (rk_ref, lnw_ref, lnb_ref, seg_ref, segt_ref)
    o_ref[...] = _rwkv_back(y_ref[...], r_ref[...], k_ref[...], v_ref[...], g_ref[...], back).astype(BF16)


def _rwkv_decode(rw, shift0, wkv0, front_params, back_params):
    b = rw.shape[0]
    vm = pl.BlockSpec(memory_space=pltpu.VMEM)
    wide = jax.ShapeDtypeStruct((b, RWKV_WIDTH), F32)
    r, w, k, v, kk, bb, g = pl.pallas_call(
        _rwkv_dec_front_kernel,
        in_specs=[vm] * (2 + len(front_params)),
        out_specs=[vm] * 7,
        out_shape=[wide] * 7,
        name="rwkv_dec_front",
    )(rw, shift0, *front_params)
    rowspec = pl.BlockSpec((None, 1, RWKV_WIDTH), lambda i: (i, 0, 0))
    stspec = pl.BlockSpec((None, RWKV_HEADS, HEAD_DIM, HEAD_DIM), lambda i: (i, 0, 0, 0))
    as_rows = lambda t: t.reshape(b, 1, RWKV_WIDTH)
    y, s_new = pl.pallas_call(
        _rwkv_dec_state_kernel,
        grid=(b,),
        in_specs=[rowspec] * 6 + [stspec],
        out_specs=[rowspec, stspec],
        out_shape=[jax.ShapeDtypeStruct((b, 1, RWKV_WIDTH), F32),
                   jax.ShapeDtypeStruct((b, RWKV_HEADS, HEAD_DIM, HEAD_DIM), F32)],
        compiler_params=_cparams(("arbitrary",)),
        name="rwkv_dec_state",
    )(as_rows(r), as_rows(w), as_rows(k), as_rows(v), as_rows(kk), as_rows(bb), wkv0)
    o = pl.pallas_call(
        _rwkv_dec_back_kernel,
        in_specs=[vm] * (5 + len(back_params)),
        out_specs=vm,
        out_shape=jax.ShapeDtypeStruct((b, RWKV_WIDTH), BF16),
        name="rwkv_dec_back",
    )(y.reshape(b, RWKV_WIDTH), r, k, v, g, *back_params)
    return o, s_new


def _split_w(w):
    hi = w.astype(BF16)
    return hi, (w - hi.astype(F32)).astype(BF16)


def _pad_ff(w, axis):
    pad = [(0, 0), (0, 0)]
    pad[axis] = (0, D_FF_PAD - D_FF)
    return jnp.pad(w.astype(BF16), pad)


def kernel(x_prompt, x_sample, cache_k, cache_v, state_wkv, state_shift, rel_bias, ffn1_norm, ffn1_w_gate, ffn1_w_up, ffn1_w_down, mix_norm, w_in, attn_sinks, shift_mu, decay_w0, decay_w2, aaa_a0, aaa_a2, gate_g2, key_k, key_a, bonus_r_k, ln_x_w, ln_x_b, w_out, ffn2_norm, ffn2_w_gate, ffn2_w_up, ffn2_w_down, final_norm):
    batch, seq, _ = x_prompt.shape
    dec_b = x_sample.shape[0]
    lw = cache_k.shape[2]
    l = 0
    row = lambda t: t.reshape(1, -1)

    f1 = (_pad_ff(ffn1_w_gate[l], 1), _pad_ff(ffn1_w_up[l], 1), _pad_ff(ffn1_w_down[l], 0))
    f2 = (_pad_ff(ffn2_w_gate[l], 1), _pad_ff(ffn2_w_up[l], 1), _pad_ff(ffn2_w_down[l], 0))
    w_in_b = w_in[l].astype(BF16)
    w_out_b = w_out[l].astype(BF16)
    lora_w = jnp.zeros((LANES, 2 * RWKV_WIDTH), F32)
    lora_w = lora_w.at[0:DECAY_LORA, 0:RWKV_WIDTH].set(decay_w2[l])
    lora_w = lora_w.at[DECAY_LORA:LANES, RWKV_WIDTH:].set(aaa_a2[l])
    lh, ll = _split_w(lora_w)
    gh, gl = _split_w(gate_g2[l])
    seg_np = (np.arange(RWKV_WIDTH)[:, None] // HEAD_DIM == np.arange(LANES)[None, :]).astype(np.float32)
    seg = jnp.asarray(seg_np, BF16)
    segt = jnp.asarray(seg_np.T, BF16)
    front_params = (row(shift_mu[l]), row(decay_w0[l]), row(aaa_a0[l]), row(key_k[l]), row(key_a[l]),
                    lh, ll, gh, gl, seg, segt)
    back_params = (row(bonus_r_k[l]), row(ln_x_w[l]), row(ln_x_b[l]), seg, segt)

    bias = _bias_table(rel_bias)
    sinks = attn_sinks[l]

    xp = x_prompt.reshape(batch * seq, D_MODEL)
    x1, h1 = _ffn(xp, row(ffn1_norm[l]), *f1, row(mix_norm[l]), tm=512, emit_x=True, n_dtype=BF16)
    q, kv, rw = _proj(h1, w_in_b, tm=256)
    o_att = _attn_prompt(q, kv, bias, sinks, batch=batch, seq=seq)
    o_rw, s_p = _rwkv_prompt(rw, front_params + back_params[:3], batch=batch, seq=seq)
    x2 = _outproj(x1, o_att, o_rw, w_out_b, tm=512)
    (y_p,) = _ffn(x2, row(ffn2_norm[l]), *f2, row(final_norm), tm=512, emit_x=False, n_dtype=F32)

    xs = x_sample.reshape(dec_b, D_MODEL)
    x1s, h1s = _ffn(xs, row(ffn1_norm[l]), *f1, row(mix_norm[l]), tm=dec_b, emit_x=True, n_dtype=BF16)
    qs, kvs, rws = _proj(h1s, w_in_b, tm=dec_b)
    sinks_b = jnp.broadcast_to(sinks[:, None], (N_Q_HEADS, LANES))
    o_att_s = _attn_decode(qs, kvs, cache_k[l], cache_v[l], bias[:, 0, :], sinks_b)
    o_rw_s, s_s = _rwkv_decode(rws, state_shift[l], state_wkv[l], front_params, back_params)
    x2s = _outproj(x1s, o_att_s.astype(BF16), o_rw_s, w_out_b, tm=dec_b)
    (y_s,) = _ffn(x2s, row(ffn2_norm[l]), *f2, row(final_norm), tm=dec_b, emit_x=False, n_dtype=F32)

    kv3 = kv.reshape(batch, seq, 2 * KV_WIDTH)
    lp = min(WINDOW, seq)
    new_k_p = kv3[:, seq - lp:, 0:KV_WIDTH].reshape(1, batch, lp, N_KV_HEADS, HEAD_DIM)
    new_v_p = kv3[:, seq - lp:, KV_WIDTH:].reshape(1, batch, lp, N_KV_HEADS, HEAD_DIM)
    new_shift_p = rw.reshape(batch, seq, SHIFT_COLS)[:, seq - 1][None]
    k_new = kvs[:, 0:KV_WIDTH].reshape(dec_b, 1, N_KV_HEADS, HEAD_DIM)
    v_new = kvs[:, KV_WIDTH:].reshape(dec_b, 1, N_KV_HEADS, HEAD_DIM)
    new_k_s = jnp.concatenate([cache_k[l], k_new], axis=1)[:, -lw:][None]
    new_v_s = jnp.concatenate([cache_v[l], v_new], axis=1)[:, -lw:][None]
    return (y_p.reshape(batch, seq, D_MODEL), y_s.reshape(dec_b, 1, D_MODEL),
            new_k_p, new_v_p, s_p[None], new_shift_p,
            new_k_s, new_v_s, s_s[None], rws[None])
```

```python
import functools
import math

import numpy as np
import jax
import jax.numpy as jnp
from jax import lax
from jax.experimental import pallas as pl
from jax.experimental.pallas import tpu as pltpu

F32 = jnp.float32
BF16 = jnp.bfloat16

D_MODEL = 2048
HEAD_DIM = 64
ATT_WIDTH = 1024
N_Q_HEADS = 16
N_KV_HEADS = 4
GQA_GROUP = 4
KV_WIDTH = 256
RWKV_WIDTH = 1024
RWKV_HEADS = 16
WINDOW = 128
BLOCK = 128
N_BUCKETS = 32
MAX_DISTANCE = 128
DECAY_LORA = 64
AAA_LORA = 64
GATE_LORA = 128
D_FF = 5504
ATT_COLS = ATT_WIDTH + 2 * KV_WIDTH
SHIFT_COLS = 3 * RWKV_WIDTH + DECAY_LORA + AAA_LORA + GATE_LORA
IN_COLS = ATT_COLS + SHIFT_COLS
RMS_EPS = 1e-5
GN_EPS = 64e-5
FFN_RES = 0.5

LANES = 128
VMEM_LIMIT_BYTES = 56 * 1024 * 1024

FF_TILE = 512
FF_STEPS = -(-D_FF // FF_TILE)
FF_TAIL = D_FF - (FF_STEPS - 1) * FF_TILE
CHUNK = 64
HEAD_PAIRS = RWKV_HEADS // 2
NEG_BIG = -1e30

_NN = (((1,), (0,)), ((), ()))
_NT = (((1,), (1,)), ((), ()))
_TN = (((0,), (0,)), ((), ()))


def _dg(a, b, dims=_NN):
    return lax.dot_general(a, b, dims, preferred_element_type=F32)


def _mm(a, b, dims=_NN):
    return _dg(a.astype(BF16), b.astype(BF16), dims)


def _split2(x):
    hi = x.astype(BF16)
    lo = (x - hi.astype(F32)).astype(BF16)
    return hi, lo


def _mm3(a, b, dims=_NN):
    ah, al = _split2(a)
    bh, bl = _split2(b)
    return _dg(ah, bh, dims) + _dg(ah, bl, dims) + _dg(al, bh, dims)


def _mm2(a, b_bf16, dims=_NN):
    ah, al = _split2(a)
    return _dg(ah, b_bf16, dims) + _dg(al, b_bf16, dims)


def _mm2r(a_bf16, b, dims=_NN):
    bh, bl = _split2(b)
    return _dg(a_bf16, bh, dims) + _dg(a_bf16, bl, dims)


def _mm_w(a, w_hi, w_lo):
    ah, al = _split2(a)
    return _dg(ah, w_hi) + _dg(ah, w_lo) + _dg(al, w_hi)


def _sigmoid(x):
    return 1.0 / (1.0 + jnp.exp(-x))


def _softplus(x):
    return jnp.maximum(x, 0.0) + jnp.log(1.0 + jnp.exp(-jnp.abs(x)))


def _rms(x, g):
    ms = jnp.mean(x * x, axis=-1, keepdims=True)
    return x * lax.rsqrt(ms + RMS_EPS) * g


def _cparams(sem):
    return pltpu.CompilerParams(dimension_semantics=sem, vmem_limit_bytes=VMEM_LIMIT_BYTES)


def _ffn_kernel(x_ref, g_ref, wg_ref, wu_ref, wd_ref, g2_ref, *rest, emit_x, n_dtype):
    if emit_x:
        ox_ref, on_ref, xn_scr, acc_scr = rest
    else:
        on_ref, xn_scr, acc_scr = rest
    j = pl.program_id(1)

    @pl.when(j == 0)
    def _():
        xn_scr[...] = _rms(x_ref[...], g_ref[...]).astype(BF16)
        acc_scr[...] = jnp.zeros_like(acc_scr)

    last = j == pl.num_programs(1) - 1
    xn = xn_scr[...]
    gate = jnp.dot(xn, wg_ref[...], preferred_element_type=F32)
    up = jnp.dot(xn, wu_ref[...], preferred_element_type=F32)
    h = gate * _sigmoid(gate) * up
    h = jnp.concatenate([h[:, :FF_TAIL], jnp.where(last, 0.0, h[:, FF_TAIL:])], axis=1).astype(BF16)
    wd = jnp.concatenate([wd_ref[0:FF_TAIL, :],
                          jnp.where(last, jnp.zeros((), BF16), wd_ref[FF_TAIL:FF_TILE, :])], axis=0)
    acc_scr[...] += jnp.dot(h, wd, preferred_element_type=F32)

    @pl.when(last)
    def _():
        y = x_ref[...] + FFN_RES * acc_scr[...]
        if emit_x:
            ox_ref[...] = y
        on_ref[...] = _rms(y, g2_ref[...]).astype(n_dtype)


def _ffn(x, g, wg, wu, wd, g2, *, tm, emit_x, n_dtype):
    m = x.shape[0]
    grid = (m // tm, FF_STEPS)
    row = pl.BlockSpec((tm, D_MODEL), lambda i, j: (i, 0))
    vec = pl.BlockSpec((1, D_MODEL), lambda i, j: (0, 0))
    out_shape = [jax.ShapeDtypeStruct((m, D_MODEL), n_dtype)]
    out_specs = [row]
    if emit_x:
        out_shape = [jax.ShapeDtypeStruct((m, D_MODEL), F32)] + out_shape
        out_specs = [row, row]
    return pl.pallas_call(
        functools.partial(_ffn_kernel, emit_x=emit_x, n_dtype=n_dtype),
        grid=grid,
        in_specs=[row, vec,
                  pl.BlockSpec((D_MODEL, FF_TILE), lambda i, j: (0, j)),
                  pl.BlockSpec((D_MODEL, FF_TILE), lambda i, j: (0, j)),
                  pl.BlockSpec((FF_TILE, D_MODEL), lambda i, j: (j, 0)),
                  vec],
        out_specs=out_specs,
        out_shape=out_shape,
        scratch_shapes=[pltpu.VMEM((tm, D_MODEL), BF16), pltpu.VMEM((tm, D_MODEL), F32)],
        compiler_params=_cparams(("arbitrary", "arbitrary")),
        name="ffn",
    )(x, g, wg, wu, wd, g2)


def _proj_kernel(h_ref, w_ref, q_ref, kv_ref, rw_ref):
    h = h_ref[...]
    q = jnp.dot(h, w_ref[:, 0:ATT_WIDTH], preferred_element_type=F32)
    q_ref[...] = (q * (HEAD_DIM ** -0.5)).astype(BF16)
    kv_ref[...] = jnp.dot(h, w_ref[:, ATT_WIDTH:ATT_COLS], preferred_element_type=F32)
    rw_ref[...] = jnp.dot(h, w_ref[:, ATT_COLS:IN_COLS], preferred_element_type=F32)


def _proj(h, w, *, tm):
    m = h.shape[0]
    return pl.pallas_call(
        _proj_kernel,
        grid=(m // tm,),
        in_specs=[pl.BlockSpec((tm, D_MODEL), lambda i: (i, 0)),
                  pl.BlockSpec((D_MODEL, IN_COLS), lambda i: (0, 0), pipeline_mode=pl.Buffered(1))],
        out_specs=[pl.BlockSpec((tm, ATT_WIDTH), lambda i: (i, 0)),
                   pl.BlockSpec((tm, 2 * KV_WIDTH), lambda i: (i, 0)),
                   pl.BlockSpec((tm, SHIFT_COLS), lambda i: (i, 0))],
        out_shape=[jax.ShapeDtypeStruct((m, ATT_WIDTH), BF16),
                   jax.ShapeDtypeStruct((m, 2 * KV_WIDTH), F32),
                   jax.ShapeDtypeStruct((m, SHIFT_COLS), F32)],
        compiler_params=_cparams(("arbitrary",)),
        name="in_proj",
    )(h, w)


def _outproj_kernel(x_ref, oa_ref, orw_ref, w_ref, o_ref):
    acc = jnp.dot(oa_ref[...], w_ref[0:ATT_WIDTH, :], preferred_element_type=F32)
    acc += jnp.dot(orw_ref[...], w_ref[ATT_WIDTH:D_MODEL, :], preferred_element_type=F32)
    o_ref[...] = x_ref[...] + acc


def _outproj(x, oa, orw, w, *, tm):
    m = x.shape[0]
    row = pl.BlockSpec((tm, D_MODEL), lambda i: (i, 0))
    half = pl.BlockSpec((tm, ATT_WIDTH), lambda i: (i, 0))
    return pl.pallas_call(
        _outproj_kernel,
        grid=(m // tm,),
        in_specs=[row, half, half,
                  pl.BlockSpec((D_MODEL, D_MODEL), lambda i: (0, 0), pipeline_mode=pl.Buffered(1))],
        out_specs=row,
        out_shape=jax.ShapeDtypeStruct((m, D_MODEL), F32),
        compiler_params=_cparams(("arbitrary",)),
        name="out_proj",
    )(x, oa, orw, w)


def _bucket_table():
    qi = np.arange(BLOCK)[:, None]
    kj = np.arange(2 * BLOCK)[None, :]
    dist = BLOCK + qi - kj
    n = np.maximum(dist, 0)
    max_exact = N_BUCKETS // 2
    nf = np.maximum(n, 1).astype(np.float32)
    large = max_exact + (np.log(nf / np.float32(max_exact)) / np.float32(math.log(MAX_DISTANCE / max_exact))
                         * np.float32(N_BUCKETS - max_exact)).astype(np.int32)
    large = np.minimum(large, N_BUCKETS - 1)
    bucket = np.where(n < max_exact, n, large).astype(np.int32)
    valid = (dist >= 0) & (dist <= WINDOW)
    return np.where(valid, bucket, -1).astype(np.int32)


def _bias_kernel(bucket_ref, rb_ref, o_ref):
    bucket = bucket_ref[...]
    for h in range(N_Q_HEADS):
        acc = jnp.full((BLOCK, 2 * BLOCK), NEG_BIG, F32)
        for n in range(N_BUCKETS):
            acc = jnp.where(bucket == n, rb_ref[n, h], acc)
        o_ref[h] = acc


def _bias_table(rel_bias):
    bucket = jnp.asarray(_bucket_table())
    return pl.pallas_call(
        _bias_kernel,
        in_specs=[pl.BlockSpec(memory_space=pltpu.VMEM), pl.BlockSpec(memory_space=pltpu.SMEM)],
        out_specs=pl.BlockSpec(memory_space=pltpu.VMEM),
        out_shape=jax.ShapeDtypeStruct((N_Q_HEADS, BLOCK, 2 * BLOCK), F32),
        name="bias_table",
    )(bucket, rel_bias)


def _attn_kernel(sink_ref, q_ref, kvp_ref, kvc_ref, bias_ref, o_ref):
    j = pl.program_id(1)
    kj = lax.broadcasted_iota(jnp.int32, (BLOCK, 2 * BLOCK), 1)
    blk_ok = jnp.logical_or(kj >= BLOCK, j > 0)
    kvp = kvp_ref[...]
    kvc = kvc_ref[...]
    for g in range(N_KV_HEADS):
        ks = slice(g * HEAD_DIM, (g + 1) * HEAD_DIM)
        vs = slice(KV_WIDTH + g * HEAD_DIM, KV_WIDTH + (g + 1) * HEAD_DIM)
        kcat = jnp.concatenate([kvp[:, ks], kvc[:, ks]], axis=0).astype(BF16)
        vcat = jnp.concatenate([kvp[:, vs], kvc[:, vs]], axis=0).astype(BF16)
        for u in range(GQA_GROUP):
            h = g * GQA_GROUP + u
            hs = slice(h * HEAD_DIM, (h + 1) * HEAD_DIM)
            s = _dg(q_ref[:, hs], kcat, _NT)
            s = jnp.where(blk_ok, s + bias_ref[h], NEG_BIG)
            sink = sink_ref[h]
            m = jnp.maximum(jnp.max(s, axis=-1, keepdims=True), sink)
            p = jnp.exp(s - m)
            denom = jnp.sum(p, axis=-1, keepdims=True) + jnp.exp(sink - m)
            o = jnp.dot(p.astype(BF16), vcat, preferred_element_type=F32)
            o_ref[:, hs] = (o / denom).astype(BF16)


def _attn_prompt(q, kv, bias, sinks, *, batch, seq):
    nb = seq // BLOCK
    q3 = q.reshape(batch, seq, ATT_WIDTH)
    kv3 = kv.reshape(batch, seq, 2 * KV_WIDTH)
    out = pl.pallas_call(
        _attn_kernel,
        grid=(batch, nb),
        in_specs=[pl.BlockSpec(memory_space=pltpu.SMEM),
                  pl.BlockSpec((None, BLOCK, ATT_WIDTH), lambda b, j: (b, j, 0)),
                  pl.BlockSpec((None, BLOCK, 2 * KV_WIDTH), lambda b, j: (b, jnp.maximum(j - 1, 0), 0)),
                  pl.BlockSpec((None, BLOCK, 2 * KV_WIDTH), lambda b, j: (b, j, 0)),
                  pl.BlockSpec((N_Q_HEADS, BLOCK, 2 * BLOCK), lambda b, j: (0, 0, 0))],
        out_specs=pl.BlockSpec((None, BLOCK, ATT_WIDTH), lambda b, j: (b, j, 0)),
        out_shape=jax.ShapeDtypeStruct((batch, seq, ATT_WIDTH), BF16),
        compiler_params=_cparams(("arbitrary", "arbitrary")),
        name="attn_prompt",
    )(sinks, q3, kv3, kv3, bias)
    return out.reshape(batch * seq, ATT_WIDTH)


def _attn_decode_kernel(sink_ref, q_ref, kvn_ref, ck_ref, cv_ref, bias_ref, o_ref):
    q = q_ref[...].astype(F32)
    kvn = kvn_ref[...]
    for g in range(N_KV_HEADS):
        ks = slice(g * HEAD_DIM, (g + 1) * HEAD_DIM)
        vs = slice(KV_WIDTH + g * HEAD_DIM, KV_WIDTH + (g + 1) * HEAD_DIM)
        hs = slice(g * GQA_GROUP, (g + 1) * GQA_GROUP)
        qg = q[:, hs, :]
        kc = ck_ref[:, :, ks]
        vc = cv_ref[:, :, ks]
        kn = kvn[:, ks].astype(BF16).astype(F32)[:, None, :]
        vn = kvn[:, vs].astype(BF16).astype(F32)[:, None, :]
        s = jnp.einsum("bqd,bkd->bqk", qg.astype(BF16), kc.astype(BF16), preferred_element_type=F32)
        s = s + bias_ref[hs, 0:WINDOW][None]
        s_new = jnp.sum(qg * kn, axis=-1, keepdims=True) + bias_ref[hs, WINDOW:WINDOW + 1][None]
        sink = sink_ref[hs, 0:1][None]
        m = jnp.maximum(jnp.maximum(jnp.max(s, axis=-1, keepdims=True), s_new), sink)
        p = jnp.exp(s - m)
        p_new = jnp.exp(s_new - m)
        denom = jnp.sum(p, axis=-1, keepdims=True) + p_new + jnp.exp(sink - m)
        o = jnp.einsum("bqk,bkd->bqd", p.astype(BF16), vc.astype(BF16), preferred_element_type=F32)
        o = o + p_new * vn
        o_ref[:, hs, :] = o / denom


def _attn_decode(q, kvn, cache_k, cache_v, bias_row, sinks_b):
    b = q.shape[0]
    lw = cache_k.shape[1]
    vm = pl.BlockSpec(memory_space=pltpu.VMEM)
    out = pl.pallas_call(
        _attn_decode_kernel,
        in_specs=[vm, vm, vm, vm, vm, vm],
        out_specs=vm,
        out_shape=jax.ShapeDtypeStruct((b, N_Q_HEADS, HEAD_DIM), F32),
        compiler_params=pltpu.CompilerParams(vmem_limit_bytes=VMEM_LIMIT_BYTES),
        name="attn_decode",
    )(sinks_b, q.reshape(b, N_Q_HEADS, HEAD_DIM), kvn,
      cache_k.reshape(b, lw, KV_WIDTH), cache_v.reshape(b, lw, KV_WIDTH), bias_row)
    return out.reshape(b, ATT_WIDTH)


def _seg_expand(x, seg_ref, segt_ref):
    s = _mm2(x, seg_ref[...])
    return _mm2(s, segt_ref[...])


def _rwkv_front(x, prev, p):
    (mu_ref, w0_ref, a0_ref, kk_ref, ka_ref, lh_ref, ll_ref, gh_ref, gl_ref, seg_ref, segt_ref) = p
    xm = x + mu_ref[...] * (prev - x)
    o3 = 3 * RWKV_WIDTH
    r = xm[:, 0:RWKV_WIDTH]
    k = xm[:, RWKV_WIDTH:2 * RWKV_WIDTH]
    v = xm[:, 2 * RWKV_WIDTH:o3]
    wa = xm[:, o3:o3 + LANES]
    lane = lax.broadcasted_iota(jnp.int32, wa.shape, 1)
    z = jnp.where(lane < DECAY_LORA, jnp.tanh(wa), wa)
    lora = _mm_w(z, lh_ref[...], ll_ref[...])
    w = -_softplus(-(w0_ref[...] + lora[:, 0:RWKV_WIDTH])) - 0.5
    logw = -jnp.exp(w)
    a = _sigmoid(a0_ref[...] + lora[:, RWKV_WIDTH:2 * RWKV_WIDTH])
    g = _mm_w(_sigmoid(xm[:, o3 + LANES:SHIFT_COLS]), gh_ref[...], gl_ref[...])
    kk = k * kk_ref[...]
    kmod = k * (1.0 + (a - 1.0) * ka_ref[...])
    nrm = jnp.sqrt(_seg_expand(kk * kk, seg_ref, segt_ref))
    kk = kk / jnp.maximum(nrm, 1e-12)
    return r, logw, kmod, v, kk, kk * a, g


def _rwkv_back(y, r, kmod, v, g, q):
    rk_ref, lnw_ref, lnb_ref, seg_ref, segt_ref = q
    mean = _seg_expand(y, seg_ref, segt_ref) * (1.0 / HEAD_DIM)
    d = y - mean
    var = _seg_expand(d * d, seg_ref, segt_ref) * (1.0 / HEAD_DIM)
    yn = d * lax.rsqrt(var + GN_EPS) * lnw_ref[...] + lnb_ref[...]
    bonus = _seg_expand(r * kmod * rk_ref[...], seg_ref, segt_ref) * v
    return (yn + bonus) * g


def _rwkv_chunk_kernel(rw_ref, mu_ref, w0_ref, a0_ref, kk_ref, ka_ref, lh_ref, ll_ref, gh_ref, gl_ref,
                       seg_ref, segt_ref, rk_ref, lnw_ref, lnb_ref,
                       o_ref, s_out_ref,
                       s_scr, prev_scr, y_scr, ea_scr, er_scr, eb_scr, ek_scr, be_scr, ke_scr, v_scr):
    c = pl.program_id(1)
    C = CHUNK

    @pl.when(c == 0)
    def _():
        s_scr[...] = jnp.zeros_like(s_scr)
        prev_scr[...] = jnp.zeros_like(prev_scr)

    x = rw_ref[...]
    row = lax.broadcasted_iota(jnp.int32, x.shape, 0)
    prev = jnp.where(row == 0, prev_scr[...], pltpu.roll(x, 1, axis=0))
    prev_scr[...] = x[C - 1:C, :]
    front = (mu_ref, w0_ref, a0_ref, kk_ref, ka_ref, lh_ref, ll_ref, gh_ref, gl_ref, seg_ref, segt_ref)
    r, logw, kmod, v, kk, bb, g = _rwkv_front(x, prev, front)

    ti = lax.broadcasted_iota(jnp.int32, (C, C), 0)
    si = lax.broadcasted_iota(jnp.int32, (C, C), 1)
    cum = _mm2r(jnp.where(si <= ti, 1.0, 0.0).astype(BF16), logw)
    cum_end = cum[C - 1:C, :]
    e_in = jnp.exp(cum)
    e_neg = jnp.exp(-cum)
    e_end = jnp.exp(cum_end - cum)
    ea_scr[...] = kk * jnp.exp(cum - logw)
    er_scr[...] = r * e_in
    eb_scr[...] = bb * e_neg
    ek_scr[...] = kmod * e_neg
    be_scr[...] = bb * e_end
    ke_scr[...] = kmod * e_end
    v_scr[...] = v
    w_end = jnp.exp(cum_end)

    lane = lax.broadcasted_iota(jnp.int32, (C, LANES), 1)
    lo_half = lane < HEAD_DIM
    ri = lax.broadcasted_iota(jnp.int32, (LANES, LANES), 0)
    ci = lax.broadcasted_iota(jnp.int32, (LANES, LANES), 1)
    same_head = jnp.right_shift(ri, 6) == jnp.right_shift(ci, 6)
    eye = ri == ci
    t_loc = jnp.bitwise_and(ri, C - 1)
    s_loc = jnp.bitwise_and(ci, C - 1)
    bd_strict = jnp.logical_and(same_head, s_loc < t_loc)
    bd_incl = jnp.logical_and(same_head, s_loc <= t_loc)
    zero_blk = jnp.zeros((2 * C, LANES), F32)

    def stack2(t):
        return jnp.concatenate([jnp.where(lo_half, t, 0.0), jnp.where(lo_half, 0.0, t)], axis=0)

    def fold2(t):
        return t[0:C] + t[C:2 * C]

    pairs = range(HEAD_PAIRS)
    sl = [slice(p * LANES, (p + 1) * LANES) for p in pairs]
    v2, x1, x2, pw, lrbk = [], [], [], [], []
    for p in pairs:
        a2 = stack2(ea_scr[:, sl[p]])
        r2 = stack2(er_scr[:, sl[p]])
        b_p = eb_scr[:, sl[p]]
        k_p = ek_scr[:, sl[p]]
        m = _mm(jnp.concatenate([a2, r2], axis=0), jnp.concatenate([b_p, b_p, k_p, k_p], axis=0), _NT)
        l_ab = jnp.where(bd_strict, m[0:2 * C, 0:LANES], 0.0)
        l_ak = jnp.where(bd_strict, m[0:2 * C, LANES:2 * LANES], 0.0)
        l_rb = jnp.where(bd_incl, m[2 * C:4 * C, 0:LANES], 0.0)
        l_rk = jnp.where(bd_incl, m[2 * C:4 * C, LANES:2 * LANES], 0.0)
        v2.append(stack2(v_scr[:, sl[p]]))
        x1.append(a2)
        pw.append(-l_ab)
        lrbk.append(jnp.concatenate([l_rb, l_rk], axis=1))
        x2.append(l_ak)
    for p in pairs:
        x2[p] = -_mm(x2[p], v2[p])
    for it in range(6):
        for p in pairs:
            if it < 5:
                px = _mm(pw[p], jnp.concatenate([x1[p], x2[p], pw[p]], axis=1))
                pw[p] = px[:, 2 * LANES:3 * LANES]
            else:
                px = _mm(pw[p], jnp.concatenate([x1[p], x2[p]], axis=1))
            x1[p] = x1[p] + px[:, 0:LANES]
            x2[p] = x2[p] + px[:, LANES:2 * LANES]
    at, ul, rt, yl = [], [], [], []
    for p in pairs:
        rhs = jnp.concatenate([jnp.concatenate([x1[p], x2[p]], axis=1),
                               jnp.concatenate([zero_blk, v2[p]], axis=1)], axis=0)
        lb = fold2(_mm(lrbk[p], rhs))
        at.append(fold2(x1[p]))
        ul.append(fold2(x2[p]))
        rt.append(er_scr[:, sl[p]] - lb[:, 0:LANES])
        yl.append(lb[:, LANES:2 * LANES])
    for p in pairs:
        be_p = be_scr[:, sl[p]]
        d_p = jnp.where(eye, w_end[:, sl[p]], 0.0) - jnp.where(same_head, _mm(at[p], be_p, _TN), 0.0)
        g_p = jnp.where(same_head, _mm(ul[p], be_p, _TN) + _mm(v_scr[:, sl[p]], ke_scr[:, sl[p]], _TN), 0.0)
        s0 = s_scr[p]
        y_scr[:, sl[p]] = yl[p] + _mm3(rt[p], s0, _NT)
        s_scr[p] = _mm3(s0, d_p) + g_p

    back = (rk_ref, lnw_ref, lnb_ref, seg_ref, segt_ref)
    o_ref[...] = _rwkv_back(y_scr[...], r, kmod, v, g, back).astype(BF16)

    @pl.when(c == pl.num_programs(1) - 1)
    def _():
        for p in range(HEAD_PAIRS):
            s_out_ref[2 * p] = s_scr[p, 0:HEAD_DIM, 0:HEAD_DIM]
            s_out_ref[2 * p + 1] = s_scr[p, HEAD_DIM:LANES, HEAD_DIM:LANES]


def _rwkv_prompt(rw, params, *, batch, seq):
    rw3 = rw.reshape(batch, seq, SHIFT_COLS)
    nc = seq // CHUNK
    const = lambda shape: pl.BlockSpec(shape, lambda b, c: tuple(0 for _ in shape))
    in_specs = [pl.BlockSpec((None, CHUNK, SHIFT_COLS), lambda b, c: (b, c, 0))]
    in_specs += [const(p.shape) for p in params]
    wide = pltpu.VMEM((CHUNK, RWKV_WIDTH), F32)
    o, s_out = pl.pallas_call(
        _rwkv_chunk_kernel,
        grid=(batch, nc),
        in_specs=in_specs,
        out_specs=[pl.BlockSpec((None, CHUNK, RWKV_WIDTH), lambda b, c: (b, c, 0)),
                   pl.BlockSpec((None, RWKV_HEADS, HEAD_DIM, HEAD_DIM), lambda b, c: (b, 0, 0, 0))],
        out_shape=[jax.ShapeDtypeStruct((batch, seq, RWKV_WIDTH), BF16),
                   jax.ShapeDtypeStruct((batch, RWKV_HEADS, HEAD_DIM, HEAD_DIM), F32)],
        scratch_shapes=[pltpu.VMEM((HEAD_PAIRS, LANES, LANES), F32),
                        pltpu.VMEM((1, SHIFT_COLS), F32),
                        wide, wide, wide, wide, wide, wide, wide, wide],
        compiler_params=_cparams(("arbitrary", "arbitrary")),
        name="rwkv_prompt",
    )(rw3, *params)
    return o.reshape(batch * seq, RWKV_WIDTH), s_out


def _rwkv_dec_front_kernel(rw_ref, prev_ref, mu_ref, w0_ref, a0_ref, kk_ref, ka_ref, lh_ref, ll_ref,
                           gh_ref, gl_ref, seg_ref, segt_ref,
                           r_ref, w_ref, k_ref, v_ref, kkn_ref, b_ref, g_ref):
    front = (mu_ref, w0_ref, a0_ref, kk_ref, ka_ref, lh_ref, ll_ref, gh_ref, gl_ref, seg_ref, segt_ref)
    r, logw, kmod, v, kk, bb, g = _rwkv_front(rw_ref[...], prev_ref[...], front)
    r_ref[...] = r
    w_ref[...] = jnp.exp(logw)
    k_ref[...] = kmod
    v_ref[...] = v
    kkn_ref[...] = kk
    b_ref[...] = bb
    g_ref[...] = g


def _rwkv_dec_state_kernel(r_ref, w_ref, k_ref, v_ref, kk_ref, b_ref, s_ref, y_ref, so_ref):
    ri = lax.broadcasted_iota(jnp.int32, (HEAD_DIM, HEAD_DIM), 0)
    ci = lax.broadcasted_iota(jnp.int32, (HEAD_DIM, HEAD_DIM), 1)
    eye = ri == ci
    heads = range(RWKV_HEADS)
    hs = [slice(h * HEAD_DIM, (h + 1) * HEAD_DIM) for h in heads]
    s = [s_ref[h] for h in heads]
    sa = [-jnp.sum(s[h] * kk_ref[:, hs[h]], axis=1, keepdims=True) for h in heads]
    v_col = [jnp.sum(jnp.where(eye, v_ref[:, hs[h]], 0.0), axis=1, keepdims=True) for h in heads]
    s_new = [s[h] * w_ref[:, hs[h]] + sa[h] * b_ref[:, hs[h]] + v_col[h] * k_ref[:, hs[h]] for h in heads]
    for h in heads:
        so_ref[h] = s_new[h]
    y_col = [jnp.sum(s_new[h] * r_ref[:, hs[h]], axis=1, keepdims=True) for h in heads]
    for h in heads:
        y_ref[:, hs[h]] = jnp.sum(jnp.where(eye, y_col[h], 0.0), axis=0, keepdims=True)


def _rwkv_dec_back_kernel(y_ref, r_ref, k_ref, v_ref, g_ref, rk_ref, lnw_ref, lnb_ref, seg_ref, segt_ref, o_ref):
    back = (rk_ref, lnw_ref, lnb_ref, seg_ref, segt_ref)
    o_ref[...] = _rwkv_back(y_ref[...], r_ref[...], k_ref[...], v_ref[...], g_ref[...], back).astype(BF16)


def _rwkv_decode(rw, shift0, wkv0, front_params, back_params):
    b = rw.shape[0]
    vm = pl.BlockSpec(memory_space=pltpu.VMEM)
    wide = jax.ShapeDtypeStruct((b, RWKV_WIDTH), F32)
    r, w, k, v, kk, bb, g = pl.pallas_call(
        _rwkv_dec_front_kernel,
        in_specs=[vm] * (2 + len(front_params)),
        out_specs=[vm] * 7,
        out_shape=[wide] * 7,
        name="rwkv_dec_front",
    )(rw, shift0, *front_params)
    rowspec = pl.BlockSpec((None, 1, RWKV_WIDTH), lambda i: (i, 0, 0))
    stspec = pl.BlockSpec((None, RWKV_HEADS, HEAD_DIM, HEAD_DIM), lambda i: (i, 0, 0, 0))
    as_rows = lambda t: t.reshape(b, 1, RWKV_WIDTH)
    y, s_new = pl.pallas_call(
        _rwkv_dec_state_kernel,
        grid=(b,),
        in_specs=[rowspec] * 6 + [stspec],
        out_specs=[rowspec, stspec],
        out_shape=[jax.ShapeDtypeStruct((b, 1, RWKV_WIDTH), F32),
                   jax.ShapeDtypeStruct((b, RWKV_HEADS, HEAD_DIM, HEAD_DIM), F32)],
        compiler_params=_cparams(("arbitrary",)),
        name="rwkv_dec_state",
    )(as_rows(r), as_rows(w), as_rows(k), as_rows(v), as_rows(kk), as_rows(bb), wkv0)
    o = pl.pallas_call(
        _rwkv_dec_back_kernel,
        in_specs=[vm] * (5 + len(back_params)),
        out_specs=vm,
        out_shape=jax.ShapeDtypeStruct((b, RWKV_WIDTH), BF16),
        name="rwkv_dec_back",
    )(y.reshape(b, RWKV_WIDTH), r, k, v, g, *back_params)
    return o, s_new


def _split_w(w):
    hi = w.astype(BF16)
    return hi, (w - hi.astype(F32)).astype(BF16)


def kernel(x_prompt, x_sample, cache_k, cache_v, state_wkv, state_shift, rel_bias, ffn1_norm, ffn1_w_gate, ffn1_w_up, ffn1_w_down, mix_norm, w_in, attn_sinks, shift_mu, decay_w0, decay_w2, aaa_a0, aaa_a2, gate_g2, key_k, key_a, bonus_r_k, ln_x_w, ln_x_b, w_out, ffn2_norm, ffn2_w_gate, ffn2_w_up, ffn2_w_down, final_norm):
    batch, seq, _ = x_prompt.shape
    dec_b = x_sample.shape[0]
    lw = cache_k.shape[2]
    l = 0
    row = lambda t: t.reshape(1, -1)

    f1 = (ffn1_w_gate[l].astype(BF16), ffn1_w_up[l].astype(BF16), ffn1_w_down[l].astype(BF16))
    f2 = (ffn2_w_gate[l].astype(BF16), ffn2_w_up[l].astype(BF16), ffn2_w_down[l].astype(BF16))
    w_in_b = w_in[l].astype(BF16)
    w_out_b = w_out[l].astype(BF16)
    lora_w = jnp.zeros((LANES, 2 * RWKV_WIDTH), F32)
    lora_w = lora_w.at[0:DECAY_LORA, 0:RWKV_WIDTH].set(decay_w2[l])
    lora_w = lora_w.at[DECAY_LORA:LANES, RWKV_WIDTH:].set(aaa_a2[l])
    lh, ll = _split_w(lora_w)
    gh, gl = _split_w(gate_g2[l])
    seg_np = (np.arange(RWKV_WIDTH)[:, None] // HEAD_DIM == np.arange(LANES)[None, :]).astype(np.float32)
    seg = jnp.asarray(seg_np, BF16)
    segt = jnp.asarray(seg_np.T, BF16)
    front_params = (row(shift_mu[l]), row(decay_w0[l]), row(aaa_a0[l]), row(key_k[l]), row(key_a[l]),
                    lh, ll, gh, gl, seg, segt)
    back_params = (row(bonus_r_k[l]), row(ln_x_w[l]), row(ln_x_b[l]), seg, segt)

    bias = _bias_table(rel_bias)
    sinks = attn_sinks[l]

    xp = x_prompt.reshape(batch * seq, D_MODEL)
    x1, h1 = _ffn(xp, row(ffn1_norm[l]), *f1, row(mix_norm[l]), tm=512, emit_x=True, n_dtype=BF16)
    q, kv, rw = _proj(h1, w_in_b, tm=256)
    o_att = _attn_prompt(q, kv, bias, sinks, batch=batch, seq=seq)
    o_rw, s_p = _rwkv_prompt(rw, front_params + back_params[:3], batch=batch, seq=seq)
    x2 = _outproj(x1, o_att, o_rw, w_out_b, tm=512)
    (y_p,) = _ffn(x2, row(ffn2_norm[l]), *f2, row(final_norm), tm=512, emit_x=False, n_dtype=F32)

    xs = x_sample.reshape(dec_b, D_MODEL)
    x1s, h1s = _ffn(xs, row(ffn1_norm[l]), *f1, row(mix_norm[l]), tm=dec_b, emit_x=True, n_dtype=BF16)
    qs, kvs, rws = _proj(h1s, w_in_b, tm=dec_b)
    sinks_b = jnp.broadcast_to(sinks[:, None], (N_Q_HEADS, LANES))
    o_att_s = _attn_decode(qs, kvs, cache_k[l], cache_v[l], bias[:, 0, :], sinks_b)
    o_rw_s, s_s = _rwkv_decode(rws, state_shift[l], state_wkv[l], front_params, back_params)
    x2s = _outproj(x1s, o_att_s.astype(BF16), o_rw_s, w_out_b, tm=dec_b)
    (y_s,) = _ffn(x2s, row(ffn2_norm[l]), *f2, row(final_norm), tm=dec_b, emit_x=False, n_dtype=F32)

    kv3 = kv.reshape(batch, seq, 2 * KV_WIDTH)
    lp = min(WINDOW, seq)
    new_k_p = kv3[:, seq - lp:, 0:KV_WIDTH].reshape(1, batch, lp, N_KV_HEADS, HEAD_DIM)
    new_v_p = kv3[:, seq - lp:, KV_WIDTH:].reshape(1, batch, lp, N_KV_HEADS, HEAD_DIM)
    new_shift_p = rw.reshape(batch, seq, SHIFT_COLS)[:, seq - 1][None]
    k_new = kvs[:, 0:KV_WIDTH].reshape(dec_b, 1, N_KV_HEADS, HEAD_DIM)
    v_new = kvs[:, KV_WIDTH:].reshape(dec_b, 1, N_KV_HEADS, HEAD_DIM)
    new_k_s = jnp.concatenate([cache_k[l], k_new], axis=1)[:, -lw:][None]
    new_v_s = jnp.concatenate([cache_v[l], v_new], axis=1)[:, -lw:][None]
    return (y_p.reshape(batch, seq, D_MODEL), y_s.reshape(dec_b, 1, D_MODEL),
            new_k_p, new_v_p, s_p[None], new_shift_p,
            new_k_s, new_v_s, s_s[None], rws[None])
```

```python
import functools
import math

import numpy as np
import jax
import jax.numpy as jnp
from jax import lax
from jax.experimental import pallas as pl
from jax.experimental.pallas import tpu as pltpu

F32 = jnp.float32
BF16 = jnp.bfloat16

D_MODEL = 2048
HEAD_DIM = 64
ATT_WIDTH = 1024
N_Q_HEADS = 16
N_KV_HEADS = 4
GQA_GROUP = 4
KV_WIDTH = 256
RWKV_WIDTH = 1024
RWKV_HEADS = 16
WINDOW = 128
BLOCK = 128
N_BUCKETS = 32
MAX_DISTANCE = 128
DECAY_LORA = 64
AAA_LORA = 64
GATE_LORA = 128
D_FF = 5504
ATT_COLS = ATT_WIDTH + 2 * KV_WIDTH
SHIFT_COLS = 3 * RWKV_WIDTH + DECAY_LORA + AAA_LORA + GATE_LORA
IN_COLS = ATT_COLS + SHIFT_COLS
RMS_EPS = 1e-5
GN_EPS = 64e-5
FFN_RES = 0.5

LANES = 128
VMEM_LIMIT_BYTES = 56 * 1024 * 1024

FF_TILE = 512
FF_STEPS = -(-D_FF // FF_TILE)
FF_TAIL = D_FF - (FF_STEPS - 1) * FF_TILE
CHUNK = 64
HEAD_PAIRS = RWKV_HEADS // 2
NEG_BIG = -1e30

_NN = (((1,), (0,)), ((), ()))
_NT = (((1,), (1,)), ((), ()))
_TN = (((0,), (0,)), ((), ()))


def _dg(a, b, dims=_NN):
    return lax.dot_general(a, b, dims, preferred_element_type=F32)


def _mm(a, b, dims=_NN):
    return _dg(a.astype(BF16), b.astype(BF16), dims)


def _split2(x):
    hi = x.astype(BF16)
    lo = (x - hi.astype(F32)).astype(BF16)
    return hi, lo


def _mm3(a, b, dims=_NN):
    ah, al = _split2(a)
    bh, bl = _split2(b)
    return _dg(ah, bh, dims) + _dg(ah, bl, dims) + _dg(al, bh, dims)


def _mm2(a, b_bf16, dims=_NN):
    ah, al = _split2(a)
    return _dg(ah, b_bf16, dims) + _dg(al, b_bf16, dims)


def _mm2r(a_bf16, b, dims=_NN):
    bh, bl = _split2(b)
    return _dg(a_bf16, bh, dims) + _dg(a_bf16, bl, dims)


def _sigmoid(x):
    return 1.0 / (1.0 + jnp.exp(-x))


def _softplus(x):
    return jnp.maximum(x, 0.0) + jnp.log(1.0 + jnp.exp(-jnp.abs(x)))


def _rms(x, g):
    ms = jnp.mean(x * x, axis=-1, keepdims=True)
    return x * lax.rsqrt(ms + RMS_EPS) * g


def _cparams(sem):
    return pltpu.CompilerParams(dimension_semantics=sem, vmem_limit_bytes=VMEM_LIMIT_BYTES)


def _ffn_kernel(x_ref, g_ref, wg_ref, wu_ref, wd_ref, g2_ref, *rest, emit_x, n_dtype):
    if emit_x:
        ox_ref, on_ref, xn_scr, acc_scr = rest
    else:
        on_ref, xn_scr, acc_scr = rest
    j = pl.program_id(1)

    @pl.when(j == 0)
    def _():
        xn_scr[...] = _rms(x_ref[...], g_ref[...]).astype(BF16)
        acc_scr[...] = jnp.zeros_like(acc_scr)

    last = j == pl.num_programs(1) - 1
    xn = xn_scr[...]
    gate = jnp.dot(xn, wg_ref[...], preferred_element_type=F32)
    up = jnp.dot(xn, wu_ref[...], preferred_element_type=F32)
    h = gate * _sigmoid(gate) * up
    h = jnp.concatenate([h[:, :FF_TAIL], jnp.where(last, 0.0, h[:, FF_TAIL:])], axis=1).astype(BF16)
    wd = jnp.concatenate([wd_ref[0:FF_TAIL, :],
                          jnp.where(last, jnp.zeros((), BF16), wd_ref[FF_TAIL:FF_TILE, :])], axis=0)
    acc_scr[...] += jnp.dot(h, wd, preferred_element_type=F32)

    @pl.when(last)
    def _():
        y = x_ref[...] + FFN_RES * acc_scr[...]
        if emit_x:
            ox_ref[...] = y
        on_ref[...] = _rms(y, g2_ref[...]).astype(n_dtype)


def _ffn(x, g, wg, wu, wd, g2, *, tm, emit_x, n_dtype):
    m = x.shape[0]
    grid = (m // tm, FF_STEPS)
    row = pl.BlockSpec((tm, D_MODEL), lambda i, j: (i, 0))
    vec = pl.BlockSpec((1, D_MODEL), lambda i, j: (0, 0))
    out_shape = [jax.ShapeDtypeStruct((m, D_MODEL), n_dtype)]
    out_specs = [row]
    if emit_x:
        out_shape = [jax.ShapeDtypeStruct((m, D_MODEL), F32)] + out_shape
        out_specs = [row, row]
    return pl.pallas_call(
        functools.partial(_ffn_kernel, emit_x=emit_x, n_dtype=n_dtype),
        grid=grid,
        in_specs=[row, vec,
                  pl.BlockSpec((D_MODEL, FF_TILE), lambda i, j: (0, j)),
                  pl.BlockSpec((D_MODEL, FF_TILE), lambda i, j: (0, j)),
                  pl.BlockSpec((FF_TILE, D_MODEL), lambda i, j: (j, 0)),
                  vec],
        out_specs=out_specs,
        out_shape=out_shape,
        scratch_shapes=[pltpu.VMEM((tm, D_MODEL), BF16), pltpu.VMEM((tm, D_MODEL), F32)],
        compiler_params=_cparams(("arbitrary", "arbitrary")),
        name="ffn",
    )(x, g, wg, wu, wd, g2)


def _proj_kernel(h_ref, w_ref, q_ref, kv_ref, rw_ref):
    h = h_ref[...]
    q = jnp.dot(h, w_ref[:, 0:ATT_WIDTH], preferred_element_type=F32)
    q_ref[...] = (q * (HEAD_DIM ** -0.5)).astype(BF16)
    kv_ref[...] = jnp.dot(h, w_ref[:, ATT_WIDTH:ATT_COLS], preferred_element_type=F32)
    rw_ref[...] = jnp.dot(h, w_ref[:, ATT_COLS:IN_COLS], preferred_element_type=F32)


def _proj(h, w, *, tm):
    m = h.shape[0]
    return pl.pallas_call(
        _proj_kernel,
        grid=(m // tm,),
        in_specs=[pl.BlockSpec((tm, D_MODEL), lambda i: (i, 0)),
                  pl.BlockSpec((D_MODEL, IN_COLS), lambda i: (0, 0), pipeline_mode=pl.Buffered(1))],
        out_specs=[pl.BlockSpec((tm, ATT_WIDTH), lambda i: (i, 0)),
                   pl.BlockSpec((tm, 2 * KV_WIDTH), lambda i: (i, 0)),
                   pl.BlockSpec((tm, SHIFT_COLS), lambda i: (i, 0))],
        out_shape=[jax.ShapeDtypeStruct((m, ATT_WIDTH), BF16),
                   jax.ShapeDtypeStruct((m, 2 * KV_WIDTH), F32),
                   jax.ShapeDtypeStruct((m, SHIFT_COLS), F32)],
        compiler_params=_cparams(("arbitrary",)),
        name="in_proj",
    )(h, w)


def _outproj_kernel(x_ref, oa_ref, orw_ref, w_ref, o_ref):
    acc = jnp.dot(oa_ref[...], w_ref[0:ATT_WIDTH, :], preferred_element_type=F32)
    acc += jnp.dot(orw_ref[...], w_ref[ATT_WIDTH:D_MODEL, :], preferred_element_type=F32)
    o_ref[...] = x_ref[...] + acc


def _outproj(x, oa, orw, w, *, tm):
    m = x.shape[0]
    row = pl.BlockSpec((tm, D_MODEL), lambda i: (i, 0))
    half = pl.BlockSpec((tm, ATT_WIDTH), lambda i: (i, 0))
    return pl.pallas_call(
        _outproj_kernel,
        grid=(m // tm,),
        in_specs=[row, half, half,
                  pl.BlockSpec((D_MODEL, D_MODEL), lambda i: (0, 0), pipeline_mode=pl.Buffered(1))],
        out_specs=row,
        out_shape=jax.ShapeDtypeStruct((m, D_MODEL), F32),
        compiler_params=_cparams(("arbitrary",)),
        name="out_proj",
    )(x, oa, orw, w)


def _bucket_table():
    qi = np.arange(BLOCK)[:, None]
    kj = np.arange(2 * BLOCK)[None, :]
    dist = BLOCK + qi - kj
    n = np.maximum(dist, 0)
    max_exact = N_BUCKETS // 2
    nf = np.maximum(n, 1).astype(np.float32)
    large = max_exact + (np.log(nf / np.float32(max_exact)) / np.float32(math.log(MAX_DISTANCE / max_exact))
                         * np.float32(N_BUCKETS - max_exact)).astype(np.int32)
    large = np.minimum(large, N_BUCKETS - 1)
    bucket = np.where(n < max_exact, n, large).astype(np.int32)
    valid = (dist >= 0) & (dist <= WINDOW)
    return np.where(valid, bucket, -1).astype(np.int32)


def _bias_kernel(bucket_ref, rb_ref, o_ref):
    bucket = bucket_ref[...]
    for h in range(N_Q_HEADS):
        acc = jnp.full((BLOCK, 2 * BLOCK), NEG_BIG, F32)
        for n in range(N_BUCKETS):
            acc = jnp.where(bucket == n, rb_ref[n, h], acc)
        o_ref[h] = acc


def _bias_table(rel_bias):
    bucket = jnp.asarray(_bucket_table())
    return pl.pallas_call(
        _bias_kernel,
        in_specs=[pl.BlockSpec(memory_space=pltpu.VMEM), pl.BlockSpec(memory_space=pltpu.SMEM)],
        out_specs=pl.BlockSpec(memory_space=pltpu.VMEM),
        out_shape=jax.ShapeDtypeStruct((N_Q_HEADS, BLOCK, 2 * BLOCK), F32),
        name="bias_table",
    )(bucket, rel_bias)


def _attn_kernel(sink_ref, q_ref, kvp_ref, kvc_ref, bias_ref, o_ref):
    j = pl.program_id(1)
    kj = lax.broadcasted_iota(jnp.int32, (BLOCK, 2 * BLOCK), 1)
    blk_ok = jnp.logical_or(kj >= BLOCK, j > 0)
    kvp = kvp_ref[...]
    kvc = kvc_ref[...]
    for g in range(N_KV_HEADS):
        ks = slice(g * HEAD_DIM, (g + 1) * HEAD_DIM)
        vs = slice(KV_WIDTH + g * HEAD_DIM, KV_WIDTH + (g + 1) * HEAD_DIM)
        kcat = jnp.concatenate([kvp[:, ks], kvc[:, ks]], axis=0).astype(BF16)
        vcat = jnp.concatenate([kvp[:, vs], kvc[:, vs]], axis=0).astype(BF16)
        for u in range(GQA_GROUP):
            h = g * GQA_GROUP + u
            hs = slice(h * HEAD_DIM, (h + 1) * HEAD_DIM)
            s = _dg(q_ref[:, hs], kcat, _NT)
            s = jnp.where(blk_ok, s + bias_ref[h], NEG_BIG)
            sink = sink_ref[h]
            m = jnp.maximum(jnp.max(s, axis=-1, keepdims=True), sink)
            p = jnp.exp(s - m)
            denom = jnp.sum(p, axis=-1, keepdims=True) + jnp.exp(sink - m)
            o = jnp.dot(p.astype(BF16), vcat, preferred_element_type=F32)
            o_ref[:, hs] = (o / denom).astype(BF16)


def _attn_prompt(q, kv, bias, sinks, *, batch, seq):
    nb = seq // BLOCK
    q3 = q.reshape(batch, seq, ATT_WIDTH)
    kv3 = kv.reshape(batch, seq, 2 * KV_WIDTH)
    out = pl.pallas_call(
        _attn_kernel,
        grid=(batch, nb),
        in_specs=[pl.BlockSpec(memory_space=pltpu.SMEM),
                  pl.BlockSpec((None, BLOCK, ATT_WIDTH), lambda b, j: (b, j, 0)),
                  pl.BlockSpec((None, BLOCK, 2 * KV_WIDTH), lambda b, j: (b, jnp.maximum(j - 1, 0), 0)),
                  pl.BlockSpec((None, BLOCK, 2 * KV_WIDTH), lambda b, j: (b, j, 0)),
                  pl.BlockSpec((N_Q_HEADS, BLOCK, 2 * BLOCK), lambda b, j: (0, 0, 0))],
        out_specs=pl.BlockSpec((None, BLOCK, ATT_WIDTH), lambda b, j: (b, j, 0)),
        out_shape=jax.ShapeDtypeStruct((batch, seq, ATT_WIDTH), BF16),
        compiler_params=_cparams(("arbitrary", "arbitrary")),
        name="attn_prompt",
    )(sinks, q3, kv3, kv3, bias)
    return out.reshape(batch * seq, ATT_WIDTH)


def _attn_decode_kernel(sink_ref, q_ref, kvn_ref, ck_ref, cv_ref, bias_ref, o_ref):
    q = q_ref[...].astype(F32)
    kvn = kvn_ref[...]
    for g in range(N_KV_HEADS):
        ks = slice(g * HEAD_DIM, (g + 1) * HEAD_DIM)
        vs = slice(KV_WIDTH + g * HEAD_DIM, KV_WIDTH + (g + 1) * HEAD_DIM)
        hs = slice(g * GQA_GROUP, (g + 1) * GQA_GROUP)
        qg = q[:, hs, :]
        kc = ck_ref[:, :, ks]
        vc = cv_ref[:, :, ks]
        kn = kvn[:, ks].astype(BF16).astype(F32)[:, None, :]
        vn = kvn[:, vs].astype(BF16).astype(F32)[:, None, :]
        s = jnp.einsum("bqd,bkd->bqk", qg.astype(BF16), kc.astype(BF16), preferred_element_type=F32)
        s = s + bias_ref[hs, 0:WINDOW][None]
        s_new = jnp.sum(qg * kn, axis=-1, keepdims=True) + bias_ref[hs, WINDOW:WINDOW + 1][None]
        sink = sink_ref[hs, 0:1][None]
        m = jnp.maximum(jnp.maximum(jnp.max(s, axis=-1, keepdims=True), s_new), sink)
        p = jnp.exp(s - m)
        p_new = jnp.exp(s_new - m)
        denom = jnp.sum(p, axis=-1, keepdims=True) + p_new + jnp.exp(sink - m)
        o = jnp.einsum("bqk,bkd->bqd", p.astype(BF16), vc.astype(BF16), preferred_element_type=F32)
        o = o + p_new * vn
        o_ref[:, hs, :] = o / denom


def _attn_decode(q, kvn, cache_k, cache_v, bias_row, sinks_b):
    b = q.shape[0]
    lw = cache_k.shape[1]
    vm = pl.BlockSpec(memory_space=pltpu.VMEM)
    out = pl.pallas_call(
        _attn_decode_kernel,
        in_specs=[vm, vm, vm, vm, vm, vm],
        out_specs=vm,
        out_shape=jax.ShapeDtypeStruct((b, N_Q_HEADS, HEAD_DIM), F32),
        compiler_params=pltpu.CompilerParams(vmem_limit_bytes=VMEM_LIMIT_BYTES),
        name="attn_decode",
    )(sinks_b, q.reshape(b, N_Q_HEADS, HEAD_DIM), kvn,
      cache_k.reshape(b, lw, KV_WIDTH), cache_v.reshape(b, lw, KV_WIDTH), bias_row)
    return out.reshape(b, ATT_WIDTH)


def _seg_expand(x, bd_ref):
    rows = x.shape[0]
    groups = x.shape[1] // LANES
    xs = jnp.concatenate([x[:, i * LANES:(i + 1) * LANES] for i in range(groups)], axis=0)
    s = _mm2(xs, bd_ref[...])
    return jnp.concatenate([s[i * rows:(i + 1) * rows] for i in range(groups)], axis=1)


def _rwkv_front(x, prev, p):
    mu_ref, w0_ref, a0_ref, kk_ref, ka_ref, lw_ref, g2_ref, bd_ref = p
    xm = x + mu_ref[...] * (prev - x)
    o3 = 3 * RWKV_WIDTH
    r = xm[:, 0:RWKV_WIDTH]
    k = xm[:, RWKV_WIDTH:2 * RWKV_WIDTH]
    v = xm[:, 2 * RWKV_WIDTH:o3]
    wa = xm[:, o3:o3 + LANES]
    lane = lax.broadcasted_iota(jnp.int32, wa.shape, 1)
    z = jnp.where(lane < DECAY_LORA, jnp.tanh(wa), wa)
    lora = _mm(z, lw_ref[...])
    w = -_softplus(-(w0_ref[...] + lora[:, 0:RWKV_WIDTH])) - 0.5
    logw = -jnp.exp(w)
    a = _sigmoid(a0_ref[...] + lora[:, RWKV_WIDTH:2 * RWKV_WIDTH])
    g = _mm(_sigmoid(xm[:, o3 + LANES:SHIFT_COLS]), g2_ref[...])
    kk = k * kk_ref[...]
    kmod = k * (1.0 + (a - 1.0) * ka_ref[...])
    nrm = jnp.sqrt(_seg_expand(kk * kk, bd_ref))
    kk = kk / jnp.maximum(nrm, 1e-12)
    return r, logw, kmod, v, kk, kk * a, g


def _rwkv_back(y, r, kmod, v, g, q):
    rk_ref, lnw_ref, lnb_ref, bd_ref = q
    mean = _seg_expand(y, bd_ref) * (1.0 / HEAD_DIM)
    d = y - mean
    var = _seg_expand(d * d, bd_ref) * (1.0 / HEAD_DIM)
    yn = d * lax.rsqrt(var + GN_EPS) * lnw_ref[...] + lnb_ref[...]
    bonus = _seg_expand(r * kmod * rk_ref[...], bd_ref) * v
    return (yn + bonus) * g


def _rwkv_chunk_kernel(rw_ref, mu_ref, w0_ref, a0_ref, kk_ref, ka_ref, lw_ref, g2_ref, bd_ref,
                       rk_ref, lnw_ref, lnb_ref,
                       o_ref, s_out_ref,
                       s_scr, prev_scr, y_scr, ea_scr, er_scr, eb_scr, ek_scr, be_scr, ke_scr, v_scr):
    c = pl.program_id(1)
    C = CHUNK

    @pl.when(c == 0)
    def _():
        s_scr[...] = jnp.zeros_like(s_scr)
        prev_scr[...] = jnp.zeros_like(prev_scr)

    x = rw_ref[...]
    row = lax.broadcasted_iota(jnp.int32, x.shape, 0)
    prev = jnp.where(row == 0, prev_scr[...], pltpu.roll(x, 1, axis=0))
    prev_scr[...] = x[C - 1:C, :]
    front = (mu_ref, w0_ref, a0_ref, kk_ref, ka_ref, lw_ref, g2_ref, bd_ref)
    r, logw, kmod, v, kk, bb, g = _rwkv_front(x, prev, front)

    ti = lax.broadcasted_iota(jnp.int32, (C, C), 0)
    si = lax.broadcasted_iota(jnp.int32, (C, C), 1)
    cum = _mm2r(jnp.where(si <= ti, 1.0, 0.0).astype(BF16), logw)
    cum_end = cum[C - 1:C, :]
    e_in = jnp.exp(cum)
    e_neg = jnp.exp(-cum)
    e_end = jnp.exp(cum_end - cum)
    ea_scr[...] = kk * jnp.exp(cum - logw)
    er_scr[...] = r * e_in
    eb_scr[...] = bb * e_neg
    ek_scr[...] = kmod * e_neg
    be_scr[...] = bb * e_end
    ke_scr[...] = kmod * e_end
    v_scr[...] = v
    w_end = jnp.exp(cum_end)

    lane = lax.broadcasted_iota(jnp.int32, (C, LANES), 1)
    lo_half = lane < HEAD_DIM
    ri = lax.broadcasted_iota(jnp.int32, (LANES, LANES), 0)
    ci = lax.broadcasted_iota(jnp.int32, (LANES, LANES), 1)
    same_head = jnp.right_shift(ri, 6) == jnp.right_shift(ci, 6)
    eye = ri == ci
    t_loc = jnp.bitwise_and(ri, C - 1)
    s_loc = jnp.bitwise_and(ci, C - 1)
    bd_strict = jnp.logical_and(same_head, s_loc < t_loc)
    bd_incl = jnp.logical_and(same_head, s_loc <= t_loc)

    def swap(t):
        return pltpu.roll(t, HEAD_DIM, axis=1)

    def stack2(t):
        return jnp.concatenate([jnp.where(lo_half, t, 0.0), jnp.where(lo_half, 0.0, t)], axis=0)

    def stack2x(t):
        ts = swap(t)
        return jnp.concatenate([jnp.where(lo_half, 0.0, ts), jnp.where(lo_half, ts, 0.0)], axis=0)

    def own(t):
        return jnp.where(lo_half, t[0:C], t[C:2 * C])

    def other(t):
        return swap(jnp.where(lo_half, t[C:2 * C], t[0:C]))

    pairs = range(HEAD_PAIRS)
    sl = [slice(p * LANES, (p + 1) * LANES) for p in pairs]
    a2 = [stack2(ea_scr[:, sl[p]]) for p in pairs]
    m = []
    for p in pairs:
        r2 = stack2(er_scr[:, sl[p]])
        b_p = eb_scr[:, sl[p]]
        k_p = ek_scr[:, sl[p]]
        m.append(_mm(jnp.concatenate([a2[p], r2], axis=0), jnp.concatenate([b_p, b_p, k_p, k_p], axis=0), _NT))
    l_ak = [jnp.where(bd_strict, m[p][0:2 * C, LANES:2 * LANES], 0.0) for p in pairs]
    pw = [-jnp.where(bd_strict, m[p][0:2 * C, 0:LANES], 0.0) for p in pairs]
    lrbk = [jnp.concatenate([jnp.where(bd_incl, m[p][2 * C:4 * C, 0:LANES], 0.0),
                             jnp.where(bd_incl, m[p][2 * C:4 * C, LANES:2 * LANES], 0.0)], axis=1) for p in pairs]
    v2x = [stack2x(v_scr[:, sl[p]]) for p in pairs]
    xc = [a2[p] - _mm(l_ak[p], v2x[p]) for p in pairs]
    for it in range(6):
        for p in pairs:
            if it < 5:
                px = _mm(pw[p], jnp.concatenate([xc[p], pw[p]], axis=1))
                pw[p] = px[:, LANES:2 * LANES]
                xc[p] = xc[p] + px[:, 0:LANES]
            else:
                xc[p] = xc[p] + _mm(pw[p], xc[p])
    xv = [jnp.concatenate([xc[p], v2x[p]], axis=0) for p in pairs]
    lb = [_mm(lrbk[p], xv[p]) for p in pairs]
    z = [_mm(xv[p], jnp.concatenate([stack2(be_scr[:, sl[p]]), stack2(ke_scr[:, sl[p]])], axis=0), _TN)
         for p in pairs]
    rt = [er_scr[:, sl[p]] - own(lb[p]) for p in pairs]
    yl = [other(lb[p]) for p in pairs]
    d = [jnp.where(eye, w_end[:, sl[p]], 0.0) - jnp.where(same_head, z[p], 0.0) for p in pairs]
    gg = [jnp.where(same_head, jnp.concatenate([z[p][C:2 * C], z[p][0:C]], axis=0), 0.0) for p in pairs]
    s0 = [s_scr[p] for p in pairs]
    ys = [_mm3(rt[p], s0[p], _NT) for p in pairs]
    for p in pairs:
        y_scr[:, sl[p]] = yl[p] + ys[p]
    sn = [_mm3(s0[p], d[p]) for p in pairs]
    for p in pairs:
        s_scr[p] = sn[p] + gg[p]

    back = (rk_ref, lnw_ref, lnb_ref, bd_ref)
    o_ref[...] = _rwkv_back(y_scr[...], r, kmod, v, g, back).astype(BF16)

    @pl.when(c == pl.num_programs(1) - 1)
    def _():
        for p in range(HEAD_PAIRS):
            s_out_ref[2 * p] = s_scr[p, 0:HEAD_DIM, 0:HEAD_DIM]
            s_out_ref[2 * p + 1] = s_scr[p, HEAD_DIM:LANES, HEAD_DIM:LANES]


def _rwkv_prompt(rw, params, *, batch, seq):
    rw3 = rw.reshape(batch, seq, SHIFT_COLS)
    nc = seq // CHUNK
    const = lambda shape: pl.BlockSpec(shape, lambda b, c: tuple(0 for _ in shape))
    in_specs = [pl.BlockSpec((None, CHUNK, SHIFT_COLS), lambda b, c: (b, c, 0))]
    in_specs += [const(p.shape) for p in params]
    wide = pltpu.VMEM((CHUNK, RWKV_WIDTH), F32)
    o, s_out = pl.pallas_call(
        _rwkv_chunk_kernel,
        grid=(batch, nc),
        in_specs=in_specs,
        out_specs=[pl.BlockSpec((None, CHUNK, RWKV_WIDTH), lambda b, c: (b, c, 0)),
                   pl.BlockSpec((None, RWKV_HEADS, HEAD_DIM, HEAD_DIM), lambda b, c: (b, 0, 0, 0))],
        out_shape=[jax.ShapeDtypeStruct((batch, seq, RWKV_WIDTH), BF16),
                   jax.ShapeDtypeStruct((batch, RWKV_HEADS, HEAD_DIM, HEAD_DIM), F32)],
        scratch_shapes=[pltpu.VMEM((HEAD_PAIRS, LANES, LANES), F32),
                        pltpu.VMEM((1, SHIFT_COLS), F32),
                        wide, wide, wide, wide, wide, wide, wide, wide],
        compiler_params=_cparams(("arbitrary", "arbitrary")),
        name="rwkv_prompt",
    )(rw3, *params)
    return o.reshape(batch * seq, RWKV_WIDTH), s_out


def _rwkv_dec_front_kernel(rw_ref, prev_ref, mu_ref, w0_ref, a0_ref, kk_ref, ka_ref, lw_ref, g2_ref, bd_ref,
                           r_ref, w_ref, k_ref, v_ref, kkn_ref, b_ref, g_ref):
    front = (mu_ref, w0_ref, a0_ref, kk_ref, ka_ref, lw_ref, g2_ref, bd_ref)
    r, logw, kmod, v, kk, bb, g = _rwkv_front(rw_ref[...], prev_ref[...], front)
    r_ref[...] = r
    w_ref[...] = jnp.exp(logw)
    k_ref[...] = kmod
    v_ref[...] = v
    kkn_ref[...] = kk
    b_ref[...] = bb
    g_ref[...] = g


def _rwkv_dec_state_kernel(r_ref, w_ref, k_ref, v_ref, kk_ref, b_ref, s_ref, y_ref, so_ref):
    ri = lax.broadcasted_iota(jnp.int32, (HEAD_DIM, HEAD_DIM), 0)
    ci = lax.broadcasted_iota(jnp.int32, (HEAD_DIM, HEAD_DIM), 1)
    eye = ri == ci
    heads = range(RWKV_HEADS)
    hs = [slice(h * HEAD_DIM, (h + 1) * HEAD_DIM) for h in heads]
    s = [s_ref[h] for h in heads]
    sa = [-jnp.sum(s[h] * kk_ref[:, hs[h]], axis=1, keepdims=True) for h in heads]
    v_col = [jnp.sum(jnp.where(eye, v_ref[:, hs[h]], 0.0), axis=1, keepdims=True) for h in heads]
    s_new = [s[h] * w_ref[:, hs[h]] + sa[h] * b_ref[:, hs[h]] + v_col[h] * k_ref[:, hs[h]] for h in heads]
    for h in heads:
        so_ref[h] = s_new[h]
    y_col = [jnp.sum(s_new[h] * r_ref[:, hs[h]], axis=1, keepdims=True) for h in heads]
    for h in heads:
        y_ref[:, hs[h]] = jnp.sum(jnp.where(eye, y_col[h], 0.0), axis=0, keepdims=True)


def _rwkv_dec_back_kernel(y_ref, r_ref, k_ref, v_ref, g_ref, rk_ref, lnw_ref, lnb_ref, bd_ref, o_ref):
    back = (rk_ref, lnw_ref, lnb_ref, bd_ref)
    o_ref[...] = _rwkv_back(y_ref[...], r_ref[...], k_ref[...], v_ref[...], g_ref[...], back).astype(BF16)


def _rwkv_decode(rw, shift0, wkv0, front_params, back_params):
    b = rw.shape[0]
    vm = pl.BlockSpec(memory_space=pltpu.VMEM)
    wide = jax.ShapeDtypeStruct((b, RWKV_WIDTH), F32)
    r, w, k, v, kk, bb, g = pl.pallas_call(
        _rwkv_dec_front_kernel,
        in_specs=[vm] * (2 + len(front_params)),
        out_specs=[vm] * 7,
        out_shape=[wide] * 7,
        name="rwkv_dec_front",
    )(rw, shift0, *front_params)
    rowspec = pl.BlockSpec((None, 1, RWKV_WIDTH), lambda i: (i, 0, 0))
    stspec = pl.BlockSpec((None, RWKV_HEADS, HEAD_DIM, HEAD_DIM), lambda i: (i, 0, 0, 0))
    as_rows = lambda t: t.reshape(b, 1, RWKV_WIDTH)
    y, s_new = pl.pallas_call(
        _rwkv_dec_state_kernel,
        grid=(b,),
        in_specs=[rowspec] * 6 + [stspec],
        out_specs=[rowspec, stspec],
        out_shape=[jax.ShapeDtypeStruct((b, 1, RWKV_WIDTH), F32),
                   jax.ShapeDtypeStruct((b, RWKV_HEADS, HEAD_DIM, HEAD_DIM), F32)],
        compiler_params=_cparams(("arbitrary",)),
        name="rwkv_dec_state",
    )(as_rows(r), as_rows(w), as_rows(k), as_rows(v), as_rows(kk), as_rows(bb), wkv0)
    o = pl.pallas_call(
        _rwkv_dec_back_kernel,
        in_specs=[vm] * (5 + len(back_params)),
        out_specs=vm,
        out_shape=jax.ShapeDtypeStruct((b, RWKV_WIDTH), BF16),
        name="rwkv_dec_back",
    )(y.reshape(b, RWKV_WIDTH), r, k, v, g, *back_params)
    return o, s_new


def kernel(x_prompt, x_sample, cache_k, cache_v, state_wkv, state_shift, rel_bias, ffn1_norm, ffn1_w_gate, ffn1_w_up, ffn1_w_down, mix_norm, w_in, attn_sinks, shift_mu, decay_w0, decay_w2, aaa_a0, aaa_a2, gate_g2, key_k, key_a, bonus_r_k, ln_x_w, ln_x_b, w_out, ffn2_norm, ffn2_w_gate, ffn2_w_up, ffn2_w_down, final_norm):
    batch, seq, _ = x_prompt.shape
    dec_b = x_sample.shape[0]
    lw = cache_k.shape[2]
    l = 0
    row = lambda t: t.reshape(1, -1)

    f1 = (ffn1_w_gate[l].astype(BF16), ffn1_w_up[l].astype(BF16), ffn1_w_down[l].astype(BF16))
    f2 = (ffn2_w_gate[l].astype(BF16), ffn2_w_up[l].astype(BF16), ffn2_w_down[l].astype(BF16))
    w_in_b = w_in[l].astype(BF16)
    w_out_b = w_out[l].astype(BF16)
    lora_w = jnp.zeros((LANES, 2 * RWKV_WIDTH), F32)
    lora_w = lora_w.at[0:DECAY_LORA, 0:RWKV_WIDTH].set(decay_w2[l])
    lora_w = lora_w.at[DECAY_LORA:LANES, RWKV_WIDTH:].set(aaa_a2[l])
    head_of = np.arange(LANES) // HEAD_DIM
    bd_ones = jnp.asarray(head_of[:, None] == head_of[None, :], BF16)
    front_params = (row(shift_mu[l]), row(decay_w0[l]), row(aaa_a0[l]), row(key_k[l]), row(key_a[l]),
                    lora_w.astype(BF16), gate_g2[l].astype(BF16), bd_ones)
    back_params = (row(bonus_r_k[l]), row(ln_x_w[l]), row(ln_x_b[l]), bd_ones)

    bias = _bias_table(rel_bias)
    sinks = attn_sinks[l]

    xp = x_prompt.reshape(batch * seq, D_MODEL)
    x1, h1 = _ffn(xp, row(ffn1_norm[l]), *f1, row(mix_norm[l]), tm=512, emit_x=True, n_dtype=BF16)
    q, kv, rw = _proj(h1, w_in_b, tm=256)
    o_att = _attn_prompt(q, kv, bias, sinks, batch=batch, seq=seq)
    o_rw, s_p = _rwkv_prompt(rw, front_params + back_params[:3], batch=batch, seq=seq)
    x2 = _outproj(x1, o_att, o_rw, w_out_b, tm=512)
    (y_p,) = _ffn(x2, row(ffn2_norm[l]), *f2, row(final_norm), tm=512, emit_x=False, n_dtype=F32)

    xs = x_sample.reshape(dec_b, D_MODEL)
    x1s, h1s = _ffn(xs, row(ffn1_norm[l]), *f1, row(mix_norm[l]), tm=dec_b, emit_x=True, n_dtype=BF16)
    qs, kvs, rws = _proj(h1s, w_in_b, tm=dec_b)
    sinks_b = jnp.broadcast_to(sinks[:, None], (N_Q_HEADS, LANES))
    o_att_s = _attn_decode(qs, kvs, cache_k[l], cache_v[l], bias[:, 0, :], sinks_b)
    o_rw_s, s_s = _rwkv_decode(rws, state_shift[l], state_wkv[l], front_params, back_params)
    x2s = _outproj(x1s, o_att_s.astype(BF16), o_rw_s, w_out_b, tm=dec_b)
    (y_s,) = _ffn(x2s, row(ffn2_norm[l]), *f2, row(final_norm), tm=dec_b, emit_x=False, n_dtype=F32)

    kv3 = kv.reshape(batch, seq, 2 * KV_WIDTH)
    lp = min(WINDOW, seq)
    new_k_p = kv3[:, seq - lp:, 0:KV_WIDTH].reshape(1, batch, lp, N_KV_HEADS, HEAD_DIM)
    new_v_p = kv3[:, seq - lp:, KV_WIDTH:].reshape(1, batch, lp, N_KV_HEADS, HEAD_DIM)
    new_shift_p = rw.reshape(batch, seq, SHIFT_COLS)[:, seq - 1][None]
    k_new = kvs[:, 0:KV_WIDTH].reshape(dec_b, 1, N_KV_HEADS, HEAD_DIM)
    v_new = kvs[:, KV_WIDTH:].reshape(dec_b, 1, N_KV_HEADS, HEAD_DIM)
    new_k_s = jnp.concatenate([cache_k[l], k_new], axis=1)[:, -lw:][None]
    new_v_s = jnp.concatenate([cache_v[l], v_new], axis=1)[:, -lw:][None]
    return (y_p.reshape(batch, seq, D_MODEL), y_s.reshape(dec_b, 1, D_MODEL),
            new_k_p, new_v_p, s_p[None], new_shift_p,
            new_k_s, new_v_s, s_s[None], rws[None])
```

```python
import functools
import math

import numpy as np
import jax
import jax.numpy as jnp
from jax import lax
from jax.experimental import pallas as pl
from jax.experimental.pallas import tpu as pltpu

F32 = jnp.float32
BF16 = jnp.bfloat16

D_MODEL = 2048
HEAD_DIM = 64
ATT_WIDTH = 1024
N_Q_HEADS = 16
N_KV_HEADS = 4
GQA_GROUP = 4
KV_WIDTH = 256
RWKV_WIDTH = 1024
RWKV_HEADS = 16
WINDOW = 128
BLOCK = 128
N_BUCKETS = 32
MAX_DISTANCE = 128
DECAY_LORA = 64
AAA_LORA = 64
GATE_LORA = 128
D_FF = 5504
ATT_COLS = ATT_WIDTH + 2 * KV_WIDTH
SHIFT_COLS = 3 * RWKV_WIDTH + DECAY_LORA + AAA_LORA + GATE_LORA
IN_COLS = ATT_COLS + SHIFT_COLS
RMS_EPS = 1e-5
GN_EPS = 64e-5
FFN_RES = 0.5

LANES = 128
VMEM_LIMIT_BYTES = 56 * 1024 * 1024

FF_TILE = 512
FF_STEPS = -(-D_FF // FF_TILE)
FF_TAIL = D_FF - (FF_STEPS - 1) * FF_TILE
CHUNK = 64
HEAD_PAIRS = RWKV_HEADS // 2
NEG_BIG = -1e30

_NN = (((1,), (0,)), ((), ()))
_NT = (((1,), (1,)), ((), ()))
_TN = (((0,), (0,)), ((), ()))


def _dg(a, b, dims=_NN):
    return lax.dot_general(a, b, dims, preferred_element_type=F32)


def _mm(a, b, dims=_NN):
    return _dg(a.astype(BF16), b.astype(BF16), dims)


def _split2(x):
    hi = x.astype(BF16)
    lo = (x - hi.astype(F32)).astype(BF16)
    return hi, lo


def _mm3(a, b, dims=_NN):
    ah, al = _split2(a)
    bh, bl = _split2(b)
    return _dg(ah, bh, dims) + _dg(ah, bl, dims) + _dg(al, bh, dims)


def _mm2(a, b_bf16, dims=_NN):
    ah, al = _split2(a)
    return _dg(ah, b_bf16, dims) + _dg(al, b_bf16, dims)


def _mm2r(a_bf16, b, dims=_NN):
    bh, bl = _split2(b)
    return _dg(a_bf16, bh, dims) + _dg(a_bf16, bl, dims)


def _sigmoid(x):
    return 1.0 / (1.0 + jnp.exp(-x))


def _softplus(x):
    return jnp.maximum(x, 0.0) + jnp.log(1.0 + jnp.exp(-jnp.abs(x)))


def _rms(x, g):
    ms = jnp.mean(x * x, axis=-1, keepdims=True)
    return x * lax.rsqrt(ms + RMS_EPS) * g


def _cparams(sem):
    return pltpu.CompilerParams(dimension_semantics=sem, vmem_limit_bytes=VMEM_LIMIT_BYTES)


def _ffn_kernel(x_ref, g_ref, wg_ref, wu_ref, wd_ref, g2_ref, *rest, emit_x, n_dtype):
    if emit_x:
        ox_ref, on_ref, xn_scr, acc_scr = rest
    else:
        on_ref, xn_scr, acc_scr = rest
    j = pl.program_id(1)

    @pl.when(j == 0)
    def _():
        xn_scr[...] = _rms(x_ref[...], g_ref[...]).astype(BF16)
        acc_scr[...] = jnp.zeros_like(acc_scr)

    last = j == pl.num_programs(1) - 1
    xn = xn_scr[...]
    gate = jnp.dot(xn, wg_ref[...], preferred_element_type=F32)
    up = jnp.dot(xn, wu_ref[...], preferred_element_type=F32)
    h = gate * _sigmoid(gate) * up
    h = jnp.concatenate([h[:, :FF_TAIL], jnp.where(last, 0.0, h[:, FF_TAIL:])], axis=1).astype(BF16)
    wd = jnp.concatenate([wd_ref[0:FF_TAIL, :],
                          jnp.where(last, jnp.zeros((), BF16), wd_ref[FF_TAIL:FF_TILE, :])], axis=0)
    acc_scr[...] += jnp.dot(h, wd, preferred_element_type=F32)

    @pl.when(last)
    def _():
        y = x_ref[...] + FFN_RES * acc_scr[...]
        if emit_x:
            ox_ref[...] = y
        on_ref[...] = _rms(y, g2_ref[...]).astype(n_dtype)


def _ffn(x, g, wg, wu, wd, g2, *, tm, emit_x, n_dtype):
    m = x.shape[0]
    grid = (m // tm, FF_STEPS)
    row = pl.BlockSpec((tm, D_MODEL), lambda i, j: (i, 0))
    vec = pl.BlockSpec((1, D_MODEL), lambda i, j: (0, 0))
    out_shape = [jax.ShapeDtypeStruct((m, D_MODEL), n_dtype)]
    out_specs = [row]
    if emit_x:
        out_shape = [jax.ShapeDtypeStruct((m, D_MODEL), F32)] + out_shape
        out_specs = [row, row]
    return pl.pallas_call(
        functools.partial(_ffn_kernel, emit_x=emit_x, n_dtype=n_dtype),
        grid=grid,
        in_specs=[row, vec,
                  pl.BlockSpec((D_MODEL, FF_TILE), lambda i, j: (0, j)),
                  pl.BlockSpec((D_MODEL, FF_TILE), lambda i, j: (0, j)),
                  pl.BlockSpec((FF_TILE, D_MODEL), lambda i, j: (j, 0)),
                  vec],
        out_specs=out_specs,
        out_shape=out_shape,
        scratch_shapes=[pltpu.VMEM((tm, D_MODEL), BF16), pltpu.VMEM((tm, D_MODEL), F32)],
        compiler_params=_cparams(("arbitrary", "arbitrary")),
        name="ffn",
    )(x, g, wg, wu, wd, g2)


def _proj_kernel(h_ref, w_ref, q_ref, kv_ref, rw_ref):
    h = h_ref[...]
    q = jnp.dot(h, w_ref[:, 0:ATT_WIDTH], preferred_element_type=F32)
    q_ref[...] = (q * (HEAD_DIM ** -0.5)).astype(BF16)
    kv_ref[...] = jnp.dot(h, w_ref[:, ATT_WIDTH:ATT_COLS], preferred_element_type=F32)
    rw_ref[...] = jnp.dot(h, w_ref[:, ATT_COLS:IN_COLS], preferred_element_type=F32)


def _proj(h, w, *, tm):
    m = h.shape[0]
    return pl.pallas_call(
        _proj_kernel,
        grid=(m // tm,),
        in_specs=[pl.BlockSpec((tm, D_MODEL), lambda i: (i, 0)),
                  pl.BlockSpec((D_MODEL, IN_COLS), lambda i: (0, 0), pipeline_mode=pl.Buffered(1))],
        out_specs=[pl.BlockSpec((tm, ATT_WIDTH), lambda i: (i, 0)),
                   pl.BlockSpec((tm, 2 * KV_WIDTH), lambda i: (i, 0)),
                   pl.BlockSpec((tm, SHIFT_COLS), lambda i: (i, 0))],
        out_shape=[jax.ShapeDtypeStruct((m, ATT_WIDTH), BF16),
                   jax.ShapeDtypeStruct((m, 2 * KV_WIDTH), F32),
                   jax.ShapeDtypeStruct((m, SHIFT_COLS), F32)],
        compiler_params=_cparams(("arbitrary",)),
        name="in_proj",
    )(h, w)


def _outproj_kernel(x_ref, oa_ref, orw_ref, w_ref, o_ref):
    acc = jnp.dot(oa_ref[...], w_ref[0:ATT_WIDTH, :], preferred_element_type=F32)
    acc += jnp.dot(orw_ref[...], w_ref[ATT_WIDTH:D_MODEL, :], preferred_element_type=F32)
    o_ref[...] = x_ref[...] + acc


def _outproj(x, oa, orw, w, *, tm):
    m = x.shape[0]
    row = pl.BlockSpec((tm, D_MODEL), lambda i: (i, 0))
    half = pl.BlockSpec((tm, ATT_WIDTH), lambda i: (i, 0))
    return pl.pallas_call(
        _outproj_kernel,
        grid=(m // tm,),
        in_specs=[row, half, half,
                  pl.BlockSpec((D_MODEL, D_MODEL), lambda i: (0, 0), pipeline_mode=pl.Buffered(1))],
        out_specs=row,
        out_shape=jax.ShapeDtypeStruct((m, D_MODEL), F32),
        compiler_params=_cparams(("arbitrary",)),
        name="out_proj",
    )(x, oa, orw, w)


def _bucket_table():
    qi = np.arange(BLOCK)[:, None]
    kj = np.arange(2 * BLOCK)[None, :]
    dist = BLOCK + qi - kj
    n = np.maximum(dist, 0)
    max_exact = N_BUCKETS // 2
    nf = np.maximum(n, 1).astype(np.float32)
    large = max_exact + (np.log(nf / np.float32(max_exact)) / np.float32(math.log(MAX_DISTANCE / max_exact))
                         * np.float32(N_BUCKETS - max_exact)).astype(np.int32)
    large = np.minimum(large, N_BUCKETS - 1)
    bucket = np.where(n < max_exact, n, large).astype(np.int32)
    valid = (dist >= 0) & (dist <= WINDOW)
    return np.where(valid, bucket, -1).astype(np.int32)


def _bias_kernel(bucket_ref, rb_ref, o_ref):
    bucket = bucket_ref[...]
    kj = lax.broadcasted_iota(jnp.int32, (BLOCK, 2 * BLOCK), 1)
    for h in range(N_Q_HEADS):
        acc = jnp.full((BLOCK, 2 * BLOCK), NEG_BIG, F32)
        for n in range(N_BUCKETS):
            acc = jnp.where(bucket == n, rb_ref[n, h], acc)
        o_ref[0, h] = jnp.where(kj >= BLOCK, acc, NEG_BIG)
        o_ref[1, h] = acc


def _bias_table(rel_bias):
    bucket = jnp.asarray(_bucket_table())
    return pl.pallas_call(
        _bias_kernel,
        in_specs=[pl.BlockSpec(memory_space=pltpu.VMEM), pl.BlockSpec(memory_space=pltpu.SMEM)],
        out_specs=pl.BlockSpec(memory_space=pltpu.VMEM),
        out_shape=jax.ShapeDtypeStruct((2, N_Q_HEADS, BLOCK, 2 * BLOCK), F32),
        name="bias_table",
    )(bucket, rel_bias)


def _attn_kernel(sink_ref, q_ref, kvp_ref, kvc_ref, bias_ref, o_ref):
    kvp = kvp_ref[...]
    kvc = kvc_ref[...]
    half = N_KV_HEADS // 2
    for g0 in range(0, N_KV_HEADS, half):
        kcat, vcat = {}, {}
        for g in range(g0, g0 + half):
            ks = slice(g * HEAD_DIM, (g + 1) * HEAD_DIM)
            vs = slice(KV_WIDTH + g * HEAD_DIM, KV_WIDTH + (g + 1) * HEAD_DIM)
            kcat[g] = jnp.concatenate([kvp[:, ks], kvc[:, ks]], axis=0).astype(BF16)
            vcat[g] = jnp.concatenate([kvp[:, vs], kvc[:, vs]], axis=0).astype(BF16)
        heads = range(g0 * GQA_GROUP, (g0 + half) * GQA_GROUP)
        hs = {h: slice(h * HEAD_DIM, (h + 1) * HEAD_DIM) for h in heads}
        s = {h: _dg(q_ref[:, hs[h]], kcat[h // GQA_GROUP], _NT) + bias_ref[h] for h in heads}
        m = {h: jnp.maximum(jnp.max(s[h], axis=-1, keepdims=True), sink_ref[h]) for h in heads}
        p = {h: jnp.exp(s[h] - m[h]) for h in heads}
        denom = {h: jnp.sum(p[h], axis=-1, keepdims=True) + jnp.exp(sink_ref[h] - m[h]) for h in heads}
        o = {h: jnp.dot(p[h].astype(BF16), vcat[h // GQA_GROUP], preferred_element_type=F32) for h in heads}
        for h in heads:
            o_ref[:, hs[h]] = (o[h] / denom[h]).astype(BF16)


def _attn_prompt(q, kv, bias, sinks, *, batch, seq):
    nb = seq // BLOCK
    q3 = q.reshape(batch, seq, ATT_WIDTH)
    kv3 = kv.reshape(batch, seq, 2 * KV_WIDTH)
    out = pl.pallas_call(
        _attn_kernel,
        grid=(batch, nb),
        in_specs=[pl.BlockSpec(memory_space=pltpu.SMEM),
                  pl.BlockSpec((None, BLOCK, ATT_WIDTH), lambda b, j: (b, j, 0)),
                  pl.BlockSpec((None, BLOCK, 2 * KV_WIDTH), lambda b, j: (b, jnp.maximum(j - 1, 0), 0)),
                  pl.BlockSpec((None, BLOCK, 2 * KV_WIDTH), lambda b, j: (b, j, 0)),
                  pl.BlockSpec((None, N_Q_HEADS, BLOCK, 2 * BLOCK), lambda b, j: (jnp.minimum(j, 1), 0, 0, 0))],
        out_specs=pl.BlockSpec((None, BLOCK, ATT_WIDTH), lambda b, j: (b, j, 0)),
        out_shape=jax.ShapeDtypeStruct((batch, seq, ATT_WIDTH), BF16),
        compiler_params=_cparams(("arbitrary", "arbitrary")),
        name="attn_prompt",
    )(sinks, q3, kv3, kv3, bias)
    return out.reshape(batch * seq, ATT_WIDTH)


def _attn_decode_kernel(sink_ref, q_ref, kvn_ref, ck_ref, cv_ref, bias_ref, o_ref):
    q = q_ref[...].astype(F32)
    kvn = kvn_ref[...]
    for g in range(N_KV_HEADS):
        ks = slice(g * HEAD_DIM, (g + 1) * HEAD_DIM)
        vs = slice(KV_WIDTH + g * HEAD_DIM, KV_WIDTH + (g + 1) * HEAD_DIM)
        hs = slice(g * GQA_GROUP, (g + 1) * GQA_GROUP)
        qg = q[:, hs, :]
        kc = ck_ref[:, :, ks]
        vc = cv_ref[:, :, ks]
        kn = kvn[:, ks].astype(BF16).astype(F32)[:, None, :]
        vn = kvn[:, vs].astype(BF16).astype(F32)[:, None, :]
        s = jnp.einsum("bqd,bkd->bqk", qg.astype(BF16), kc.astype(BF16), preferred_element_type=F32)
        s = s + bias_ref[hs, 0:WINDOW][None]
        s_new = jnp.sum(qg * kn, axis=-1, keepdims=True) + bias_ref[hs, WINDOW:WINDOW + 1][None]
        sink = sink_ref[hs, 0:1][None]
        m = jnp.maximum(jnp.maximum(jnp.max(s, axis=-1, keepdims=True), s_new), sink)
        p = jnp.exp(s - m)
        p_new = jnp.exp(s_new - m)
        denom = jnp.sum(p, axis=-1, keepdims=True) + p_new + jnp.exp(sink - m)
        o = jnp.einsum("bqk,bkd->bqd", p.astype(BF16), vc.astype(BF16), preferred_element_type=F32)
        o = o + p_new * vn
        o_ref[:, hs, :] = o / denom


def _attn_decode(q, kvn, cache_k, cache_v, bias_row, sinks_b):
    b = q.shape[0]
    lw = cache_k.shape[1]
    vm = pl.BlockSpec(memory_space=pltpu.VMEM)
    out = pl.pallas_call(
        _attn_decode_kernel,
        in_specs=[vm, vm, vm, vm, vm, vm],
        out_specs=vm,
        out_shape=jax.ShapeDtypeStruct((b, N_Q_HEADS, HEAD_DIM), F32),
        compiler_params=pltpu.CompilerParams(vmem_limit_bytes=VMEM_LIMIT_BYTES),
        name="attn_decode",
    )(sinks_b, q.reshape(b, N_Q_HEADS, HEAD_DIM), kvn,
      cache_k.reshape(b, lw, KV_WIDTH), cache_v.reshape(b, lw, KV_WIDTH), bias_row)
    return out.reshape(b, ATT_WIDTH)


def _seg_expand(x, bd_ref):
    rows = x.shape[0]
    groups = x.shape[1] // LANES
    xs = jnp.concatenate([x[:, i * LANES:(i + 1) * LANES] for i in range(groups)], axis=0)
    s = _mm(xs, bd_ref[...])
    return jnp.concatenate([s[i * rows:(i + 1) * rows] for i in range(groups)], axis=1)


def _rwkv_front(x, prev, p):
    mu_ref, w0_ref, a0_ref, kk_ref, ka_ref, lw_ref, g2_ref, bd_ref = p
    xm = x + mu_ref[...] * (prev - x)
    o3 = 3 * RWKV_WIDTH
    r = xm[:, 0:RWKV_WIDTH]
    k = xm[:, RWKV_WIDTH:2 * RWKV_WIDTH]
    v = xm[:, 2 * RWKV_WIDTH:o3]
    wa = xm[:, o3:o3 + LANES]
    lane = lax.broadcasted_iota(jnp.int32, wa.shape, 1)
    z = jnp.where(lane < DECAY_LORA, jnp.tanh(wa), wa)
    lora = _mm(z, lw_ref[...])
    w = -_softplus(-(w0_ref[...] + lora[:, 0:RWKV_WIDTH])) - 0.5
    logw = -jnp.exp(w)
    a = _sigmoid(a0_ref[...] + lora[:, RWKV_WIDTH:2 * RWKV_WIDTH])
    g = _mm(_sigmoid(xm[:, o3 + LANES:SHIFT_COLS]), g2_ref[...])
    kk = k * kk_ref[...]
    kmod = k * (1.0 + (a - 1.0) * ka_ref[...])
    nrm = jnp.sqrt(_seg_expand(kk * kk, bd_ref))
    kk = kk / jnp.maximum(nrm, 1e-12)
    return r, logw, kmod, v, kk, kk * a, g


def _rwkv_back(y, r, kmod, v, g, q):
    rk_ref, lnw_ref, lnb_ref, bd_ref = q
    mean = _seg_expand(y, bd_ref) * (1.0 / HEAD_DIM)
    d = y - mean
    var = _seg_expand(d * d, bd_ref) * (1.0 / HEAD_DIM)
    yn = d * lax.rsqrt(var + GN_EPS) * lnw_ref[...] + lnb_ref[...]
    bonus = _seg_expand(r * kmod * rk_ref[...], bd_ref) * v
    return (yn + bonus) * g


def _rwkv_chunk_kernel(rw_ref, mu_ref, w0_ref, a0_ref, kk_ref, ka_ref, lw_ref, g2_ref, bd_ref,
                       rk_ref, lnw_ref, lnb_ref,
                       o_ref, s_out_ref,
                       s_scr, prev_scr, y_scr, ea_scr, er_scr, eb_scr, ek_scr, be_scr, ke_scr, v_scr):
    c = pl.program_id(1)
    C = CHUNK

    @pl.when(c == 0)
    def _():
        s_scr[...] = jnp.zeros_like(s_scr)
        prev_scr[...] = jnp.zeros_like(prev_scr)

    x = rw_ref[...]
    row = lax.broadcasted_iota(jnp.int32, x.shape, 0)
    prev = jnp.where(row == 0, prev_scr[...], pltpu.roll(x, 1, axis=0))
    prev_scr[...] = x[C - 1:C, :]
    front = (mu_ref, w0_ref, a0_ref, kk_ref, ka_ref, lw_ref, g2_ref, bd_ref)
    r, logw, kmod, v, kk, bb, g = _rwkv_front(x, prev, front)

    ti = lax.broadcasted_iota(jnp.int32, (C, C), 0)
    si = lax.broadcasted_iota(jnp.int32, (C, C), 1)
    cum = _mm2r(jnp.where(si <= ti, 1.0, 0.0).astype(BF16), logw)
    cum_end = cum[C - 1:C, :]
    e_in = jnp.exp(cum)
    e_neg = jnp.exp(-cum)
    e_end = jnp.exp(cum_end - cum)
    ea_scr[...] = kk * jnp.exp(cum - logw)
    er_scr[...] = r * e_in
    eb_scr[...] = bb * e_neg
    ek_scr[...] = kmod * e_neg
    be_scr[...] = bb * e_end
    ke_scr[...] = kmod * e_end
    v_scr[...] = v
    w_end = jnp.exp(cum_end)

    lane = lax.broadcasted_iota(jnp.int32, (C, LANES), 1)
    lo_half = lane < HEAD_DIM
    ri = lax.broadcasted_iota(jnp.int32, (LANES, LANES), 0)
    ci = lax.broadcasted_iota(jnp.int32, (LANES, LANES), 1)
    same_head = jnp.right_shift(ri, 6) == jnp.right_shift(ci, 6)
    eye = ri == ci
    t_loc = jnp.bitwise_and(ri, C - 1)
    s_loc = jnp.bitwise_and(ci, C - 1)
    bd_strict = jnp.logical_and(same_head, s_loc < t_loc)
    bd_incl = jnp.logical_and(same_head, s_loc <= t_loc)

    def swap(t):
        return pltpu.roll(t, HEAD_DIM, axis=1)

    def stack2(t):
        return jnp.concatenate([jnp.where(lo_half, t, 0.0), jnp.where(lo_half, 0.0, t)], axis=0)

    def stack2x(t):
        ts = swap(t)
        return jnp.concatenate([jnp.where(lo_half, 0.0, ts), jnp.where(lo_half, ts, 0.0)], axis=0)

    def own(t):
        return jnp.where(lo_half, t[0:C], t[C:2 * C])

    def other(t):
        return swap(jnp.where(lo_half, t[C:2 * C], t[0:C]))

    pairs = range(HEAD_PAIRS)
    sl = [slice(p * LANES, (p + 1) * LANES) for p in pairs]
    a2 = [stack2(ea_scr[:, sl[p]]) for p in pairs]
    m = []
    for p in pairs:
        r2 = stack2(er_scr[:, sl[p]])
        b_p = eb_scr[:, sl[p]]
        k_p = ek_scr[:, sl[p]]
        m.append(_mm(jnp.concatenate([a2[p], r2], axis=0), jnp.concatenate([b_p, b_p, k_p, k_p], axis=0), _NT))
    l_ak = [jnp.where(bd_strict, m[p][0:2 * C, LANES:2 * LANES], 0.0) for p in pairs]
    pw = [-jnp.where(bd_strict, m[p][0:2 * C, 0:LANES], 0.0) for p in pairs]
    lrbk = [jnp.concatenate([jnp.where(bd_incl, m[p][2 * C:4 * C, 0:LANES], 0.0),
                             jnp.where(bd_incl, m[p][2 * C:4 * C, LANES:2 * LANES], 0.0)], axis=1) for p in pairs]
    v2x = [stack2x(v_scr[:, sl[p]]) for p in pairs]
    xc = [a2[p] - _mm(l_ak[p], v2x[p]) for p in pairs]
    for it in range(6):
        for p in pairs:
            if it < 5:
                px = _mm(pw[p], jnp.concatenate([xc[p], pw[p]], axis=1))
                pw[p] = px[:, LANES:2 * LANES]
                xc[p] = xc[p] + px[:, 0:LANES]
            else:
                xc[p] = xc[p] + _mm(pw[p], xc[p])
    xv = [jnp.concatenate([xc[p], v2x[p]], axis=0) for p in pairs]
    lb = [_mm(lrbk[p], xv[p]) for p in pairs]
    z = [_mm(xv[p], jnp.concatenate([stack2(be_scr[:, sl[p]]), stack2(ke_scr[:, sl[p]])], axis=0), _TN)
         for p in pairs]
    rt = [er_scr[:, sl[p]] - own(lb[p]) for p in pairs]
    yl = [other(lb[p]) for p in pairs]
    d = [jnp.where(eye, w_end[:, sl[p]], 0.0) - jnp.where(same_head, z[p], 0.0) for p in pairs]
    gg = [jnp.where(same_head, jnp.concatenate([z[p][C:2 * C], z[p][0:C]], axis=0), 0.0) for p in pairs]
    s0 = [s_scr[p] for p in pairs]
    ys = [_mm(rt[p], s0[p], _NT) for p in pairs]
    for p in pairs:
        y_scr[:, sl[p]] = yl[p] + ys[p]
    sn = [_mm(s0[p], d[p]) for p in pairs]
    for p in pairs:
        s_scr[p] = sn[p] + gg[p]

    back = (rk_ref, lnw_ref, lnb_ref, bd_ref)
    o_ref[...] = _rwkv_back(y_scr[...], r, kmod, v, g, back).astype(BF16)

    @pl.when(c == pl.num_programs(1) - 1)
    def _():
        for p in range(HEAD_PAIRS):
            s_out_ref[2 * p] = s_scr[p, 0:HEAD_DIM, 0:HEAD_DIM]
            s_out_ref[2 * p + 1] = s_scr[p, HEAD_DIM:LANES, HEAD_DIM:LANES]


def _rwkv_prompt(rw, params, *, batch, seq):
    rw3 = rw.reshape(batch, seq, SHIFT_COLS)
    nc = seq // CHUNK
    const = lambda shape: pl.BlockSpec(shape, lambda b, c: tuple(0 for _ in shape))
    in_specs = [pl.BlockSpec((None, CHUNK, SHIFT_COLS), lambda b, c: (b, c, 0))]
    in_specs += [const(p.shape) for p in params]
    wide = pltpu.VMEM((CHUNK, RWKV_WIDTH), F32)
    o, s_out = pl.pallas_call(
        _rwkv_chunk_kernel,
        grid=(batch, nc),
        in_specs=in_specs,
        out_specs=[pl.BlockSpec((None, CHUNK, RWKV_WIDTH), lambda b, c: (b, c, 0)),
                   pl.BlockSpec((None, RWKV_HEADS, HEAD_DIM, HEAD_DIM), lambda b, c: (b, 0, 0, 0))],
        out_shape=[jax.ShapeDtypeStruct((batch, seq, RWKV_WIDTH), BF16),
                   jax.ShapeDtypeStruct((batch, RWKV_HEADS, HEAD_DIM, HEAD_DIM), F32)],
        scratch_shapes=[pltpu.VMEM((HEAD_PAIRS, LANES, LANES), F32),
                        pltpu.VMEM((1, SHIFT_COLS), F32),
                        wide, wide, wide, wide, wide, wide, wide, wide],
        compiler_params=_cparams(("arbitrary", "arbitrary")),
        name="rwkv_prompt",
    )(rw3, *params)
    return o.reshape(batch * seq, RWKV_WIDTH), s_out


def _rwkv_dec_front_kernel(rw_ref, prev_ref, mu_ref, w0_ref, a0_ref, kk_ref, ka_ref, lw_ref, g2_ref, bd_ref,
                           r_ref, w_ref, k_ref, v_ref, kkn_ref, b_ref, g_ref):
    front = (mu_ref, w0_ref, a0_ref, kk_ref, ka_ref, lw_ref, g2_ref, bd_ref)
    r, logw, kmod, v, kk, bb, g = _rwkv_front(rw_ref[...], prev_ref[...], front)
    r_ref[...] = r
    w_ref[...] = jnp.exp(logw)
    k_ref[...] = kmod
    v_ref[...] = v
    kkn_ref[...] = kk
    b_ref[...] = bb
    g_ref[...] = g


def _rwkv_dec_state_kernel(r_ref, w_ref, k_ref, v_ref, kk_ref, b_ref, s_ref, y_ref, so_ref):
    ri = lax.broadcasted_iota(jnp.int32, (HEAD_DIM, HEAD_DIM), 0)
    ci = lax.broadcasted_iota(jnp.int32, (HEAD_DIM, HEAD_DIM), 1)
    eye = ri == ci
    ones = jnp.ones((HEAD_DIM, LANES), BF16)
    heads = range(RWKV_HEADS)
    hs = [slice(h * HEAD_DIM, (h + 1) * HEAD_DIM) for h in heads]
    n = RWKV_HEADS * HEAD_DIM
    s = [s_ref[h] for h in heads]
    lhs = [s[h] * kk_ref[:, hs[h]] for h in heads] + [jnp.where(eye, v_ref[:, hs[h]], 0.0) for h in heads]
    red = _mm2(jnp.concatenate(lhs, axis=0), ones)
    s_new = []
    for h in heads:
        sa = -red[h * HEAD_DIM:(h + 1) * HEAD_DIM, 0:HEAD_DIM]
        v_col = red[n + h * HEAD_DIM:n + (h + 1) * HEAD_DIM, 0:HEAD_DIM]
        s_new.append(s[h] * w_ref[:, hs[h]] + sa * b_ref[:, hs[h]] + v_col * k_ref[:, hs[h]])
        so_ref[h] = s_new[h]
    yb = _mm2(jnp.concatenate([s_new[h] * r_ref[:, hs[h]] for h in heads], axis=0), ones)
    for h in heads:
        y_blk = yb[h * HEAD_DIM:(h + 1) * HEAD_DIM, 0:HEAD_DIM]
        y_ref[:, hs[h]] = jnp.sum(jnp.where(eye, y_blk, 0.0), axis=0, keepdims=True)


def _rwkv_dec_back_kernel(y_ref, r_ref, k_ref, v_ref, g_ref, rk_ref, lnw_ref, lnb_ref, bd_ref, o_ref):
    back = (rk_ref, lnw_ref, lnb_ref, bd_ref)
    o_ref[...] = _rwkv_back(y_ref[...], r_ref[...], k_ref[...], v_ref[...], g_ref[...], back).astype(BF16)


def _rwkv_decode(rw, shift0, wkv0, front_params, back_params):
    b = rw.shape[0]
    vm = pl.BlockSpec(memory_space=pltpu.VMEM)
    wide = jax.ShapeDtypeStruct((b, RWKV_WIDTH), F32)
    r, w, k, v, kk, bb, g = pl.pallas_call(
        _rwkv_dec_front_kernel,
        in_specs=[vm] * (2 + len(front_params)),
        out_specs=[vm] * 7,
        out_shape=[wide] * 7,
        name="rwkv_dec_front",
    )(rw, shift0, *front_params)
    rowspec = pl.BlockSpec((None, 1, RWKV_WIDTH), lambda i: (i, 0, 0))
    stspec = pl.BlockSpec((None, RWKV_HEADS, HEAD_DIM, HEAD_DIM), lambda i: (i, 0, 0, 0))
    as_rows = lambda t: t.reshape(b, 1, RWKV_WIDTH)
    y, s_new = pl.pallas_call(
        _rwkv_dec_state_kernel,
        grid=(b,),
        in_specs=[rowspec] * 6 + [stspec],
        out_specs=[rowspec, stspec],
        out_shape=[jax.ShapeDtypeStruct((b, 1, RWKV_WIDTH), F32),
                   jax.ShapeDtypeStruct((b, RWKV_HEADS, HEAD_DIM, HEAD_DIM), F32)],
        compiler_params=_cparams(("arbitrary",)),
        name="rwkv_dec_state",
    )(as_rows(r), as_rows(w), as_rows(k), as_rows(v), as_rows(kk), as_rows(bb), wkv0)
    o = pl.pallas_call(
        _rwkv_dec_back_kernel,
        in_specs=[vm] * (5 + len(back_params)),
        out_specs=vm,
        out_shape=jax.ShapeDtypeStruct((b, RWKV_WIDTH), BF16),
        name="rwkv_dec_back",
    )(y.reshape(b, RWKV_WIDTH), r, k, v, g, *back_params)
    return o, s_new


def kernel(x_prompt, x_sample, cache_k, cache_v, state_wkv, state_shift, rel_bias, ffn1_norm, ffn1_w_gate, ffn1_w_up, ffn1_w_down, mix_norm, w_in, attn_sinks, shift_mu, decay_w0, decay_w2, aaa_a0, aaa_a2, gate_g2, key_k, key_a, bonus_r_k, ln_x_w, ln_x_b, w_out, ffn2_norm, ffn2_w_gate, ffn2_w_up, ffn2_w_down, final_norm):
    batch, seq, _ = x_prompt.shape
    dec_b = x_sample.shape[0]
    lw = cache_k.shape[2]
    l = 0
    row = lambda t: t.reshape(1, -1)

    f1 = (ffn1_w_gate[l].astype(BF16), ffn1_w_up[l].astype(BF16), ffn1_w_down[l].astype(BF16))
    f2 = (ffn2_w_gate[l].astype(BF16), ffn2_w_up[l].astype(BF16), ffn2_w_down[l].astype(BF16))
    w_in_b = w_in[l].astype(BF16)
    w_out_b = w_out[l].astype(BF16)
    lora_w = jnp.zeros((LANES, 2 * RWKV_WIDTH), F32)
    lora_w = lora_w.at[0:DECAY_LORA, 0:RWKV_WIDTH].set(decay_w2[l])
    lora_w = lora_w.at[DECAY_LORA:LANES, RWKV_WIDTH:].set(aaa_a2[l])
    head_of = np.arange(LANES) // HEAD_DIM
    bd_ones = jnp.asarray(head_of[:, None] == head_of[None, :], BF16)
    front_params = (row(shift_mu[l]), row(decay_w0[l]), row(aaa_a0[l]), row(key_k[l]), row(key_a[l]),
                    lora_w.astype(BF16), gate_g2[l].astype(BF16), bd_ones)
    back_params = (row(bonus_r_k[l]), row(ln_x_w[l]), row(ln_x_b[l]), bd_ones)

    bias = _bias_table(rel_bias)
    sinks = attn_sinks[l]

    xp = x_prompt.reshape(batch * seq, D_MODEL)
    x1, h1 = _ffn(xp, row(ffn1_norm[l]), *f1, row(mix_norm[l]), tm=512, emit_x=True, n_dtype=BF16)
    q, kv, rw = _proj(h1, w_in_b, tm=256)
    o_att = _attn_prompt(q, kv, bias, sinks, batch=batch, seq=seq)
    o_rw, s_p = _rwkv_prompt(rw, front_params + back_params[:3], batch=batch, seq=seq)
    x2 = _outproj(x1, o_att, o_rw, w_out_b, tm=512)
    (y_p,) = _ffn(x2, row(ffn2_norm[l]), *f2, row(final_norm), tm=512, emit_x=False, n_dtype=F32)

    xs = x_sample.reshape(dec_b, D_MODEL)
    x1s, h1s = _ffn(xs, row(ffn1_norm[l]), *f1, row(mix_norm[l]), tm=dec_b, emit_x=True, n_dtype=BF16)
    qs, kvs, rws = _proj(h1s, w_in_b, tm=dec_b)
    sinks_b = jnp.broadcast_to(sinks[:, None], (N_Q_HEADS, LANES))
    o_att_s = _attn_decode(qs, kvs, cache_k[l], cache_v[l], bias[1, :, 0, :], sinks_b)
    o_rw_s, s_s = _rwkv_decode(rws, state_shift[l], state_wkv[l], front_params, back_params)
    x2s = _outproj(x1s, o_att_s.astype(BF16), o_rw_s, w_out_b, tm=dec_b)
    (y_s,) = _ffn(x2s, row(ffn2_norm[l]), *f2, row(final_norm), tm=dec_b, emit_x=False, n_dtype=F32)

    kv3 = kv.reshape(batch, seq, 2 * KV_WIDTH)
    lp = min(WINDOW, seq)
    new_k_p = kv3[:, seq - lp:, 0:KV_WIDTH].reshape(1, batch, lp, N_KV_HEADS, HEAD_DIM)
    new_v_p = kv3[:, seq - lp:, KV_WIDTH:].reshape(1, batch, lp, N_KV_HEADS, HEAD_DIM)
    new_shift_p = rw.reshape(batch, seq, SHIFT_COLS)[:, seq - 1][None]
    k_new = kvs[:, 0:KV_WIDTH].reshape(dec_b, 1, N_KV_HEADS, HEAD_DIM)
    v_new = kvs[:, KV_WIDTH:].reshape(dec_b, 1, N_KV_HEADS, HEAD_DIM)
    new_k_s = jnp.concatenate([cache_k[l], k_new], axis=1)[:, -lw:][None]
    new_v_s = jnp.concatenate([cache_v[l], v_new], axis=1)[:, -lw:][None]
    return (y_p.reshape(batch, seq, D_MODEL), y_s.reshape(dec_b, 1, D_MODEL),
            new_k_p, new_v_p, s_p[None], new_shift_p,
            new_k_s, new_v_s, s_s[None], rws[None])
```

```python
import functools
import math

import numpy as np
import jax
import jax.numpy as jnp
from jax import lax
from jax.experimental import pallas as pl
from jax.experimental.pallas import tpu as pltpu

F32 = jnp.float32
BF16 = jnp.bfloat16

D_MODEL = 2048
HEAD_DIM = 64
ATT_WIDTH = 1024
N_Q_HEADS = 16
N_KV_HEADS = 4
GQA_GROUP = 4
KV_WIDTH = 256
RWKV_WIDTH = 1024
RWKV_HEADS = 16
WINDOW = 128
BLOCK = 128
N_BUCKETS = 32
MAX_DISTANCE = 128
DECAY_LORA = 64
AAA_LORA = 64
GATE_LORA = 128
D_FF = 5504
ATT_COLS = ATT_WIDTH + 2 * KV_WIDTH
SHIFT_COLS = 3 * RWKV_WIDTH + DECAY_LORA + AAA_LORA + GATE_LORA
IN_COLS = ATT_COLS + SHIFT_COLS
RMS_EPS = 1e-5
GN_EPS = 64e-5
FFN_RES = 0.5

LANES = 128
VMEM_LIMIT_BYTES = 60 * 1024 * 1024

FF_TILE = 1024
FF_STEPS = -(-D_FF // FF_TILE)
FF_TAIL = D_FF - (FF_STEPS - 1) * FF_TILE
CHUNK = 64
HEAD_PAIRS = RWKV_HEADS // 2
NEG_BIG = -1e30

_NN = (((1,), (0,)), ((), ()))
_NT = (((1,), (1,)), ((), ()))
_TN = (((0,), (0,)), ((), ()))


def _dg(a, b, dims=_NN):
    return lax.dot_general(a, b, dims, preferred_element_type=F32)


def _mm(a, b, dims=_NN):
    return _dg(a.astype(BF16), b.astype(BF16), dims)


def _split2(x):
    hi = x.astype(BF16)
    lo = (x - hi.astype(F32)).astype(BF16)
    return hi, lo


def _mm3(a, b, dims=_NN):
    ah, al = _split2(a)
    bh, bl = _split2(b)
    return _dg(ah, bh, dims) + _dg(ah, bl, dims) + _dg(al, bh, dims)


def _mm2(a, b_bf16, dims=_NN):
    ah, al = _split2(a)
    return _dg(ah, b_bf16, dims) + _dg(al, b_bf16, dims)


def _mm2r(a_bf16, b, dims=_NN):
    bh, bl = _split2(b)
    return _dg(a_bf16, bh, dims) + _dg(a_bf16, bl, dims)


def _sigmoid(x):
    return 1.0 / (1.0 + jnp.exp(-x))


def _softplus(x):
    return jnp.maximum(x, 0.0) + jnp.log(1.0 + jnp.exp(-jnp.abs(x)))


def _rms(x, g):
    ms = jnp.mean(x * x, axis=-1, keepdims=True)
    return x * lax.rsqrt(ms + RMS_EPS) * g


def _cparams(sem):
    return pltpu.CompilerParams(dimension_semantics=sem, vmem_limit_bytes=VMEM_LIMIT_BYTES)


def _ffn_kernel(x_ref, g_ref, wg_ref, wu_ref, wd_ref, g2_ref, *rest, emit_x, n_dtype):
    if emit_x:
        ox_ref, on_ref, xn_scr, acc_scr = rest
    else:
        on_ref, xn_scr, acc_scr = rest
    j = pl.program_id(1)

    @pl.when(j == 0)
    def _():
        xn_scr[...] = _rms(x_ref[...], g_ref[...]).astype(BF16)
        acc_scr[...] = jnp.zeros_like(acc_scr)

    last = j == pl.num_programs(1) - 1

    def accumulate(width):
        xn = xn_scr[...]
        gate = jnp.dot(xn, wg_ref[:, 0:width], preferred_element_type=F32)
        up = jnp.dot(xn, wu_ref[:, 0:width], preferred_element_type=F32)
        h = (gate * _sigmoid(gate) * up).astype(BF16)
        acc_scr[...] += jnp.dot(h, wd_ref[0:width, :], preferred_element_type=F32)

    @pl.when(jnp.logical_not(last))
    def _():
        accumulate(FF_TILE)

    @pl.when(last)
    def _():
        accumulate(FF_TAIL)
        y = x_ref[...] + FFN_RES * acc_scr[...]
        if emit_x:
            ox_ref[...] = y
        on_ref[...] = _rms(y, g2_ref[...]).astype(n_dtype)


def _ffn(x, g, wg, wu, wd, g2, *, tm, emit_x, n_dtype):
    m = x.shape[0]
    grid = (m // tm, FF_STEPS)
    row = pl.BlockSpec((tm, D_MODEL), lambda i, j: (i, 0))
    vec = pl.BlockSpec((1, D_MODEL), lambda i, j: (0, 0))
    out_shape = [jax.ShapeDtypeStruct((m, D_MODEL), n_dtype)]
    out_specs = [row]
    if emit_x:
        out_shape = [jax.ShapeDtypeStruct((m, D_MODEL), F32)] + out_shape
        out_specs = [row, row]
    return pl.pallas_call(
        functools.partial(_ffn_kernel, emit_x=emit_x, n_dtype=n_dtype),
        grid=grid,
        in_specs=[row, vec,
                  pl.BlockSpec((D_MODEL, FF_TILE), lambda i, j: (0, j)),
                  pl.BlockSpec((D_MODEL, FF_TILE), lambda i, j: (0, j)),
                  pl.BlockSpec((FF_TILE, D_MODEL), lambda i, j: (j, 0)),
                  vec],
        out_specs=out_specs,
        out_shape=out_shape,
        scratch_shapes=[pltpu.VMEM((tm, D_MODEL), BF16), pltpu.VMEM((tm, D_MODEL), F32)],
        compiler_params=_cparams(("arbitrary", "arbitrary")),
        name="ffn",
    )(x, g, wg, wu, wd, g2)


def _proj_kernel(h_ref, w_ref, q_ref, kv_ref, rw_ref):
    h = h_ref[...]
    q = jnp.dot(h, w_ref[:, 0:ATT_WIDTH], preferred_element_type=F32)
    q_ref[...] = (q * (HEAD_DIM ** -0.5)).astype(BF16)
    kv_ref[...] = jnp.dot(h, w_ref[:, ATT_WIDTH:ATT_COLS], preferred_element_type=F32)
    rw_ref[...] = jnp.dot(h, w_ref[:, ATT_COLS:IN_COLS], preferred_element_type=F32)


def _proj(h, w, *, tm):
    m = h.shape[0]
    return pl.pallas_call(
        _proj_kernel,
        grid=(m // tm,),
        in_specs=[pl.BlockSpec((tm, D_MODEL), lambda i: (i, 0)),
                  pl.BlockSpec((D_MODEL, IN_COLS), lambda i: (0, 0), pipeline_mode=pl.Buffered(1))],
        out_specs=[pl.BlockSpec((tm, ATT_WIDTH), lambda i: (i, 0)),
                   pl.BlockSpec((tm, 2 * KV_WIDTH), lambda i: (i, 0)),
                   pl.BlockSpec((tm, SHIFT_COLS), lambda i: (i, 0))],
        out_shape=[jax.ShapeDtypeStruct((m, ATT_WIDTH), BF16),
                   jax.ShapeDtypeStruct((m, 2 * KV_WIDTH), F32),
                   jax.ShapeDtypeStruct((m, SHIFT_COLS), F32)],
        compiler_params=_cparams(("arbitrary",)),
        name="in_proj",
    )(h, w)


def _outproj_kernel(x_ref, oa_ref, orw_ref, w_ref, o_ref):
    acc = jnp.dot(oa_ref[...], w_ref[0:ATT_WIDTH, :], preferred_element_type=F32)
    acc += jnp.dot(orw_ref[...], w_ref[ATT_WIDTH:D_MODEL, :], preferred_element_type=F32)
    o_ref[...] = x_ref[...] + acc


def _outproj(x, oa, orw, w, *, tm):
    m = x.shape[0]
    row = pl.BlockSpec((tm, D_MODEL), lambda i: (i, 0))
    half = pl.BlockSpec((tm, ATT_WIDTH), lambda i: (i, 0))
    return pl.pallas_call(
        _outproj_kernel,
        grid=(m // tm,),
        in_specs=[row, half, half,
                  pl.BlockSpec((D_MODEL, D_MODEL), lambda i: (0, 0), pipeline_mode=pl.Buffered(1))],
        out_specs=row,
        out_shape=jax.ShapeDtypeStruct((m, D_MODEL), F32),
        compiler_params=_cparams(("arbitrary",)),
        name="out_proj",
    )(x, oa, orw, w)


def _bucket_table():
    qi = np.arange(BLOCK)[:, None]
    kj = np.arange(2 * BLOCK)[None, :]
    dist = BLOCK + qi - kj
    n = np.maximum(dist, 0)
    max_exact = N_BUCKETS // 2
    nf = np.maximum(n, 1).astype(np.float32)
    large = max_exact + (np.log(nf / np.float32(max_exact)) / np.float32(math.log(MAX_DISTANCE / max_exact))
                         * np.float32(N_BUCKETS - max_exact)).astype(np.int32)
    large = np.minimum(large, N_BUCKETS - 1)
    bucket = np.where(n < max_exact, n, large).astype(np.int32)
    valid = (dist >= 0) & (dist <= WINDOW)
    return np.where(valid, bucket, -1).astype(np.int32)


def _bias_kernel(bucket_ref, rb_ref, o_ref):
    bucket = bucket_ref[...]
    kj = lax.broadcasted_iota(jnp.int32, (BLOCK, 2 * BLOCK), 1)
    for h in range(N_Q_HEADS):
        acc = jnp.full((BLOCK, 2 * BLOCK), NEG_BIG, F32)
        for n in range(N_BUCKETS):
            acc = jnp.where(bucket == n, rb_ref[n, h], acc)
        o_ref[0, h] = jnp.where(kj >= BLOCK, acc, NEG_BIG)
        o_ref[1, h] = acc


def _bias_table(rel_bias):
    bucket = jnp.asarray(_bucket_table())
    return pl.pallas_call(
        _bias_kernel,
        in_specs=[pl.BlockSpec(memory_space=pltpu.VMEM), pl.BlockSpec(memory_space=pltpu.SMEM)],
        out_specs=pl.BlockSpec(memory_space=pltpu.VMEM),
        out_shape=jax.ShapeDtypeStruct((2, N_Q_HEADS, BLOCK, 2 * BLOCK), F32),
        name="bias_table",
    )(bucket, rel_bias)


def _attn_kernel(sink_ref, q_ref, kvp_ref, kvc_ref, bias_ref, o_ref):
    kvp = kvp_ref[...]
    kvc = kvc_ref[...]
    half = N_KV_HEADS // 2
    for g0 in range(0, N_KV_HEADS, half):
        kcat, vcat = {}, {}
        for g in range(g0, g0 + half):
            ks = slice(g * HEAD_DIM, (g + 1) * HEAD_DIM)
            vs = slice(KV_WIDTH + g * HEAD_DIM, KV_WIDTH + (g + 1) * HEAD_DIM)
            kcat[g] = jnp.concatenate([kvp[:, ks], kvc[:, ks]], axis=0).astype(BF16)
            vcat[g] = jnp.concatenate([kvp[:, vs], kvc[:, vs]], axis=0).astype(BF16)
        heads = range(g0 * GQA_GROUP, (g0 + half) * GQA_GROUP)
        hs = {h: slice(h * HEAD_DIM, (h + 1) * HEAD_DIM) for h in heads}
        s = {h: _dg(q_ref[:, hs[h]], kcat[h // GQA_GROUP], _NT) + bias_ref[h] for h in heads}
        m = {h: jnp.maximum(jnp.max(s[h], axis=-1, keepdims=True), sink_ref[h]) for h in heads}
        p = {h: jnp.exp(s[h] - m[h]) for h in heads}
        denom = {h: jnp.sum(p[h], axis=-1, keepdims=True) + jnp.exp(sink_ref[h] - m[h]) for h in heads}
        o = {h: jnp.dot(p[h].astype(BF16), vcat[h // GQA_GROUP], preferred_element_type=F32) for h in heads}
        for h in heads:
            o_ref[:, hs[h]] = (o[h] / denom[h]).astype(BF16)


def _attn_prompt(q, kv, bias, sinks, *, batch, seq):
    nb = seq // BLOCK
    q3 = q.reshape(batch, seq, ATT_WIDTH)
    kv3 = kv.reshape(batch, seq, 2 * KV_WIDTH)
    out = pl.pallas_call(
        _attn_kernel,
        grid=(batch, nb),
        in_specs=[pl.BlockSpec(memory_space=pltpu.SMEM),
                  pl.BlockSpec((None, BLOCK, ATT_WIDTH), lambda b, j: (b, j, 0)),
                  pl.BlockSpec((None, BLOCK, 2 * KV_WIDTH), lambda b, j: (b, jnp.maximum(j - 1, 0), 0)),
                  pl.BlockSpec((None, BLOCK, 2 * KV_WIDTH), lambda b, j: (b, j, 0)),
                  pl.BlockSpec((None, N_Q_HEADS, BLOCK, 2 * BLOCK), lambda b, j: (jnp.minimum(j, 1), 0, 0, 0))],
        out_specs=pl.BlockSpec((None, BLOCK, ATT_WIDTH), lambda b, j: (b, j, 0)),
        out_shape=jax.ShapeDtypeStruct((batch, seq, ATT_WIDTH), BF16),
        compiler_params=_cparams(("arbitrary", "arbitrary")),
        name="attn_prompt",
    )(sinks, q3, kv3, kv3, bias)
    return out.reshape(batch * seq, ATT_WIDTH)


def _attn_decode_kernel(sink_ref, q_ref, kvn_ref, ck_ref, cv_ref, bias_ref, o_ref):
    q = q_ref[...].astype(F32)
    kvn = kvn_ref[...]
    for g in range(N_KV_HEADS):
        ks = slice(g * HEAD_DIM, (g + 1) * HEAD_DIM)
        vs = slice(KV_WIDTH + g * HEAD_DIM, KV_WIDTH + (g + 1) * HEAD_DIM)
        hs = slice(g * GQA_GROUP, (g + 1) * GQA_GROUP)
        qg = q[:, hs, :]
        kc = ck_ref[:, :, ks]
        vc = cv_ref[:, :, ks]
        kn = kvn[:, ks].astype(BF16).astype(F32)[:, None, :]
        vn = kvn[:, vs].astype(BF16).astype(F32)[:, None, :]
        s = jnp.einsum("bqd,bkd->bqk", qg.astype(BF16), kc.astype(BF16), preferred_element_type=F32)
        s = s + bias_ref[hs, 0:WINDOW][None]
        s_new = jnp.sum(qg * kn, axis=-1, keepdims=True) + bias_ref[hs, WINDOW:WINDOW + 1][None]
        sink = sink_ref[hs, 0:1][None]
        m = jnp.maximum(jnp.maximum(jnp.max(s, axis=-1, keepdims=True), s_new), sink)
        p = jnp.exp(s - m)
        p_new = jnp.exp(s_new - m)
        denom = jnp.sum(p, axis=-1, keepdims=True) + p_new + jnp.exp(sink - m)
        o = jnp.einsum("bqk,bkd->bqd", p.astype(BF16), vc.astype(BF16), preferred_element_type=F32)
        o = o + p_new * vn
        o_ref[:, hs, :] = o / denom


def _attn_decode(q, kvn, cache_k, cache_v, bias_row, sinks_b):
    b = q.shape[0]
    lw = cache_k.shape[1]
    vm = pl.BlockSpec(memory_space=pltpu.VMEM)
    out = pl.pallas_call(
        _attn_decode_kernel,
        in_specs=[vm, vm, vm, vm, vm, vm],
        out_specs=vm,
        out_shape=jax.ShapeDtypeStruct((b, N_Q_HEADS, HEAD_DIM), F32),
        compiler_params=pltpu.CompilerParams(vmem_limit_bytes=VMEM_LIMIT_BYTES),
        name="attn_decode",
    )(sinks_b, q.reshape(b, N_Q_HEADS, HEAD_DIM), kvn,
      cache_k.reshape(b, lw, KV_WIDTH), cache_v.reshape(b, lw, KV_WIDTH), bias_row)
    return out.reshape(b, ATT_WIDTH)


def _seg_expand(x, bd_ref):
    rows = x.shape[0]
    groups = x.shape[1] // LANES
    xs = jnp.concatenate([x[:, i * LANES:(i + 1) * LANES] for i in range(groups)], axis=0)
    s = _mm(xs, bd_ref[...])
    return jnp.concatenate([s[i * rows:(i + 1) * rows] for i in range(groups)], axis=1)


def _rwkv_front(x, prev, p):
    mu_ref, w0_ref, a0_ref, kk_ref, ka_ref, lw_ref, g2_ref, bd_ref = p
    xm = x + mu_ref[...] * (prev - x)
    o3 = 3 * RWKV_WIDTH
    r = xm[:, 0:RWKV_WIDTH]
    k = xm[:, RWKV_WIDTH:2 * RWKV_WIDTH]
    v = xm[:, 2 * RWKV_WIDTH:o3]
    wa = xm[:, o3:o3 + LANES]
    lane = lax.broadcasted_iota(jnp.int32, wa.shape, 1)
    z = jnp.where(lane < DECAY_LORA, jnp.tanh(wa), wa)
    lora = _mm(z, lw_ref[...])
    w = -_softplus(-(w0_ref[...] + lora[:, 0:RWKV_WIDTH])) - 0.5
    logw = -jnp.exp(w)
    a = _sigmoid(a0_ref[...] + lora[:, RWKV_WIDTH:2 * RWKV_WIDTH])
    g = _mm(_sigmoid(xm[:, o3 + LANES:SHIFT_COLS]), g2_ref[...])
    kk = k * kk_ref[...]
    kmod = k * (1.0 + (a - 1.0) * ka_ref[...])
    nrm = jnp.sqrt(_seg_expand(kk * kk, bd_ref))
    kk = kk / jnp.maximum(nrm, 1e-12)
    return r, logw, kmod, v, kk, kk * a, g


def _rwkv_back(y, r, kmod, v, g, q):
    rk_ref, lnw_ref, lnb_ref, bd_ref = q
    mean = _seg_expand(y, bd_ref) * (1.0 / HEAD_DIM)
    d = y - mean
    var = _seg_expand(d * d, bd_ref) * (1.0 / HEAD_DIM)
    yn = d * lax.rsqrt(var + GN_EPS) * lnw_ref[...] + lnb_ref[...]
    bonus = _seg_expand(r * kmod * rk_ref[...], bd_ref) * v
    return (yn + bonus) * g


def _rwkv_chunk_kernel(rw_ref, mu_ref, w0_ref, a0_ref, kk_ref, ka_ref, lw_ref, g2_ref, bd_ref,
                       rk_ref, lnw_ref, lnb_ref,
                       o_ref, s_out_ref,
                       s_scr, prev_scr, y_scr, ea_scr, er_scr, eb_scr, ek_scr, be_scr, ke_scr, v_scr):
    c = pl.program_id(1)
    C = CHUNK

    @pl.when(c == 0)
    def _():
        s_scr[...] = jnp.zeros_like(s_scr)
        prev_scr[...] = jnp.zeros_like(prev_scr)

    x = rw_ref[...]
    row = lax.broadcasted_iota(jnp.int32, x.shape, 0)
    prev = jnp.where(row == 0, prev_scr[...], pltpu.roll(x, 1, axis=0))
    prev_scr[...] = x[C - 1:C, :]
    front = (mu_ref, w0_ref, a0_ref, kk_ref, ka_ref, lw_ref, g2_ref, bd_ref)
    r, logw, kmod, v, kk, bb, g = _rwkv_front(x, prev, front)

    ti = lax.broadcasted_iota(jnp.int32, (C, C), 0)
    si = lax.broadcasted_iota(jnp.int32, (C, C), 1)
    cum = _mm2r(jnp.where(si <= ti, 1.0, 0.0).astype(BF16), logw)
    cum_end = cum[C - 1:C, :]
    e_in = jnp.exp(cum)
    e_neg = jnp.exp(-cum)
    e_end = jnp.exp(cum_end - cum)
    ea_scr[...] = kk * jnp.exp(cum - logw)
    er_scr[...] = r * e_in
    eb_scr[...] = bb * e_neg
    ek_scr[...] = kmod * e_neg
    be_scr[...] = bb * e_end
    ke_scr[...] = kmod * e_end
    v_scr[...] = v
    w_end = jnp.exp(cum_end)

    lane = lax.broadcasted_iota(jnp.int32, (C, LANES), 1)
    lo_half = lane < HEAD_DIM
    ri = lax.broadcasted_iota(jnp.int32, (LANES, LANES), 0)
    ci = lax.broadcasted_iota(jnp.int32, (LANES, LANES), 1)
    same_head = jnp.right_shift(ri, 6) == jnp.right_shift(ci, 6)
    eye = ri == ci
    t_loc = jnp.bitwise_and(ri, C - 1)
    s_loc = jnp.bitwise_and(ci, C - 1)
    bd_strict = jnp.logical_and(same_head, s_loc < t_loc)
    bd_incl = jnp.logical_and(same_head, s_loc <= t_loc)

    def swap(t):
        return pltpu.roll(t, HEAD_DIM, axis=1)

    def stack2(t):
        return jnp.concatenate([jnp.where(lo_half, t, 0.0), jnp.where(lo_half, 0.0, t)], axis=0)

    def stack2x(t):
        ts = swap(t)
        return jnp.concatenate([jnp.where(lo_half, 0.0, ts), jnp.where(lo_half, ts, 0.0)], axis=0)

    def own(t):
        return jnp.where(lo_half, t[0:C], t[C:2 * C])

    def other(t):
        return swap(jnp.where(lo_half, t[C:2 * C], t[0:C]))

    pairs = range(HEAD_PAIRS)
    sl = [slice(p * LANES, (p + 1) * LANES) for p in pairs]
    a2 = [stack2(ea_scr[:, sl[p]]) for p in pairs]
    m = []
    for p in pairs:
        r2 = stack2(er_scr[:, sl[p]])
        b_p = eb_scr[:, sl[p]]
        k_p = ek_scr[:, sl[p]]
        m.append(_mm(jnp.concatenate([a2[p], r2], axis=0), jnp.concatenate([b_p, b_p, k_p, k_p], axis=0), _NT))
    l_ak = [jnp.where(bd_strict, m[p][0:2 * C, LANES:2 * LANES], 0.0) for p in pairs]
    pw = [-jnp.where(bd_strict, m[p][0:2 * C, 0:LANES], 0.0) for p in pairs]
    lrbk = [jnp.concatenate([jnp.where(bd_incl, m[p][2 * C:4 * C, 0:LANES], 0.0),
                             jnp.where(bd_incl, m[p][2 * C:4 * C, LANES:2 * LANES], 0.0)], axis=1) for p in pairs]
    v2x = [stack2x(v_scr[:, sl[p]]) for p in pairs]
    xc = [a2[p] - _mm(l_ak[p], v2x[p]) for p in pairs]
    for it in range(6):
        for p in pairs:
            if it < 5:
                px = _mm(pw[p], jnp.concatenate([xc[p], pw[p]], axis=1))
                pw[p] = px[:, LANES:2 * LANES]
                xc[p] = xc[p] + px[:, 0:LANES]
            else:
                xc[p] = xc[p] + _mm(pw[p], xc[p])
    xv = [jnp.concatenate([xc[p], v2x[p]], axis=0) for p in pairs]
    lb = [_mm(lrbk[p], xv[p]) for p in pairs]
    z = [_mm(xv[p], jnp.concatenate([stack2(be_scr[:, sl[p]]), stack2(ke_scr[:, sl[p]])], axis=0), _TN)
         for p in pairs]
    rt = [er_scr[:, sl[p]] - own(lb[p]) for p in pairs]
    yl = [other(lb[p]) for p in pairs]
    d = [jnp.where(eye, w_end[:, sl[p]], 0.0) - jnp.where(same_head, z[p], 0.0) for p in pairs]
    gg = [jnp.where(same_head, jnp.concatenate([z[p][C:2 * C], z[p][0:C]], axis=0), 0.0) for p in pairs]
    s0 = [s_scr[p] for p in pairs]
    ys = [_mm(rt[p], s0[p], _NT) for p in pairs]
    for p in pairs:
        y_scr[:, sl[p]] = yl[p] + ys[p]
    sn = [_mm(s0[p], d[p]) for p in pairs]
    for p in pairs:
        s_scr[p] = sn[p] + gg[p]

    back = (rk_ref, lnw_ref, lnb_ref, bd_ref)
    o_ref[...] = _rwkv_back(y_scr[...], r, kmod, v, g, back).astype(BF16)

    @pl.when(c == pl.num_programs(1) - 1)
    def _():
        for p in range(HEAD_PAIRS):
            s_out_ref[2 * p] = s_scr[p, 0:HEAD_DIM, 0:HEAD_DIM]
            s_out_ref[2 * p + 1] = s_scr[p, HEAD_DIM:LANES, HEAD_DIM:LANES]


def _rwkv_prompt(rw, params, *, batch, seq):
    rw3 = rw.reshape(batch, seq, SHIFT_COLS)
    nc = seq // CHUNK
    const = lambda shape: pl.BlockSpec(shape, lambda b, c: tuple(0 for _ in shape))
    in_specs = [pl.BlockSpec((None, CHUNK, SHIFT_COLS), lambda b, c: (b, c, 0))]
    in_specs += [const(p.shape) for p in params]
    wide = pltpu.VMEM((CHUNK, RWKV_WIDTH), F32)
    o, s_out = pl.pallas_call(
        _rwkv_chunk_kernel,
        grid=(batch, nc),
        in_specs=in_specs,
        out_specs=[pl.BlockSpec((None, CHUNK, RWKV_WIDTH), lambda b, c: (b, c, 0)),
                   pl.BlockSpec((None, RWKV_HEADS, HEAD_DIM, HEAD_DIM), lambda b, c: (b, 0, 0, 0))],
        out_shape=[jax.ShapeDtypeStruct((batch, seq, RWKV_WIDTH), BF16),
                   jax.ShapeDtypeStruct((batch, RWKV_HEADS, HEAD_DIM, HEAD_DIM), F32)],
        scratch_shapes=[pltpu.VMEM((HEAD_PAIRS, LANES, LANES), F32),
                        pltpu.VMEM((1, SHIFT_COLS), F32),
                        wide, wide, wide, wide, wide, wide, wide, wide],
        compiler_params=_cparams(("arbitrary", "arbitrary")),
        name="rwkv_prompt",
    )(rw3, *params)
    return o.reshape(batch * seq, RWKV_WIDTH), s_out


def _rwkv_dec_front_kernel(rw_ref, prev_ref, mu_ref, w0_ref, a0_ref, kk_ref, ka_ref, lw_ref, g2_ref, bd_ref,
                           r_ref, w_ref, k_ref, v_ref, kkn_ref, b_ref, g_ref):
    front = (mu_ref, w0_ref, a0_ref, kk_ref, ka_ref, lw_ref, g2_ref, bd_ref)
    r, logw, kmod, v, kk, bb, g = _rwkv_front(rw_ref[...], prev_ref[...], front)
    r_ref[...] = r
    w_ref[...] = jnp.exp(logw)
    k_ref[...] = kmod
    v_ref[...] = v
    kkn_ref[...] = kk
    b_ref[...] = bb
    g_ref[...] = g


def _rwkv_dec_state_kernel(r_ref, w_ref, k_ref, v_ref, kk_ref, b_ref, s_ref, y_ref, so_ref):
    ri = lax.broadcasted_iota(jnp.int32, (HEAD_DIM, HEAD_DIM), 0)
    ci = lax.broadcasted_iota(jnp.int32, (HEAD_DIM, HEAD_DIM), 1)
    eye = ri == ci
    ones = jnp.ones((HEAD_DIM, LANES), BF16)
    heads = range(RWKV_HEADS)
    hs = [slice(h * HEAD_DIM, (h + 1) * HEAD_DIM) for h in heads]
    n = RWKV_HEADS * HEAD_DIM
    s = [s_ref[h] for h in heads]
    lhs = [s[h] * kk_ref[:, hs[h]] for h in heads] + [jnp.where(eye, v_ref[:, hs[h]], 0.0) for h in heads]
    red = _mm2(jnp.concatenate(lhs, axis=0), ones)
    s_new = []
    for h in heads:
        sa = -red[h * HEAD_DIM:(h + 1) * HEAD_DIM, 0:HEAD_DIM]
        v_col = red[n + h * HEAD_DIM:n + (h + 1) * HEAD_DIM, 0:HEAD_DIM]
        s_new.append(s[h] * w_ref[:, hs[h]] + sa * b_ref[:, hs[h]] + v_col * k_ref[:, hs[h]])
        so_ref[h] = s_new[h]
    yb = _mm2(jnp.concatenate([s_new[h] * r_ref[:, hs[h]] for h in heads], axis=0), ones)
    for h in heads:
        y_blk = yb[h * HEAD_DIM:(h + 1) * HEAD_DIM, 0:HEAD_DIM]
        y_ref[:, hs[h]] = jnp.sum(jnp.where(eye, y_blk, 0.0), axis=0, keepdims=True)


def _rwkv_dec_back_kernel(y_ref, r_ref, k_ref, v_ref, g_ref, rk_ref, lnw_ref, lnb_ref, bd_ref, o_ref):
    back = (rk_ref, lnw_ref, lnb_ref, bd_ref)
    o_ref[...] = _rwkv_back(y_ref[...], r_ref[...], k_ref[...], v_ref[...], g_ref[...], back).astype(BF16)


def _rwkv_decode(rw, shift0, wkv0, front_params, back_params):
    b = rw.shape[0]
    vm = pl.BlockSpec(memory_space=pltpu.VMEM)
    wide = jax.ShapeDtypeStruct((b, RWKV_WIDTH), F32)
    r, w, k, v, kk, bb, g = pl.pallas_call(
        _rwkv_dec_front_kernel,
        in_specs=[vm] * (2 + len(front_params)),
        out_specs=[vm] * 7,
        out_shape=[wide] * 7,
        name="rwkv_dec_front",
    )(rw, shift0, *front_params)
    rowspec = pl.BlockSpec((None, 1, RWKV_WIDTH), lambda i: (i, 0, 0))
    stspec = pl.BlockSpec((None, RWKV_HEADS, HEAD_DIM, HEAD_DIM), lambda i: (i, 0, 0, 0))
    as_rows = lambda t: t.reshape(b, 1, RWKV_WIDTH)
    y, s_new = pl.pallas_call(
        _rwkv_dec_state_kernel,
        grid=(b,),
        in_specs=[rowspec] * 6 + [stspec],
        out_specs=[rowspec, stspec],
        out_shape=[jax.ShapeDtypeStruct((b, 1, RWKV_WIDTH), F32),
                   jax.ShapeDtypeStruct((b, RWKV_HEADS, HEAD_DIM, HEAD_DIM), F32)],
        compiler_params=_cparams(("arbitrary",)),
        name="rwkv_dec_state",
    )(as_rows(r), as_rows(w), as_rows(k), as_rows(v), as_rows(kk), as_rows(bb), wkv0)
    o = pl.pallas_call(
        _rwkv_dec_back_kernel,
        in_specs=[vm] * (5 + len(back_params)),
        out_specs=vm,
        out_shape=jax.ShapeDtypeStruct((b, RWKV_WIDTH), BF16),
        name="rwkv_dec_back",
    )(y.reshape(b, RWKV_WIDTH), r, k, v, g, *back_params)
    return o, s_new


def kernel(x_prompt, x_sample, cache_k, cache_v, state_wkv, state_shift, rel_bias, ffn1_norm, ffn1_w_gate, ffn1_w_up, ffn1_w_down, mix_norm, w_in, attn_sinks, shift_mu, decay_w0, decay_w2, aaa_a0, aaa_a2, gate_g2, key_k, key_a, bonus_r_k, ln_x_w, ln_x_b, w_out, ffn2_norm, ffn2_w_gate, ffn2_w_up, ffn2_w_down, final_norm):
    batch, seq, _ = x_prompt.shape
    dec_b = x_sample.shape[0]
    lw = cache_k.shape[2]
    l = 0
    row = lambda t: t.reshape(1, -1)

    f1 = (ffn1_w_gate[l].astype(BF16), ffn1_w_up[l].astype(BF16), ffn1_w_down[l].astype(BF16))
    f2 = (ffn2_w_gate[l].astype(BF16), ffn2_w_up[l].astype(BF16), ffn2_w_down[l].astype(BF16))
    w_in_b = w_in[l].astype(BF16)
    w_out_b = w_out[l].astype(BF16)
    lora_w = jnp.zeros((LANES, 2 * RWKV_WIDTH), F32)
    lora_w = lora_w.at[0:DECAY_LORA, 0:RWKV_WIDTH].set(decay_w2[l])
    lora_w = lora_w.at[DECAY_LORA:LANES, RWKV_WIDTH:].set(aaa_a2[l])
    head_of = np.arange(LANES) // HEAD_DIM
    bd_ones = jnp.asarray(head_of[:, None] == head_of[None, :], BF16)
    front_params = (row(shift_mu[l]), row(decay_w0[l]), row(aaa_a0[l]), row(key_k[l]), row(key_a[l]),
                    lora_w.astype(BF16), gate_g2[l].astype(BF16), bd_ones)
    back_params = (row(bonus_r_k[l]), row(ln_x_w[l]), row(ln_x_b[l]), bd_ones)

    bias = _bias_table(rel_bias)
    sinks = attn_sinks[l]

    xp = x_prompt.reshape(batch * seq, D_MODEL)
    x1, h1 = _ffn(xp, row(ffn1_norm[l]), *f1, row(mix_norm[l]), tm=512, emit_x=True, n_dtype=BF16)
    q, kv, rw = _proj(h1, w_in_b, tm=256)
    o_att = _attn_prompt(q, kv, bias, sinks, batch=batch, seq=seq)
    o_rw, s_p = _rwkv_prompt(rw, front_params + back_params[:3], batch=batch, seq=seq)
    x2 = _outproj(x1, o_att, o_rw, w_out_b, tm=512)
    (y_p,) = _ffn(x2, row(ffn2_norm[l]), *f2, row(final_norm), tm=512, emit_x=False, n_dtype=F32)

    xs = x_sample.reshape(dec_b, D_MODEL)
    x1s, h1s = _ffn(xs, row(ffn1_norm[l]), *f1, row(mix_norm[l]), tm=dec_b, emit_x=True, n_dtype=BF16)
    qs, kvs, rws = _proj(h1s, w_in_b, tm=dec_b)
    sinks_b = jnp.broadcast_to(sinks[:, None], (N_Q_HEADS, LANES))
    o_att_s = _attn_decode(qs, kvs, cache_k[l], cache_v[l], bias[1, :, 0, :], sinks_b)
    o_rw_s, s_s = _rwkv_decode(rws, state_shift[l], state_wkv[l], front_params, back_params)
    x2s = _outproj(x1s, o_att_s.astype(BF16), o_rw_s, w_out_b, tm=dec_b)
    (y_s,) = _ffn(x2s, row(ffn2_norm[l]), *f2, row(final_norm), tm=dec_b, emit_x=False, n_dtype=F32)

    kv3 = kv.reshape(batch, seq, 2 * KV_WIDTH)
    lp = min(WINDOW, seq)
    new_k_p = kv3[:, seq - lp:, 0:KV_WIDTH].reshape(1, batch, lp, N_KV_HEADS, HEAD_DIM)
    new_v_p = kv3[:, seq - lp:, KV_WIDTH:].reshape(1, batch, lp, N_KV_HEADS, HEAD_DIM)
    new_shift_p = rw.reshape(batch, seq, SHIFT_COLS)[:, seq - 1][None]
    k_new = kvs[:, 0:KV_WIDTH].reshape(dec_b, 1, N_KV_HEADS, HEAD_DIM)
    v_new = kvs[:, KV_WIDTH:].reshape(dec_b, 1, N_KV_HEADS, HEAD_DIM)
    new_k_s = jnp.concatenate([cache_k[l], k_new], axis=1)[:, -lw:][None]
    new_v_s = jnp.concatenate([cache_v[l], v_new], axis=1)[:, -lw:][None]
    return (y_p.reshape(batch, seq, D_MODEL), y_s.reshape(dec_b, 1, D_MODEL),
            new_k_p, new_v_p, s_p[None], new_shift_p,
            new_k_s, new_v_s, s_s[None], rws[None])
```

```python
import functools
import math

import numpy as np
import jax
import jax.numpy as jnp
from jax import lax
from jax.experimental import pallas as pl
from jax.experimental.pallas import tpu as pltpu

F32 = jnp.float32
BF16 = jnp.bfloat16

D_MODEL = 2048
HEAD_DIM = 64
ATT_WIDTH = 1024
N_Q_HEADS = 16
N_KV_HEADS = 4
GQA_GROUP = 4
KV_WIDTH = 256
RWKV_WIDTH = 1024
RWKV_HEADS = 16
WINDOW = 128
BLOCK = 128
N_BUCKETS = 32
MAX_DISTANCE = 128
DECAY_LORA = 64
AAA_LORA = 64
GATE_LORA = 128
D_FF = 5504
ATT_COLS = ATT_WIDTH + 2 * KV_WIDTH
SHIFT_COLS = 3 * RWKV_WIDTH + DECAY_LORA + AAA_LORA + GATE_LORA
IN_COLS = ATT_COLS + SHIFT_COLS
RMS_EPS = 1e-5
GN_EPS = 64e-5
FFN_RES = 0.5

LANES = 128
VMEM_LIMIT_BYTES = 60 * 1024 * 1024

FF_TILE = 1024
FF_TILE_CAST = 512
CHUNK = 64
HEAD_PAIRS = RWKV_HEADS // 2
NEG_BIG = -1e30
DECAY_SCALE = math.exp(-0.5)

_NN = (((1,), (0,)), ((), ()))
_NT = (((1,), (1,)), ((), ()))
_TN = (((0,), (0,)), ((), ()))


def _dg(a, b, dims=_NN):
    return lax.dot_general(a, b, dims, preferred_element_type=F32)


def _mm(a, b, dims=_NN):
    return _dg(a.astype(BF16), b.astype(BF16), dims)


def _split2(x):
    hi = x.astype(BF16)
    lo = (x - hi.astype(F32)).astype(BF16)
    return hi, lo


def _mm3(a, b, dims=_NN):
    ah, al = _split2(a)
    bh, bl = _split2(b)
    return _dg(ah, bh, dims) + _dg(ah, bl, dims) + _dg(al, bh, dims)


def _mm2(a, b_bf16, dims=_NN):
    ah, al = _split2(a)
    return _dg(ah, b_bf16, dims) + _dg(al, b_bf16, dims)


def _mm2r(a_bf16, b, dims=_NN):
    bh, bl = _split2(b)
    return _dg(a_bf16, bh, dims) + _dg(a_bf16, bl, dims)


def _sigmoid(x):
    return 1.0 / (1.0 + jnp.exp(-x))


def _rms(x, g):
    ms = jnp.mean(x * x, axis=-1, keepdims=True)
    return x * lax.rsqrt(ms + RMS_EPS) * g


def _cparams(sem):
    return pltpu.CompilerParams(dimension_semantics=sem, vmem_limit_bytes=VMEM_LIMIT_BYTES)


def _ffn_kernel(x_ref, g_ref, wg_ref, wu_ref, wd_ref, g2_ref, *rest, emit_x, n_dtype, tf, emit_w):
    rest = list(rest)
    ox_ref = rest.pop(0) if emit_x else None
    on_ref = rest.pop(0)
    wgo_ref, wuo_ref, wdo_ref = (rest.pop(0), rest.pop(0), rest.pop(0)) if emit_w else (None, None, None)
    xn_scr, acc_scr = rest
    j = pl.program_id(1)

    @pl.when(j == 0)
    def _():
        xn_scr[...] = _rms(x_ref[...], g_ref[...]).astype(BF16)
        acc_scr[...] = jnp.zeros_like(acc_scr)

    last = j == pl.num_programs(1) - 1

    def accumulate(width):
        wg = wg_ref[:, 0:width].astype(BF16)
        wu = wu_ref[:, 0:width].astype(BF16)
        wd = wd_ref[0:width, :].astype(BF16)
        if emit_w:
            wgo_ref[:, 0:width] = wg
            wuo_ref[:, 0:width] = wu
            wdo_ref[0:width, :] = wd
        xn = xn_scr[...]
        gate = jnp.dot(xn, wg, preferred_element_type=F32)
        up = jnp.dot(xn, wu, preferred_element_type=F32)
        h = (gate * _sigmoid(gate) * up).astype(BF16)
        acc_scr[...] += jnp.dot(h, wd, preferred_element_type=F32)

    @pl.when(jnp.logical_not(last))
    def _():
        accumulate(tf)

    @pl.when(last)
    def _():
        accumulate(D_FF - (pl.cdiv(D_FF, tf) - 1) * tf)
        y = x_ref[...] + FFN_RES * acc_scr[...]
        if emit_x:
            ox_ref[...] = y
        on_ref[...] = _rms(y, g2_ref[...]).astype(n_dtype)


def _ffn(x, g, wg, wu, wd, g2, *, tm, tf, emit_x, n_dtype, emit_w=False):
    m = x.shape[0]
    assert not emit_w or m == tm
    grid = (m // tm, pl.cdiv(D_FF, tf))
    row = pl.BlockSpec((tm, D_MODEL), lambda i, j: (i, 0))
    vec = pl.BlockSpec((1, D_MODEL), lambda i, j: (0, 0))
    w_in = pl.BlockSpec((D_MODEL, tf), lambda i, j: (0, j))
    w_dn = pl.BlockSpec((tf, D_MODEL), lambda i, j: (j, 0))
    out_shape = [jax.ShapeDtypeStruct((m, D_MODEL), n_dtype)]
    out_specs = [row]
    if emit_x:
        out_shape = [jax.ShapeDtypeStruct((m, D_MODEL), F32)] + out_shape
        out_specs = [row, row]
    if emit_w:
        out_shape += [jax.ShapeDtypeStruct(w.shape, BF16) for w in (wg, wu, wd)]
        out_specs += [w_in, w_in, w_dn]
    return pl.pallas_call(
        functools.partial(_ffn_kernel, emit_x=emit_x, n_dtype=n_dtype, tf=tf, emit_w=emit_w),
        grid=grid,
        in_specs=[row, vec, w_in, w_in, w_dn, vec],
        out_specs=out_specs,
        out_shape=out_shape,
        scratch_shapes=[pltpu.VMEM((tm, D_MODEL), BF16), pltpu.VMEM((tm, D_MODEL), F32)],
        compiler_params=_cparams(("arbitrary", "arbitrary")),
        name="ffn_cast" if emit_w else "ffn",
    )(x, g, wg, wu, wd, g2)


def _proj_kernel(h_ref, w_ref, q_ref, kv_ref, rw_ref):
    h = h_ref[...]
    q = jnp.dot(h, w_ref[:, 0:ATT_WIDTH], preferred_element_type=F32)
    q_ref[...] = (q * (HEAD_DIM ** -0.5)).astype(BF16)
    kv_ref[...] = jnp.dot(h, w_ref[:, ATT_WIDTH:ATT_COLS], preferred_element_type=F32)
    rw_ref[...] = jnp.dot(h, w_ref[:, ATT_COLS:IN_COLS], preferred_element_type=F32)


def _proj(h, w, *, tm):
    m = h.shape[0]
    return pl.pallas_call(
        _proj_kernel,
        grid=(m // tm,),
        in_specs=[pl.BlockSpec((tm, D_MODEL), lambda i: (i, 0)),
                  pl.BlockSpec((D_MODEL, IN_COLS), lambda i: (0, 0), pipeline_mode=pl.Buffered(1))],
        out_specs=[pl.BlockSpec((tm, ATT_WIDTH), lambda i: (i, 0)),
                   pl.BlockSpec((tm, 2 * KV_WIDTH), lambda i: (i, 0)),
                   pl.BlockSpec((tm, SHIFT_COLS), lambda i: (i, 0))],
        out_shape=[jax.ShapeDtypeStruct((m, ATT_WIDTH), BF16),
                   jax.ShapeDtypeStruct((m, 2 * KV_WIDTH), F32),
                   jax.ShapeDtypeStruct((m, SHIFT_COLS), F32)],
        compiler_params=_cparams(("arbitrary",)),
        name="in_proj",
    )(h, w)


def _outproj_kernel(x_ref, oa_ref, orw_ref, w_ref, o_ref):
    acc = jnp.dot(oa_ref[...], w_ref[0:ATT_WIDTH, :], preferred_element_type=F32)
    acc += jnp.dot(orw_ref[...], w_ref[ATT_WIDTH:D_MODEL, :], preferred_element_type=F32)
    o_ref[...] = x_ref[...] + acc


def _outproj(x, oa, orw, w, *, tm):
    m = x.shape[0]
    row = pl.BlockSpec((tm, D_MODEL), lambda i: (i, 0))
    half = pl.BlockSpec((tm, ATT_WIDTH), lambda i: (i, 0))
    return pl.pallas_call(
        _outproj_kernel,
        grid=(m // tm,),
        in_specs=[row, half, half,
                  pl.BlockSpec((D_MODEL, D_MODEL), lambda i: (0, 0), pipeline_mode=pl.Buffered(1))],
        out_specs=row,
        out_shape=jax.ShapeDtypeStruct((m, D_MODEL), F32),
        compiler_params=_cparams(("arbitrary",)),
        name="out_proj",
    )(x, oa, orw, w)


def _bucket_table():
    qi = np.arange(BLOCK)[:, None]
    kj = np.arange(2 * BLOCK)[None, :]
    dist = BLOCK + qi - kj
    n = np.maximum(dist, 0)
    max_exact = N_BUCKETS // 2
    nf = np.maximum(n, 1).astype(np.float32)
    large = max_exact + (np.log(nf / np.float32(max_exact)) / np.float32(math.log(MAX_DISTANCE / max_exact))
                         * np.float32(N_BUCKETS - max_exact)).astype(np.int32)
    large = np.minimum(large, N_BUCKETS - 1)
    bucket = np.where(n < max_exact, n, large).astype(np.int32)
    valid = (dist >= 0) & (dist <= WINDOW)
    return np.where(valid, bucket, -1).astype(np.int32)


def _bias_kernel(bucket_ref, rb_ref, o_ref):
    bucket = bucket_ref[...]
    kj = lax.broadcasted_iota(jnp.int32, (BLOCK, 2 * BLOCK), 1)
    for h in range(N_Q_HEADS):
        acc = jnp.full((BLOCK, 2 * BLOCK), NEG_BIG, F32)
        for n in range(N_BUCKETS):
            acc = jnp.where(bucket == n, rb_ref[n, h], acc)
        o_ref[0, h] = jnp.where(kj >= BLOCK, acc, NEG_BIG)
        o_ref[1, h] = acc


def _bias_table(rel_bias):
    bucket = jnp.asarray(_bucket_table())
    return pl.pallas_call(
        _bias_kernel,
        in_specs=[pl.BlockSpec(memory_space=pltpu.VMEM), pl.BlockSpec(memory_space=pltpu.SMEM)],
        out_specs=pl.BlockSpec(memory_space=pltpu.VMEM),
        out_shape=jax.ShapeDtypeStruct((2, N_Q_HEADS, BLOCK, 2 * BLOCK), F32),
        name="bias_table",
    )(bucket, rel_bias)


def _attn_kernel(sink_ref, q_ref, kvp_ref, kvc_ref, bias_ref, o_ref):
    kvp = kvp_ref[...]
    kvc = kvc_ref[...]
    half = N_KV_HEADS // 2
    for g0 in range(0, N_KV_HEADS, half):
        kcat, vcat = {}, {}
        for g in range(g0, g0 + half):
            ks = slice(g * HEAD_DIM, (g + 1) * HEAD_DIM)
            vs = slice(KV_WIDTH + g * HEAD_DIM, KV_WIDTH + (g + 1) * HEAD_DIM)
            kcat[g] = jnp.concatenate([kvp[:, ks], kvc[:, ks]], axis=0).astype(BF16)
            vcat[g] = jnp.concatenate([kvp[:, vs], kvc[:, vs]], axis=0).astype(BF16)
        heads = range(g0 * GQA_GROUP, (g0 + half) * GQA_GROUP)
        hs = {h: slice(h * HEAD_DIM, (h + 1) * HEAD_DIM) for h in heads}
        s = {h: _dg(q_ref[:, hs[h]], kcat[h // GQA_GROUP], _NT) + bias_ref[h] for h in heads}
        m = {h: jnp.maximum(jnp.max(s[h], axis=-1, keepdims=True), sink_ref[h]) for h in heads}
        p = {h: jnp.exp(s[h] - m[h]) for h in heads}
        denom = {h: jnp.sum(p[h], axis=-1, keepdims=True) + jnp.exp(sink_ref[h] - m[h]) for h in heads}
        o = {h: jnp.dot(p[h].astype(BF16), vcat[h // GQA_GROUP], preferred_element_type=F32) for h in heads}
        for h in heads:
            o_ref[:, hs[h]] = (o[h] / denom[h]).astype(BF16)


def _attn_prompt(q, kv, bias, sinks, *, batch, seq):
    nb = seq // BLOCK
    q3 = q.reshape(batch, seq, ATT_WIDTH)
    kv3 = kv.reshape(batch, seq, 2 * KV_WIDTH)
    out = pl.pallas_call(
        _attn_kernel,
        grid=(batch, nb),
        in_specs=[pl.BlockSpec(memory_space=pltpu.SMEM),
                  pl.BlockSpec((None, BLOCK, ATT_WIDTH), lambda b, j: (b, j, 0)),
                  pl.BlockSpec((None, BLOCK, 2 * KV_WIDTH), lambda b, j: (b, jnp.maximum(j - 1, 0), 0)),
                  pl.BlockSpec((None, BLOCK, 2 * KV_WIDTH), lambda b, j: (b, j, 0)),
                  pl.BlockSpec((None, N_Q_HEADS, BLOCK, 2 * BLOCK), lambda b, j: (jnp.minimum(j, 1), 0, 0, 0))],
        out_specs=pl.BlockSpec((None, BLOCK, ATT_WIDTH), lambda b, j: (b, j, 0)),
        out_shape=jax.ShapeDtypeStruct((batch, seq, ATT_WIDTH), BF16),
        compiler_params=_cparams(("arbitrary", "arbitrary")),
        name="attn_prompt",
    )(sinks, q3, kv3, kv3, bias)
    return out.reshape(batch * seq, ATT_WIDTH)


def _attn_decode_kernel(sink_ref, q_ref, kvn_ref, ck_ref, cv_ref, bias_ref, o_ref):
    q = q_ref[...].astype(F32)
    kvn = kvn_ref[...]
    for g in range(N_KV_HEADS):
        ks = slice(g * HEAD_DIM, (g + 1) * HEAD_DIM)
        vs = slice(KV_WIDTH + g * HEAD_DIM, KV_WIDTH + (g + 1) * HEAD_DIM)
        hs = slice(g * GQA_GROUP, (g + 1) * GQA_GROUP)
        qg = q[:, hs, :]
        kc = ck_ref[:, :, ks]
        vc = cv_ref[:, :, ks]
        kn = kvn[:, ks].astype(BF16).astype(F32)[:, None, :]
        vn = kvn[:, vs].astype(BF16).astype(F32)[:, None, :]
        s = jnp.einsum("bqd,bkd->bqk", qg.astype(BF16), kc.astype(BF16), preferred_element_type=F32)
        s = s + bias_ref[hs, 0:WINDOW][None]
        s_new = jnp.sum(qg * kn, axis=-1, keepdims=True) + bias_ref[hs, WINDOW:WINDOW + 1][None]
        sink = sink_ref[hs, 0:1][None]
        m = jnp.maximum(jnp.maximum(jnp.max(s, axis=-1, keepdims=True), s_new), sink)
        p = jnp.exp(s - m)
        p_new = jnp.exp(s_new - m)
        denom = jnp.sum(p, axis=-1, keepdims=True) + p_new + jnp.exp(sink - m)
        o = jnp.einsum("bqk,bkd->bqd", p.astype(BF16), vc.astype(BF16), preferred_element_type=F32)
        o = o + p_new * vn
        o_ref[:, hs, :] = o / denom


def _attn_decode(q, kvn, cache_k, cache_v, bias_row, sinks_b):
    b = q.shape[0]
    lw = cache_k.shape[1]
    vm = pl.BlockSpec(memory_space=pltpu.VMEM)
    out = pl.pallas_call(
        _attn_decode_kernel,
        in_specs=[vm, vm, vm, vm, vm, vm],
        out_specs=vm,
        out_shape=jax.ShapeDtypeStruct((b, N_Q_HEADS, HEAD_DIM), F32),
        compiler_params=pltpu.CompilerParams(vmem_limit_bytes=VMEM_LIMIT_BYTES),
        name="attn_decode",
    )(sinks_b, q.reshape(b, N_Q_HEADS, HEAD_DIM), kvn,
      cache_k.reshape(b, lw, KV_WIDTH), cache_v.reshape(b, lw, KV_WIDTH), bias_row)
    return out.reshape(b, ATT_WIDTH)


def _seg_expand(x, bd_ref):
    rows = x.shape[0]
    groups = x.shape[1] // LANES
    xs = jnp.concatenate([x[:, i * LANES:(i + 1) * LANES] for i in range(groups)], axis=0)
    s = _mm(xs, bd_ref[...])
    return jnp.concatenate([s[i * rows:(i + 1) * rows] for i in range(groups)], axis=1)


def _rwkv_front(x, prev, p):
    mu_ref, w0_ref, a0_ref, kk_ref, ka_ref, lw_ref, g2_ref, bd_ref = p
    xm = x + mu_ref[...] * (prev - x)
    o3 = 3 * RWKV_WIDTH
    r = xm[:, 0:RWKV_WIDTH]
    k = xm[:, RWKV_WIDTH:2 * RWKV_WIDTH]
    v = xm[:, 2 * RWKV_WIDTH:o3]
    wa = xm[:, o3:o3 + LANES]
    lane = lax.broadcasted_iota(jnp.int32, wa.shape, 1)
    z = jnp.where(lane < DECAY_LORA, jnp.tanh(wa), wa)
    lora = _mm(z, lw_ref[...])
    logw = -DECAY_SCALE * _sigmoid(w0_ref[...] + lora[:, 0:RWKV_WIDTH])
    a = _sigmoid(a0_ref[...] + lora[:, RWKV_WIDTH:2 * RWKV_WIDTH])
    g = _mm(_sigmoid(xm[:, o3 + LANES:SHIFT_COLS]), g2_ref[...])
    kk = k * kk_ref[...]
    kmod = k * (1.0 + (a - 1.0) * ka_ref[...])
    nrm = jnp.sqrt(_seg_expand(kk * kk, bd_ref))
    kk = kk / jnp.maximum(nrm, 1e-12)
    return r, logw, kmod, v, kk, kk * a, g


def _rwkv_back(y, r, kmod, v, g, q):
    rk_ref, lnw_ref, lnb_ref, bd_ref = q
    mean = _seg_expand(y, bd_ref) * (1.0 / HEAD_DIM)
    d = y - mean
    var = _seg_expand(d * d, bd_ref) * (1.0 / HEAD_DIM)
    yn = d * lax.rsqrt(var + GN_EPS) * lnw_ref[...] + lnb_ref[...]
    bonus = _seg_expand(r * kmod * rk_ref[...], bd_ref) * v
    return (yn + bonus) * g


def _rwkv_chunk_kernel(rw_ref, mu_ref, w0_ref, a0_ref, kk_ref, ka_ref, lw_ref, g2_ref, bd_ref,
                       rk_ref, lnw_ref, lnb_ref,
                       o_ref, s_out_ref,
                       s_scr, prev_scr, y_scr, ea_scr, er_scr, eb_scr, ek_scr, be_scr, ke_scr, v_scr):
    c = pl.program_id(1)
    C = CHUNK

    @pl.when(c == 0)
    def _():
        s_scr[...] = jnp.zeros_like(s_scr)
        prev_scr[...] = jnp.zeros_like(prev_scr)

    x = rw_ref[...]
    row = lax.broadcasted_iota(jnp.int32, x.shape, 0)
    prev = jnp.where(row == 0, prev_scr[...], pltpu.roll(x, 1, axis=0))
    prev_scr[...] = x[C - 1:C, :]
    front = (mu_ref, w0_ref, a0_ref, kk_ref, ka_ref, lw_ref, g2_ref, bd_ref)
    r, logw, kmod, v, kk, bb, g = _rwkv_front(x, prev, front)

    ti = lax.broadcasted_iota(jnp.int32, (C, C), 0)
    si = lax.broadcasted_iota(jnp.int32, (C, C), 1)
    cum = _mm2r(jnp.where(si <= ti, 1.0, 0.0).astype(BF16), logw)
    cum_end = cum[C - 1:C, :]
    e_in = jnp.exp(cum)
    e_neg = jnp.exp(-cum)
    e_end = jnp.exp(cum_end - cum)
    ea_scr[...] = kk * jnp.exp(cum - logw)
    er_scr[...] = r * e_in
    eb_scr[...] = bb * e_neg
    ek_scr[...] = kmod * e_neg
    be_scr[...] = bb * e_end
    ke_scr[...] = kmod * e_end
    v_scr[...] = v
    w_end = jnp.exp(cum_end)

    lane = lax.broadcasted_iota(jnp.int32, (C, LANES), 1)
    lo_half = lane < HEAD_DIM
    ri = lax.broadcasted_iota(jnp.int32, (LANES, LANES), 0)
    ci = lax.broadcasted_iota(jnp.int32, (LANES, LANES), 1)
    same_head = jnp.right_shift(ri, 6) == jnp.right_shift(ci, 6)
    eye = ri == ci
    t_loc = jnp.bitwise_and(ri, C - 1)
    s_loc = jnp.bitwise_and(ci, C - 1)
    bd_strict = jnp.logical_and(same_head, s_loc < t_loc)
    bd_incl = jnp.logical_and(same_head, s_loc <= t_loc)

    def swap(t):
        return pltpu.roll(t, HEAD_DIM, axis=1)

    def stack2(t):
        return jnp.concatenate([jnp.where(lo_half, t, 0.0), jnp.where(lo_half, 0.0, t)], axis=0)

    def stack2x(t):
        ts = swap(t)
        return jnp.concatenate([jnp.where(lo_half, 0.0, ts), jnp.where(lo_half, ts, 0.0)], axis=0)

    def own(t):
        return jnp.where(lo_half, t[0:C], t[C:2 * C])

    def other(t):
        return swap(jnp.where(lo_half, t[C:2 * C], t[0:C]))

    pairs = range(HEAD_PAIRS)
    sl = [slice(p * LANES, (p + 1) * LANES) for p in pairs]
    a2 = [stack2(ea_scr[:, sl[p]]) for p in pairs]
    m = []
    for p in pairs:
        r2 = stack2(er_scr[:, sl[p]])
        b_p = eb_scr[:, sl[p]]
        k_p = ek_scr[:, sl[p]]
        m.append(_mm(jnp.concatenate([a2[p], r2], axis=0), jnp.concatenate([b_p, b_p, k_p, k_p], axis=0), _NT))
    l_ak = [jnp.where(bd_strict, m[p][0:2 * C, LANES:2 * LANES], 0.0) for p in pairs]
    pw = [-jnp.where(bd_strict, m[p][0:2 * C, 0:LANES], 0.0) for p in pairs]
    lrbk = [jnp.concatenate([jnp.where(bd_incl, m[p][2 * C:4 * C, 0:LANES], 0.0),
                             jnp.where(bd_incl, m[p][2 * C:4 * C, LANES:2 * LANES], 0.0)], axis=1) for p in pairs]
    v2x = [stack2x(v_scr[:, sl[p]]) for p in pairs]
    xc = [a2[p] - _mm(l_ak[p], v2x[p]) for p in pairs]
    for it in range(6):
        for p in pairs:
            pw_b = pw[p].astype(BF16)
            xc_b = xc[p].astype(BF16)
            if it < 5:
                px = _dg(pw_b, jnp.concatenate([xc_b, pw_b], axis=1))
                pw[p] = px[:, LANES:2 * LANES]
                xc[p] = xc[p] + px[:, 0:LANES]
            else:
                xc[p] = xc[p] + _dg(pw_b, xc_b)
    xv = [jnp.concatenate([xc[p], v2x[p]], axis=0).astype(BF16) for p in pairs]
    lb = [_mm(lrbk[p], xv[p]) for p in pairs]
    z = [_mm(xv[p], jnp.concatenate([stack2(be_scr[:, sl[p]]), stack2(ke_scr[:, sl[p]])], axis=0), _TN)
         for p in pairs]
    rt = [er_scr[:, sl[p]] - own(lb[p]) for p in pairs]
    yl = [other(lb[p]) for p in pairs]
    d = [jnp.where(eye, w_end[:, sl[p]], 0.0) - jnp.where(same_head, z[p], 0.0) for p in pairs]
    gg = [jnp.where(same_head, jnp.concatenate([z[p][C:2 * C], z[p][0:C]], axis=0), 0.0) for p in pairs]
    s0 = [s_scr[p].astype(BF16) for p in pairs]
    ys = [_mm(rt[p], s0[p], _NT) for p in pairs]
    for p in pairs:
        y_scr[:, sl[p]] = yl[p] + ys[p]
    sn = [_mm(s0[p], d[p]) for p in pairs]
    for p in pairs:
        s_scr[p] = sn[p] + gg[p]

    back = (rk_ref, lnw_ref, lnb_ref, bd_ref)
    o_ref[...] = _rwkv_back(y_scr[...], r, kmod, v, g, back).astype(BF16)

    @pl.when(c == pl.num_programs(1) - 1)
    def _():
        for p in range(HEAD_PAIRS):
            s_out_ref[2 * p] = s_scr[p, 0:HEAD_DIM, 0:HEAD_DIM]
            s_out_ref[2 * p + 1] = s_scr[p, HEAD_DIM:LANES, HEAD_DIM:LANES]


def _rwkv_prompt(rw, params, *, batch, seq):
    rw3 = rw.reshape(batch, seq, SHIFT_COLS)
    nc = seq // CHUNK
    const = lambda shape: pl.BlockSpec(shape, lambda b, c: tuple(0 for _ in shape))
    in_specs = [pl.BlockSpec((None, CHUNK, SHIFT_COLS), lambda b, c: (b, c, 0))]
    in_specs += [const(p.shape) for p in params]
    wide = pltpu.VMEM((CHUNK, RWKV_WIDTH), F32)
    o, s_out = pl.pallas_call(
        _rwkv_chunk_kernel,
        grid=(batch, nc),
        in_specs=in_specs,
        out_specs=[pl.BlockSpec((None, CHUNK, RWKV_WIDTH), lambda b, c: (b, c, 0)),
                   pl.BlockSpec((None, RWKV_HEADS, HEAD_DIM, HEAD_DIM), lambda b, c: (b, 0, 0, 0))],
        out_shape=[jax.ShapeDtypeStruct((batch, seq, RWKV_WIDTH), BF16),
                   jax.ShapeDtypeStruct((batch, RWKV_HEADS, HEAD_DIM, HEAD_DIM), F32)],
        scratch_shapes=[pltpu.VMEM((HEAD_PAIRS, LANES, LANES), F32),
                        pltpu.VMEM((1, SHIFT_COLS), F32),
                        wide, wide, wide, wide, wide, wide, wide, wide],
        compiler_params=_cparams(("arbitrary", "arbitrary")),
        name="rwkv_prompt",
    )(rw3, *params)
    return o.reshape(batch * seq, RWKV_WIDTH), s_out


def _rwkv_dec_front_kernel(rw_ref, prev_ref, mu_ref, w0_ref, a0_ref, kk_ref, ka_ref, lw_ref, g2_ref, bd_ref,
                           r_ref, w_ref, k_ref, v_ref, kkn_ref, b_ref, g_ref):
    front = (mu_ref, w0_ref, a0_ref, kk_ref, ka_ref, lw_ref, g2_ref, bd_ref)
    r, logw, kmod, v, kk, bb, g = _rwkv_front(rw_ref[...], prev_ref[...], front)
    r_ref[...] = r
    w_ref[...] = jnp.exp(logw)
    k_ref[...] = kmod
    v_ref[...] = v
    kkn_ref[...] = kk
    b_ref[...] = bb
    g_ref[...] = g


def _rwkv_dec_state_kernel(r_ref, w_ref, k_ref, v_ref, kk_ref, b_ref, s_ref, y_ref, so_ref):
    ri = lax.broadcasted_iota(jnp.int32, (HEAD_DIM, HEAD_DIM), 0)
    ci = lax.broadcasted_iota(jnp.int32, (HEAD_DIM, HEAD_DIM), 1)
    eye = ri == ci
    ones = jnp.ones((HEAD_DIM, LANES), BF16)
    heads = range(RWKV_HEADS)
    hs = [slice(h * HEAD_DIM, (h + 1) * HEAD_DIM) for h in heads]
    n = RWKV_HEADS * HEAD_DIM
    s = [s_ref[h] for h in heads]
    lhs = [s[h] * kk_ref[:, hs[h]] for h in heads] + [jnp.where(eye, v_ref[:, hs[h]], 0.0) for h in heads]
    red = _mm2(jnp.concatenate(lhs, axis=0), ones)
    s_new = []
    for h in heads:
        sa = -red[h * HEAD_DIM:(h + 1) * HEAD_DIM, 0:HEAD_DIM]
        v_col = red[n + h * HEAD_DIM:n + (h + 1) * HEAD_DIM, 0:HEAD_DIM]
        s_new.append(s[h] * w_ref[:, hs[h]] + sa * b_ref[:, hs[h]] + v_col * k_ref[:, hs[h]])
        so_ref[h] = s_new[h]
    yb = _mm2(jnp.concatenate([s_new[h] * r_ref[:, hs[h]] for h in heads], axis=0), ones)
    for h in heads:
        y_blk = yb[h * HEAD_DIM:(h + 1) * HEAD_DIM, 0:HEAD_DIM]
        y_ref[:, hs[h]] = jnp.sum(jnp.where(eye, y_blk, 0.0), axis=0, keepdims=True)


def _rwkv_dec_back_kernel(y_ref, r_ref, k_ref, v_ref, g_ref, rk_ref, lnw_ref, lnb_ref, bd_ref, o_ref):
    back = (rk_ref, lnw_ref, lnb_ref, bd_ref)
    o_ref[...] = _rwkv_back(y_ref[...], r_ref[...], k_ref[...], v_ref[...], g_ref[...], back).astype(BF16)


def _rwkv_decode(rw, shift0, wkv0, front_params, back_params):
    b = rw.shape[0]
    vm = pl.BlockSpec(memory_space=pltpu.VMEM)
    wide = jax.ShapeDtypeStruct((b, RWKV_WIDTH), F32)
    r, w, k, v, kk, bb, g = pl.pallas_call(
        _rwkv_dec_front_kernel,
        in_specs=[vm] * (2 + len(front_params)),
        out_specs=[vm] * 7,
        out_shape=[wide] * 7,
        name="rwkv_dec_front",
    )(rw, shift0, *front_params)
    rowspec = pl.BlockSpec((None, 1, RWKV_WIDTH), lambda i: (i, 0, 0))
    stspec = pl.BlockSpec((None, RWKV_HEADS, HEAD_DIM, HEAD_DIM), lambda i: (i, 0, 0, 0))
    as_rows = lambda t: t.reshape(b, 1, RWKV_WIDTH)
    y, s_new = pl.pallas_call(
        _rwkv_dec_state_kernel,
        grid=(b,),
        in_specs=[rowspec] * 6 + [stspec],
        out_specs=[rowspec, stspec],
        out_shape=[jax.ShapeDtypeStruct((b, 1, RWKV_WIDTH), F32),
                   jax.ShapeDtypeStruct((b, RWKV_HEADS, HEAD_DIM, HEAD_DIM), F32)],
        compiler_params=_cparams(("arbitrary",)),
        name="rwkv_dec_state",
    )(as_rows(r), as_rows(w), as_rows(k), as_rows(v), as_rows(kk), as_rows(bb), wkv0)
    o = pl.pallas_call(
        _rwkv_dec_back_kernel,
        in_specs=[vm] * (5 + len(back_params)),
        out_specs=vm,
        out_shape=jax.ShapeDtypeStruct((b, RWKV_WIDTH), BF16),
        name="rwkv_dec_back",
    )(y.reshape(b, RWKV_WIDTH), r, k, v, g, *back_params)
    return o, s_new


def kernel(x_prompt, x_sample, cache_k, cache_v, state_wkv, state_shift, rel_bias, ffn1_norm, ffn1_w_gate, ffn1_w_up, ffn1_w_down, mix_norm, w_in, attn_sinks, shift_mu, decay_w0, decay_w2, aaa_a0, aaa_a2, gate_g2, key_k, key_a, bonus_r_k, ln_x_w, ln_x_b, w_out, ffn2_norm, ffn2_w_gate, ffn2_w_up, ffn2_w_down, final_norm):
    batch, seq, _ = x_prompt.shape
    dec_b = x_sample.shape[0]
    lw = cache_k.shape[2]
    l = 0
    row = lambda t: t.reshape(1, -1)

    w_in_b = w_in[l].astype(BF16)
    w_out_b = w_out[l].astype(BF16)
    lora_w = jnp.zeros((LANES, 2 * RWKV_WIDTH), F32)
    lora_w = lora_w.at[0:DECAY_LORA, 0:RWKV_WIDTH].set(decay_w2[l])
    lora_w = lora_w.at[DECAY_LORA:LANES, RWKV_WIDTH:].set(aaa_a2[l])
    head_of = np.arange(LANES) // HEAD_DIM
    bd_ones = jnp.asarray(head_of[:, None] == head_of[None, :], BF16)
    front_params = (row(shift_mu[l]), row(decay_w0[l]), row(aaa_a0[l]), row(key_k[l]), row(key_a[l]),
                    lora_w.astype(BF16), gate_g2[l].astype(BF16), bd_ones)
    back_params = (row(bonus_r_k[l]), row(ln_x_w[l]), row(ln_x_b[l]), bd_ones)

    bias = _bias_table(rel_bias)
    sinks = attn_sinks[l]

    xs = x_sample.reshape(dec_b, D_MODEL)
    x1s, h1s, *f1 = _ffn(xs, row(ffn1_norm[l]), ffn1_w_gate[l], ffn1_w_up[l], ffn1_w_down[l], row(mix_norm[l]),
                         tm=dec_b, tf=FF_TILE_CAST, emit_x=True, n_dtype=BF16, emit_w=True)
    qs, kvs, rws = _proj(h1s, w_in_b, tm=dec_b)
    sinks_b = jnp.broadcast_to(sinks[:, None], (N_Q_HEADS, LANES))
    o_att_s = _attn_decode(qs, kvs, cache_k[l], cache_v[l], bias[1, :, 0, :], sinks_b)
    o_rw_s, s_s = _rwkv_decode(rws, state_shift[l], state_wkv[l], front_params, back_params)
    x2s = _outproj(x1s, o_att_s.astype(BF16), o_rw_s, w_out_b, tm=dec_b)
    y_s, *f2 = _ffn(x2s, row(ffn2_norm[l]), ffn2_w_gate[l], ffn2_w_up[l], ffn2_w_down[l], row(final_norm),
                    tm=dec_b, tf=FF_TILE_CAST, emit_x=False, n_dtype=F32, emit_w=True)

    xp = x_prompt.reshape(batch * seq, D_MODEL)
    x1, h1 = _ffn(xp, row(ffn1_norm[l]), *f1, row(mix_norm[l]), tm=512, tf=FF_TILE, emit_x=True, n_dtype=BF16)
    q, kv, rw = _proj(h1, w_in_b, tm=256)
    o_att = _attn_prompt(q, kv, bias, sinks, batch=batch, seq=seq)
    o_rw, s_p = _rwkv_prompt(rw, front_params + back_params[:3], batch=batch, seq=seq)
    x2 = _outproj(x1, o_att, o_rw, w_out_b, tm=512)
    (y_p,) = _ffn(x2, row(ffn2_norm[l]), *f2, row(final_norm), tm=512, tf=FF_TILE, emit_x=False, n_dtype=F32)

    kv3 = kv.reshape(batch, seq, 2 * KV_WIDTH)
    lp = min(WINDOW, seq)
    new_k_p = kv3[:, seq - lp:, 0:KV_WIDTH].reshape(1, batch, lp, N_KV_HEADS, HEAD_DIM)
    new_v_p = kv3[:, seq - lp:, KV_WIDTH:].reshape(1, batch, lp, N_KV_HEADS, HEAD_DIM)
    new_shift_p = rw.reshape(batch, seq, SHIFT_COLS)[:, seq - 1][None]
    k_new = kvs[:, 0:KV_WIDTH].reshape(dec_b, 1, N_KV_HEADS, HEAD_DIM)
    v_new = kvs[:, KV_WIDTH:].reshape(dec_b, 1, N_KV_HEADS, HEAD_DIM)
    new_k_s = jnp.concatenate([cache_k[l], k_new], axis=1)[:, -lw:][None]
    new_v_s = jnp.concatenate([cache_v[l], v_new], axis=1)[:, -lw:][None]
    return (y_p.reshape(batch, seq, D_MODEL), y_s.reshape(dec_b, 1, D_MODEL),
            new_k_p, new_v_p, s_p[None], new_shift_p,
            new_k_s, new_v_s, s_s[None], rws[None])
```

```python
import functools
import math

import numpy as np
import jax
import jax.numpy as jnp
from jax import lax
from jax.experimental import pallas as pl
from jax.experimental.pallas import tpu as pltpu

F32 = jnp.float32
BF16 = jnp.bfloat16

D_MODEL = 2048
HEAD_DIM = 64
ATT_WIDTH = 1024
N_Q_HEADS = 16
N_KV_HEADS = 4
GQA_GROUP = 4
KV_WIDTH = 256
RWKV_WIDTH = 1024
RWKV_HEADS = 16
WINDOW = 128
BLOCK = 128
N_BUCKETS = 32
MAX_DISTANCE = 128
DECAY_LORA = 64
AAA_LORA = 64
GATE_LORA = 128
D_FF = 5504
ATT_COLS = ATT_WIDTH + 2 * KV_WIDTH
SHIFT_COLS = 3 * RWKV_WIDTH + DECAY_LORA + AAA_LORA + GATE_LORA
IN_COLS = ATT_COLS + SHIFT_COLS
RMS_EPS = 1e-5
GN_EPS = 64e-5
FFN_RES = 0.5

LANES = 128
VMEM_LIMIT_BYTES = 60 * 1024 * 1024

FF_TILE = 1024
FF_TILE_CAST = 512
CHUNK = 64
HEAD_PAIRS = RWKV_HEADS // 2
NEG_BIG = -1e30
DECAY_SCALE = math.exp(-0.5)

_NN = (((1,), (0,)), ((), ()))
_NT = (((1,), (1,)), ((), ()))
_TN = (((0,), (0,)), ((), ()))


def _dg(a, b, dims=_NN):
    return lax.dot_general(a, b, dims, preferred_element_type=F32)


def _mm(a, b, dims=_NN):
    return _dg(a.astype(BF16), b.astype(BF16), dims)


def _split2(x):
    hi = x.astype(BF16)
    lo = (x - hi.astype(F32)).astype(BF16)
    return hi, lo


def _mm3(a, b, dims=_NN):
    ah, al = _split2(a)
    bh, bl = _split2(b)
    return _dg(ah, bh, dims) + _dg(ah, bl, dims) + _dg(al, bh, dims)


def _mm2(a, b_bf16, dims=_NN):
    ah, al = _split2(a)
    return _dg(ah, b_bf16, dims) + _dg(al, b_bf16, dims)


def _mm2r(a_bf16, b, dims=_NN):
    bh, bl = _split2(b)
    return _dg(a_bf16, bh, dims) + _dg(a_bf16, bl, dims)


def _sigmoid(x):
    return 1.0 / (1.0 + jnp.exp(-x))


def _rms(x, g):
    ms = jnp.mean(x * x, axis=-1, keepdims=True)
    return x * lax.rsqrt(ms + RMS_EPS) * g


def _cparams(sem):
    return pltpu.CompilerParams(dimension_semantics=sem, vmem_limit_bytes=VMEM_LIMIT_BYTES)


_FF_TAIL_STEP = 2


def _ff_block(j, *, steps):
    return jnp.where(j < _FF_TAIL_STEP, j, jnp.where(j == _FF_TAIL_STEP, steps - 1, j - 1))


def _ffn_kernel(x_ref, g_ref, wg_ref, wu_ref, wd_ref, g2_ref, *rest, emit_x, n_dtype, tf, emit_w):
    rest = list(rest)
    ox_ref = rest.pop(0) if emit_x else None
    on_ref = rest.pop(0)
    wgo_ref, wuo_ref, wdo_ref = (rest.pop(0), rest.pop(0), rest.pop(0)) if emit_w else (None, None, None)
    xn_scr, acc_scr = rest
    j = pl.program_id(1)

    @pl.when(j == 0)
    def _():
        xn_scr[...] = _rms(x_ref[...], g_ref[...]).astype(BF16)
        acc_scr[...] = jnp.zeros_like(acc_scr)

    steps = pl.cdiv(D_FF, tf)
    last = j == steps - 1
    tail = j == _FF_TAIL_STEP

    def accumulate(width):
        wg = wg_ref[:, 0:width].astype(BF16)
        wu = wu_ref[:, 0:width].astype(BF16)
        wd = wd_ref[0:width, :].astype(BF16)
        if emit_w:
            wgo_ref[:, 0:width] = wg
            wuo_ref[:, 0:width] = wu
            wdo_ref[0:width, :] = wd
        xn = xn_scr[...]
        gate = jnp.dot(xn, wg, preferred_element_type=F32)
        up = jnp.dot(xn, wu, preferred_element_type=F32)
        h = (gate * _sigmoid(gate) * up).astype(BF16)
        acc_scr[...] += jnp.dot(h, wd, preferred_element_type=F32)

    @pl.when(jnp.logical_not(tail))
    def _():
        accumulate(tf)

    @pl.when(tail)
    def _():
        accumulate(D_FF - (steps - 1) * tf)

    @pl.when(last)
    def _():
        y = x_ref[...] + FFN_RES * acc_scr[...]
        if emit_x:
            ox_ref[...] = y
        on_ref[...] = _rms(y, g2_ref[...]).astype(n_dtype)


def _ffn(x, g, wg, wu, wd, g2, *, tm, tf, emit_x, n_dtype, emit_w=False):
    m = x.shape[0]
    assert not emit_w or m == tm
    grid = (m // tm, pl.cdiv(D_FF, tf))
    row = pl.BlockSpec((tm, D_MODEL), lambda i, j: (i, 0))
    vec = pl.BlockSpec((1, D_MODEL), lambda i, j: (0, 0))
    steps = pl.cdiv(D_FF, tf)
    blk = functools.partial(_ff_block, steps=steps)
    w_in = pl.BlockSpec((D_MODEL, tf), lambda i, j: (0, blk(j)))
    w_dn = pl.BlockSpec((tf, D_MODEL), lambda i, j: (blk(j), 0))
    out_shape = [jax.ShapeDtypeStruct((m, D_MODEL), n_dtype)]
    out_specs = [row]
    if emit_x:
        out_shape = [jax.ShapeDtypeStruct((m, D_MODEL), F32)] + out_shape
        out_specs = [row, row]
    if emit_w:
        out_shape += [jax.ShapeDtypeStruct(w.shape, BF16) for w in (wg, wu, wd)]
        out_specs += [w_in, w_in, w_dn]
    return pl.pallas_call(
        functools.partial(_ffn_kernel, emit_x=emit_x, n_dtype=n_dtype, tf=tf, emit_w=emit_w),
        grid=grid,
        in_specs=[row, vec, w_in, w_in, w_dn, vec],
        out_specs=out_specs,
        out_shape=out_shape,
        scratch_shapes=[pltpu.VMEM((tm, D_MODEL), BF16), pltpu.VMEM((tm, D_MODEL), F32)],
        compiler_params=_cparams(("arbitrary", "arbitrary")),
        name="ffn_cast" if emit_w else "ffn",
    )(x, g, wg, wu, wd, g2)


def _proj_kernel(h_ref, w_ref, q_ref, kv_ref, rw_ref):
    h = h_ref[...]
    q = jnp.dot(h, w_ref[:, 0:ATT_WIDTH], preferred_element_type=F32)
    q_ref[...] = (q * (HEAD_DIM ** -0.5)).astype(BF16)
    kv_ref[...] = jnp.dot(h, w_ref[:, ATT_WIDTH:ATT_COLS], preferred_element_type=F32)
    rw_ref[...] = jnp.dot(h, w_ref[:, ATT_COLS:IN_COLS], preferred_element_type=F32)


def _proj(h, w, *, tm):
    m = h.shape[0]
    return pl.pallas_call(
        _proj_kernel,
        grid=(m // tm,),
        in_specs=[pl.BlockSpec((tm, D_MODEL), lambda i: (i, 0)),
                  pl.BlockSpec((D_MODEL, IN_COLS), lambda i: (0, 0), pipeline_mode=pl.Buffered(1))],
        out_specs=[pl.BlockSpec((tm, ATT_WIDTH), lambda i: (i, 0)),
                   pl.BlockSpec((tm, 2 * KV_WIDTH), lambda i: (i, 0)),
                   pl.BlockSpec((tm, SHIFT_COLS), lambda i: (i, 0))],
        out_shape=[jax.ShapeDtypeStruct((m, ATT_WIDTH), BF16),
                   jax.ShapeDtypeStruct((m, 2 * KV_WIDTH), F32),
                   jax.ShapeDtypeStruct((m, SHIFT_COLS), F32)],
        compiler_params=_cparams(("arbitrary",)),
        name="in_proj",
    )(h, w)


def _outproj_kernel(x_ref, oa_ref, orw_ref, w_ref, o_ref):
    acc = jnp.dot(oa_ref[...], w_ref[0:ATT_WIDTH, :], preferred_element_type=F32)
    acc += jnp.dot(orw_ref[...], w_ref[ATT_WIDTH:D_MODEL, :], preferred_element_type=F32)
    o_ref[...] = x_ref[...] + acc


def _outproj(x, oa, orw, w, *, tm):
    m = x.shape[0]
    row = pl.BlockSpec((tm, D_MODEL), lambda i: (i, 0))
    half = pl.BlockSpec((tm, ATT_WIDTH), lambda i: (i, 0))
    return pl.pallas_call(
        _outproj_kernel,
        grid=(m // tm,),
        in_specs=[row, half, half,
                  pl.BlockSpec((D_MODEL, D_MODEL), lambda i: (0, 0), pipeline_mode=pl.Buffered(1))],
        out_specs=row,
        out_shape=jax.ShapeDtypeStruct((m, D_MODEL), F32),
        compiler_params=_cparams(("arbitrary",)),
        name="out_proj",
    )(x, oa, orw, w)


def _bucket_table():
    qi = np.arange(BLOCK)[:, None]
    kj = np.arange(2 * BLOCK)[None, :]
    dist = BLOCK + qi - kj
    n = np.maximum(dist, 0)
    max_exact = N_BUCKETS // 2
    nf = np.maximum(n, 1).astype(np.float32)
    large = max_exact + (np.log(nf / np.float32(max_exact)) / np.float32(math.log(MAX_DISTANCE / max_exact))
                         * np.float32(N_BUCKETS - max_exact)).astype(np.int32)
    large = np.minimum(large, N_BUCKETS - 1)
    bucket = np.where(n < max_exact, n, large).astype(np.int32)
    valid = (dist >= 0) & (dist <= WINDOW)
    return np.where(valid, bucket, -1).astype(np.int32)


def _bias_kernel(bucket_ref, rb_ref, o_ref):
    bucket = bucket_ref[...]
    kj = lax.broadcasted_iota(jnp.int32, (BLOCK, 2 * BLOCK), 1)
    for h in range(N_Q_HEADS):
        acc = jnp.full((BLOCK, 2 * BLOCK), NEG_BIG, F32)
        for n in range(N_BUCKETS):
            acc = jnp.where(bucket == n, rb_ref[n, h], acc)
        o_ref[0, h] = jnp.where(kj >= BLOCK, acc, NEG_BIG)
        o_ref[1, h] = acc


def _bias_table(rel_bias):
    bucket = jnp.asarray(_bucket_table())
    return pl.pallas_call(
        _bias_kernel,
        in_specs=[pl.BlockSpec(memory_space=pltpu.VMEM), pl.BlockSpec(memory_space=pltpu.SMEM)],
        out_specs=pl.BlockSpec(memory_space=pltpu.VMEM),
        out_shape=jax.ShapeDtypeStruct((2, N_Q_HEADS, BLOCK, 2 * BLOCK), F32),
        name="bias_table",
    )(bucket, rel_bias)


def _attn_kernel(sink_ref, q_ref, kvp_ref, kvc_ref, bias_ref, o_ref):
    kvp = kvp_ref[...]
    kvc = kvc_ref[...]
    half = N_KV_HEADS // 2
    for g0 in range(0, N_KV_HEADS, half):
        kcat, vcat = {}, {}
        for g in range(g0, g0 + half):
            ks = slice(g * HEAD_DIM, (g + 1) * HEAD_DIM)
            vs = slice(KV_WIDTH + g * HEAD_DIM, KV_WIDTH + (g + 1) * HEAD_DIM)
            kcat[g] = jnp.concatenate([kvp[:, ks], kvc[:, ks]], axis=0).astype(BF16)
            vcat[g] = jnp.concatenate([kvp[:, vs], kvc[:, vs]], axis=0).astype(BF16)
        heads = range(g0 * GQA_GROUP, (g0 + half) * GQA_GROUP)
        hs = {h: slice(h * HEAD_DIM, (h + 1) * HEAD_DIM) for h in heads}
        s = {h: _dg(q_ref[:, hs[h]], kcat[h // GQA_GROUP], _NT) + bias_ref[h] for h in heads}
        m = {h: jnp.maximum(jnp.max(s[h], axis=-1, keepdims=True), sink_ref[h]) for h in heads}
        p = {h: jnp.exp(s[h] - m[h]) for h in heads}
        denom = {h: jnp.sum(p[h], axis=-1, keepdims=True) + jnp.exp(sink_ref[h] - m[h]) for h in heads}
        o = {h: jnp.dot(p[h].astype(BF16), vcat[h // GQA_GROUP], preferred_element_type=F32) for h in heads}
        for h in heads:
            o_ref[:, hs[h]] = (o[h] / denom[h]).astype(BF16)


def _attn_prompt(q, kv, bias, sinks, *, batch, seq):
    nb = seq // BLOCK
    q3 = q.reshape(batch, seq, ATT_WIDTH)
    kv3 = kv.reshape(batch, seq, 2 * KV_WIDTH)
    out = pl.pallas_call(
        _attn_kernel,
        grid=(batch, nb),
        in_specs=[pl.BlockSpec(memory_space=pltpu.SMEM),
                  pl.BlockSpec((None, BLOCK, ATT_WIDTH), lambda b, j: (b, j, 0)),
                  pl.BlockSpec((None, BLOCK, 2 * KV_WIDTH), lambda b, j: (b, jnp.maximum(j - 1, 0), 0)),
                  pl.BlockSpec((None, BLOCK, 2 * KV_WIDTH), lambda b, j: (b, j, 0)),
                  pl.BlockSpec((None, N_Q_HEADS, BLOCK, 2 * BLOCK), lambda b, j: (jnp.minimum(j, 1), 0, 0, 0))],
        out_specs=pl.BlockSpec((None, BLOCK, ATT_WIDTH), lambda b, j: (b, j, 0)),
        out_shape=jax.ShapeDtypeStruct((batch, seq, ATT_WIDTH), BF16),
        compiler_params=_cparams(("arbitrary", "arbitrary")),
        name="attn_prompt",
    )(sinks, q3, kv3, kv3, bias)
    return out.reshape(batch * seq, ATT_WIDTH)


def _attn_decode_kernel(sink_ref, q_ref, kvn_ref, ck_ref, cv_ref, bias_ref, o_ref):
    q = q_ref[...].astype(F32)
    kvn = kvn_ref[...]
    for g in range(N_KV_HEADS):
        ks = slice(g * HEAD_DIM, (g + 1) * HEAD_DIM)
        vs = slice(KV_WIDTH + g * HEAD_DIM, KV_WIDTH + (g + 1) * HEAD_DIM)
        hs = slice(g * GQA_GROUP, (g + 1) * GQA_GROUP)
        qg = q[:, hs, :]
        kc = ck_ref[:, :, ks]
        vc = cv_ref[:, :, ks]
        kn = kvn[:, ks].astype(BF16).astype(F32)[:, None, :]
        vn = kvn[:, vs].astype(BF16).astype(F32)[:, None, :]
        s = jnp.einsum("bqd,bkd->bqk", qg.astype(BF16), kc.astype(BF16), preferred_element_type=F32)
        s = s + bias_ref[hs, 0:WINDOW][None]
        s_new = jnp.sum(qg * kn, axis=-1, keepdims=True) + bias_ref[hs, WINDOW:WINDOW + 1][None]
        sink = sink_ref[hs, 0:1][None]
        m = jnp.maximum(jnp.maximum(jnp.max(s, axis=-1, keepdims=True), s_new), sink)
        p = jnp.exp(s - m)
        p_new = jnp.exp(s_new - m)
        denom = jnp.sum(p, axis=-1, keepdims=True) + p_new + jnp.exp(sink - m)
        o = jnp.einsum("bqk,bkd->bqd", p.astype(BF16), vc.astype(BF16), preferred_element_type=F32)
        o = o + p_new * vn
        o_ref[:, hs, :] = o / denom


def _attn_decode(q, kvn, cache_k, cache_v, bias_row, sinks_b):
    b = q.shape[0]
    lw = cache_k.shape[1]
    vm = pl.BlockSpec(memory_space=pltpu.VMEM)
    out = pl.pallas_call(
        _attn_decode_kernel,
        in_specs=[vm, vm, vm, vm, vm, vm],
        out_specs=vm,
        out_shape=jax.ShapeDtypeStruct((b, N_Q_HEADS, HEAD_DIM), F32),
        compiler_params=pltpu.CompilerParams(vmem_limit_bytes=VMEM_LIMIT_BYTES),
        name="attn_decode",
    )(sinks_b, q.reshape(b, N_Q_HEADS, HEAD_DIM), kvn,
      cache_k.reshape(b, lw, KV_WIDTH), cache_v.reshape(b, lw, KV_WIDTH), bias_row)
    return out.reshape(b, ATT_WIDTH)


def _seg_expand(x, bd_ref):
    rows = x.shape[0]
    groups = x.shape[1] // LANES
    xs = jnp.concatenate([x[:, i * LANES:(i + 1) * LANES] for i in range(groups)], axis=0)
    s = _mm(xs, bd_ref[...])
    return jnp.concatenate([s[i * rows:(i + 1) * rows] for i in range(groups)], axis=1)


def _rwkv_front(x, prev, p):
    mu_ref, w0_ref, a0_ref, kk_ref, ka_ref, lw_ref, g2_ref, bd_ref = p
    xm = x + mu_ref[...] * (prev - x)
    o3 = 3 * RWKV_WIDTH
    r = xm[:, 0:RWKV_WIDTH]
    k = xm[:, RWKV_WIDTH:2 * RWKV_WIDTH]
    v = xm[:, 2 * RWKV_WIDTH:o3]
    wa = xm[:, o3:o3 + LANES]
    lane = lax.broadcasted_iota(jnp.int32, wa.shape, 1)
    z = jnp.where(lane < DECAY_LORA, jnp.tanh(wa), wa)
    lora = _mm(z, lw_ref[...])
    logw = -DECAY_SCALE * _sigmoid(w0_ref[...] + lora[:, 0:RWKV_WIDTH])
    a = _sigmoid(a0_ref[...] + lora[:, RWKV_WIDTH:2 * RWKV_WIDTH])
    g = _mm(_sigmoid(xm[:, o3 + LANES:SHIFT_COLS]), g2_ref[...])
    kk = k * kk_ref[...]
    kmod = k * (1.0 + (a - 1.0) * ka_ref[...])
    nrm = jnp.sqrt(_seg_expand(kk * kk, bd_ref))
    kk = kk / jnp.maximum(nrm, 1e-12)
    return r, logw, kmod, v, kk, kk * a, g


def _rwkv_back(y, r, kmod, v, g, q):
    rk_ref, lnw_ref, lnb_ref, bd_ref = q
    mean = _seg_expand(y, bd_ref) * (1.0 / HEAD_DIM)
    d = y - mean
    var = _seg_expand(d * d, bd_ref) * (1.0 / HEAD_DIM)
    yn = d * lax.rsqrt(var + GN_EPS) * lnw_ref[...] + lnb_ref[...]
    bonus = _seg_expand(r * kmod * rk_ref[...], bd_ref) * v
    return (yn + bonus) * g


def _rwkv_chunk_kernel(rw_ref, mu_ref, w0_ref, a0_ref, kk_ref, ka_ref, lw_ref, g2_ref, bd_ref,
                       rk_ref, lnw_ref, lnb_ref,
                       o_ref, s_out_ref,
                       s_scr, prev_scr, y_scr, ea_scr, er_scr, eb_scr, ek_scr, be_scr, ke_scr, v_scr):
    c = pl.program_id(1)
    C = CHUNK

    @pl.when(c == 0)
    def _():
        s_scr[...] = jnp.zeros_like(s_scr)
        prev_scr[...] = jnp.zeros_like(prev_scr)

    x = rw_ref[...]
    row = lax.broadcasted_iota(jnp.int32, x.shape, 0)
    prev = jnp.where(row == 0, prev_scr[...], pltpu.roll(x, 1, axis=0))
    prev_scr[...] = x[C - 1:C, :]
    front = (mu_ref, w0_ref, a0_ref, kk_ref, ka_ref, lw_ref, g2_ref, bd_ref)
    r, logw, kmod, v, kk, bb, g = _rwkv_front(x, prev, front)

    ti = lax.broadcasted_iota(jnp.int32, (C, C), 0)
    si = lax.broadcasted_iota(jnp.int32, (C, C), 1)
    cum = _mm2r(jnp.where(si <= ti, 1.0, 0.0).astype(BF16), logw)
    cum_end = cum[C - 1:C, :]
    e_in = jnp.exp(cum)
    e_neg = jnp.exp(-cum)
    e_end = jnp.exp(cum_end - cum)
    ea_scr[...] = kk * jnp.exp(cum - logw)
    er_scr[...] = r * e_in
    eb_scr[...] = bb * e_neg
    ek_scr[...] = kmod * e_neg
    be_scr[...] = bb * e_end
    ke_scr[...] = kmod * e_end
    v_scr[...] = v
    w_end = jnp.exp(cum_end)

    lane = lax.broadcasted_iota(jnp.int32, (C, LANES), 1)
    lo_half = lane < HEAD_DIM
    ri = lax.broadcasted_iota(jnp.int32, (LANES, LANES), 0)
    ci = lax.broadcasted_iota(jnp.int32, (LANES, LANES), 1)
    same_head = jnp.right_shift(ri, 6) == jnp.right_shift(ci, 6)
    eye = ri == ci
    t_loc = jnp.bitwise_and(ri, C - 1)
    s_loc = jnp.bitwise_and(ci, C - 1)
    bd_strict = jnp.logical_and(same_head, s_loc < t_loc)
    bd_incl = jnp.logical_and(same_head, s_loc <= t_loc)

    def swap(t):
        return pltpu.roll(t, HEAD_DIM, axis=1)

    def stack2(t):
        return jnp.concatenate([jnp.where(lo_half, t, 0.0), jnp.where(lo_half, 0.0, t)], axis=0)

    def stack2x(t):
        ts = swap(t)
        return jnp.concatenate([jnp.where(lo_half, 0.0, ts), jnp.where(lo_half, ts, 0.0)], axis=0)

    def own(t):
        return jnp.where(lo_half, t[0:C], t[C:2 * C])

    def other(t):
        return swap(jnp.where(lo_half, t[C:2 * C], t[0:C]))

    pairs = range(HEAD_PAIRS)
    sl = [slice(p * LANES, (p + 1) * LANES) for p in pairs]
    a2 = [stack2(ea_scr[:, sl[p]]) for p in pairs]
    m = []
    for p in pairs:
        r2 = stack2(er_scr[:, sl[p]])
        b_p = eb_scr[:, sl[p]]
        k_p = ek_scr[:, sl[p]]
        m.append(_mm(jnp.concatenate([a2[p], r2], axis=0), jnp.concatenate([b_p, b_p, k_p, k_p], axis=0), _NT))
    l_ak = [jnp.where(bd_strict, m[p][0:2 * C, LANES:2 * LANES], 0.0) for p in pairs]
    pw = [-jnp.where(bd_strict, m[p][0:2 * C, 0:LANES], 0.0) for p in pairs]
    lrbk = [jnp.concatenate([jnp.where(bd_incl, m[p][2 * C:4 * C, 0:LANES], 0.0),
                             jnp.where(bd_incl, m[p][2 * C:4 * C, LANES:2 * LANES], 0.0)], axis=1) for p in pairs]
    v2x = [stack2x(v_scr[:, sl[p]]) for p in pairs]
    xc = [a2[p] - _mm(l_ak[p], v2x[p]) for p in pairs]
    for it in range(6):
        for p in pairs:
            pw_b = pw[p].astype(BF16)
            xc_b = xc[p].astype(BF16)
            if it < 5:
                px = _dg(pw_b, jnp.concatenate([xc_b, pw_b], axis=1))
                pw[p] = px[:, LANES:2 * LANES]
                xc[p] = xc[p] + px[:, 0:LANES]
            else:
                xc[p] = xc[p] + _dg(pw_b, xc_b)
    xv = [jnp.concatenate([xc[p], v2x[p]], axis=0).astype(BF16) for p in pairs]
    lb = [_mm(lrbk[p], xv[p]) for p in pairs]
    z = [_mm(xv[p], jnp.concatenate([stack2(be_scr[:, sl[p]]), stack2(ke_scr[:, sl[p]])], axis=0), _TN)
         for p in pairs]
    rt = [er_scr[:, sl[p]] - own(lb[p]) for p in pairs]
    yl = [other(lb[p]) for p in pairs]
    d = [jnp.where(eye, w_end[:, sl[p]], 0.0) - jnp.where(same_head, z[p], 0.0) for p in pairs]
    gg = [jnp.where(same_head, jnp.concatenate([z[p][C:2 * C], z[p][0:C]], axis=0), 0.0) for p in pairs]
    s0 = [s_scr[p].astype(BF16) for p in pairs]
    ys = [_mm(rt[p], s0[p], _NT) for p in pairs]
    for p in pairs:
        y_scr[:, sl[p]] = yl[p] + ys[p]
    sn = [_mm(s0[p], d[p]) for p in pairs]
    for p in pairs:
        s_scr[p] = sn[p] + gg[p]

    back = (rk_ref, lnw_ref, lnb_ref, bd_ref)
    o_ref[...] = _rwkv_back(y_scr[...], r, kmod, v, g, back).astype(BF16)

    @pl.when(c == pl.num_programs(1) - 1)
    def _():
        for p in range(HEAD_PAIRS):
            s_out_ref[2 * p] = s_scr[p, 0:HEAD_DIM, 0:HEAD_DIM]
            s_out_ref[2 * p + 1] = s_scr[p, HEAD_DIM:LANES, HEAD_DIM:LANES]


def _rwkv_prompt(rw, params, *, batch, seq):
    rw3 = rw.reshape(batch, seq, SHIFT_COLS)
    nc = seq // CHUNK
    const = lambda shape: pl.BlockSpec(shape, lambda b, c: tuple(0 for _ in shape))
    in_specs = [pl.BlockSpec((None, CHUNK, SHIFT_COLS), lambda b, c: (b, c, 0))]
    in_specs += [const(p.shape) for p in params]
    wide = pltpu.VMEM((CHUNK, RWKV_WIDTH), F32)
    o, s_out = pl.pallas_call(
        _rwkv_chunk_kernel,
        grid=(batch, nc),
        in_specs=in_specs,
        out_specs=[pl.BlockSpec((None, CHUNK, RWKV_WIDTH), lambda b, c: (b, c, 0)),
                   pl.BlockSpec((None, RWKV_HEADS, HEAD_DIM, HEAD_DIM), lambda b, c: (b, 0, 0, 0))],
        out_shape=[jax.ShapeDtypeStruct((batch, seq, RWKV_WIDTH), BF16),
                   jax.ShapeDtypeStruct((batch, RWKV_HEADS, HEAD_DIM, HEAD_DIM), F32)],
        scratch_shapes=[pltpu.VMEM((HEAD_PAIRS, LANES, LANES), F32),
                        pltpu.VMEM((1, SHIFT_COLS), F32),
                        wide, wide, wide, wide, wide, wide, wide, wide],
        compiler_params=_cparams(("arbitrary", "arbitrary")),
        name="rwkv_prompt",
    )(rw3, *params)
    return o.reshape(batch * seq, RWKV_WIDTH), s_out


def _rwkv_dec_front_kernel(rw_ref, prev_ref, mu_ref, w0_ref, a0_ref, kk_ref, ka_ref, lw_ref, g2_ref, bd_ref,
                           r_ref, w_ref, k_ref, v_ref, kkn_ref, b_ref, g_ref):
    front = (mu_ref, w0_ref, a0_ref, kk_ref, ka_ref, lw_ref, g2_ref, bd_ref)
    r, logw, kmod, v, kk, bb, g = _rwkv_front(rw_ref[...], prev_ref[...], front)
    r_ref[...] = r
    w_ref[...] = jnp.exp(logw)
    k_ref[...] = kmod
    v_ref[...] = v
    kkn_ref[...] = kk
    b_ref[...] = bb
    g_ref[...] = g


def _rwkv_dec_state_kernel(r_ref, w_ref, k_ref, v_ref, kk_ref, b_ref, s_ref, y_ref, so_ref):
    ri = lax.broadcasted_iota(jnp.int32, (HEAD_DIM, HEAD_DIM), 0)
    ci = lax.broadcasted_iota(jnp.int32, (HEAD_DIM, HEAD_DIM), 1)
    eye = ri == ci
    ones = jnp.ones((HEAD_DIM, LANES), BF16)
    heads = range(RWKV_HEADS)
    hs = [slice(h * HEAD_DIM, (h + 1) * HEAD_DIM) for h in heads]
    n = RWKV_HEADS * HEAD_DIM
    s = [s_ref[h] for h in heads]
    lhs = [s[h] * kk_ref[:, hs[h]] for h in heads] + [jnp.where(eye, v_ref[:, hs[h]], 0.0) for h in heads]
    red = _mm2(jnp.concatenate(lhs, axis=0), ones)
    s_new = []
    for h in heads:
        sa = -red[h * HEAD_DIM:(h + 1) * HEAD_DIM, 0:HEAD_DIM]
        v_col = red[n + h * HEAD_DIM:n + (h + 1) * HEAD_DIM, 0:HEAD_DIM]
        s_new.append(s[h] * w_ref[:, hs[h]] + sa * b_ref[:, hs[h]] + v_col * k_ref[:, hs[h]])
        so_ref[h] = s_new[h]
    yb = _mm2(jnp.concatenate([s_new[h] * r_ref[:, hs[h]] for h in heads], axis=0), ones)
    for h in heads:
        y_blk = yb[h * HEAD_DIM:(h + 1) * HEAD_DIM, 0:HEAD_DIM]
        y_ref[:, hs[h]] = jnp.sum(jnp.where(eye, y_blk, 0.0), axis=0, keepdims=True)


def _rwkv_dec_back_kernel(y_ref, r_ref, k_ref, v_ref, g_ref, rk_ref, lnw_ref, lnb_ref, bd_ref, o_ref):
    back = (rk_ref, lnw_ref, lnb_ref, bd_ref)
    o_ref[...] = _rwkv_back(y_ref[...], r_ref[...], k_ref[...], v_ref[...], g_ref[...], back).astype(BF16)


def _rwkv_decode(rw, shift0, wkv0, front_params, back_params):
    b = rw.shape[0]
    vm = pl.BlockSpec(memory_space=pltpu.VMEM)
    wide = jax.ShapeDtypeStruct((b, RWKV_WIDTH), F32)
    r, w, k, v, kk, bb, g = pl.pallas_call(
        _rwkv_dec_front_kernel,
        in_specs=[vm] * (2 + len(front_params)),
        out_specs=[vm] * 7,
        out_shape=[wide] * 7,
        name="rwkv_dec_front",
    )(rw, shift0, *front_params)
    rowspec = pl.BlockSpec((None, 1, RWKV_WIDTH), lambda i: (i, 0, 0))
    stspec = pl.BlockSpec((None, RWKV_HEADS, HEAD_DIM, HEAD_DIM), lambda i: (i, 0, 0, 0))
    as_rows = lambda t: t.reshape(b, 1, RWKV_WIDTH)
    y, s_new = pl.pallas_call(
        _rwkv_dec_state_kernel,
        grid=(b,),
        in_specs=[rowspec] * 6 + [stspec],
        out_specs=[rowspec, stspec],
        out_shape=[jax.ShapeDtypeStruct((b, 1, RWKV_WIDTH), F32),
                   jax.ShapeDtypeStruct((b, RWKV_HEADS, HEAD_DIM, HEAD_DIM), F32)],
        compiler_params=_cparams(("arbitrary",)),
        name="rwkv_dec_state",
    )(as_rows(r), as_rows(w), as_rows(k), as_rows(v), as_rows(kk), as_rows(bb), wkv0)
    o = pl.pallas_call(
        _rwkv_dec_back_kernel,
        in_specs=[vm] * (5 + len(back_params)),
        out_specs=vm,
        out_shape=jax.ShapeDtypeStruct((b, RWKV_WIDTH), BF16),
        name="rwkv_dec_back",
    )(y.reshape(b, RWKV_WIDTH), r, k, v, g, *back_params)
    return o, s_new


def kernel(x_prompt, x_sample, cache_k, cache_v, state_wkv, state_shift, rel_bias, ffn1_norm, ffn1_w_gate, ffn1_w_up, ffn1_w_down, mix_norm, w_in, attn_sinks, shift_mu, decay_w0, decay_w2, aaa_a0, aaa_a2, gate_g2, key_k, key_a, bonus_r_k, ln_x_w, ln_x_b, w_out, ffn2_norm, ffn2_w_gate, ffn2_w_up, ffn2_w_down, final_norm):
    batch, seq, _ = x_prompt.shape
    dec_b = x_sample.shape[0]
    lw = cache_k.shape[2]
    l = 0
    row = lambda t: t.reshape(1, -1)

    w_in_b = w_in[l].astype(BF16)
    w_out_b = w_out[l].astype(BF16)
    lora_w = jnp.zeros((LANES, 2 * RWKV_WIDTH), F32)
    lora_w = lora_w.at[0:DECAY_LORA, 0:RWKV_WIDTH].set(decay_w2[l])
    lora_w = lora_w.at[DECAY_LORA:LANES, RWKV_WIDTH:].set(aaa_a2[l])
    head_of = np.arange(LANES) // HEAD_DIM
    bd_ones = jnp.asarray(head_of[:, None] == head_of[None, :], BF16)
    front_params = (row(shift_mu[l]), row(decay_w0[l]), row(aaa_a0[l]), row(key_k[l]), row(key_a[l]),
                    lora_w.astype(BF16), gate_g2[l].astype(BF16), bd_ones)
    back_params = (row(bonus_r_k[l]), row(ln_x_w[l]), row(ln_x_b[l]), bd_ones)

    bias = _bias_table(rel_bias)
    sinks = attn_sinks[l]

    xs = x_sample.reshape(dec_b, D_MODEL)
    x1s, h1s, *f1 = _ffn(xs, row(ffn1_norm[l]), ffn1_w_gate[l], ffn1_w_up[l], ffn1_w_down[l], row(mix_norm[l]),
                         tm=dec_b, tf=FF_TILE_CAST, emit_x=True, n_dtype=BF16, emit_w=True)
    qs, kvs, rws = _proj(h1s, w_in_b, tm=dec_b)
    sinks_b = jnp.broadcast_to(sinks[:, None], (N_Q_HEADS, LANES))
    o_att_s = _attn_decode(qs, kvs, cache_k[l], cache_v[l], bias[1, :, 0, :], sinks_b)
    o_rw_s, s_s = _rwkv_decode(rws, state_shift[l], state_wkv[l], front_params, back_params)
    x2s = _outproj(x1s, o_att_s.astype(BF16), o_rw_s, w_out_b, tm=dec_b)
    y_s, *f2 = _ffn(x2s, row(ffn2_norm[l]), ffn2_w_gate[l], ffn2_w_up[l], ffn2_w_down[l], row(final_norm),
                    tm=dec_b, tf=FF_TILE_CAST, emit_x=False, n_dtype=F32, emit_w=True)

    xp = x_prompt.reshape(batch * seq, D_MODEL)
    x1, h1 = _ffn(xp, row(ffn1_norm[l]), *f1, row(mix_norm[l]), tm=512, tf=FF_TILE, emit_x=True, n_dtype=BF16)
    q, kv, rw = _proj(h1, w_in_b, tm=256)
    o_att = _attn_prompt(q, kv, bias, sinks, batch=batch, seq=seq)
    o_rw, s_p = _rwkv_prompt(rw, front_params + back_params[:3], batch=batch, seq=seq)
    x2 = _outproj(x1, o_att, o_rw, w_out_b, tm=512)
    (y_p,) = _ffn(x2, row(ffn2_norm[l]), *f2, row(final_norm), tm=512, tf=FF_TILE, emit_x=False, n_dtype=F32)

    kv3 = kv.reshape(batch, seq, 2 * KV_WIDTH)
    lp = min(WINDOW, seq)
    new_k_p = kv3[:, seq - lp:, 0:KV_WIDTH].reshape(1, batch, lp, N_KV_HEADS, HEAD_DIM)
    new_v_p = kv3[:, seq - lp:, KV_WIDTH:].reshape(1, batch, lp, N_KV_HEADS, HEAD_DIM)
    new_shift_p = rw.reshape(batch, seq, SHIFT_COLS)[:, seq - 1][None]
    k_new = kvs[:, 0:KV_WIDTH].reshape(dec_b, 1, N_KV_HEADS, HEAD_DIM)
    v_new = kvs[:, KV_WIDTH:].reshape(dec_b, 1, N_KV_HEADS, HEAD_DIM)
    new_k_s = jnp.concatenate([cache_k[l], k_new], axis=1)[:, -lw:][None]
    new_v_s = jnp.concatenate([cache_v[l], v_new], axis=1)[:, -lw:][None]
    return (y_p.reshape(batch, seq, D_MODEL), y_s.reshape(dec_b, 1, D_MODEL),
            new_k_p, new_v_p, s_p[None], new_shift_p,
            new_k_s, new_v_s, s_s[None], rws[None])
```

```python
import functools
import math

import numpy as np
import jax
import jax.numpy as jnp
from jax import lax
from jax.experimental import pallas as pl
from jax.experimental.pallas import tpu as pltpu

F32 = jnp.float32
BF16 = jnp.bfloat16

D_MODEL = 2048
HEAD_DIM = 64
ATT_WIDTH = 1024
N_Q_HEADS = 16
N_KV_HEADS = 4
GQA_GROUP = 4
KV_WIDTH = 256
RWKV_WIDTH = 1024
RWKV_HEADS = 16
WINDOW = 128
BLOCK = 128
N_BUCKETS = 32
MAX_DISTANCE = 128
DECAY_LORA = 64
AAA_LORA = 64
GATE_LORA = 128
D_FF = 5504
ATT_COLS = ATT_WIDTH + 2 * KV_WIDTH
SHIFT_COLS = 3 * RWKV_WIDTH + DECAY_LORA + AAA_LORA + GATE_LORA
IN_COLS = ATT_COLS + SHIFT_COLS
RMS_EPS = 1e-5
GN_EPS = 64e-5
FFN_RES = 0.5

LANES = 128
VMEM_LIMIT_BYTES = 60 * 1024 * 1024

FF_TILE = 1024
FF_TILE_CAST = 512
CHUNK = 64
HEAD_PAIRS = RWKV_HEADS // 2
NEG_BIG = -1e30
DECAY_SCALE = math.exp(-0.5)

_NN = (((1,), (0,)), ((), ()))
_NT = (((1,), (1,)), ((), ()))
_TN = (((0,), (0,)), ((), ()))


def _dg(a, b, dims=_NN):
    return lax.dot_general(a, b, dims, preferred_element_type=F32)


def _mm(a, b, dims=_NN):
    return _dg(a.astype(BF16), b.astype(BF16), dims)


def _split2(x):
    hi = x.astype(BF16)
    lo = (x - hi.astype(F32)).astype(BF16)
    return hi, lo


def _mm2(a, b_bf16, dims=_NN):
    ah, al = _split2(a)
    return _dg(ah, b_bf16, dims) + _dg(al, b_bf16, dims)


def _mm2r(a_bf16, b, dims=_NN):
    bh, bl = _split2(b)
    return _dg(a_bf16, bh, dims) + _dg(a_bf16, bl, dims)


def _sigmoid(x):
    return 1.0 / (1.0 + jnp.exp(-x))


def _rms(x, g):
    ms = jnp.mean(x * x, axis=-1, keepdims=True)
    return x * lax.rsqrt(ms + RMS_EPS) * g


def _cparams(sem):
    return pltpu.CompilerParams(dimension_semantics=sem, vmem_limit_bytes=VMEM_LIMIT_BYTES)


def _swiglu_down(xn, wg, wu, wd):
    gate = jnp.dot(xn, wg, preferred_element_type=F32)
    up = jnp.dot(xn, wu, preferred_element_type=F32)
    h = (gate * _sigmoid(gate) * up).astype(BF16)
    return jnp.dot(h, wd, preferred_element_type=F32)


_FF_TAIL_STEP = 2


def _ff_block(j, *, steps):
    return jnp.where(j < _FF_TAIL_STEP, j, jnp.where(j == _FF_TAIL_STEP, steps - 1, j - 1))


def _ffn_kernel(x_ref, g_ref, wg_ref, wu_ref, wd_ref, g2_ref, *rest, emit_x, n_dtype, tf, emit_w):
    rest = list(rest)
    ox_ref = rest.pop(0) if emit_x else None
    on_ref = rest.pop(0)
    wgo_ref, wuo_ref, wdo_ref = (rest.pop(0), rest.pop(0), rest.pop(0)) if emit_w else (None, None, None)
    xn_scr, acc_scr = rest
    j = pl.program_id(1)

    @pl.when(j == 0)
    def _():
        xn_scr[...] = _rms(x_ref[...], g_ref[...]).astype(BF16)
        acc_scr[...] = jnp.zeros_like(acc_scr)

    steps = pl.cdiv(D_FF, tf)
    last = j == steps - 1
    tail = j == _FF_TAIL_STEP

    def accumulate(width):
        wg = wg_ref[:, 0:width].astype(BF16)
        wu = wu_ref[:, 0:width].astype(BF16)
        wd = wd_ref[0:width, :].astype(BF16)
        if emit_w:
            wgo_ref[:, 0:width] = wg
            wuo_ref[:, 0:width] = wu
            wdo_ref[0:width, :] = wd
        acc_scr[...] += _swiglu_down(xn_scr[...], wg, wu, wd)

    @pl.when(jnp.logical_not(tail))
    def _():
        accumulate(tf)

    @pl.when(tail)
    def _():
        accumulate(D_FF - (steps - 1) * tf)

    @pl.when(last)
    def _():
        y = x_ref[...] + FFN_RES * acc_scr[...]
        if emit_x:
            ox_ref[...] = y
        on_ref[...] = _rms(y, g2_ref[...]).astype(n_dtype)


def _ffn(x, g, wg, wu, wd, g2, *, tm, tf, emit_x, n_dtype, emit_w=False):
    m = x.shape[0]
    assert not emit_w or m == tm
    grid = (m // tm, pl.cdiv(D_FF, tf))
    row = pl.BlockSpec((tm, D_MODEL), lambda i, j: (i, 0))
    vec = pl.BlockSpec((1, D_MODEL), lambda i, j: (0, 0))
    steps = pl.cdiv(D_FF, tf)
    blk = functools.partial(_ff_block, steps=steps)
    w_in = pl.BlockSpec((D_MODEL, tf), lambda i, j: (0, blk(j)))
    w_dn = pl.BlockSpec((tf, D_MODEL), lambda i, j: (blk(j), 0))
    out_shape = [jax.ShapeDtypeStruct((m, D_MODEL), n_dtype)]
    out_specs = [row]
    if emit_x:
        out_shape = [jax.ShapeDtypeStruct((m, D_MODEL), F32)] + out_shape
        out_specs = [row, row]
    if emit_w:
        out_shape += [jax.ShapeDtypeStruct(w.shape, BF16) for w in (wg, wu, wd)]
        out_specs += [w_in, w_in, w_dn]
    return pl.pallas_call(
        functools.partial(_ffn_kernel, emit_x=emit_x, n_dtype=n_dtype, tf=tf, emit_w=emit_w),
        grid=grid,
        in_specs=[row, vec, w_in, w_in, w_dn, vec],
        out_specs=out_specs,
        out_shape=out_shape,
        scratch_shapes=[pltpu.VMEM((tm, D_MODEL), BF16), pltpu.VMEM((tm, D_MODEL), F32)],
        compiler_params=_cparams(("arbitrary", "arbitrary")),
        name="ffn_cast" if emit_w else "ffn",
    )(x, g, wg, wu, wd, g2)


def _ff_sweep():
    steps = pl.cdiv(D_FF, FF_TILE)
    order = list(range(_FF_TAIL_STEP)) + [steps - 1] + list(range(_FF_TAIL_STEP, steps - 1))
    return [(b, min(FF_TILE, D_FF - b * FF_TILE)) for b in order]


def _ffn_stream_kernel(x_ref, g_ref, wg_hbm, wu_hbm, wd_hbm, g2_ref, *rest, emit_x, n_dtype):
    rest = list(rest)
    ox_ref = rest.pop(0) if emit_x else None
    on_ref = rest.pop(0)
    wg_buf, wu_buf, wd_buf, acc_scr, sem = rest
    i = pl.program_id(0)
    sweep = _ff_sweep()
    assert len(sweep) % 2 == 0

    def copies(pos):
        blk, width = sweep[pos]
        slot = pos % 2
        cols = pl.ds(blk * FF_TILE, width)
        return (pltpu.make_async_copy(wg_hbm.at[:, cols], wg_buf.at[slot, :, pl.ds(0, width)], sem.at[0, slot]),
                pltpu.make_async_copy(wu_hbm.at[:, cols], wu_buf.at[slot, :, pl.ds(0, width)], sem.at[1, slot]),
                pltpu.make_async_copy(wd_hbm.at[cols, :], wd_buf.at[slot, pl.ds(0, width), :], sem.at[2, slot]))

    def start(pos):
        for cp in copies(pos):
            cp.start()

    @pl.when(i == 0)
    def _():
        start(0)

    xn = _rms(x_ref[...], g_ref[...]).astype(BF16)
    for pos, (_, width) in enumerate(sweep):
        slot = pos % 2
        for cp in copies(pos):
            cp.wait()
        if pos + 1 < len(sweep):
            start(pos + 1)
        else:
            @pl.when(i + 1 < pl.num_programs(0))
            def _():
                start(0)
        part = _swiglu_down(xn, wg_buf[slot, :, 0:width], wu_buf[slot, :, 0:width], wd_buf[slot, 0:width, :])
        if pos == 0:
            acc_scr[...] = part
        else:
            acc_scr[...] += part

    y = x_ref[...] + FFN_RES * acc_scr[...]
    if emit_x:
        ox_ref[...] = y
    on_ref[...] = _rms(y, g2_ref[...]).astype(n_dtype)


def _ffn_stream(x, g, wg, wu, wd, g2, *, tm, emit_x, n_dtype):
    m = x.shape[0]
    row = pl.BlockSpec((tm, D_MODEL), lambda i: (i, 0))
    vec = pl.BlockSpec((1, D_MODEL), lambda i: (0, 0))
    hbm = pl.BlockSpec(memory_space=pl.ANY)
    out_shape = [jax.ShapeDtypeStruct((m, D_MODEL), n_dtype)]
    out_specs = [row]
    if emit_x:
        out_shape = [jax.ShapeDtypeStruct((m, D_MODEL), F32)] + out_shape
        out_specs = [row, row]
    return pl.pallas_call(
        functools.partial(_ffn_stream_kernel, emit_x=emit_x, n_dtype=n_dtype),
        grid=(m // tm,),
        in_specs=[row, vec, hbm, hbm, hbm, vec],
        out_specs=out_specs,
        out_shape=out_shape,
        scratch_shapes=[pltpu.VMEM((2, D_MODEL, FF_TILE), BF16), pltpu.VMEM((2, D_MODEL, FF_TILE), BF16),
                        pltpu.VMEM((2, FF_TILE, D_MODEL), BF16), pltpu.VMEM((tm, D_MODEL), F32),
                        pltpu.SemaphoreType.DMA((3, 2))],
        compiler_params=_cparams(("arbitrary",)),
        name="ffn",
    )(x, g, wg, wu, wd, g2)


def _proj_kernel(h_ref, w_ref, q_ref, kv_ref, rw_ref):
    h = h_ref[...]
    q = jnp.dot(h, w_ref[:, 0:ATT_WIDTH], preferred_element_type=F32)
    q_ref[...] = (q * (HEAD_DIM ** -0.5)).astype(BF16)
    kv_ref[...] = jnp.dot(h, w_ref[:, ATT_WIDTH:ATT_COLS], preferred_element_type=F32)
    rw_ref[...] = jnp.dot(h, w_ref[:, ATT_COLS:IN_COLS], preferred_element_type=F32)


def _proj(h, w, *, tm):
    m = h.shape[0]
    return pl.pallas_call(
        _proj_kernel,
        grid=(m // tm,),
        in_specs=[pl.BlockSpec((tm, D_MODEL), lambda i: (i, 0)),
                  pl.BlockSpec((D_MODEL, IN_COLS), lambda i: (0, 0), pipeline_mode=pl.Buffered(1))],
        out_specs=[pl.BlockSpec((tm, ATT_WIDTH), lambda i: (i, 0)),
                   pl.BlockSpec((tm, 2 * KV_WIDTH), lambda i: (i, 0)),
                   pl.BlockSpec((tm, SHIFT_COLS), lambda i: (i, 0))],
        out_shape=[jax.ShapeDtypeStruct((m, ATT_WIDTH), BF16),
                   jax.ShapeDtypeStruct((m, 2 * KV_WIDTH), F32),
                   jax.ShapeDtypeStruct((m, SHIFT_COLS), F32)],
        compiler_params=_cparams(("arbitrary",)),
        name="in_proj",
    )(h, w)


def _outproj_kernel(x_ref, oa_ref, orw_ref, w_ref, o_ref):
    acc = jnp.dot(oa_ref[...], w_ref[0:ATT_WIDTH, :], preferred_element_type=F32)
    acc += jnp.dot(orw_ref[...], w_ref[ATT_WIDTH:D_MODEL, :], preferred_element_type=F32)
    o_ref[...] = x_ref[...] + acc


def _outproj(x, oa, orw, w, *, tm):
    m = x.shape[0]
    row = pl.BlockSpec((tm, D_MODEL), lambda i: (i, 0))
    half = pl.BlockSpec((tm, ATT_WIDTH), lambda i: (i, 0))
    return pl.pallas_call(
        _outproj_kernel,
        grid=(m // tm,),
        in_specs=[row, half, half,
                  pl.BlockSpec((D_MODEL, D_MODEL), lambda i: (0, 0), pipeline_mode=pl.Buffered(1))],
        out_specs=row,
        out_shape=jax.ShapeDtypeStruct((m, D_MODEL), F32),
        compiler_params=_cparams(("arbitrary",)),
        name="out_proj",
    )(x, oa, orw, w)


def _bucket_table():
    qi = np.arange(BLOCK)[:, None]
    kj = np.arange(2 * BLOCK)[None, :]
    dist = BLOCK + qi - kj
    n = np.maximum(dist, 0)
    max_exact = N_BUCKETS // 2
    nf = np.maximum(n, 1).astype(np.float32)
    large = max_exact + (np.log(nf / np.float32(max_exact)) / np.float32(math.log(MAX_DISTANCE / max_exact))
                         * np.float32(N_BUCKETS - max_exact)).astype(np.int32)
    large = np.minimum(large, N_BUCKETS - 1)
    bucket = np.where(n < max_exact, n, large).astype(np.int32)
    valid = (dist >= 0) & (dist <= WINDOW)
    return np.where(valid, bucket, -1).astype(np.int32)


def _bias_kernel(bucket_ref, rb_ref, o_ref):
    bucket = bucket_ref[...]
    kj = lax.broadcasted_iota(jnp.int32, (BLOCK, 2 * BLOCK), 1)
    for h in range(N_Q_HEADS):
        acc = jnp.full((BLOCK, 2 * BLOCK), NEG_BIG, F32)
        for n in range(N_BUCKETS):
            acc = jnp.where(bucket == n, rb_ref[n, h], acc)
        o_ref[0, h] = jnp.where(kj >= BLOCK, acc, NEG_BIG)
        o_ref[1, h] = acc


def _bias_table(rel_bias):
    bucket = jnp.asarray(_bucket_table())
    return pl.pallas_call(
        _bias_kernel,
        in_specs=[pl.BlockSpec(memory_space=pltpu.VMEM), pl.BlockSpec(memory_space=pltpu.SMEM)],
        out_specs=pl.BlockSpec(memory_space=pltpu.VMEM),
        out_shape=jax.ShapeDtypeStruct((2, N_Q_HEADS, BLOCK, 2 * BLOCK), F32),
        name="bias_table",
    )(bucket, rel_bias)


def _attn_kernel(sink_ref, q_ref, kvp_ref, kvc_ref, bias_ref, o_ref):
    kvp = kvp_ref[...]
    kvc = kvc_ref[...]
    half = N_KV_HEADS // 2
    for g0 in range(0, N_KV_HEADS, half):
        kcat, vcat = {}, {}
        for g in range(g0, g0 + half):
            ks = slice(g * HEAD_DIM, (g + 1) * HEAD_DIM)
            vs = slice(KV_WIDTH + g * HEAD_DIM, KV_WIDTH + (g + 1) * HEAD_DIM)
            kcat[g] = jnp.concatenate([kvp[:, ks], kvc[:, ks]], axis=0).astype(BF16)
            vcat[g] = jnp.concatenate([kvp[:, vs], kvc[:, vs]], axis=0).astype(BF16)
        heads = range(g0 * GQA_GROUP, (g0 + half) * GQA_GROUP)
        hs = {h: slice(h * HEAD_DIM, (h + 1) * HEAD_DIM) for h in heads}
        s = {h: _dg(q_ref[:, hs[h]], kcat[h // GQA_GROUP], _NT) + bias_ref[h] for h in heads}
        m = {h: jnp.maximum(jnp.max(s[h], axis=-1, keepdims=True), sink_ref[h]) for h in heads}
        p = {h: jnp.exp(s[h] - m[h]) for h in heads}
        denom = {h: jnp.sum(p[h], axis=-1, keepdims=True) + jnp.exp(sink_ref[h] - m[h]) for h in heads}
        o = {h: jnp.dot(p[h].astype(BF16), vcat[h // GQA_GROUP], preferred_element_type=F32) for h in heads}
        for h in heads:
            o_ref[:, hs[h]] = (o[h] / denom[h]).astype(BF16)


def _attn_prompt(q, kv, bias, sinks, *, batch, seq):
    nb = seq // BLOCK
    q3 = q.reshape(batch, seq, ATT_WIDTH)
    kv3 = kv.reshape(batch, seq, 2 * KV_WIDTH)
    out = pl.pallas_call(
        _attn_kernel,
        grid=(batch, nb),
        in_specs=[pl.BlockSpec(memory_space=pltpu.SMEM),
                  pl.BlockSpec((None, BLOCK, ATT_WIDTH), lambda b, j: (b, j, 0)),
                  pl.BlockSpec((None, BLOCK, 2 * KV_WIDTH), lambda b, j: (b, jnp.maximum(j - 1, 0), 0)),
                  pl.BlockSpec((None, BLOCK, 2 * KV_WIDTH), lambda b, j: (b, j, 0)),
                  pl.BlockSpec((None, N_Q_HEADS, BLOCK, 2 * BLOCK), lambda b, j: (jnp.minimum(j, 1), 0, 0, 0))],
        out_specs=pl.BlockSpec((None, BLOCK, ATT_WIDTH), lambda b, j: (b, j, 0)),
        out_shape=jax.ShapeDtypeStruct((batch, seq, ATT_WIDTH), BF16),
        compiler_params=_cparams(("arbitrary", "arbitrary")),
        name="attn_prompt",
    )(sinks, q3, kv3, kv3, bias)
    return out.reshape(batch * seq, ATT_WIDTH)


def _attn_decode_kernel(sink_ref, q_ref, kvn_ref, ck_ref, cv_ref, bias_ref, o_ref):
    q = q_ref[...].astype(F32)
    kvn = kvn_ref[...]
    for g in range(N_KV_HEADS):
        ks = slice(g * HEAD_DIM, (g + 1) * HEAD_DIM)
        vs = slice(KV_WIDTH + g * HEAD_DIM, KV_WIDTH + (g + 1) * HEAD_DIM)
        hs = slice(g * GQA_GROUP, (g + 1) * GQA_GROUP)
        qg = q[:, hs, :]
        kc = ck_ref[:, :, ks]
        vc = cv_ref[:, :, ks]
        kn = kvn[:, ks].astype(BF16).astype(F32)[:, None, :]
        vn = kvn[:, vs].astype(BF16).astype(F32)[:, None, :]
        s = jnp.einsum("bqd,bkd->bqk", qg.astype(BF16), kc.astype(BF16), preferred_element_type=F32)
        s = s + bias_ref[hs, 0:WINDOW][None]
        s_new = jnp.sum(qg * kn, axis=-1, keepdims=True) + bias_ref[hs, WINDOW:WINDOW + 1][None]
        sink = sink_ref[hs, 0:1][None]
        m = jnp.maximum(jnp.maximum(jnp.max(s, axis=-1, keepdims=True), s_new), sink)
        p = jnp.exp(s - m)
        p_new = jnp.exp(s_new - m)
        denom = jnp.sum(p, axis=-1, keepdims=True) + p_new + jnp.exp(sink - m)
        o = jnp.einsum("bqk,bkd->bqd", p.astype(BF16), vc.astype(BF16), preferred_element_type=F32)
        o = o + p_new * vn
        o_ref[:, hs, :] = o / denom


def _attn_decode(q, kvn, cache_k, cache_v, bias_row, sinks_b):
    b = q.shape[0]
    lw = cache_k.shape[1]
    vm = pl.BlockSpec(memory_space=pltpu.VMEM)
    out = pl.pallas_call(
        _attn_decode_kernel,
        in_specs=[vm, vm, vm, vm, vm, vm],
        out_specs=vm,
        out_shape=jax.ShapeDtypeStruct((b, N_Q_HEADS, HEAD_DIM), F32),
        compiler_params=pltpu.CompilerParams(vmem_limit_bytes=VMEM_LIMIT_BYTES),
        name="attn_decode",
    )(sinks_b, q.reshape(b, N_Q_HEADS, HEAD_DIM), kvn,
      cache_k.reshape(b, lw, KV_WIDTH), cache_v.reshape(b, lw, KV_WIDTH), bias_row)
    return out.reshape(b, ATT_WIDTH)


def _seg_expand(x, bd_ref):
    rows = x.shape[0]
    groups = x.shape[1] // LANES
    xs = jnp.concatenate([x[:, i * LANES:(i + 1) * LANES] for i in range(groups)], axis=0)
    s = _mm(xs, bd_ref[...])
    return jnp.concatenate([s[i * rows:(i + 1) * rows] for i in range(groups)], axis=1)


def _rwkv_front(x, prev, p):
    mu_ref, w0_ref, a0_ref, kk_ref, ka_ref, lw_ref, g2_ref, bd_ref = p
    xm = x + mu_ref[...] * (prev - x)
    o3 = 3 * RWKV_WIDTH
    r = xm[:, 0:RWKV_WIDTH]
    k = xm[:, RWKV_WIDTH:2 * RWKV_WIDTH]
    v = xm[:, 2 * RWKV_WIDTH:o3]
    wa = xm[:, o3:o3 + LANES]
    lane = lax.broadcasted_iota(jnp.int32, wa.shape, 1)
    z = jnp.where(lane < DECAY_LORA, jnp.tanh(wa), wa)
    lora = _mm(z, lw_ref[...])
    logw = -DECAY_SCALE * _sigmoid(w0_ref[...] + lora[:, 0:RWKV_WIDTH])
    a = _sigmoid(a0_ref[...] + lora[:, RWKV_WIDTH:2 * RWKV_WIDTH])
    g = _mm(_sigmoid(xm[:, o3 + LANES:SHIFT_COLS]), g2_ref[...])
    kk = k * kk_ref[...]
    kmod = k * (1.0 + (a - 1.0) * ka_ref[...])
    nrm = jnp.sqrt(_seg_expand(kk * kk, bd_ref))
    kk = kk / jnp.maximum(nrm, 1e-12)
    return r, logw, kmod, v, kk, kk * a, g


def _rwkv_back(y, r, kmod, v, g, q):
    rk_ref, lnw_ref, lnb_ref, bd_ref = q
    mean = _seg_expand(y, bd_ref) * (1.0 / HEAD_DIM)
    d = y - mean
    var = _seg_expand(d * d, bd_ref) * (1.0 / HEAD_DIM)
    yn = d * lax.rsqrt(var + GN_EPS) * lnw_ref[...] + lnb_ref[...]
    bonus = _seg_expand(r * kmod * rk_ref[...], bd_ref) * v
    return (yn + bonus) * g


def _rwkv_chunk_kernel(rw_ref, mu_ref, w0_ref, a0_ref, kk_ref, ka_ref, lw_ref, g2_ref, bd_ref,
                       rk_ref, lnw_ref, lnb_ref,
                       o_ref, s_out_ref,
                       s_scr, prev_scr, y_scr, ea_scr, er_scr, eb_scr, ek_scr, be_scr, ke_scr, v_scr):
    c = pl.program_id(1)
    C = CHUNK

    @pl.when(c == 0)
    def _():
        s_scr[...] = jnp.zeros_like(s_scr)
        prev_scr[...] = jnp.zeros_like(prev_scr)

    x = rw_ref[...]
    row = lax.broadcasted_iota(jnp.int32, x.shape, 0)
    prev = jnp.where(row == 0, prev_scr[...], pltpu.roll(x, 1, axis=0))
    prev_scr[...] = x[C - 1:C, :]
    front = (mu_ref, w0_ref, a0_ref, kk_ref, ka_ref, lw_ref, g2_ref, bd_ref)
    r, logw, kmod, v, kk, bb, g = _rwkv_front(x, prev, front)

    ti = lax.broadcasted_iota(jnp.int32, (C, C), 0)
    si = lax.broadcasted_iota(jnp.int32, (C, C), 1)
    cum = _mm2r(jnp.where(si <= ti, 1.0, 0.0).astype(BF16), logw)
    cum_end = cum[C - 1:C, :]
    e_in = jnp.exp(cum)
    e_neg = jnp.exp(-cum)
    e_end = jnp.exp(cum_end - cum)
    ea_scr[...] = kk * jnp.exp(cum - logw)
    er_scr[...] = r * e_in
    eb_scr[...] = bb * e_neg
    ek_scr[...] = kmod * e_neg
    be_scr[...] = bb * e_end
    ke_scr[...] = kmod * e_end
    v_scr[...] = v
    w_end = jnp.exp(cum_end)

    lane = lax.broadcasted_iota(jnp.int32, (C, LANES), 1)
    lo_half = lane < HEAD_DIM
    ri = lax.broadcasted_iota(jnp.int32, (LANES, LANES), 0)
    ci = lax.broadcasted_iota(jnp.int32, (LANES, LANES), 1)
    same_head = jnp.right_shift(ri, 6) == jnp.right_shift(ci, 6)
    eye = ri == ci
    t_loc = jnp.bitwise_and(ri, C - 1)
    s_loc = jnp.bitwise_and(ci, C - 1)
    bd_strict = jnp.logical_and(same_head, s_loc < t_loc)
    bd_incl = jnp.logical_and(same_head, s_loc <= t_loc)

    def swap(t):
        return pltpu.roll(t, HEAD_DIM, axis=1)

    def stack2(t):
        return jnp.concatenate([jnp.where(lo_half, t, 0.0), jnp.where(lo_half, 0.0, t)], axis=0)

    def stack2x(t):
        ts = swap(t)
        return jnp.concatenate([jnp.where(lo_half, 0.0, ts), jnp.where(lo_half, ts, 0.0)], axis=0)

    def own(t):
        return jnp.where(lo_half, t[0:C], t[C:2 * C])

    def other(t):
        return swap(jnp.where(lo_half, t[C:2 * C], t[0:C]))

    pairs = range(HEAD_PAIRS)
    sl = [slice(p * LANES, (p + 1) * LANES) for p in pairs]
    a2 = [stack2(ea_scr[:, sl[p]]) for p in pairs]
    m = []
    for p in pairs:
        r2 = stack2(er_scr[:, sl[p]])
        b_p = eb_scr[:, sl[p]]
        k_p = ek_scr[:, sl[p]]
        m.append(_mm(jnp.concatenate([a2[p], r2], axis=0), jnp.concatenate([b_p, b_p, k_p, k_p], axis=0), _NT))
    l_ak = [jnp.where(bd_strict, m[p][0:2 * C, LANES:2 * LANES], 0.0) for p in pairs]
    pw = [-jnp.where(bd_strict, m[p][0:2 * C, 0:LANES], 0.0) for p in pairs]
    lrbk = [jnp.concatenate([jnp.where(bd_incl, m[p][2 * C:4 * C, 0:LANES], 0.0),
                             jnp.where(bd_incl, m[p][2 * C:4 * C, LANES:2 * LANES], 0.0)], axis=1) for p in pairs]
    v2x = [stack2x(v_scr[:, sl[p]]) for p in pairs]
    xc = [a2[p] - _mm(l_ak[p], v2x[p]) for p in pairs]
    for it in range(6):
        for p in pairs:
            pw_b = pw[p].astype(BF16)
            xc_b = xc[p].astype(BF16)
            if it < 5:
                px = _dg(pw_b, jnp.concatenate([xc_b, pw_b], axis=1))
                pw[p] = px[:, LANES:2 * LANES]
                xc[p] = xc[p] + px[:, 0:LANES]
            else:
                xc[p] = xc[p] + _dg(pw_b, xc_b)
    xv = [jnp.concatenate([xc[p], v2x[p]], axis=0).astype(BF16) for p in pairs]
    lb = [_mm(lrbk[p], xv[p]) for p in pairs]
    z = [_mm(xv[p], jnp.concatenate([stack2(be_scr[:, sl[p]]), stack2(ke_scr[:, sl[p]])], axis=0), _TN)
         for p in pairs]
    rt = [er_scr[:, sl[p]] - own(lb[p]) for p in pairs]
    yl = [other(lb[p]) for p in pairs]
    d = [jnp.where(eye, w_end[:, sl[p]], 0.0) - jnp.where(same_head, z[p], 0.0) for p in pairs]
    gg = [jnp.where(same_head, jnp.concatenate([z[p][C:2 * C], z[p][0:C]], axis=0), 0.0) for p in pairs]
    s0 = [s_scr[p].astype(BF16) for p in pairs]
    ys = [_mm(rt[p], s0[p], _NT) for p in pairs]
    for p in pairs:
        y_scr[:, sl[p]] = yl[p] + ys[p]
    sn = [_mm(s0[p], d[p]) for p in pairs]
    for p in pairs:
        s_scr[p] = sn[p] + gg[p]

    back = (rk_ref, lnw_ref, lnb_ref, bd_ref)
    o_ref[...] = _rwkv_back(y_scr[...], r, kmod, v, g, back).astype(BF16)

    @pl.when(c == pl.num_programs(1) - 1)
    def _():
        for p in range(HEAD_PAIRS):
            s_out_ref[2 * p] = s_scr[p, 0:HEAD_DIM, 0:HEAD_DIM]
            s_out_ref[2 * p + 1] = s_scr[p, HEAD_DIM:LANES, HEAD_DIM:LANES]


def _rwkv_prompt(rw, params, *, batch, seq):
    rw3 = rw.reshape(batch, seq, SHIFT_COLS)
    nc = seq // CHUNK
    const = lambda shape: pl.BlockSpec(shape, lambda b, c: tuple(0 for _ in shape))
    in_specs = [pl.BlockSpec((None, CHUNK, SHIFT_COLS), lambda b, c: (b, c, 0))]
    in_specs += [const(p.shape) for p in params]
    wide = pltpu.VMEM((CHUNK, RWKV_WIDTH), F32)
    o, s_out = pl.pallas_call(
        _rwkv_chunk_kernel,
        grid=(batch, nc),
        in_specs=in_specs,
        out_specs=[pl.BlockSpec((None, CHUNK, RWKV_WIDTH), lambda b, c: (b, c, 0)),
                   pl.BlockSpec((None, RWKV_HEADS, HEAD_DIM, HEAD_DIM), lambda b, c: (b, 0, 0, 0))],
        out_shape=[jax.ShapeDtypeStruct((batch, seq, RWKV_WIDTH), BF16),
                   jax.ShapeDtypeStruct((batch, RWKV_HEADS, HEAD_DIM, HEAD_DIM), F32)],
        scratch_shapes=[pltpu.VMEM((HEAD_PAIRS, LANES, LANES), F32),
                        pltpu.VMEM((1, SHIFT_COLS), F32),
                        wide, wide, wide, wide, wide, wide, wide, wide],
        compiler_params=_cparams(("arbitrary", "arbitrary")),
        name="rwkv_prompt",
    )(rw3, *params)
    return o.reshape(batch * seq, RWKV_WIDTH), s_out


def _rwkv_dec_front_kernel(rw_ref, prev_ref, mu_ref, w0_ref, a0_ref, kk_ref, ka_ref, lw_ref, g2_ref, bd_ref,
                           r_ref, w_ref, k_ref, v_ref, kkn_ref, b_ref, g_ref):
    front = (mu_ref, w0_ref, a0_ref, kk_ref, ka_ref, lw_ref, g2_ref, bd_ref)
    r, logw, kmod, v, kk, bb, g = _rwkv_front(rw_ref[...], prev_ref[...], front)
    r_ref[...] = r
    w_ref[...] = jnp.exp(logw)
    k_ref[...] = kmod
    v_ref[...] = v
    kkn_ref[...] = kk
    b_ref[...] = bb
    g_ref[...] = g


def _rwkv_dec_state_kernel(r_ref, w_ref, k_ref, v_ref, kk_ref, b_ref, s_ref, y_ref, so_ref):
    ri = lax.broadcasted_iota(jnp.int32, (HEAD_DIM, HEAD_DIM), 0)
    ci = lax.broadcasted_iota(jnp.int32, (HEAD_DIM, HEAD_DIM), 1)
    eye = ri == ci
    ones = jnp.ones((HEAD_DIM, LANES), BF16)
    heads = range(RWKV_HEADS)
    hs = [slice(h * HEAD_DIM, (h + 1) * HEAD_DIM) for h in heads]
    n = RWKV_HEADS * HEAD_DIM
    s = [s_ref[h] for h in heads]
    lhs = [s[h] * kk_ref[:, hs[h]] for h in heads] + [jnp.where(eye, v_ref[:, hs[h]], 0.0) for h in heads]
    red = _mm2(jnp.concatenate(lhs, axis=0), ones)
    s_new = []
    for h in heads:
        sa = -red[h * HEAD_DIM:(h + 1) * HEAD_DIM, 0:HEAD_DIM]
        v_col = red[n + h * HEAD_DIM:n + (h + 1) * HEAD_DIM, 0:HEAD_DIM]
        s_new.append(s[h] * w_ref[:, hs[h]] + sa * b_ref[:, hs[h]] + v_col * k_ref[:, hs[h]])
        so_ref[h] = s_new[h]
    yb = _mm2(jnp.concatenate([s_new[h] * r_ref[:, hs[h]] for h in heads], axis=0), ones)
    for h in heads:
        y_blk = yb[h * HEAD_DIM:(h + 1) * HEAD_DIM, 0:HEAD_DIM]
        y_ref[:, hs[h]] = jnp.sum(jnp.where(eye, y_blk, 0.0), axis=0, keepdims=True)


def _rwkv_dec_back_kernel(y_ref, r_ref, k_ref, v_ref, g_ref, rk_ref, lnw_ref, lnb_ref, bd_ref, o_ref):
    back = (rk_ref, lnw_ref, lnb_ref, bd_ref)
    o_ref[...] = _rwkv_back(y_ref[...], r_ref[...], k_ref[...], v_ref[...], g_ref[...], back).astype(BF16)


def _rwkv_decode(rw, shift0, wkv0, front_params, back_params):
    b = rw.shape[0]
    vm = pl.BlockSpec(memory_space=pltpu.VMEM)
    wide = jax.ShapeDtypeStruct((b, RWKV_WIDTH), F32)
    r, w, k, v, kk, bb, g = pl.pallas_call(
        _rwkv_dec_front_kernel,
        in_specs=[vm] * (2 + len(front_params)),
        out_specs=[vm] * 7,
        out_shape=[wide] * 7,
        name="rwkv_dec_front",
    )(rw, shift0, *front_params)
    rowspec = pl.BlockSpec((None, 1, RWKV_WIDTH), lambda i: (i, 0, 0))
    stspec = pl.BlockSpec((None, RWKV_HEADS, HEAD_DIM, HEAD_DIM), lambda i: (i, 0, 0, 0))
    as_rows = lambda t: t.reshape(b, 1, RWKV_WIDTH)
    y, s_new = pl.pallas_call(
        _rwkv_dec_state_kernel,
        grid=(b,),
        in_specs=[rowspec] * 6 + [stspec],
        out_specs=[rowspec, stspec],
        out_shape=[jax.ShapeDtypeStruct((b, 1, RWKV_WIDTH), F32),
                   jax.ShapeDtypeStruct((b, RWKV_HEADS, HEAD_DIM, HEAD_DIM), F32)],
        compiler_params=_cparams(("arbitrary",)),
        name="rwkv_dec_state",
    )(as_rows(r), as_rows(w), as_rows(k), as_rows(v), as_rows(kk), as_rows(bb), wkv0)
    o = pl.pallas_call(
        _rwkv_dec_back_kernel,
        in_specs=[vm] * (5 + len(back_params)),
        out_specs=vm,
        out_shape=jax.ShapeDtypeStruct((b, RWKV_WIDTH), BF16),
        name="rwkv_dec_back",
    )(y.reshape(b, RWKV_WIDTH), r, k, v, g, *back_params)
    return o, s_new


def kernel(x_prompt, x_sample, cache_k, cache_v, state_wkv, state_shift, rel_bias, ffn1_norm, ffn1_w_gate, ffn1_w_up, ffn1_w_down, mix_norm, w_in, attn_sinks, shift_mu, decay_w0, decay_w2, aaa_a0, aaa_a2, gate_g2, key_k, key_a, bonus_r_k, ln_x_w, ln_x_b, w_out, ffn2_norm, ffn2_w_gate, ffn2_w_up, ffn2_w_down, final_norm):
    batch, seq, _ = x_prompt.shape
    dec_b = x_sample.shape[0]
    lw = cache_k.shape[2]
    l = 0
    row = lambda t: t.reshape(1, -1)

    w_in_b = w_in[l].astype(BF16)
    w_out_b = w_out[l].astype(BF16)
    lora_w = jnp.zeros((LANES, 2 * RWKV_WIDTH), F32)
    lora_w = lora_w.at[0:DECAY_LORA, 0:RWKV_WIDTH].set(decay_w2[l])
    lora_w = lora_w.at[DECAY_LORA:LANES, RWKV_WIDTH:].set(aaa_a2[l])
    head_of = np.arange(LANES) // HEAD_DIM
    bd_ones = jnp.asarray(head_of[:, None] == head_of[None, :], BF16)
    front_params = (row(shift_mu[l]), row(decay_w0[l]), row(aaa_a0[l]), row(key_k[l]), row(key_a[l]),
                    lora_w.astype(BF16), gate_g2[l].astype(BF16), bd_ones)
    back_params = (row(bonus_r_k[l]), row(ln_x_w[l]), row(ln_x_b[l]), bd_ones)

    bias = _bias_table(rel_bias)
    sinks = attn_sinks[l]

    xs = x_sample.reshape(dec_b, D_MODEL)
    x1s, h1s, *f1 = _ffn(xs, row(ffn1_norm[l]), ffn1_w_gate[l], ffn1_w_up[l], ffn1_w_down[l], row(mix_norm[l]),
                         tm=dec_b, tf=FF_TILE_CAST, emit_x=True, n_dtype=BF16, emit_w=True)
    qs, kvs, rws = _proj(h1s, w_in_b, tm=dec_b)
    sinks_b = jnp.broadcast_to(sinks[:, None], (N_Q_HEADS, LANES))
    o_att_s = _attn_decode(qs, kvs, cache_k[l], cache_v[l], bias[1, :, 0, :], sinks_b)
    o_rw_s, s_s = _rwkv_decode(rws, state_shift[l], state_wkv[l], front_params, back_params)
    x2s = _outproj(x1s, o_att_s.astype(BF16), o_rw_s, w_out_b, tm=dec_b)
    y_s, *f2 = _ffn(x2s, row(ffn2_norm[l]), ffn2_w_gate[l], ffn2_w_up[l], ffn2_w_down[l], row(final_norm),
                    tm=dec_b, tf=FF_TILE_CAST, emit_x=False, n_dtype=F32, emit_w=True)

    xp = x_prompt.reshape(batch * seq, D_MODEL)
    x1, h1 = _ffn_stream(xp, row(ffn1_norm[l]), *f1, row(mix_norm[l]), tm=512, emit_x=True, n_dtype=BF16)
    q, kv, rw = _proj(h1, w_in_b, tm=256)
    o_att = _attn_prompt(q, kv, bias, sinks, batch=batch, seq=seq)
    o_rw, s_p = _rwkv_prompt(rw, front_params + back_params[:3], batch=batch, seq=seq)
    x2 = _outproj(x1, o_att, o_rw, w_out_b, tm=512)
    (y_p,) = _ffn_stream(x2, row(ffn2_norm[l]), *f2, row(final_norm), tm=512, emit_x=False, n_dtype=F32)

    kv3 = kv.reshape(batch, seq, 2 * KV_WIDTH)
    lp = min(WINDOW, seq)
    new_k_p = kv3[:, seq - lp:, 0:KV_WIDTH].reshape(1, batch, lp, N_KV_HEADS, HEAD_DIM)
    new_v_p = kv3[:, seq - lp:, KV_WIDTH:].reshape(1, batch, lp, N_KV_HEADS, HEAD_DIM)
    new_shift_p = rw.reshape(batch, seq, SHIFT_COLS)[:, seq - 1][None]
    k_new = kvs[:, 0:KV_WIDTH].reshape(dec_b, 1, N_KV_HEADS, HEAD_DIM)
    v_new = kvs[:, KV_WIDTH:].reshape(dec_b, 1, N_KV_HEADS, HEAD_DIM)
    new_k_s = jnp.concatenate([cache_k[l], k_new], axis=1)[:, -lw:][None]
    new_v_s = jnp.concatenate([cache_v[l], v_new], axis=1)[:, -lw:][None]
    return (y_p.reshape(batch, seq, D_MODEL), y_s.reshape(dec_b, 1, D_MODEL),
            new_k_p, new_v_p, s_p[None], new_shift_p,
            new_k_s, new_v_s, s_s[None], rws[None])
```

```python
import functools
import math

import numpy as np
import jax
import jax.numpy as jnp
from jax import lax
from jax.experimental import pallas as pl
from jax.experimental.pallas import tpu as pltpu

F32 = jnp.float32
BF16 = jnp.bfloat16

D_MODEL = 2048
HEAD_DIM = 64
ATT_WIDTH = 1024
N_Q_HEADS = 16
N_KV_HEADS = 4
GQA_GROUP = 4
KV_WIDTH = 256
RWKV_WIDTH = 1024
RWKV_HEADS = 16
WINDOW = 128
BLOCK = 128
N_BUCKETS = 32
MAX_DISTANCE = 128
DECAY_LORA = 64
AAA_LORA = 64
GATE_LORA = 128
D_FF = 5504
ATT_COLS = ATT_WIDTH + 2 * KV_WIDTH
SHIFT_COLS = 3 * RWKV_WIDTH + DECAY_LORA + AAA_LORA + GATE_LORA
IN_COLS = ATT_COLS + SHIFT_COLS
RMS_EPS = 1e-5
GN_EPS = 64e-5
FFN_RES = 0.5

LANES = 128
VMEM_LIMIT_BYTES = 60 * 1024 * 1024

FF_TILE = 1024
FF_TILE_CAST = 512
CHUNK = 64
CHUNKS_PER_STEP = 4
ATT_BLOCKS_PER_STEP = 4
HEAD_PAIRS = RWKV_HEADS // 2
NEG_BIG = -1e30
DECAY_SCALE = math.exp(-0.5)

_NN = (((1,), (0,)), ((), ()))
_NT = (((1,), (1,)), ((), ()))
_TN = (((0,), (0,)), ((), ()))


def _dg(a, b, dims=_NN):
    return lax.dot_general(a, b, dims, preferred_element_type=F32)


def _mm(a, b, dims=_NN):
    return _dg(a.astype(BF16), b.astype(BF16), dims)


def _split2(x):
    hi = x.astype(BF16)
    lo = (x - hi.astype(F32)).astype(BF16)
    return hi, lo


def _mm2(a, b_bf16, dims=_NN):
    ah, al = _split2(a)
    return _dg(ah, b_bf16, dims) + _dg(al, b_bf16, dims)


def _mm2r(a_bf16, b, dims=_NN):
    bh, bl = _split2(b)
    return _dg(a_bf16, bh, dims) + _dg(a_bf16, bl, dims)


def _sigmoid(x):
    return 1.0 / (1.0 + jnp.exp(-x))


def _rms(x, g):
    ms = jnp.mean(x * x, axis=-1, keepdims=True)
    return x * lax.rsqrt(ms + RMS_EPS) * g


def _cparams(sem):
    return pltpu.CompilerParams(dimension_semantics=sem, vmem_limit_bytes=VMEM_LIMIT_BYTES)


def _swiglu_down(xn, wg, wu, wd):
    gate = jnp.dot(xn, wg, preferred_element_type=F32)
    up = jnp.dot(xn, wu, preferred_element_type=F32)
    h = (gate * _sigmoid(gate) * up).astype(BF16)
    return jnp.dot(h, wd, preferred_element_type=F32)


_FF_TAIL_STEP = 2


def _ff_block(j, *, steps):
    return jnp.where(j < _FF_TAIL_STEP, j, jnp.where(j == _FF_TAIL_STEP, steps - 1, j - 1))


def _ffn_kernel(x_ref, g_ref, wg_ref, wu_ref, wd_ref, g2_ref, *rest, emit_x, n_dtype, tf, emit_w):
    rest = list(rest)
    ox_ref = rest.pop(0) if emit_x else None
    on_ref = rest.pop(0)
    wgo_ref, wuo_ref, wdo_ref = (rest.pop(0), rest.pop(0), rest.pop(0)) if emit_w else (None, None, None)
    xn_scr, acc_scr = rest
    j = pl.program_id(1)

    @pl.when(j == 0)
    def _():
        xn_scr[...] = _rms(x_ref[...], g_ref[...]).astype(BF16)
        acc_scr[...] = jnp.zeros_like(acc_scr)

    steps = pl.cdiv(D_FF, tf)
    last = j == steps - 1
    tail = j == _FF_TAIL_STEP

    def accumulate(width):
        wg = wg_ref[:, 0:width].astype(BF16)
        wu = wu_ref[:, 0:width].astype(BF16)
        wd = wd_ref[0:width, :].astype(BF16)
        if emit_w:
            wgo_ref[:, 0:width] = wg
            wuo_ref[:, 0:width] = wu
            wdo_ref[0:width, :] = wd
        acc_scr[...] += _swiglu_down(xn_scr[...], wg, wu, wd)

    @pl.when(jnp.logical_not(tail))
    def _():
        accumulate(tf)

    @pl.when(tail)
    def _():
        accumulate(D_FF - (steps - 1) * tf)

    @pl.when(last)
    def _():
        y = x_ref[...] + FFN_RES * acc_scr[...]
        if emit_x:
            ox_ref[...] = y
        on_ref[...] = _rms(y, g2_ref[...]).astype(n_dtype)


def _ffn(x, g, wg, wu, wd, g2, *, tm, tf, emit_x, n_dtype, emit_w=False):
    m = x.shape[0]
    assert not emit_w or m == tm
    grid = (m // tm, pl.cdiv(D_FF, tf))
    row = pl.BlockSpec((tm, D_MODEL), lambda i, j: (i, 0))
    vec = pl.BlockSpec((1, D_MODEL), lambda i, j: (0, 0))
    steps = pl.cdiv(D_FF, tf)
    blk = functools.partial(_ff_block, steps=steps)
    w_in = pl.BlockSpec((D_MODEL, tf), lambda i, j: (0, blk(j)))
    w_dn = pl.BlockSpec((tf, D_MODEL), lambda i, j: (blk(j), 0))
    out_shape = [jax.ShapeDtypeStruct((m, D_MODEL), n_dtype)]
    out_specs = [row]
    if emit_x:
        out_shape = [jax.ShapeDtypeStruct((m, D_MODEL), F32)] + out_shape
        out_specs = [row, row]
    if emit_w:
        out_shape += [jax.ShapeDtypeStruct(w.shape, BF16) for w in (wg, wu, wd)]
        out_specs += [w_in, w_in, w_dn]
    return pl.pallas_call(
        functools.partial(_ffn_kernel, emit_x=emit_x, n_dtype=n_dtype, tf=tf, emit_w=emit_w),
        grid=grid,
        in_specs=[row, vec, w_in, w_in, w_dn, vec],
        out_specs=out_specs,
        out_shape=out_shape,
        scratch_shapes=[pltpu.VMEM((tm, D_MODEL), BF16), pltpu.VMEM((tm, D_MODEL), F32)],
        compiler_params=_cparams(("arbitrary", "arbitrary")),
        name="ffn_cast" if emit_w else "ffn",
    )(x, g, wg, wu, wd, g2)


def _ff_sweep():
    steps = pl.cdiv(D_FF, FF_TILE)
    order = list(range(_FF_TAIL_STEP)) + [steps - 1] + list(range(_FF_TAIL_STEP, steps - 1))
    return [(b, min(FF_TILE, D_FF - b * FF_TILE)) for b in order]


def _ffn_stream_kernel(x_ref, g_ref, wg_hbm, wu_hbm, wd_hbm, g2_ref, *rest, emit_x, n_dtype):
    rest = list(rest)
    ox_ref = rest.pop(0) if emit_x else None
    on_ref = rest.pop(0)
    wg_buf, wu_buf, wd_buf, acc_scr, sem = rest
    i = pl.program_id(0)
    sweep = _ff_sweep()
    assert len(sweep) % 2 == 0

    def copies(pos):
        blk, width = sweep[pos]
        slot = pos % 2
        cols = pl.ds(blk * FF_TILE, width)
        return (pltpu.make_async_copy(wg_hbm.at[:, cols], wg_buf.at[slot, :, pl.ds(0, width)], sem.at[0, slot]),
                pltpu.make_async_copy(wu_hbm.at[:, cols], wu_buf.at[slot, :, pl.ds(0, width)], sem.at[1, slot]),
                pltpu.make_async_copy(wd_hbm.at[cols, :], wd_buf.at[slot, pl.ds(0, width), :], sem.at[2, slot]))

    def start(pos):
        for cp in copies(pos):
            cp.start()

    @pl.when(i == 0)
    def _():
        start(0)

    for pos, (_, width) in enumerate(sweep):
        slot = pos % 2
        for cp in copies(pos):
            cp.wait()
        if pos + 1 < len(sweep):
            start(pos + 1)
        else:
            @pl.when(i + 1 < pl.num_programs(0))
            def _():
                start(0)
        if pos == 0:
            xn = _rms(x_ref[...], g_ref[...]).astype(BF16)
        part = _swiglu_down(xn, wg_buf[slot, :, 0:width], wu_buf[slot, :, 0:width], wd_buf[slot, 0:width, :])
        if pos == 0:
            acc_scr[...] = part
        else:
            acc_scr[...] += part

    y = x_ref[...] + FFN_RES * acc_scr[...]
    if emit_x:
        ox_ref[...] = y
    on_ref[...] = _rms(y, g2_ref[...]).astype(n_dtype)


def _ffn_stream(x, g, wg, wu, wd, g2, *, tm, emit_x, n_dtype):
    m = x.shape[0]
    row = pl.BlockSpec((tm, D_MODEL), lambda i: (i, 0))
    vec = pl.BlockSpec((1, D_MODEL), lambda i: (0, 0))
    hbm = pl.BlockSpec(memory_space=pl.ANY)
    out_shape = [jax.ShapeDtypeStruct((m, D_MODEL), n_dtype)]
    out_specs = [row]
    if emit_x:
        out_shape = [jax.ShapeDtypeStruct((m, D_MODEL), F32)] + out_shape
        out_specs = [row, row]
    return pl.pallas_call(
        functools.partial(_ffn_stream_kernel, emit_x=emit_x, n_dtype=n_dtype),
        grid=(m // tm,),
        in_specs=[row, vec, hbm, hbm, hbm, vec],
        out_specs=out_specs,
        out_shape=out_shape,
        scratch_shapes=[pltpu.VMEM((2, D_MODEL, FF_TILE), BF16), pltpu.VMEM((2, D_MODEL, FF_TILE), BF16),
                        pltpu.VMEM((2, FF_TILE, D_MODEL), BF16), pltpu.VMEM((tm, D_MODEL), F32),
                        pltpu.SemaphoreType.DMA((3, 2))],
        compiler_params=_cparams(("arbitrary",)),
        name="ffn",
    )(x, g, wg, wu, wd, g2)


def _proj_kernel(h_ref, w_ref, q_ref, kv_ref, rw_ref):
    h = h_ref[...]
    q = jnp.dot(h, w_ref[:, 0:ATT_WIDTH], preferred_element_type=F32)
    q_ref[...] = (q * (HEAD_DIM ** -0.5)).astype(BF16)
    kv_ref[...] = jnp.dot(h, w_ref[:, ATT_WIDTH:ATT_COLS], preferred_element_type=F32)
    rw_ref[...] = jnp.dot(h, w_ref[:, ATT_COLS:IN_COLS], preferred_element_type=F32)


def _proj(h, w, *, tm):
    m = h.shape[0]
    return pl.pallas_call(
        _proj_kernel,
        grid=(m // tm,),
        in_specs=[pl.BlockSpec((tm, D_MODEL), lambda i: (i, 0)),
                  pl.BlockSpec((D_MODEL, IN_COLS), lambda i: (0, 0), pipeline_mode=pl.Buffered(1))],
        out_specs=[pl.BlockSpec((tm, ATT_WIDTH), lambda i: (i, 0)),
                   pl.BlockSpec((tm, 2 * KV_WIDTH), lambda i: (i, 0)),
                   pl.BlockSpec((tm, SHIFT_COLS), lambda i: (i, 0))],
        out_shape=[jax.ShapeDtypeStruct((m, ATT_WIDTH), BF16),
                   jax.ShapeDtypeStruct((m, 2 * KV_WIDTH), F32),
                   jax.ShapeDtypeStruct((m, SHIFT_COLS), F32)],
        compiler_params=_cparams(("arbitrary",)),
        name="in_proj",
    )(h, w)


def _outproj_kernel(x_ref, oa_ref, orw_ref, w_ref, o_ref):
    acc = jnp.dot(oa_ref[...], w_ref[0:ATT_WIDTH, :], preferred_element_type=F32)
    acc += jnp.dot(orw_ref[...], w_ref[ATT_WIDTH:D_MODEL, :], preferred_element_type=F32)
    o_ref[...] = x_ref[...] + acc


def _outproj(x, oa, orw, w, *, tm):
    m = x.shape[0]
    row = pl.BlockSpec((tm, D_MODEL), lambda i: (i, 0))
    half = pl.BlockSpec((tm, ATT_WIDTH), lambda i: (i, 0))
    return pl.pallas_call(
        _outproj_kernel,
        grid=(m // tm,),
        in_specs=[row, half, half,
                  pl.BlockSpec((D_MODEL, D_MODEL), lambda i: (0, 0), pipeline_mode=pl.Buffered(1))],
        out_specs=row,
        out_shape=jax.ShapeDtypeStruct((m, D_MODEL), F32),
        compiler_params=_cparams(("arbitrary",)),
        name="out_proj",
    )(x, oa, orw, w)


def _bucket_table():
    qi = np.arange(BLOCK)[:, None]
    kj = np.arange(2 * BLOCK)[None, :]
    dist = BLOCK + qi - kj
    n = np.maximum(dist, 0)
    max_exact = N_BUCKETS // 2
    nf = np.maximum(n, 1).astype(np.float32)
    large = max_exact + (np.log(nf / np.float32(max_exact)) / np.float32(math.log(MAX_DISTANCE / max_exact))
                         * np.float32(N_BUCKETS - max_exact)).astype(np.int32)
    large = np.minimum(large, N_BUCKETS - 1)
    bucket = np.where(n < max_exact, n, large).astype(np.int32)
    valid = (dist >= 0) & (dist <= WINDOW)
    return np.where(valid, bucket, -1).astype(np.int32)


def _bias_kernel(bucket_ref, rb_ref, o_ref):
    bucket = bucket_ref[...]
    kj = lax.broadcasted_iota(jnp.int32, (BLOCK, 2 * BLOCK), 1)
    for h in range(N_Q_HEADS):
        acc = jnp.full((BLOCK, 2 * BLOCK), NEG_BIG, F32)
        for n in range(N_BUCKETS):
            acc = jnp.where(bucket == n, rb_ref[n, h], acc)
        o_ref[0, h] = jnp.where(kj >= BLOCK, acc, NEG_BIG)
        o_ref[1, h] = acc


def _bias_table(rel_bias):
    bucket = jnp.asarray(_bucket_table())
    return pl.pallas_call(
        _bias_kernel,
        in_specs=[pl.BlockSpec(memory_space=pltpu.VMEM), pl.BlockSpec(memory_space=pltpu.SMEM)],
        out_specs=pl.BlockSpec(memory_space=pltpu.VMEM),
        out_shape=jax.ShapeDtypeStruct((2, N_Q_HEADS, BLOCK, 2 * BLOCK), F32),
        name="bias_table",
    )(bucket, rel_bias)


def _attn_kernel(sink_ref, q_ref, kvp_ref, kvc_ref, bias0_ref, bias_ref, o_ref):
    for i in range(ATT_BLOCKS_PER_STEP):
        rows = pl.ds(i * BLOCK, BLOCK)
        kvp = kvp_ref[...] if i == 0 else kvc_ref[pl.ds((i - 1) * BLOCK, BLOCK), :]
        _attn_block(sink_ref, q_ref.at[rows], kvp, kvc_ref[rows, :], bias0_ref if i == 0 else bias_ref,
                    o_ref.at[rows])


def _attn_block(sink_ref, q_ref, kvp, kvc, bias_ref, o_ref):
    half = N_KV_HEADS // 2
    for g0 in range(0, N_KV_HEADS, half):
        kcat, vcat = {}, {}
        for g in range(g0, g0 + half):
            ks = slice(g * HEAD_DIM, (g + 1) * HEAD_DIM)
            vs = slice(KV_WIDTH + g * HEAD_DIM, KV_WIDTH + (g + 1) * HEAD_DIM)
            kcat[g] = jnp.concatenate([kvp[:, ks], kvc[:, ks]], axis=0).astype(BF16)
            vcat[g] = jnp.concatenate([kvp[:, vs], kvc[:, vs]], axis=0).astype(BF16)
        heads = range(g0 * GQA_GROUP, (g0 + half) * GQA_GROUP)
        hs = {h: slice(h * HEAD_DIM, (h + 1) * HEAD_DIM) for h in heads}
        s = {h: _dg(q_ref[:, hs[h]], kcat[h // GQA_GROUP], _NT) + bias_ref[h] for h in heads}
        m = {h: jnp.maximum(jnp.max(s[h], axis=-1, keepdims=True), sink_ref[h]) for h in heads}
        p = {h: jnp.exp(s[h] - m[h]) for h in heads}
        denom = {h: jnp.sum(p[h], axis=-1, keepdims=True) + jnp.exp(sink_ref[h] - m[h]) for h in heads}
        o = {h: jnp.dot(p[h].astype(BF16), vcat[h // GQA_GROUP], preferred_element_type=F32) for h in heads}
        for h in heads:
            o_ref[:, hs[h]] = (o[h] / denom[h]).astype(BF16)


def _attn_prompt(q, kv, bias, sinks, *, batch, seq):
    n = ATT_BLOCKS_PER_STEP
    rows = n * BLOCK
    q3 = q.reshape(batch, seq, ATT_WIDTH)
    kv3 = kv.reshape(batch, seq, 2 * KV_WIDTH)
    table = (None, N_Q_HEADS, BLOCK, 2 * BLOCK)
    out = pl.pallas_call(
        _attn_kernel,
        grid=(batch, seq // rows),
        in_specs=[pl.BlockSpec(memory_space=pltpu.SMEM),
                  pl.BlockSpec((None, rows, ATT_WIDTH), lambda b, j: (b, j, 0)),
                  pl.BlockSpec((None, BLOCK, 2 * KV_WIDTH), lambda b, j: (b, jnp.maximum(n * j - 1, 0), 0)),
                  pl.BlockSpec((None, rows, 2 * KV_WIDTH), lambda b, j: (b, j, 0)),
                  pl.BlockSpec(table, lambda b, j: (jnp.minimum(j, 1), 0, 0, 0)),
                  pl.BlockSpec(table, lambda b, j: (1, 0, 0, 0))],
        out_specs=pl.BlockSpec((None, rows, ATT_WIDTH), lambda b, j: (b, j, 0)),
        out_shape=jax.ShapeDtypeStruct((batch, seq, ATT_WIDTH), BF16),
        compiler_params=_cparams(("arbitrary", "arbitrary")),
        name="attn_prompt",
    )(sinks, q3, kv3, kv3, bias, bias)
    return out.reshape(batch * seq, ATT_WIDTH)


def _attn_decode_kernel(sink_ref, q_ref, kvn_ref, ck_ref, cv_ref, bias_ref, o_ref):
    q = q_ref[...].astype(F32)
    kvn = kvn_ref[...]
    for g in range(N_KV_HEADS):
        ks = slice(g * HEAD_DIM, (g + 1) * HEAD_DIM)
        vs = slice(KV_WIDTH + g * HEAD_DIM, KV_WIDTH + (g + 1) * HEAD_DIM)
        hs = slice(g * GQA_GROUP, (g + 1) * GQA_GROUP)
        qg = q[:, hs, :]
        kc = ck_ref[:, :, ks]
        vc = cv_ref[:, :, ks]
        kn = kvn[:, ks].astype(BF16).astype(F32)[:, None, :]
        vn = kvn[:, vs].astype(BF16).astype(F32)[:, None, :]
        s = jnp.einsum("bqd,bkd->bqk", qg.astype(BF16), kc.astype(BF16), preferred_element_type=F32)
        s = s + bias_ref[hs, 0:WINDOW][None]
        s_new = jnp.sum(qg * kn, axis=-1, keepdims=True) + bias_ref[hs, WINDOW:WINDOW + 1][None]
        sink = sink_ref[hs, 0:1][None]
        m = jnp.maximum(jnp.maximum(jnp.max(s, axis=-1, keepdims=True), s_new), sink)
        p = jnp.exp(s - m)
        p_new = jnp.exp(s_new - m)
        denom = jnp.sum(p, axis=-1, keepdims=True) + p_new + jnp.exp(sink - m)
        o = jnp.einsum("bqk,bkd->bqd", p.astype(BF16), vc.astype(BF16), preferred_element_type=F32)
        o = o + p_new * vn
        o_ref[:, hs, :] = o / denom


def _attn_decode(q, kvn, cache_k, cache_v, bias_row, sinks_b):
    b = q.shape[0]
    lw = cache_k.shape[1]
    vm = pl.BlockSpec(memory_space=pltpu.VMEM)
    out = pl.pallas_call(
        _attn_decode_kernel,
        in_specs=[vm, vm, vm, vm, vm, vm],
        out_specs=vm,
        out_shape=jax.ShapeDtypeStruct((b, N_Q_HEADS, HEAD_DIM), F32),
        compiler_params=pltpu.CompilerParams(vmem_limit_bytes=VMEM_LIMIT_BYTES),
        name="attn_decode",
    )(sinks_b, q.reshape(b, N_Q_HEADS, HEAD_DIM), kvn,
      cache_k.reshape(b, lw, KV_WIDTH), cache_v.reshape(b, lw, KV_WIDTH), bias_row)
    return out.reshape(b, ATT_WIDTH)


def _seg_expand(x, bd_ref):
    rows = x.shape[0]
    groups = x.shape[1] // LANES
    xs = jnp.concatenate([x[:, i * LANES:(i + 1) * LANES] for i in range(groups)], axis=0)
    s = _mm(xs, bd_ref[...])
    return jnp.concatenate([s[i * rows:(i + 1) * rows] for i in range(groups)], axis=1)


def _rwkv_front(x, prev, p):
    mu_ref, w0_ref, a0_ref, kk_ref, ka_ref, lw_ref, g2_ref, bd_ref = p
    xm = x + mu_ref[...] * (prev - x)
    o3 = 3 * RWKV_WIDTH
    r = xm[:, 0:RWKV_WIDTH]
    k = xm[:, RWKV_WIDTH:2 * RWKV_WIDTH]
    v = xm[:, 2 * RWKV_WIDTH:o3]
    wa = xm[:, o3:o3 + LANES]
    lane = lax.broadcasted_iota(jnp.int32, wa.shape, 1)
    z = jnp.where(lane < DECAY_LORA, jnp.tanh(wa), wa)
    lora = _mm(z, lw_ref[...])
    logw = -DECAY_SCALE * _sigmoid(w0_ref[...] + lora[:, 0:RWKV_WIDTH])
    a = _sigmoid(a0_ref[...] + lora[:, RWKV_WIDTH:2 * RWKV_WIDTH])
    g = _mm(_sigmoid(xm[:, o3 + LANES:SHIFT_COLS]), g2_ref[...])
    kk = k * kk_ref[...]
    kmod = k * (1.0 + (a - 1.0) * ka_ref[...])
    nrm = jnp.sqrt(_seg_expand(kk * kk, bd_ref))
    kk = kk / jnp.maximum(nrm, 1e-12)
    return r, logw, kmod, v, kk, kk * a, g


def _rwkv_back(y, r, kmod, v, g, q):
    rk_ref, lnw_ref, lnb_ref, bd_ref = q
    mean = _seg_expand(y, bd_ref) * (1.0 / HEAD_DIM)
    d = y - mean
    var = _seg_expand(d * d, bd_ref) * (1.0 / HEAD_DIM)
    yn = d * lax.rsqrt(var + GN_EPS) * lnw_ref[...] + lnb_ref[...]
    bonus = _seg_expand(r * kmod * rk_ref[...], bd_ref) * v
    return (yn + bonus) * g


def _rwkv_chunk(rw_ref, mu_ref, w0_ref, a0_ref, kk_ref, ka_ref, lw_ref, g2_ref, bd_ref,
                rk_ref, lnw_ref, lnb_ref,
                o_ref,
                s_scr, prev_scr, y_scr, ea_scr, er_scr, eb_scr, ek_scr, be_scr, ke_scr, v_scr):
    C = CHUNK
    x = rw_ref[...]
    row = lax.broadcasted_iota(jnp.int32, x.shape, 0)
    prev = jnp.where(row == 0, prev_scr[...], pltpu.roll(x, 1, axis=0))
    prev_scr[...] = x[C - 1:C, :]
    front = (mu_ref, w0_ref, a0_ref, kk_ref, ka_ref, lw_ref, g2_ref, bd_ref)
    r, logw, kmod, v, kk, bb, g = _rwkv_front(x, prev, front)

    ti = lax.broadcasted_iota(jnp.int32, (C, C), 0)
    si = lax.broadcasted_iota(jnp.int32, (C, C), 1)
    cum = _mm2r(jnp.where(si <= ti, 1.0, 0.0).astype(BF16), logw)
    cum_end = cum[C - 1:C, :]
    e_in = jnp.exp(cum)
    e_neg = jnp.exp(-cum)
    e_end = jnp.exp(cum_end - cum)
    ea_scr[...] = kk * jnp.exp(cum - logw)
    er_scr[...] = r * e_in
    eb_scr[...] = bb * e_neg
    ek_scr[...] = kmod * e_neg
    be_scr[...] = bb * e_end
    ke_scr[...] = kmod * e_end
    v_scr[...] = v
    w_end = jnp.exp(cum_end)

    lane = lax.broadcasted_iota(jnp.int32, (C, LANES), 1)
    lo_half = lane < HEAD_DIM
    ri = lax.broadcasted_iota(jnp.int32, (LANES, LANES), 0)
    ci = lax.broadcasted_iota(jnp.int32, (LANES, LANES), 1)
    same_head = jnp.right_shift(ri, 6) == jnp.right_shift(ci, 6)
    eye = ri == ci
    t_loc = jnp.bitwise_and(ri, C - 1)
    s_loc = jnp.bitwise_and(ci, C - 1)
    bd_strict = jnp.logical_and(same_head, s_loc < t_loc)
    bd_incl = jnp.logical_and(same_head, s_loc <= t_loc)

    def swap(t):
        return pltpu.roll(t, HEAD_DIM, axis=1)

    def stack2(t):
        return jnp.concatenate([jnp.where(lo_half, t, 0.0), jnp.where(lo_half, 0.0, t)], axis=0)

    def stack2x(t):
        ts = swap(t)
        return jnp.concatenate([jnp.where(lo_half, 0.0, ts), jnp.where(lo_half, ts, 0.0)], axis=0)

    def own(t):
        return jnp.where(lo_half, t[0:C], t[C:2 * C])

    def other(t):
        return swap(jnp.where(lo_half, t[C:2 * C], t[0:C]))

    pairs = range(HEAD_PAIRS)
    sl = [slice(p * LANES, (p + 1) * LANES) for p in pairs]
    a2 = [stack2(ea_scr[:, sl[p]]) for p in pairs]
    m = []
    for p in pairs:
        r2 = stack2(er_scr[:, sl[p]])
        b_p = eb_scr[:, sl[p]]
        k_p = ek_scr[:, sl[p]]
        m.append(_mm(jnp.concatenate([a2[p], r2], axis=0), jnp.concatenate([b_p, b_p, k_p, k_p], axis=0), _NT))
    l_ak = [jnp.where(bd_strict, m[p][0:2 * C, LANES:2 * LANES], 0.0) for p in pairs]
    pw = [-jnp.where(bd_strict, m[p][0:2 * C, 0:LANES], 0.0) for p in pairs]
    lrbk = [jnp.concatenate([jnp.where(bd_incl, m[p][2 * C:4 * C, 0:LANES], 0.0),
                             jnp.where(bd_incl, m[p][2 * C:4 * C, LANES:2 * LANES], 0.0)], axis=1) for p in pairs]
    v2x = [stack2x(v_scr[:, sl[p]]) for p in pairs]
    xc = [a2[p] - _mm(l_ak[p], v2x[p]) for p in pairs]
    for it in range(6):
        for p in pairs:
            pw_b = pw[p].astype(BF16)
            xc_b = xc[p].astype(BF16)
            if it < 5:
                px = _dg(pw_b, jnp.concatenate([xc_b, pw_b], axis=1))
                pw[p] = px[:, LANES:2 * LANES]
                xc[p] = xc[p] + px[:, 0:LANES]
            else:
                xc[p] = xc[p] + _dg(pw_b, xc_b)
    xv = [jnp.concatenate([xc[p], v2x[p]], axis=0).astype(BF16) for p in pairs]
    lb = [_mm(lrbk[p], xv[p]) for p in pairs]
    z = [_mm(xv[p], jnp.concatenate([stack2(be_scr[:, sl[p]]), stack2(ke_scr[:, sl[p]])], axis=0), _TN)
         for p in pairs]
    rt = [er_scr[:, sl[p]] - own(lb[p]) for p in pairs]
    yl = [other(lb[p]) for p in pairs]
    d = [jnp.where(eye, w_end[:, sl[p]], 0.0) - jnp.where(same_head, z[p], 0.0) for p in pairs]
    gg = [jnp.where(same_head, jnp.concatenate([z[p][C:2 * C], z[p][0:C]], axis=0), 0.0) for p in pairs]
    s0 = [s_scr[p].astype(BF16) for p in pairs]
    ys = [_mm(rt[p], s0[p], _NT) for p in pairs]
    for p in pairs:
        y_scr[:, sl[p]] = yl[p] + ys[p]
    sn = [_mm(s0[p], d[p]) for p in pairs]
    for p in pairs:
        s_scr[p] = sn[p] + gg[p]

    back = (rk_ref, lnw_ref, lnb_ref, bd_ref)
    o_ref[...] = _rwkv_back(y_scr[...], r, kmod, v, g, back).astype(BF16)


def _rwkv_chunk_kernel(rw_ref, *rest):
    params, (o_ref, s_out_ref, s_scr, prev_scr), work = rest[:11], rest[11:15], rest[15:]
    c = pl.program_id(1)

    @pl.when(c == 0)
    def _():
        s_scr[...] = jnp.zeros_like(s_scr)
        prev_scr[...] = jnp.zeros_like(prev_scr)

    for i in range(CHUNKS_PER_STEP):
        rows = pl.ds(i * CHUNK, CHUNK)
        _rwkv_chunk(rw_ref.at[rows], *params, o_ref.at[rows], s_scr, prev_scr, *[w.at[i] for w in work])

    @pl.when(c == pl.num_programs(1) - 1)
    def _():
        for p in range(HEAD_PAIRS):
            s_out_ref[2 * p] = s_scr[p, 0:HEAD_DIM, 0:HEAD_DIM]
            s_out_ref[2 * p + 1] = s_scr[p, HEAD_DIM:LANES, HEAD_DIM:LANES]


def _rwkv_prompt(rw, params, *, batch, seq):
    rw3 = rw.reshape(batch, seq, SHIFT_COLS)
    rows = CHUNKS_PER_STEP * CHUNK
    nc = seq // rows
    const = lambda shape: pl.BlockSpec(shape, lambda b, c: tuple(0 for _ in shape))
    in_specs = [pl.BlockSpec((None, rows, SHIFT_COLS), lambda b, c: (b, c, 0))]
    in_specs += [const(p.shape) for p in params]
    wide = pltpu.VMEM((CHUNKS_PER_STEP, CHUNK, RWKV_WIDTH), F32)
    o, s_out = pl.pallas_call(
        _rwkv_chunk_kernel,
        grid=(batch, nc),
        in_specs=in_specs,
        out_specs=[pl.BlockSpec((None, rows, RWKV_WIDTH), lambda b, c: (b, c, 0)),
                   pl.BlockSpec((None, RWKV_HEADS, HEAD_DIM, HEAD_DIM), lambda b, c: (b, 0, 0, 0))],
        out_shape=[jax.ShapeDtypeStruct((batch, seq, RWKV_WIDTH), BF16),
                   jax.ShapeDtypeStruct((batch, RWKV_HEADS, HEAD_DIM, HEAD_DIM), F32)],
        scratch_shapes=[pltpu.VMEM((HEAD_PAIRS, LANES, LANES), F32),
                        pltpu.VMEM((1, SHIFT_COLS), F32),
                        wide, wide, wide, wide, wide, wide, wide, wide],
        compiler_params=_cparams(("arbitrary", "arbitrary")),
        name="rwkv_prompt",
    )(rw3, *params)
    return o.reshape(batch * seq, RWKV_WIDTH), s_out


def _rwkv_dec_front_kernel(rw_ref, prev_ref, mu_ref, w0_ref, a0_ref, kk_ref, ka_ref, lw_ref, g2_ref, bd_ref,
                           r_ref, w_ref, k_ref, v_ref, kkn_ref, b_ref, g_ref):
    front = (mu_ref, w0_ref, a0_ref, kk_ref, ka_ref, lw_ref, g2_ref, bd_ref)
    r, logw, kmod, v, kk, bb, g = _rwkv_front(rw_ref[...], prev_ref[...], front)
    r_ref[...] = r
    w_ref[...] = jnp.exp(logw)
    k_ref[...] = kmod
    v_ref[...] = v
    kkn_ref[...] = kk
    b_ref[...] = bb
    g_ref[...] = g


def _rwkv_dec_state_kernel(r_ref, w_ref, k_ref, v_ref, kk_ref, b_ref, s_ref, y_ref, so_ref):
    ri = lax.broadcasted_iota(jnp.int32, (HEAD_DIM, HEAD_DIM), 0)
    ci = lax.broadcasted_iota(jnp.int32, (HEAD_DIM, HEAD_DIM), 1)
    eye = ri == ci
    ones = jnp.ones((HEAD_DIM, LANES), BF16)
    heads = range(RWKV_HEADS)
    hs = [slice(h * HEAD_DIM, (h + 1) * HEAD_DIM) for h in heads]
    n = RWKV_HEADS * HEAD_DIM
    s = [s_ref[h] for h in heads]
    lhs = [s[h] * kk_ref[:, hs[h]] for h in heads] + [jnp.where(eye, v_ref[:, hs[h]], 0.0) for h in heads]
    red = _mm2(jnp.concatenate(lhs, axis=0), ones)
    s_new = []
    for h in heads:
        sa = -red[h * HEAD_DIM:(h + 1) * HEAD_DIM, 0:HEAD_DIM]
        v_col = red[n + h * HEAD_DIM:n + (h + 1) * HEAD_DIM, 0:HEAD_DIM]
        s_new.append(s[h] * w_ref[:, hs[h]] + sa * b_ref[:, hs[h]] + v_col * k_ref[:, hs[h]])
        so_ref[h] = s_new[h]
    yb = _mm2(jnp.concatenate([s_new[h] * r_ref[:, hs[h]] for h in heads], axis=0), ones)
    for h in heads:
        y_blk = yb[h * HEAD_DIM:(h + 1) * HEAD_DIM, 0:HEAD_DIM]
        y_ref[:, hs[h]] = jnp.sum(jnp.where(eye, y_blk, 0.0), axis=0, keepdims=True)


def _rwkv_dec_back_kernel(y_ref, r_ref, k_ref, v_ref, g_ref, rk_ref, lnw_ref, lnb_ref, bd_ref, o_ref):
    back = (rk_ref, lnw_ref, lnb_ref, bd_ref)
    o_ref[...] = _rwkv_back(y_ref[...], r_ref[...], k_ref[...], v_ref[...], g_ref[...], back).astype(BF16)


def _rwkv_decode(rw, shift0, wkv0, front_params, back_params):
    b = rw.shape[0]
    vm = pl.BlockSpec(memory_space=pltpu.VMEM)
    wide = jax.ShapeDtypeStruct((b, RWKV_WIDTH), F32)
    r, w, k, v, kk, bb, g = pl.pallas_call(
        _rwkv_dec_front_kernel,
        in_specs=[vm] * (2 + len(front_params)),
        out_specs=[vm] * 7,
        out_shape=[wide] * 7,
        name="rwkv_dec_front",
    )(rw, shift0, *front_params)
    rowspec = pl.BlockSpec((None, 1, RWKV_WIDTH), lambda i: (i, 0, 0))
    stspec = pl.BlockSpec((None, RWKV_HEADS, HEAD_DIM, HEAD_DIM), lambda i: (i, 0, 0, 0))
    as_rows = lambda t: t.reshape(b, 1, RWKV_WIDTH)
    y, s_new = pl.pallas_call(
        _rwkv_dec_state_kernel,
        grid=(b,),
        in_specs=[rowspec] * 6 + [stspec],
        out_specs=[rowspec, stspec],
        out_shape=[jax.ShapeDtypeStruct((b, 1, RWKV_WIDTH), F32),
                   jax.ShapeDtypeStruct((b, RWKV_HEADS, HEAD_DIM, HEAD_DIM), F32)],
        compiler_params=_cparams(("arbitrary",)),
        name="rwkv_dec_state",
    )(as_rows(r), as_rows(w), as_rows(k), as_rows(v), as_rows(kk), as_rows(bb), wkv0)
    o = pl.pallas_call(
        _rwkv_dec_back_kernel,
        in_specs=[vm] * (5 + len(back_params)),
        out_specs=vm,
        out_shape=jax.ShapeDtypeStruct((b, RWKV_WIDTH), BF16),
        name="rwkv_dec_back",
    )(y.reshape(b, RWKV_WIDTH), r, k, v, g, *back_params)
    return o, s_new


def kernel(x_prompt, x_sample, cache_k, cache_v, state_wkv, state_shift, rel_bias, ffn1_norm, ffn1_w_gate, ffn1_w_up, ffn1_w_down, mix_norm, w_in, attn_sinks, shift_mu, decay_w0, decay_w2, aaa_a0, aaa_a2, gate_g2, key_k, key_a, bonus_r_k, ln_x_w, ln_x_b, w_out, ffn2_norm, ffn2_w_gate, ffn2_w_up, ffn2_w_down, final_norm):
    batch, seq, _ = x_prompt.shape
    dec_b = x_sample.shape[0]
    lw = cache_k.shape[2]
    l = 0
    row = lambda t: t.reshape(1, -1)

    w_in_b = w_in[l].astype(BF16)
    w_out_b = w_out[l].astype(BF16)
    lora_w = jnp.zeros((LANES, 2 * RWKV_WIDTH), F32)
    lora_w = lora_w.at[0:DECAY_LORA, 0:RWKV_WIDTH].set(decay_w2[l])
    lora_w = lora_w.at[DECAY_LORA:LANES, RWKV_WIDTH:].set(aaa_a2[l])
    head_of = np.arange(LANES) // HEAD_DIM
    bd_ones = jnp.asarray(head_of[:, None] == head_of[None, :], BF16)
    front_params = (row(shift_mu[l]), row(decay_w0[l]), row(aaa_a0[l]), row(key_k[l]), row(key_a[l]),
                    lora_w.astype(BF16), gate_g2[l].astype(BF16), bd_ones)
    back_params = (row(bonus_r_k[l]), row(ln_x_w[l]), row(ln_x_b[l]), bd_ones)

    bias = _bias_table(rel_bias)
    sinks = attn_sinks[l]

    xs = x_sample.reshape(dec_b, D_MODEL)
    x1s, h1s, *f1 = _ffn(xs, row(ffn1_norm[l]), ffn1_w_gate[l], ffn1_w_up[l], ffn1_w_down[l], row(mix_norm[l]),
                         tm=dec_b, tf=FF_TILE_CAST, emit_x=True, n_dtype=BF16, emit_w=True)
    qs, kvs, rws = _proj(h1s, w_in_b, tm=dec_b)
    sinks_b = jnp.broadcast_to(sinks[:, None], (N_Q_HEADS, LANES))
    o_att_s = _attn_decode(qs, kvs, cache_k[l], cache_v[l], bias[1, :, 0, :], sinks_b)
    o_rw_s, s_s = _rwkv_decode(rws, state_shift[l], state_wkv[l], front_params, back_params)
    x2s = _outproj(x1s, o_att_s.astype(BF16), o_rw_s, w_out_b, tm=dec_b)
    y_s, *f2 = _ffn(x2s, row(ffn2_norm[l]), ffn2_w_gate[l], ffn2_w_up[l], ffn2_w_down[l], row(final_norm),
                    tm=dec_b, tf=FF_TILE_CAST, emit_x=False, n_dtype=F32, emit_w=True)

    xp = x_prompt.reshape(batch * seq, D_MODEL)
    x1, h1 = _ffn_stream(xp, row(ffn1_norm[l]), *f1, row(mix_norm[l]), tm=512, emit_x=True, n_dtype=BF16)
    q, kv, rw = _proj(h1, w_in_b, tm=256)
    o_att = _attn_prompt(q, kv, bias, sinks, batch=batch, seq=seq)
    o_rw, s_p = _rwkv_prompt(rw, front_params + back_params[:3], batch=batch, seq=seq)
    x2 = _outproj(x1, o_att, o_rw, w_out_b, tm=512)
    (y_p,) = _ffn_stream(x2, row(ffn2_norm[l]), *f2, row(final_norm), tm=512, emit_x=False, n_dtype=F32)

    kv3 = kv.reshape(batch, seq, 2 * KV_WIDTH)
    lp = min(WINDOW, seq)
    new_k_p = kv3[:, seq - lp:, 0:KV_WIDTH].reshape(1, batch, lp, N_KV_HEADS, HEAD_DIM)
    new_v_p = kv3[:, seq - lp:, KV_WIDTH:].reshape(1, batch, lp, N_KV_HEADS, HEAD_DIM)
    new_shift_p = rw.reshape(batch, seq, SHIFT_COLS)[:, seq - 1][None]
    k_new = kvs[:, 0:KV_WIDTH].reshape(dec_b, 1, N_KV_HEADS, HEAD_DIM)
    v_new = kvs[:, KV_WIDTH:].reshape(dec_b, 1, N_KV_HEADS, HEAD_DIM)
    new_k_s = jnp.concatenate([cache_k[l], k_new], axis=1)[:, -lw:][None]
    new_v_s = jnp.concatenate([cache_v[l], v_new], axis=1)[:, -lw:][None]
    return (y_p.reshape(batch, seq, D_MODEL), y_s.reshape(dec_b, 1, D_MODEL),
            new_k_p, new_v_p, s_p[None], new_shift_p,
            new_k_s, new_v_s, s_s[None], rws[None])
```

```python
import functools
import math

import numpy as np
import jax
import jax.numpy as jnp
from jax import lax
from jax.experimental import pallas as pl
from jax.experimental.pallas import tpu as pltpu

F32 = jnp.float32
BF16 = jnp.bfloat16

D_MODEL = 2048
HEAD_DIM = 64
ATT_WIDTH = 1024
N_Q_HEADS = 16
N_KV_HEADS = 4
GQA_GROUP = 4
KV_WIDTH = 256
RWKV_WIDTH = 1024
RWKV_HEADS = 16
WINDOW = 128
BLOCK = 128
N_BUCKETS = 32
MAX_DISTANCE = 128
DECAY_LORA = 64
AAA_LORA = 64
GATE_LORA = 128
D_FF = 5504
ATT_COLS = ATT_WIDTH + 2 * KV_WIDTH
SHIFT_COLS = 3 * RWKV_WIDTH + DECAY_LORA + AAA_LORA + GATE_LORA
IN_COLS = ATT_COLS + SHIFT_COLS
RMS_EPS = 1e-5
GN_EPS = 64e-5
FFN_RES = 0.5

LANES = 128
VMEM_LIMIT_BYTES = 60 * 1024 * 1024

FF_TILE = 1024
FF_TILE_CAST = 512
CHUNK = 64
CHUNKS_PER_STEP = 8
ATT_BLOCKS_PER_STEP = 8
HEAD_PAIRS = RWKV_HEADS // 2
NEG_BIG = -1e30
DECAY_SCALE = math.exp(-0.5)

_NN = (((1,), (0,)), ((), ()))
_NT = (((1,), (1,)), ((), ()))
_TN = (((0,), (0,)), ((), ()))


def _dg(a, b, dims=_NN):
    return lax.dot_general(a, b, dims, preferred_element_type=F32)


def _mm(a, b, dims=_NN):
    return _dg(a.astype(BF16), b.astype(BF16), dims)


def _split2(x):
    hi = x.astype(BF16)
    lo = (x - hi.astype(F32)).astype(BF16)
    return hi, lo


def _mm2(a, b_bf16, dims=_NN):
    ah, al = _split2(a)
    return _dg(ah, b_bf16, dims) + _dg(al, b_bf16, dims)


def _mm2r(a_bf16, b, dims=_NN):
    bh, bl = _split2(b)
    return _dg(a_bf16, bh, dims) + _dg(a_bf16, bl, dims)


def _sigmoid(x):
    return 1.0 / (1.0 + jnp.exp(-x))


def _rms(x, g):
    ms = jnp.mean(x * x, axis=-1, keepdims=True)
    return x * lax.rsqrt(ms + RMS_EPS) * g


def _cparams(sem):
    return pltpu.CompilerParams(dimension_semantics=sem, vmem_limit_bytes=VMEM_LIMIT_BYTES)


def _swiglu_down(xn, wg, wu, wd):
    gate = jnp.dot(xn, wg, preferred_element_type=F32)
    up = jnp.dot(xn, wu, preferred_element_type=F32)
    h = (gate * _sigmoid(gate) * up).astype(BF16)
    return jnp.dot(h, wd, preferred_element_type=F32)


_FF_TAIL_STEP = 2


def _ff_block(j, *, steps):
    return jnp.where(j < _FF_TAIL_STEP, j, jnp.where(j == _FF_TAIL_STEP, steps - 1, j - 1))


def _ffn_kernel(x_ref, g_ref, wg_ref, wu_ref, wd_ref, g2_ref, *rest, emit_x, n_dtype, tf, emit_w):
    rest = list(rest)
    ox_ref = rest.pop(0) if emit_x else None
    on_ref = rest.pop(0)
    wgo_ref, wuo_ref, wdo_ref = (rest.pop(0), rest.pop(0), rest.pop(0)) if emit_w else (None, None, None)
    xn_scr, acc_scr = rest
    j = pl.program_id(1)

    @pl.when(j == 0)
    def _():
        xn_scr[...] = _rms(x_ref[...], g_ref[...]).astype(BF16)
        acc_scr[...] = jnp.zeros_like(acc_scr)

    steps = pl.cdiv(D_FF, tf)
    last = j == steps - 1
    tail = j == _FF_TAIL_STEP

    def accumulate(width):
        wg = wg_ref[:, 0:width].astype(BF16)
        wu = wu_ref[:, 0:width].astype(BF16)
        wd = wd_ref[0:width, :].astype(BF16)
        if emit_w:
            wgo_ref[:, 0:width] = wg
            wuo_ref[:, 0:width] = wu
            wdo_ref[0:width, :] = wd
        acc_scr[...] += _swiglu_down(xn_scr[...], wg, wu, wd)

    @pl.when(jnp.logical_not(tail))
    def _():
        accumulate(tf)

    @pl.when(tail)
    def _():
        accumulate(D_FF - (steps - 1) * tf)

    @pl.when(last)
    def _():
        y = x_ref[...] + FFN_RES * acc_scr[...]
        if emit_x:
            ox_ref[...] = y
        on_ref[...] = _rms(y, g2_ref[...]).astype(n_dtype)


def _ffn(x, g, wg, wu, wd, g2, *, tm, tf, emit_x, n_dtype, emit_w=False):
    m = x.shape[0]
    assert not emit_w or m == tm
    grid = (m // tm, pl.cdiv(D_FF, tf))
    row = pl.BlockSpec((tm, D_MODEL), lambda i, j: (i, 0))
    vec = pl.BlockSpec((1, D_MODEL), lambda i, j: (0, 0))
    steps = pl.cdiv(D_FF, tf)
    blk = functools.partial(_ff_block, steps=steps)
    w_in = pl.BlockSpec((D_MODEL, tf), lambda i, j: (0, blk(j)))
    w_dn = pl.BlockSpec((tf, D_MODEL), lambda i, j: (blk(j), 0))
    out_shape = [jax.ShapeDtypeStruct((m, D_MODEL), n_dtype)]
    out_specs = [row]
    if emit_x:
        out_shape = [jax.ShapeDtypeStruct((m, D_MODEL), F32)] + out_shape
        out_specs = [row, row]
    if emit_w:
        out_shape += [jax.ShapeDtypeStruct(w.shape, BF16) for w in (wg, wu, wd)]
        out_specs += [w_in, w_in, w_dn]
    return pl.pallas_call(
        functools.partial(_ffn_kernel, emit_x=emit_x, n_dtype=n_dtype, tf=tf, emit_w=emit_w),
        grid=grid,
        in_specs=[row, vec, w_in, w_in, w_dn, vec],
        out_specs=out_specs,
        out_shape=out_shape,
        scratch_shapes=[pltpu.VMEM((tm, D_MODEL), BF16), pltpu.VMEM((tm, D_MODEL), F32)],
        compiler_params=_cparams(("arbitrary", "arbitrary")),
        name="ffn_cast" if emit_w else "ffn",
    )(x, g, wg, wu, wd, g2)


def _ff_sweep():
    steps = pl.cdiv(D_FF, FF_TILE)
    order = list(range(_FF_TAIL_STEP)) + [steps - 1] + list(range(_FF_TAIL_STEP, steps - 1))
    return [(b, min(FF_TILE, D_FF - b * FF_TILE)) for b in order]


def _ffn_stream_kernel(x_ref, g_ref, wg_hbm, wu_hbm, wd_hbm, g2_ref, *rest, emit_x, n_dtype):
    rest = list(rest)
    ox_ref = rest.pop(0) if emit_x else None
    on_ref = rest.pop(0)
    wg_buf, wu_buf, wd_buf, acc_scr, sem = rest
    i = pl.program_id(0)
    sweep = _ff_sweep()
    assert len(sweep) % 2 == 0

    def copies(pos):
        blk, width = sweep[pos]
        slot = pos % 2
        cols = pl.ds(blk * FF_TILE, width)
        return (pltpu.make_async_copy(wg_hbm.at[:, cols], wg_buf.at[slot, :, pl.ds(0, width)], sem.at[0, slot]),
                pltpu.make_async_copy(wu_hbm.at[:, cols], wu_buf.at[slot, :, pl.ds(0, width)], sem.at[1, slot]),
                pltpu.make_async_copy(wd_hbm.at[cols, :], wd_buf.at[slot, pl.ds(0, width), :], sem.at[2, slot]))

    def start(pos):
        for cp in copies(pos):
            cp.start()

    @pl.when(i == 0)
    def _():
        start(0)

    for pos, (_, width) in enumerate(sweep):
        slot = pos % 2
        for cp in copies(pos):
            cp.wait()
        if pos + 1 < len(sweep):
            start(pos + 1)
        else:
            @pl.when(i + 1 < pl.num_programs(0))
            def _():
                start(0)
        if pos == 0:
            xn = _rms(x_ref[...], g_ref[...]).astype(BF16)
        part = _swiglu_down(xn, wg_buf[slot, :, 0:width], wu_buf[slot, :, 0:width], wd_buf[slot, 0:width, :])
        if pos == 0:
            acc_scr[...] = part
        else:
            acc_scr[...] += part

    y = x_ref[...] + FFN_RES * acc_scr[...]
    if emit_x:
        ox_ref[...] = y
    on_ref[...] = _rms(y, g2_ref[...]).astype(n_dtype)


def _ffn_stream(x, g, wg, wu, wd, g2, *, tm, emit_x, n_dtype):
    m = x.shape[0]
    row = pl.BlockSpec((tm, D_MODEL), lambda i: (i, 0))
    vec = pl.BlockSpec((1, D_MODEL), lambda i: (0, 0))
    hbm = pl.BlockSpec(memory_space=pl.ANY)
    out_shape = [jax.ShapeDtypeStruct((m, D_MODEL), n_dtype)]
    out_specs = [row]
    if emit_x:
        out_shape = [jax.ShapeDtypeStruct((m, D_MODEL), F32)] + out_shape
        out_specs = [row, row]
    return pl.pallas_call(
        functools.partial(_ffn_stream_kernel, emit_x=emit_x, n_dtype=n_dtype),
        grid=(m // tm,),
        in_specs=[row, vec, hbm, hbm, hbm, vec],
        out_specs=out_specs,
        out_shape=out_shape,
        scratch_shapes=[pltpu.VMEM((2, D_MODEL, FF_TILE), BF16), pltpu.VMEM((2, D_MODEL, FF_TILE), BF16),
                        pltpu.VMEM((2, FF_TILE, D_MODEL), BF16), pltpu.VMEM((tm, D_MODEL), F32),
                        pltpu.SemaphoreType.DMA((3, 2))],
        compiler_params=_cparams(("arbitrary",)),
        name="ffn",
    )(x, g, wg, wu, wd, g2)


def _proj_kernel(h_ref, w_ref, q_ref, kv_ref, rw_ref):
    h = h_ref[...]
    q = jnp.dot(h, w_ref[:, 0:ATT_WIDTH], preferred_element_type=F32)
    q_ref[...] = (q * (HEAD_DIM ** -0.5)).astype(BF16)
    kv_ref[...] = jnp.dot(h, w_ref[:, ATT_WIDTH:ATT_COLS], preferred_element_type=F32)
    rw_ref[...] = jnp.dot(h, w_ref[:, ATT_COLS:IN_COLS], preferred_element_type=F32)


def _proj(h, w, *, tm):
    m = h.shape[0]
    return pl.pallas_call(
        _proj_kernel,
        grid=(m // tm,),
        in_specs=[pl.BlockSpec((tm, D_MODEL), lambda i: (i, 0)),
                  pl.BlockSpec((D_MODEL, IN_COLS), lambda i: (0, 0), pipeline_mode=pl.Buffered(1))],
        out_specs=[pl.BlockSpec((tm, ATT_WIDTH), lambda i: (i, 0)),
                   pl.BlockSpec((tm, 2 * KV_WIDTH), lambda i: (i, 0)),
                   pl.BlockSpec((tm, SHIFT_COLS), lambda i: (i, 0))],
        out_shape=[jax.ShapeDtypeStruct((m, ATT_WIDTH), BF16),
                   jax.ShapeDtypeStruct((m, 2 * KV_WIDTH), F32),
                   jax.ShapeDtypeStruct((m, SHIFT_COLS), F32)],
        compiler_params=_cparams(("arbitrary",)),
        name="in_proj",
    )(h, w)


def _outproj_kernel(x_ref, oa_ref, orw_ref, w_ref, o_ref):
    acc = jnp.dot(oa_ref[...], w_ref[0:ATT_WIDTH, :], preferred_element_type=F32)
    acc += jnp.dot(orw_ref[...], w_ref[ATT_WIDTH:D_MODEL, :], preferred_element_type=F32)
    o_ref[...] = x_ref[...] + acc


def _outproj(x, oa, orw, w, *, tm):
    m = x.shape[0]
    row = pl.BlockSpec((tm, D_MODEL), lambda i: (i, 0))
    half = pl.BlockSpec((tm, ATT_WIDTH), lambda i: (i, 0))
    return pl.pallas_call(
        _outproj_kernel,
        grid=(m // tm,),
        in_specs=[row, half, half,
                  pl.BlockSpec((D_MODEL, D_MODEL), lambda i: (0, 0), pipeline_mode=pl.Buffered(1))],
        out_specs=row,
        out_shape=jax.ShapeDtypeStruct((m, D_MODEL), F32),
        compiler_params=_cparams(("arbitrary",)),
        name="out_proj",
    )(x, oa, orw, w)


def _bucket_table():
    qi = np.arange(BLOCK)[:, None]
    kj = np.arange(2 * BLOCK)[None, :]
    dist = BLOCK + qi - kj
    n = np.maximum(dist, 0)
    max_exact = N_BUCKETS // 2
    nf = np.maximum(n, 1).astype(np.float32)
    large = max_exact + (np.log(nf / np.float32(max_exact)) / np.float32(math.log(MAX_DISTANCE / max_exact))
                         * np.float32(N_BUCKETS - max_exact)).astype(np.int32)
    large = np.minimum(large, N_BUCKETS - 1)
    bucket = np.where(n < max_exact, n, large).astype(np.int32)
    valid = (dist >= 0) & (dist <= WINDOW)
    return np.where(valid, bucket, -1).astype(np.int32)


def _bias_kernel(bucket_ref, rb_ref, o_ref):
    bucket = bucket_ref[...]
    kj = lax.broadcasted_iota(jnp.int32, (BLOCK, 2 * BLOCK), 1)
    for h in range(N_Q_HEADS):
        acc = jnp.full((BLOCK, 2 * BLOCK), NEG_BIG, F32)
        for n in range(N_BUCKETS):
            acc = jnp.where(bucket == n, rb_ref[n, h], acc)
        o_ref[0, h] = jnp.where(kj >= BLOCK, acc, NEG_BIG)
        o_ref[1, h] = acc


def _bias_table(rel_bias):
    bucket = jnp.asarray(_bucket_table())
    return pl.pallas_call(
        _bias_kernel,
        in_specs=[pl.BlockSpec(memory_space=pltpu.VMEM), pl.BlockSpec(memory_space=pltpu.SMEM)],
        out_specs=pl.BlockSpec(memory_space=pltpu.VMEM),
        out_shape=jax.ShapeDtypeStruct((2, N_Q_HEADS, BLOCK, 2 * BLOCK), F32),
        name="bias_table",
    )(bucket, rel_bias)


def _attn_kernel(sink_ref, q_ref, kvp_ref, kvc_ref, bias_ref, o_ref, kv_scr):
    j = pl.program_id(1)
    kv_scr[0:BLOCK, :] = kvp_ref[...]
    kv_scr[BLOCK:, :] = kvc_ref[...]

    def block(i, carry):
        r0 = pl.multiple_of(i * BLOCK, BLOCK)
        rows = pl.ds(r0, BLOCK)
        table = jnp.where(jnp.logical_and(j == 0, i == 0), 0, 1)
        _attn_block(sink_ref, q_ref.at[rows], kv_scr[rows, :], kv_scr[pl.ds(r0 + BLOCK, BLOCK), :],
                    bias_ref.at[table], o_ref.at[rows])
        return carry

    lax.fori_loop(0, ATT_BLOCKS_PER_STEP, block, 0)


def _attn_block(sink_ref, q_ref, kvp, kvc, bias_ref, o_ref):
    half = N_KV_HEADS // 2
    for g0 in range(0, N_KV_HEADS, half):
        kcat, vcat = {}, {}
        for g in range(g0, g0 + half):
            ks = slice(g * HEAD_DIM, (g + 1) * HEAD_DIM)
            vs = slice(KV_WIDTH + g * HEAD_DIM, KV_WIDTH + (g + 1) * HEAD_DIM)
            kcat[g] = jnp.concatenate([kvp[:, ks], kvc[:, ks]], axis=0).astype(BF16)
            vcat[g] = jnp.concatenate([kvp[:, vs], kvc[:, vs]], axis=0).astype(BF16)
        heads = range(g0 * GQA_GROUP, (g0 + half) * GQA_GROUP)
        hs = {h: slice(h * HEAD_DIM, (h + 1) * HEAD_DIM) for h in heads}
        s = {h: _dg(q_ref[:, hs[h]], kcat[h // GQA_GROUP], _NT) + bias_ref[h] for h in heads}
        m = {h: jnp.maximum(jnp.max(s[h], axis=-1, keepdims=True), sink_ref[h]) for h in heads}
        p = {h: jnp.exp(s[h] - m[h]) for h in heads}
        denom = {h: jnp.sum(p[h], axis=-1, keepdims=True) + jnp.exp(sink_ref[h] - m[h]) for h in heads}
        o = {h: jnp.dot(p[h].astype(BF16), vcat[h // GQA_GROUP], preferred_element_type=F32) for h in heads}
        for h in heads:
            o_ref[:, hs[h]] = (o[h] / denom[h]).astype(BF16)


def _attn_prompt(q, kv, bias, sinks, *, batch, seq):
    n = ATT_BLOCKS_PER_STEP
    rows = n * BLOCK
    q3 = q.reshape(batch, seq, ATT_WIDTH)
    kv3 = kv.reshape(batch, seq, 2 * KV_WIDTH)
    out = pl.pallas_call(
        _attn_kernel,
        grid=(batch, seq // rows),
        in_specs=[pl.BlockSpec(memory_space=pltpu.SMEM),
                  pl.BlockSpec((None, rows, ATT_WIDTH), lambda b, j: (b, j, 0)),
                  pl.BlockSpec((None, BLOCK, 2 * KV_WIDTH), lambda b, j: (b, jnp.maximum(n * j - 1, 0), 0)),
                  pl.BlockSpec((None, rows, 2 * KV_WIDTH), lambda b, j: (b, j, 0)),
                  pl.BlockSpec((2, N_Q_HEADS, BLOCK, 2 * BLOCK), lambda b, j: (0, 0, 0, 0))],
        out_specs=pl.BlockSpec((None, rows, ATT_WIDTH), lambda b, j: (b, j, 0)),
        out_shape=jax.ShapeDtypeStruct((batch, seq, ATT_WIDTH), BF16),
        scratch_shapes=[pltpu.VMEM((rows + BLOCK, 2 * KV_WIDTH), F32)],
        compiler_params=_cparams(("arbitrary", "arbitrary")),
        name="attn_prompt",
    )(sinks, q3, kv3, kv3, bias)
    return out.reshape(batch * seq, ATT_WIDTH)


def _attn_decode_kernel(sink_ref, q_ref, kvn_ref, ck_ref, cv_ref, bias_ref, o_ref):
    q = q_ref[...].astype(F32)
    kvn = kvn_ref[...]
    for g in range(N_KV_HEADS):
        ks = slice(g * HEAD_DIM, (g + 1) * HEAD_DIM)
        vs = slice(KV_WIDTH + g * HEAD_DIM, KV_WIDTH + (g + 1) * HEAD_DIM)
        hs = slice(g * GQA_GROUP, (g + 1) * GQA_GROUP)
        qg = q[:, hs, :]
        kc = ck_ref[:, :, ks]
        vc = cv_ref[:, :, ks]
        kn = kvn[:, ks].astype(BF16).astype(F32)[:, None, :]
        vn = kvn[:, vs].astype(BF16).astype(F32)[:, None, :]
        s = jnp.einsum("bqd,bkd->bqk", qg.astype(BF16), kc.astype(BF16), preferred_element_type=F32)
        s = s + bias_ref[hs, 0:WINDOW][None]
        s_new = jnp.sum(qg * kn, axis=-1, keepdims=True) + bias_ref[hs, WINDOW:WINDOW + 1][None]
        sink = sink_ref[hs, 0:1][None]
        m = jnp.maximum(jnp.maximum(jnp.max(s, axis=-1, keepdims=True), s_new), sink)
        p = jnp.exp(s - m)
        p_new = jnp.exp(s_new - m)
        denom = jnp.sum(p, axis=-1, keepdims=True) + p_new + jnp.exp(sink - m)
        o = jnp.einsum("bqk,bkd->bqd", p.astype(BF16), vc.astype(BF16), preferred_element_type=F32)
        o = o + p_new * vn
        o_ref[:, hs, :] = o / denom


def _attn_decode(q, kvn, cache_k, cache_v, bias_row, sinks_b):
    b = q.shape[0]
    lw = cache_k.shape[1]
    vm = pl.BlockSpec(memory_space=pltpu.VMEM)
    out = pl.pallas_call(
        _attn_decode_kernel,
        in_specs=[vm, vm, vm, vm, vm, vm],
        out_specs=vm,
        out_shape=jax.ShapeDtypeStruct((b, N_Q_HEADS, HEAD_DIM), F32),
        compiler_params=pltpu.CompilerParams(vmem_limit_bytes=VMEM_LIMIT_BYTES),
        name="attn_decode",
    )(sinks_b, q.reshape(b, N_Q_HEADS, HEAD_DIM), kvn,
      cache_k.reshape(b, lw, KV_WIDTH), cache_v.reshape(b, lw, KV_WIDTH), bias_row)
    return out.reshape(b, ATT_WIDTH)


def _seg_expand(x, bd_ref):
    rows = x.shape[0]
    groups = x.shape[1] // LANES
    xs = jnp.concatenate([x[:, i * LANES:(i + 1) * LANES] for i in range(groups)], axis=0)
    s = _mm(xs, bd_ref[...])
    return jnp.concatenate([s[i * rows:(i + 1) * rows] for i in range(groups)], axis=1)


def _rwkv_front(x, prev, p):
    mu_ref, w0_ref, a0_ref, kk_ref, ka_ref, lw_ref, g2_ref, bd_ref = p
    xm = x + mu_ref[...] * (prev - x)
    o3 = 3 * RWKV_WIDTH
    r = xm[:, 0:RWKV_WIDTH]
    k = xm[:, RWKV_WIDTH:2 * RWKV_WIDTH]
    v = xm[:, 2 * RWKV_WIDTH:o3]
    wa = xm[:, o3:o3 + LANES]
    lane = lax.broadcasted_iota(jnp.int32, wa.shape, 1)
    z = jnp.where(lane < DECAY_LORA, jnp.tanh(wa), wa)
    lora = _mm(z, lw_ref[...])
    logw = -DECAY_SCALE * _sigmoid(w0_ref[...] + lora[:, 0:RWKV_WIDTH])
    a = _sigmoid(a0_ref[...] + lora[:, RWKV_WIDTH:2 * RWKV_WIDTH])
    g = _mm(_sigmoid(xm[:, o3 + LANES:SHIFT_COLS]), g2_ref[...])
    kk = k * kk_ref[...]
    kmod = k * (1.0 + (a - 1.0) * ka_ref[...])
    nrm = jnp.sqrt(_seg_expand(kk * kk, bd_ref))
    kk = kk / jnp.maximum(nrm, 1e-12)
    return r, logw, kmod, v, kk, kk * a, g


def _rwkv_back(y, r, kmod, v, g, q):
    rk_ref, lnw_ref, lnb_ref, bd_ref = q
    mean = _seg_expand(y, bd_ref) * (1.0 / HEAD_DIM)
    d = y - mean
    var = _seg_expand(d * d, bd_ref) * (1.0 / HEAD_DIM)
    yn = d * lax.rsqrt(var + GN_EPS) * lnw_ref[...] + lnb_ref[...]
    bonus = _seg_expand(r * kmod * rk_ref[...], bd_ref) * v
    return (yn + bonus) * g


def _rwkv_chunk(rw_ref, mu_ref, w0_ref, a0_ref, kk_ref, ka_ref, lw_ref, g2_ref, bd_ref,
                rk_ref, lnw_ref, lnb_ref,
                o_ref,
                s_scr, prev_scr, y_scr, ea_scr, er_scr, eb_scr, ek_scr, be_scr, ke_scr, v_scr):
    C = CHUNK
    x = rw_ref[...]
    row = lax.broadcasted_iota(jnp.int32, x.shape, 0)
    prev = jnp.where(row == 0, prev_scr[...], pltpu.roll(x, 1, axis=0))
    prev_scr[...] = x[C - 1:C, :]
    front = (mu_ref, w0_ref, a0_ref, kk_ref, ka_ref, lw_ref, g2_ref, bd_ref)
    r, logw, kmod, v, kk, bb, g = _rwkv_front(x, prev, front)

    ti = lax.broadcasted_iota(jnp.int32, (C, C), 0)
    si = lax.broadcasted_iota(jnp.int32, (C, C), 1)
    cum = _mm2r(jnp.where(si <= ti, 1.0, 0.0).astype(BF16), logw)
    cum_end = cum[C - 1:C, :]
    e_in = jnp.exp(cum)
    e_neg = jnp.exp(-cum)
    e_end = jnp.exp(cum_end - cum)
    ea_scr[...] = kk * jnp.exp(cum - logw)
    er_scr[...] = r * e_in
    eb_scr[...] = bb * e_neg
    ek_scr[...] = kmod * e_neg
    be_scr[...] = bb * e_end
    ke_scr[...] = kmod * e_end
    v_scr[...] = v
    w_end = jnp.exp(cum_end)

    lane = lax.broadcasted_iota(jnp.int32, (C, LANES), 1)
    lo_half = lane < HEAD_DIM
    ri = lax.broadcasted_iota(jnp.int32, (LANES, LANES), 0)
    ci = lax.broadcasted_iota(jnp.int32, (LANES, LANES), 1)
    same_head = jnp.right_shift(ri, 6) == jnp.right_shift(ci, 6)
    eye = ri == ci
    t_loc = jnp.bitwise_and(ri, C - 1)
    s_loc = jnp.bitwise_and(ci, C - 1)
    bd_strict = jnp.logical_and(same_head, s_loc < t_loc)
    bd_incl = jnp.logical_and(same_head, s_loc <= t_loc)

    def swap(t):
        return pltpu.roll(t, HEAD_DIM, axis=1)

    def stack2(t):
        return jnp.concatenate([jnp.where(lo_half, t, 0.0), jnp.where(lo_half, 0.0, t)], axis=0)

    def stack2x(t):
        ts = swap(t)
        return jnp.concatenate([jnp.where(lo_half, 0.0, ts), jnp.where(lo_half, ts, 0.0)], axis=0)

    def own(t):
        return jnp.where(lo_half, t[0:C], t[C:2 * C])

    def other(t):
        return swap(jnp.where(lo_half, t[C:2 * C], t[0:C]))

    pairs = range(HEAD_PAIRS)
    sl = [slice(p * LANES, (p + 1) * LANES) for p in pairs]
    a2 = [stack2(ea_scr[:, sl[p]]) for p in pairs]
    m = []
    for p in pairs:
        r2 = stack2(er_scr[:, sl[p]])
        b_p = eb_scr[:, sl[p]]
        k_p = ek_scr[:, sl[p]]
        m.append(_mm(jnp.concatenate([a2[p], r2], axis=0), jnp.concatenate([b_p, b_p, k_p, k_p], axis=0), _NT))
    l_ak = [jnp.where(bd_strict, m[p][0:2 * C, LANES:2 * LANES], 0.0) for p in pairs]
    pw = [-jnp.where(bd_strict, m[p][0:2 * C, 0:LANES], 0.0) for p in pairs]
    lrbk = [jnp.concatenate([jnp.where(bd_incl, m[p][2 * C:4 * C, 0:LANES], 0.0),
                             jnp.where(bd_incl, m[p][2 * C:4 * C, LANES:2 * LANES], 0.0)], axis=1) for p in pairs]
    v2x = [stack2x(v_scr[:, sl[p]]) for p in pairs]
    xc = [a2[p] - _mm(l_ak[p], v2x[p]) for p in pairs]
    for it in range(6):
        for p in pairs:
            pw_b = pw[p].astype(BF16)
            xc_b = xc[p].astype(BF16)
            if it < 5:
                px = _dg(pw_b, jnp.concatenate([xc_b, pw_b], axis=1))
                pw[p] = px[:, LANES:2 * LANES]
                xc[p] = xc[p] + px[:, 0:LANES]
            else:
                xc[p] = xc[p] + _dg(pw_b, xc_b)
    xv = [jnp.concatenate([xc[p], v2x[p]], axis=0).astype(BF16) for p in pairs]
    lb = [_mm(lrbk[p], xv[p]) for p in pairs]
    z = [_mm(xv[p], jnp.concatenate([stack2(be_scr[:, sl[p]]), stack2(ke_scr[:, sl[p]])], axis=0), _TN)
         for p in pairs]
    rt = [er_scr[:, sl[p]] - own(lb[p]) for p in pairs]
    yl = [other(lb[p]) for p in pairs]
    d = [jnp.where(eye, w_end[:, sl[p]], 0.0) - jnp.where(same_head, z[p], 0.0) for p in pairs]
    gg = [jnp.where(same_head, jnp.concatenate([z[p][C:2 * C], z[p][0:C]], axis=0), 0.0) for p in pairs]
    s0 = [s_scr[p].astype(BF16) for p in pairs]
    ys = [_mm(rt[p], s0[p], _NT) for p in pairs]
    for p in pairs:
        y_scr[:, sl[p]] = yl[p] + ys[p]
    sn = [_mm(s0[p], d[p]) for p in pairs]
    for p in pairs:
        s_scr[p] = sn[p] + gg[p]

    back = (rk_ref, lnw_ref, lnb_ref, bd_ref)
    o_ref[...] = _rwkv_back(y_scr[...], r, kmod, v, g, back).astype(BF16)


def _rwkv_chunk_kernel(rw_ref, *rest):
    params, (o_ref, s_out_ref, s_scr, prev_scr), work = rest[:11], rest[11:15], rest[15:]
    c = pl.program_id(1)

    @pl.when(c == 0)
    def _():
        s_scr[...] = jnp.zeros_like(s_scr)
        prev_scr[...] = jnp.zeros_like(prev_scr)

    def chunk(i, carry):
        rows = pl.ds(pl.multiple_of(i * CHUNK, CHUNK), CHUNK)
        _rwkv_chunk(rw_ref.at[rows], *params, o_ref.at[rows], s_scr, prev_scr, *work)
        return carry

    lax.fori_loop(0, CHUNKS_PER_STEP, chunk, 0)

    @pl.when(c == pl.num_programs(1) - 1)
    def _():
        for p in range(HEAD_PAIRS):
            s_out_ref[2 * p] = s_scr[p, 0:HEAD_DIM, 0:HEAD_DIM]
            s_out_ref[2 * p + 1] = s_scr[p, HEAD_DIM:LANES, HEAD_DIM:LANES]


def _rwkv_prompt(rw, params, *, batch, seq):
    rw3 = rw.reshape(batch, seq, SHIFT_COLS)
    rows = CHUNKS_PER_STEP * CHUNK
    nc = seq // rows
    const = lambda shape: pl.BlockSpec(shape, lambda b, c: tuple(0 for _ in shape))
    in_specs = [pl.BlockSpec((None, rows, SHIFT_COLS), lambda b, c: (b, c, 0))]
    in_specs += [const(p.shape) for p in params]
    wide = pltpu.VMEM((CHUNK, RWKV_WIDTH), F32)
    o, s_out = pl.pallas_call(
        _rwkv_chunk_kernel,
        grid=(batch, nc),
        in_specs=in_specs,
        out_specs=[pl.BlockSpec((None, rows, RWKV_WIDTH), lambda b, c: (b, c, 0)),
                   pl.BlockSpec((None, RWKV_HEADS, HEAD_DIM, HEAD_DIM), lambda b, c: (b, 0, 0, 0))],
        out_shape=[jax.ShapeDtypeStruct((batch, seq, RWKV_WIDTH), BF16),
                   jax.ShapeDtypeStruct((batch, RWKV_HEADS, HEAD_DIM, HEAD_DIM), F32)],
        scratch_shapes=[pltpu.VMEM((HEAD_PAIRS, LANES, LANES), F32),
                        pltpu.VMEM((1, SHIFT_COLS), F32),
                        wide, wide, wide, wide, wide, wide, wide, wide],
        compiler_params=_cparams(("arbitrary", "arbitrary")),
        name="rwkv_prompt",
    )(rw3, *params)
    return o.reshape(batch * seq, RWKV_WIDTH), s_out


def _rwkv_dec_front_kernel(rw_ref, prev_ref, mu_ref, w0_ref, a0_ref, kk_ref, ka_ref, lw_ref, g2_ref, bd_ref,
                           r_ref, w_ref, k_ref, v_ref, kkn_ref, b_ref, g_ref):
    front = (mu_ref, w0_ref, a0_ref, kk_ref, ka_ref, lw_ref, g2_ref, bd_ref)
    r, logw, kmod, v, kk, bb, g = _rwkv_front(rw_ref[...], prev_ref[...], front)
    r_ref[...] = r
    w_ref[...] = jnp.exp(logw)
    k_ref[...] = kmod
    v_ref[...] = v
    kkn_ref[...] = kk
    b_ref[...] = bb
    g_ref[...] = g


def _rwkv_dec_state_kernel(r_ref, w_ref, k_ref, v_ref, kk_ref, b_ref, s_ref, y_ref, so_ref):
    ri = lax.broadcasted_iota(jnp.int32, (HEAD_DIM, HEAD_DIM), 0)
    ci = lax.broadcasted_iota(jnp.int32, (HEAD_DIM, HEAD_DIM), 1)
    eye = ri == ci
    ones = jnp.ones((HEAD_DIM, LANES), BF16)
    heads = range(RWKV_HEADS)
    hs = [slice(h * HEAD_DIM, (h + 1) * HEAD_DIM) for h in heads]
    n = RWKV_HEADS * HEAD_DIM
    s = [s_ref[h] for h in heads]
    lhs = [s[h] * kk_ref[:, hs[h]] for h in heads] + [jnp.where(eye, v_ref[:, hs[h]], 0.0) for h in heads]
    red = _mm2(jnp.concatenate(lhs, axis=0), ones)
    s_new = []
    for h in heads:
        sa = -red[h * HEAD_DIM:(h + 1) * HEAD_DIM, 0:HEAD_DIM]
        v_col = red[n + h * HEAD_DIM:n + (h + 1) * HEAD_DIM, 0:HEAD_DIM]
        s_new.append(s[h] * w_ref[:, hs[h]] + sa * b_ref[:, hs[h]] + v_col * k_ref[:, hs[h]])
        so_ref[h] = s_new[h]
    yb = _mm2(jnp.concatenate([s_new[h] * r_ref[:, hs[h]] for h in heads], axis=0), ones)
    for h in heads:
        y_blk = yb[h * HEAD_DIM:(h + 1) * HEAD_DIM, 0:HEAD_DIM]
        y_ref[:, hs[h]] = jnp.sum(jnp.where(eye, y_blk, 0.0), axis=0, keepdims=True)


def _rwkv_dec_back_kernel(y_ref, r_ref, k_ref, v_ref, g_ref, rk_ref, lnw_ref, lnb_ref, bd_ref, o_ref):
    back = (rk_ref, lnw_ref, lnb_ref, bd_ref)
    o_ref[...] = _rwkv_back(y_ref[...], r_ref[...], k_ref[...], v_ref[...], g_ref[...], back).astype(BF16)


def _rwkv_decode(rw, shift0, wkv0, front_params, back_params):
    b = rw.shape[0]
    vm = pl.BlockSpec(memory_space=pltpu.VMEM)
    wide = jax.ShapeDtypeStruct((b, RWKV_WIDTH), F32)
    r, w, k, v, kk, bb, g = pl.pallas_call(
        _rwkv_dec_front_kernel,
        in_specs=[vm] * (2 + len(front_params)),
        out_specs=[vm] * 7,
        out_shape=[wide] * 7,
        name="rwkv_dec_front",
    )(rw, shift0, *front_params)
    rowspec = pl.BlockSpec((None, 1, RWKV_WIDTH), lambda i: (i, 0, 0))
    stspec = pl.BlockSpec((None, RWKV_HEADS, HEAD_DIM, HEAD_DIM), lambda i: (i, 0, 0, 0))
    as_rows = lambda t: t.reshape(b, 1, RWKV_WIDTH)
    y, s_new = pl.pallas_call(
        _rwkv_dec_state_kernel,
        grid=(b,),
        in_specs=[rowspec] * 6 + [stspec],
        out_specs=[rowspec, stspec],
        out_shape=[jax.ShapeDtypeStruct((b, 1, RWKV_WIDTH), F32),
                   jax.ShapeDtypeStruct((b, RWKV_HEADS, HEAD_DIM, HEAD_DIM), F32)],
        compiler_params=_cparams(("arbitrary",)),
        name="rwkv_dec_state",
    )(as_rows(r), as_rows(w), as_rows(k), as_rows(v), as_rows(kk), as_rows(bb), wkv0)
    o = pl.pallas_call(
        _rwkv_dec_back_kernel,
        in_specs=[vm] * (5 + len(back_params)),
        out_specs=vm,
        out_shape=jax.ShapeDtypeStruct((b, RWKV_WIDTH), BF16),
        name="rwkv_dec_back",
    )(y.reshape(b, RWKV_WIDTH), r, k, v, g, *back_params)
    return o, s_new


def kernel(x_prompt, x_sample, cache_k, cache_v, state_wkv, state_shift, rel_bias, ffn1_norm, ffn1_w_gate, ffn1_w_up, ffn1_w_down, mix_norm, w_in, attn_sinks, shift_mu, decay_w0, decay_w2, aaa_a0, aaa_a2, gate_g2, key_k, key_a, bonus_r_k, ln_x_w, ln_x_b, w_out, ffn2_norm, ffn2_w_gate, ffn2_w_up, ffn2_w_down, final_norm):
    batch, seq, _ = x_prompt.shape
    dec_b = x_sample.shape[0]
    lw = cache_k.shape[2]
    l = 0
    row = lambda t: t.reshape(1, -1)

    w_in_b = w_in[l].astype(BF16)
    w_out_b = w_out[l].astype(BF16)
    lora_w = jnp.zeros((LANES, 2 * RWKV_WIDTH), F32)
    lora_w = lora_w.at[0:DECAY_LORA, 0:RWKV_WIDTH].set(decay_w2[l])
    lora_w = lora_w.at[DECAY_LORA:LANES, RWKV_WIDTH:].set(aaa_a2[l])
    head_of = np.arange(LANES) // HEAD_DIM
    bd_ones = jnp.asarray(head_of[:, None] == head_of[None, :], BF16)
    front_params = (row(shift_mu[l]), row(decay_w0[l]), row(aaa_a0[l]), row(key_k[l]), row(key_a[l]),
                    lora_w.astype(BF16), gate_g2[l].astype(BF16), bd_ones)
    back_params = (row(bonus_r_k[l]), row(ln_x_w[l]), row(ln_x_b[l]), bd_ones)

    bias = _bias_table(rel_bias)
    sinks = attn_sinks[l]

    xs = x_sample.reshape(dec_b, D_MODEL)
    x1s, h1s, *f1 = _ffn(xs, row(ffn1_norm[l]), ffn1_w_gate[l], ffn1_w_up[l], ffn1_w_down[l], row(mix_norm[l]),
                         tm=dec_b, tf=FF_TILE_CAST, emit_x=True, n_dtype=BF16, emit_w=True)
    qs, kvs, rws = _proj(h1s, w_in_b, tm=dec_b)
    sinks_b = jnp.broadcast_to(sinks[:, None], (N_Q_HEADS, LANES))
    o_att_s = _attn_decode(qs, kvs, cache_k[l], cache_v[l], bias[1, :, 0, :], sinks_b)
    o_rw_s, s_s = _rwkv_decode(rws, state_shift[l], state_wkv[l], front_params, back_params)
    x2s = _outproj(x1s, o_att_s.astype(BF16), o_rw_s, w_out_b, tm=dec_b)
    y_s, *f2 = _ffn(x2s, row(ffn2_norm[l]), ffn2_w_gate[l], ffn2_w_up[l], ffn2_w_down[l], row(final_norm),
                    tm=dec_b, tf=FF_TILE_CAST, emit_x=False, n_dtype=F32, emit_w=True)

    xp = x_prompt.reshape(batch * seq, D_MODEL)
    x1, h1 = _ffn_stream(xp, row(ffn1_norm[l]), *f1, row(mix_norm[l]), tm=512, emit_x=True, n_dtype=BF16)
    q, kv, rw = _proj(h1, w_in_b, tm=128)
    o_att = _attn_prompt(q, kv, bias, sinks, batch=batch, seq=seq)
    o_rw, s_p = _rwkv_prompt(rw, front_params + back_params[:3], batch=batch, seq=seq)
    x2 = _outproj(x1, o_att, o_rw, w_out_b, tm=512)
    (y_p,) = _ffn_stream(x2, row(ffn2_norm[l]), *f2, row(final_norm), tm=512, emit_x=False, n_dtype=F32)

    kv3 = kv.reshape(batch, seq, 2 * KV_WIDTH)
    lp = min(WINDOW, seq)
    new_k_p = kv3[:, seq - lp:, 0:KV_WIDTH].reshape(1, batch, lp, N_KV_HEADS, HEAD_DIM)
    new_v_p = kv3[:, seq - lp:, KV_WIDTH:].reshape(1, batch, lp, N_KV_HEADS, HEAD_DIM)
    new_shift_p = rw.reshape(batch, seq, SHIFT_COLS)[:, seq - 1][None]
    k_new = kvs[:, 0:KV_WIDTH].reshape(dec_b, 1, N_KV_HEADS, HEAD_DIM)
    v_new = kvs[:, KV_WIDTH:].reshape(dec_b, 1, N_KV_HEADS, HEAD_DIM)
    new_k_s = jnp.concatenate([cache_k[l], k_new], axis=1)[:, -lw:][None]
    new_v_s = jnp.concatenate([cache_v[l], v_new], axis=1)[:, -lw:][None]
    return (y_p.reshape(batch, seq, D_MODEL), y_s.reshape(dec_b, 1, D_MODEL),
            new_k_p, new_v_p, s_p[None], new_shift_p,
            new_k_s, new_v_s, s_s[None], rws[None])
```

```python
import functools
import math

import numpy as np
import jax
import jax.numpy as jnp
from jax import lax
from jax.experimental import pallas as pl
from jax.experimental.pallas import tpu as pltpu

F32 = jnp.float32
BF16 = jnp.bfloat16

D_MODEL = 2048
HEAD_DIM = 64
ATT_WIDTH = 1024
N_Q_HEADS = 16
N_KV_HEADS = 4
GQA_GROUP = 4
KV_WIDTH = 256
RWKV_WIDTH = 1024
RWKV_HEADS = 16
WINDOW = 128
BLOCK = 128
N_BUCKETS = 32
MAX_DISTANCE = 128
DECAY_LORA = 64
AAA_LORA = 64
GATE_LORA = 128
D_FF = 5504
ATT_COLS = ATT_WIDTH + 2 * KV_WIDTH
SHIFT_COLS = 3 * RWKV_WIDTH + DECAY_LORA + AAA_LORA + GATE_LORA
IN_COLS = ATT_COLS + SHIFT_COLS
RMS_EPS = 1e-5
GN_EPS = 64e-5
FFN_RES = 0.5

LANES = 128
VMEM_LIMIT_BYTES = 60 * 1024 * 1024

FF_TILE = 1024
FF_TILE_CAST = 512
CHUNK = 64
CHUNKS_PER_STEP = 4
ATT_BLOCKS_PER_STEP = 8
HEAD_PAIRS = RWKV_HEADS // 2
NEG_BIG = -1e30
DECAY_SCALE = math.exp(-0.5)

_NN = (((1,), (0,)), ((), ()))
_NT = (((1,), (1,)), ((), ()))
_TN = (((0,), (0,)), ((), ()))


def _dg(a, b, dims=_NN):
    return lax.dot_general(a, b, dims, preferred_element_type=F32)


def _mm(a, b, dims=_NN):
    return _dg(a.astype(BF16), b.astype(BF16), dims)


def _split2(x):
    hi = x.astype(BF16)
    lo = (x - hi.astype(F32)).astype(BF16)
    return hi, lo


def _mm2(a, b_bf16, dims=_NN):
    ah, al = _split2(a)
    return _dg(ah, b_bf16, dims) + _dg(al, b_bf16, dims)


def _mm2r(a_bf16, b, dims=_NN):
    bh, bl = _split2(b)
    return _dg(a_bf16, bh, dims) + _dg(a_bf16, bl, dims)


def _sigmoid(x):
    return 1.0 / (1.0 + jnp.exp(-x))


def _rms(x, g):
    ms = jnp.mean(x * x, axis=-1, keepdims=True)
    return x * lax.rsqrt(ms + RMS_EPS) * g


def _cparams(sem):
    return pltpu.CompilerParams(dimension_semantics=sem, vmem_limit_bytes=VMEM_LIMIT_BYTES)


def _swiglu_down(xn, wg, wu, wd):
    gate = jnp.dot(xn, wg, preferred_element_type=F32)
    up = jnp.dot(xn, wu, preferred_element_type=F32)
    h = (gate * _sigmoid(gate) * up).astype(BF16)
    return jnp.dot(h, wd, preferred_element_type=F32)


_FF_TAIL_STEP = 2


def _ff_block(j, *, steps):
    return jnp.where(j < _FF_TAIL_STEP, j, jnp.where(j == _FF_TAIL_STEP, steps - 1, j - 1))


def _ffn_kernel(x_ref, g_ref, wg_ref, wu_ref, wd_ref, g2_ref, *rest, emit_x, n_dtype, tf, emit_w):
    rest = list(rest)
    ox_ref = rest.pop(0) if emit_x else None
    on_ref = rest.pop(0)
    wgo_ref, wuo_ref, wdo_ref = (rest.pop(0), rest.pop(0), rest.pop(0)) if emit_w else (None, None, None)
    xn_scr, acc_scr = rest
    j = pl.program_id(1)

    @pl.when(j == 0)
    def _():
        xn_scr[...] = _rms(x_ref[...], g_ref[...]).astype(BF16)
        acc_scr[...] = jnp.zeros_like(acc_scr)

    steps = pl.cdiv(D_FF, tf)
    last = j == steps - 1
    tail = j == _FF_TAIL_STEP

    def accumulate(width):
        wg = wg_ref[:, 0:width].astype(BF16)
        wu = wu_ref[:, 0:width].astype(BF16)
        wd = wd_ref[0:width, :].astype(BF16)
        if emit_w:
            wgo_ref[:, 0:width] = wg
            wuo_ref[:, 0:width] = wu
            wdo_ref[0:width, :] = wd
        acc_scr[...] += _swiglu_down(xn_scr[...], wg, wu, wd)

    @pl.when(jnp.logical_not(tail))
    def _():
        accumulate(tf)

    @pl.when(tail)
    def _():
        accumulate(D_FF - (steps - 1) * tf)

    @pl.when(last)
    def _():
        y = x_ref[...] + FFN_RES * acc_scr[...]
        if emit_x:
            ox_ref[...] = y
        on_ref[...] = _rms(y, g2_ref[...]).astype(n_dtype)


def _ffn(x, g, wg, wu, wd, g2, *, tm, tf, emit_x, n_dtype, emit_w=False):
    m = x.shape[0]
    assert not emit_w or m == tm
    grid = (m // tm, pl.cdiv(D_FF, tf))
    row = pl.BlockSpec((tm, D_MODEL), lambda i, j: (i, 0))
    vec = pl.BlockSpec((1, D_MODEL), lambda i, j: (0, 0))
    steps = pl.cdiv(D_FF, tf)
    blk = functools.partial(_ff_block, steps=steps)
    w_in = pl.BlockSpec((D_MODEL, tf), lambda i, j: (0, blk(j)))
    w_dn = pl.BlockSpec((tf, D_MODEL), lambda i, j: (blk(j), 0))
    out_shape = [jax.ShapeDtypeStruct((m, D_MODEL), n_dtype)]
    out_specs = [row]
    if emit_x:
        out_shape = [jax.ShapeDtypeStruct((m, D_MODEL), F32)] + out_shape
        out_specs = [row, row]
    if emit_w:
        out_shape += [jax.ShapeDtypeStruct(w.shape, BF16) for w in (wg, wu, wd)]
        out_specs += [w_in, w_in, w_dn]
    return pl.pallas_call(
        functools.partial(_ffn_kernel, emit_x=emit_x, n_dtype=n_dtype, tf=tf, emit_w=emit_w),
        grid=grid,
        in_specs=[row, vec, w_in, w_in, w_dn, vec],
        out_specs=out_specs,
        out_shape=out_shape,
        scratch_shapes=[pltpu.VMEM((tm, D_MODEL), BF16), pltpu.VMEM((tm, D_MODEL), F32)],
        compiler_params=_cparams(("arbitrary", "arbitrary")),
        name="ffn_cast" if emit_w else "ffn",
    )(x, g, wg, wu, wd, g2)


def _ff_sweep():
    steps = pl.cdiv(D_FF, FF_TILE)
    order = list(range(_FF_TAIL_STEP)) + [steps - 1] + list(range(_FF_TAIL_STEP, steps - 1))
    return [(b, min(FF_TILE, D_FF - b * FF_TILE)) for b in order]


def _ffn_stream_kernel(x_ref, g_ref, wg_hbm, wu_hbm, wd_hbm, g2_ref, *rest, emit_x, n_dtype):
    rest = list(rest)
    ox_ref = rest.pop(0) if emit_x else None
    on_ref = rest.pop(0)
    wg_buf, wu_buf, wd_buf, acc_scr, sem = rest
    i = pl.program_id(0)
    sweep = _ff_sweep()
    assert len(sweep) % 2 == 0

    def copies(pos):
        blk, width = sweep[pos]
        slot = pos % 2
        cols = pl.ds(blk * FF_TILE, width)
        return (pltpu.make_async_copy(wg_hbm.at[:, cols], wg_buf.at[slot, :, pl.ds(0, width)], sem.at[0, slot]),
                pltpu.make_async_copy(wu_hbm.at[:, cols], wu_buf.at[slot, :, pl.ds(0, width)], sem.at[1, slot]),
                pltpu.make_async_copy(wd_hbm.at[cols, :], wd_buf.at[slot, pl.ds(0, width), :], sem.at[2, slot]))

    def start(pos):
        for cp in copies(pos):
            cp.start()

    @pl.when(i == 0)
    def _():
        start(0)

    for pos, (_, width) in enumerate(sweep):
        slot = pos % 2
        for cp in copies(pos):
            cp.wait()
        if pos + 1 < len(sweep):
            start(pos + 1)
        else:
            @pl.when(i + 1 < pl.num_programs(0))
            def _():
                start(0)
        if pos == 0:
            xn = _rms(x_ref[...], g_ref[...]).astype(BF16)
        part = _swiglu_down(xn, wg_buf[slot, :, 0:width], wu_buf[slot, :, 0:width], wd_buf[slot, 0:width, :])
        if pos == 0:
            acc_scr[...] = part
        else:
            acc_scr[...] += part

    y = x_ref[...] + FFN_RES * acc_scr[...]
    if emit_x:
        ox_ref[...] = y
    on_ref[...] = _rms(y, g2_ref[...]).astype(n_dtype)


def _ffn_stream(x, g, wg, wu, wd, g2, *, tm, emit_x, n_dtype):
    m = x.shape[0]
    row = pl.BlockSpec((tm, D_MODEL), lambda i: (i, 0))
    vec = pl.BlockSpec((1, D_MODEL), lambda i: (0, 0))
    hbm = pl.BlockSpec(memory_space=pl.ANY)
    out_shape = [jax.ShapeDtypeStruct((m, D_MODEL), n_dtype)]
    out_specs = [row]
    if emit_x:
        out_shape = [jax.ShapeDtypeStruct((m, D_MODEL), F32)] + out_shape
        out_specs = [row, row]
    return pl.pallas_call(
        functools.partial(_ffn_stream_kernel, emit_x=emit_x, n_dtype=n_dtype),
        grid=(m // tm,),
        in_specs=[row, vec, hbm, hbm, hbm, vec],
        out_specs=out_specs,
        out_shape=out_shape,
        scratch_shapes=[pltpu.VMEM((2, D_MODEL, FF_TILE), BF16), pltpu.VMEM((2, D_MODEL, FF_TILE), BF16),
                        pltpu.VMEM((2, FF_TILE, D_MODEL), BF16), pltpu.VMEM((tm, D_MODEL), F32),
                        pltpu.SemaphoreType.DMA((3, 2))],
        compiler_params=_cparams(("arbitrary",)),
        name="ffn",
    )(x, g, wg, wu, wd, g2)


def _proj_kernel(h_ref, w_ref, q_ref, kv_ref, rw_ref):
    h = h_ref[...]
    q = jnp.dot(h, w_ref[:, 0:ATT_WIDTH], preferred_element_type=F32)
    q_ref[...] = (q * (HEAD_DIM ** -0.5)).astype(BF16)
    kv_ref[...] = jnp.dot(h, w_ref[:, ATT_WIDTH:ATT_COLS], preferred_element_type=F32)
    rw_ref[...] = jnp.dot(h, w_ref[:, ATT_COLS:IN_COLS], preferred_element_type=F32)


def _proj(h, w, *, tm):
    m = h.shape[0]
    return pl.pallas_call(
        _proj_kernel,
        grid=(m // tm,),
        in_specs=[pl.BlockSpec((tm, D_MODEL), lambda i: (i, 0)),
                  pl.BlockSpec((D_MODEL, IN_COLS), lambda i: (0, 0), pipeline_mode=pl.Buffered(1))],
        out_specs=[pl.BlockSpec((tm, ATT_WIDTH), lambda i: (i, 0)),
                   pl.BlockSpec((tm, 2 * KV_WIDTH), lambda i: (i, 0)),
                   pl.BlockSpec((tm, SHIFT_COLS), lambda i: (i, 0))],
        out_shape=[jax.ShapeDtypeStruct((m, ATT_WIDTH), BF16),
                   jax.ShapeDtypeStruct((m, 2 * KV_WIDTH), F32),
                   jax.ShapeDtypeStruct((m, SHIFT_COLS), F32)],
        compiler_params=_cparams(("arbitrary",)),
        name="in_proj",
    )(h, w)


def _outproj_kernel(x_ref, oa_ref, orw_ref, w_ref, o_ref):
    acc = jnp.dot(oa_ref[...], w_ref[0:ATT_WIDTH, :], preferred_element_type=F32)
    acc += jnp.dot(orw_ref[...], w_ref[ATT_WIDTH:D_MODEL, :], preferred_element_type=F32)
    o_ref[...] = x_ref[...] + acc


def _outproj(x, oa, orw, w, *, tm):
    m = x.shape[0]
    row = pl.BlockSpec((tm, D_MODEL), lambda i: (i, 0))
    half = pl.BlockSpec((tm, ATT_WIDTH), lambda i: (i, 0))
    return pl.pallas_call(
        _outproj_kernel,
        grid=(m // tm,),
        in_specs=[row, half, half,
                  pl.BlockSpec((D_MODEL, D_MODEL), lambda i: (0, 0), pipeline_mode=pl.Buffered(1))],
        out_specs=row,
        out_shape=jax.ShapeDtypeStruct((m, D_MODEL), F32),
        compiler_params=_cparams(("arbitrary",)),
        name="out_proj",
    )(x, oa, orw, w)


def _bucket_table():
    qi = np.arange(BLOCK)[:, None]
    kj = np.arange(2 * BLOCK)[None, :]
    dist = BLOCK + qi - kj
    n = np.maximum(dist, 0)
    max_exact = N_BUCKETS // 2
    nf = np.maximum(n, 1).astype(np.float32)
    large = max_exact + (np.log(nf / np.float32(max_exact)) / np.float32(math.log(MAX_DISTANCE / max_exact))
                         * np.float32(N_BUCKETS - max_exact)).astype(np.int32)
    large = np.minimum(large, N_BUCKETS - 1)
    bucket = np.where(n < max_exact, n, large).astype(np.int32)
    valid = (dist >= 0) & (dist <= WINDOW)
    return np.where(valid, bucket, -1).astype(np.int32)


def _bias_kernel(bucket_ref, rb_ref, o_ref):
    bucket = bucket_ref[...]
    kj = lax.broadcasted_iota(jnp.int32, (BLOCK, 2 * BLOCK), 1)
    for h in range(N_Q_HEADS):
        acc = jnp.full((BLOCK, 2 * BLOCK), NEG_BIG, F32)
        for n in range(N_BUCKETS):
            acc = jnp.where(bucket == n, rb_ref[n, h], acc)
        o_ref[0, h] = jnp.where(kj >= BLOCK, acc, NEG_BIG)
        o_ref[1, h] = acc


def _bias_table(rel_bias):
    bucket = jnp.asarray(_bucket_table())
    return pl.pallas_call(
        _bias_kernel,
        in_specs=[pl.BlockSpec(memory_space=pltpu.VMEM), pl.BlockSpec(memory_space=pltpu.SMEM)],
        out_specs=pl.BlockSpec(memory_space=pltpu.VMEM),
        out_shape=jax.ShapeDtypeStruct((2, N_Q_HEADS, BLOCK, 2 * BLOCK), F32),
        name="bias_table",
    )(bucket, rel_bias)


def _attn_kernel(sink_ref, q_ref, kvp_ref, kvc_ref, bias_ref, o_ref, kv_scr):
    j = pl.program_id(1)
    kv_scr[0:BLOCK, :] = kvp_ref[...]
    kv_scr[BLOCK:, :] = kvc_ref[...]

    def block(i, carry):
        r0 = pl.multiple_of(i * BLOCK, BLOCK)
        rows = pl.ds(r0, BLOCK)
        table = jnp.where(jnp.logical_and(j == 0, i == 0), 0, 1)
        _attn_block(sink_ref, q_ref.at[rows], kv_scr[rows, :], kv_scr[pl.ds(r0 + BLOCK, BLOCK), :],
                    bias_ref.at[table], o_ref.at[rows])
        return carry

    lax.fori_loop(0, ATT_BLOCKS_PER_STEP, block, 0)


def _attn_block(sink_ref, q_ref, kvp, kvc, bias_ref, o_ref):
    half = N_KV_HEADS // 2
    for g0 in range(0, N_KV_HEADS, half):
        kcat, vcat = {}, {}
        for g in range(g0, g0 + half):
            ks = slice(g * HEAD_DIM, (g + 1) * HEAD_DIM)
            vs = slice(KV_WIDTH + g * HEAD_DIM, KV_WIDTH + (g + 1) * HEAD_DIM)
            kcat[g] = jnp.concatenate([kvp[:, ks], kvc[:, ks]], axis=0).astype(BF16)
            vcat[g] = jnp.concatenate([kvp[:, vs], kvc[:, vs]], axis=0).astype(BF16)
        heads = range(g0 * GQA_GROUP, (g0 + half) * GQA_GROUP)
        hs = {h: slice(h * HEAD_DIM, (h + 1) * HEAD_DIM) for h in heads}
        s = {h: _dg(q_ref[:, hs[h]], kcat[h // GQA_GROUP], _NT) + bias_ref[h] for h in heads}
        m = {h: jnp.maximum(jnp.max(s[h], axis=-1, keepdims=True), sink_ref[h]) for h in heads}
        p = {h: jnp.exp(s[h] - m[h]) for h in heads}
        denom = {h: jnp.sum(p[h], axis=-1, keepdims=True) + jnp.exp(sink_ref[h] - m[h]) for h in heads}
        o = {h: jnp.dot(p[h].astype(BF16), vcat[h // GQA_GROUP], preferred_element_type=F32) for h in heads}
        for h in heads:
            o_ref[:, hs[h]] = (o[h] / denom[h]).astype(BF16)


def _attn_prompt(q, kv, bias, sinks, *, batch, seq):
    n = ATT_BLOCKS_PER_STEP
    rows = n * BLOCK
    q3 = q.reshape(batch, seq, ATT_WIDTH)
    kv3 = kv.reshape(batch, seq, 2 * KV_WIDTH)
    out = pl.pallas_call(
        _attn_kernel,
        grid=(batch, seq // rows),
        in_specs=[pl.BlockSpec(memory_space=pltpu.SMEM),
                  pl.BlockSpec((None, rows, ATT_WIDTH), lambda b, j: (b, j, 0)),
                  pl.BlockSpec((None, BLOCK, 2 * KV_WIDTH), lambda b, j: (b, jnp.maximum(n * j - 1, 0), 0)),
                  pl.BlockSpec((None, rows, 2 * KV_WIDTH), lambda b, j: (b, j, 0)),
                  pl.BlockSpec((2, N_Q_HEADS, BLOCK, 2 * BLOCK), lambda b, j: (0, 0, 0, 0))],
        out_specs=pl.BlockSpec((None, rows, ATT_WIDTH), lambda b, j: (b, j, 0)),
        out_shape=jax.ShapeDtypeStruct((batch, seq, ATT_WIDTH), BF16),
        scratch_shapes=[pltpu.VMEM((rows + BLOCK, 2 * KV_WIDTH), F32)],
        compiler_params=_cparams(("arbitrary", "arbitrary")),
        name="attn_prompt",
    )(sinks, q3, kv3, kv3, bias)
    return out.reshape(batch * seq, ATT_WIDTH)


def _attn_decode_kernel(sink_ref, q_ref, kvn_ref, ck_ref, cv_ref, bias_ref, o_ref):
    q = q_ref[...].astype(F32)
    kvn = kvn_ref[...]
    for g in range(N_KV_HEADS):
        ks = slice(g * HEAD_DIM, (g + 1) * HEAD_DIM)
        vs = slice(KV_WIDTH + g * HEAD_DIM, KV_WIDTH + (g + 1) * HEAD_DIM)
        hs = slice(g * GQA_GROUP, (g + 1) * GQA_GROUP)
        qg = q[:, hs, :]
        kc = ck_ref[:, :, ks]
        vc = cv_ref[:, :, ks]
        kn = kvn[:, ks].astype(BF16).astype(F32)[:, None, :]
        vn = kvn[:, vs].astype(BF16).astype(F32)[:, None, :]
        s = jnp.einsum("bqd,bkd->bqk", qg.astype(BF16), kc.astype(BF16), preferred_element_type=F32)
        s = s + bias_ref[hs, 0:WINDOW][None]
        s_new = jnp.sum(qg * kn, axis=-1, keepdims=True) + bias_ref[hs, WINDOW:WINDOW + 1][None]
        sink = sink_ref[hs, 0:1][None]
        m = jnp.maximum(jnp.maximum(jnp.max(s, axis=-1, keepdims=True), s_new), sink)
        p = jnp.exp(s - m)
        p_new = jnp.exp(s_new - m)
        denom = jnp.sum(p, axis=-1, keepdims=True) + p_new + jnp.exp(sink - m)
        o = jnp.einsum("bqk,bkd->bqd", p.astype(BF16), vc.astype(BF16), preferred_element_type=F32)
        o = o + p_new * vn
        o_ref[:, hs, :] = o / denom


def _attn_decode(q, kvn, cache_k, cache_v, bias_row, sinks_b):
    b = q.shape[0]
    lw = cache_k.shape[1]
    vm = pl.BlockSpec(memory_space=pltpu.VMEM)
    out = pl.pallas_call(
        _attn_decode_kernel,
        in_specs=[vm, vm, vm, vm, vm, vm],
        out_specs=vm,
        out_shape=jax.ShapeDtypeStruct((b, N_Q_HEADS, HEAD_DIM), F32),
        compiler_params=pltpu.CompilerParams(vmem_limit_bytes=VMEM_LIMIT_BYTES),
        name="attn_decode",
    )(sinks_b, q.reshape(b, N_Q_HEADS, HEAD_DIM), kvn,
      cache_k.reshape(b, lw, KV_WIDTH), cache_v.reshape(b, lw, KV_WIDTH), bias_row)
    return out.reshape(b, ATT_WIDTH)


def _seg_expand(x, bd_ref):
    rows = x.shape[0]
    groups = x.shape[1] // LANES
    xs = jnp.concatenate([x[:, i * LANES:(i + 1) * LANES] for i in range(groups)], axis=0)
    s = _mm(xs, bd_ref[...])
    return jnp.concatenate([s[i * rows:(i + 1) * rows] for i in range(groups)], axis=1)


def _rwkv_front(x, prev, p):
    mu_ref, w0_ref, a0_ref, kk_ref, ka_ref, lw_ref, g2_ref, bd_ref = p
    xm = x + mu_ref[...] * (prev - x)
    o3 = 3 * RWKV_WIDTH
    r = xm[:, 0:RWKV_WIDTH]
    k = xm[:, RWKV_WIDTH:2 * RWKV_WIDTH]
    v = xm[:, 2 * RWKV_WIDTH:o3]
    wa = xm[:, o3:o3 + LANES]
    lane = lax.broadcasted_iota(jnp.int32, wa.shape, 1)
    z = jnp.where(lane < DECAY_LORA, jnp.tanh(wa), wa)
    lora = _mm(z, lw_ref[...])
    logw = -DECAY_SCALE * _sigmoid(w0_ref[...] + lora[:, 0:RWKV_WIDTH])
    a = _sigmoid(a0_ref[...] + lora[:, RWKV_WIDTH:2 * RWKV_WIDTH])
    g = _mm(_sigmoid(xm[:, o3 + LANES:SHIFT_COLS]), g2_ref[...])
    kk = k * kk_ref[...]
    kmod = k * (1.0 + (a - 1.0) * ka_ref[...])
    nrm = jnp.sqrt(_seg_expand(kk * kk, bd_ref))
    kk = kk / jnp.maximum(nrm, 1e-12)
    return r, logw, kmod, v, kk, kk * a, g


def _rwkv_back(y, r, kmod, v, g, q):
    rk_ref, lnw_ref, lnb_ref, bd_ref = q
    mean = _seg_expand(y, bd_ref) * (1.0 / HEAD_DIM)
    d = y - mean
    var = _seg_expand(d * d, bd_ref) * (1.0 / HEAD_DIM)
    yn = d * lax.rsqrt(var + GN_EPS) * lnw_ref[...] + lnb_ref[...]
    bonus = _seg_expand(r * kmod * rk_ref[...], bd_ref) * v
    return (yn + bonus) * g


def _rwkv_chunk(rw_ref, mu_ref, w0_ref, a0_ref, kk_ref, ka_ref, lw_ref, g2_ref, bd_ref,
                rk_ref, lnw_ref, lnb_ref,
                o_ref,
                s_scr, prev_scr, y_scr, ea_scr, er_scr, eb_scr, ek_scr, be_scr, ke_scr, v_scr):
    C = CHUNK
    x = rw_ref[...]
    row = lax.broadcasted_iota(jnp.int32, x.shape, 0)
    prev = jnp.where(row == 0, prev_scr[...], pltpu.roll(x, 1, axis=0))
    prev_scr[...] = x[C - 1:C, :]
    front = (mu_ref, w0_ref, a0_ref, kk_ref, ka_ref, lw_ref, g2_ref, bd_ref)
    r, logw, kmod, v, kk, bb, g = _rwkv_front(x, prev, front)

    ti = lax.broadcasted_iota(jnp.int32, (C, C), 0)
    si = lax.broadcasted_iota(jnp.int32, (C, C), 1)
    cum = _mm2r(jnp.where(si <= ti, 1.0, 0.0).astype(BF16), logw)
    cum_end = cum[C - 1:C, :]
    e_in = jnp.exp(cum)
    e_neg = jnp.exp(-cum)
    e_end = jnp.exp(cum_end - cum)
    ea_scr[...] = kk * jnp.exp(cum - logw)
    er_scr[...] = r * e_in
    eb_scr[...] = bb * e_neg
    ek_scr[...] = kmod * e_neg
    be_scr[...] = bb * e_end
    ke_scr[...] = kmod * e_end
    v_scr[...] = v
    w_end = jnp.exp(cum_end)

    lane = lax.broadcasted_iota(jnp.int32, (C, LANES), 1)
    lo_half = lane < HEAD_DIM
    ri = lax.broadcasted_iota(jnp.int32, (LANES, LANES), 0)
    ci = lax.broadcasted_iota(jnp.int32, (LANES, LANES), 1)
    same_head = jnp.right_shift(ri, 6) == jnp.right_shift(ci, 6)
    eye = ri == ci
    t_loc = jnp.bitwise_and(ri, C - 1)
    s_loc = jnp.bitwise_and(ci, C - 1)
    bd_strict = jnp.logical_and(same_head, s_loc < t_loc)
    bd_incl = jnp.logical_and(same_head, s_loc <= t_loc)

    def swap(t):
        return pltpu.roll(t, HEAD_DIM, axis=1)

    def stack2(t):
        return jnp.concatenate([jnp.where(lo_half, t, 0.0), jnp.where(lo_half, 0.0, t)], axis=0)

    def stack2x(t):
        ts = swap(t)
        return jnp.concatenate([jnp.where(lo_half, 0.0, ts), jnp.where(lo_half, ts, 0.0)], axis=0)

    def own(t):
        return jnp.where(lo_half, t[0:C], t[C:2 * C])

    def other(t):
        return swap(jnp.where(lo_half, t[C:2 * C], t[0:C]))

    pairs = range(HEAD_PAIRS)
    sl = [slice(p * LANES, (p + 1) * LANES) for p in pairs]
    a2 = [stack2(ea_scr[:, sl[p]]) for p in pairs]
    m = []
    for p in pairs:
        r2 = stack2(er_scr[:, sl[p]])
        b_p = eb_scr[:, sl[p]]
        k_p = ek_scr[:, sl[p]]
        m.append(_mm(jnp.concatenate([a2[p], r2], axis=0), jnp.concatenate([b_p, b_p, k_p, k_p], axis=0), _NT))
    l_ak = [jnp.where(bd_strict, m[p][0:2 * C, LANES:2 * LANES], 0.0) for p in pairs]
    pw = [-jnp.where(bd_strict, m[p][0:2 * C, 0:LANES], 0.0) for p in pairs]
    lrbk = [jnp.concatenate([jnp.where(bd_incl, m[p][2 * C:4 * C, 0:LANES], 0.0),
                             jnp.where(bd_incl, m[p][2 * C:4 * C, LANES:2 * LANES], 0.0)], axis=1) for p in pairs]
    v2x = [stack2x(v_scr[:, sl[p]]) for p in pairs]
    xc = [a2[p] - _mm(l_ak[p], v2x[p]) for p in pairs]
    for it in range(6):
        for p in pairs:
            pw_b = pw[p].astype(BF16)
            xc_b = xc[p].astype(BF16)
            if it < 5:
                px = _dg(pw_b, jnp.concatenate([xc_b, pw_b], axis=1))
                pw[p] = px[:, LANES:2 * LANES]
                xc[p] = xc[p] + px[:, 0:LANES]
            else:
                xc[p] = xc[p] + _dg(pw_b, xc_b)
    xv = [jnp.concatenate([xc[p], v2x[p]], axis=0).astype(BF16) for p in pairs]
    lb = [_mm(lrbk[p], xv[p]) for p in pairs]
    z = [_mm(xv[p], jnp.concatenate([stack2(be_scr[:, sl[p]]), stack2(ke_scr[:, sl[p]])], axis=0), _TN)
         for p in pairs]
    rt = [er_scr[:, sl[p]] - own(lb[p]) for p in pairs]
    yl = [other(lb[p]) for p in pairs]
    d = [jnp.where(eye, w_end[:, sl[p]], 0.0) - jnp.where(same_head, z[p], 0.0) for p in pairs]
    gg = [jnp.where(same_head, jnp.concatenate([z[p][C:2 * C], z[p][0:C]], axis=0), 0.0) for p in pairs]
    s0 = [s_scr[p].astype(BF16) for p in pairs]
    ys = [_mm(rt[p], s0[p], _NT) for p in pairs]
    for p in pairs:
        y_scr[:, sl[p]] = yl[p] + ys[p]
    sn = [_mm(s0[p], d[p]) for p in pairs]
    for p in pairs:
        s_scr[p] = sn[p] + gg[p]

    back = (rk_ref, lnw_ref, lnb_ref, bd_ref)
    o_ref[...] = _rwkv_back(y_scr[...], r, kmod, v, g, back).astype(BF16)


def _rwkv_chunk_kernel(rw_ref, *rest):
    params, (o_ref, s_out_ref, s_scr, prev_scr), work = rest[:11], rest[11:15], rest[15:]
    c = pl.program_id(1)

    @pl.when(c == 0)
    def _():
        s_scr[...] = jnp.zeros_like(s_scr)
        prev_scr[...] = jnp.zeros_like(prev_scr)

    for i in range(CHUNKS_PER_STEP):
        rows = pl.ds(i * CHUNK, CHUNK)
        _rwkv_chunk(rw_ref.at[rows], *params, o_ref.at[rows], s_scr, prev_scr, *[w.at[i] for w in work])

    @pl.when(c == pl.num_programs(1) - 1)
    def _():
        for p in range(HEAD_PAIRS):
            s_out_ref[2 * p] = s_scr[p, 0:HEAD_DIM, 0:HEAD_DIM]
            s_out_ref[2 * p + 1] = s_scr[p, HEAD_DIM:LANES, HEAD_DIM:LANES]


def _rwkv_prompt(rw, params, *, batch, seq):
    rw3 = rw.reshape(batch, seq, SHIFT_COLS)
    rows = CHUNKS_PER_STEP * CHUNK
    nc = seq // rows
    const = lambda shape: pl.BlockSpec(shape, lambda b, c: tuple(0 for _ in shape))
    in_specs = [pl.BlockSpec((None, rows, SHIFT_COLS), lambda b, c: (b, c, 0))]
    in_specs += [const(p.shape) for p in params]
    wide = pltpu.VMEM((CHUNKS_PER_STEP, CHUNK, RWKV_WIDTH), F32)
    o, s_out = pl.pallas_call(
        _rwkv_chunk_kernel,
        grid=(batch, nc),
        in_specs=in_specs,
        out_specs=[pl.BlockSpec((None, rows, RWKV_WIDTH), lambda b, c: (b, c, 0)),
                   pl.BlockSpec((None, RWKV_HEADS, HEAD_DIM, HEAD_DIM), lambda b, c: (b, 0, 0, 0))],
        out_shape=[jax.ShapeDtypeStruct((batch, seq, RWKV_WIDTH), BF16),
                   jax.ShapeDtypeStruct((batch, RWKV_HEADS, HEAD_DIM, HEAD_DIM), F32)],
        scratch_shapes=[pltpu.VMEM((HEAD_PAIRS, LANES, LANES), F32),
                        pltpu.VMEM((1, SHIFT_COLS), F32),
                        wide, wide, wide, wide, wide, wide, wide, wide],
        compiler_params=_cparams(("arbitrary", "arbitrary")),
        name="rwkv_prompt",
    )(rw3, *params)
    return o.reshape(batch * seq, RWKV_WIDTH), s_out


def _rwkv_dec_front_kernel(rw_ref, prev_ref, mu_ref, w0_ref, a0_ref, kk_ref, ka_ref, lw_ref, g2_ref, bd_ref,
                           r_ref, w_ref, k_ref, v_ref, kkn_ref, b_ref, g_ref):
    front = (mu_ref, w0_ref, a0_ref, kk_ref, ka_ref, lw_ref, g2_ref, bd_ref)
    r, logw, kmod, v, kk, bb, g = _rwkv_front(rw_ref[...], prev_ref[...], front)
    r_ref[...] = r
    w_ref[...] = jnp.exp(logw)
    k_ref[...] = kmod
    v_ref[...] = v
    kkn_ref[...] = kk
    b_ref[...] = bb
    g_ref[...] = g


def _rwkv_dec_state_kernel(r_ref, w_ref, k_ref, v_ref, kk_ref, b_ref, s_ref, y_ref, so_ref):
    ri = lax.broadcasted_iota(jnp.int32, (HEAD_DIM, HEAD_DIM), 0)
    ci = lax.broadcasted_iota(jnp.int32, (HEAD_DIM, HEAD_DIM), 1)
    eye = ri == ci
    ones = jnp.ones((HEAD_DIM, LANES), BF16)
    heads = range(RWKV_HEADS)
    hs = [slice(h * HEAD_DIM, (h + 1) * HEAD_DIM) for h in heads]
    n = RWKV_HEADS * HEAD_DIM
    s = [s_ref[h] for h in heads]
    lhs = [s[h] * kk_ref[:, hs[h]] for h in heads] + [jnp.where(eye, v_ref[:, hs[h]], 0.0) for h in heads]
    red = _mm2(jnp.concatenate(lhs, axis=0), ones)
    s_new = []
    for h in heads:
        sa = -red[h * HEAD_DIM:(h + 1) * HEAD_DIM, 0:HEAD_DIM]
        v_col = red[n + h * HEAD_DIM:n + (h + 1) * HEAD_DIM, 0:HEAD_DIM]
        s_new.append(s[h] * w_ref[:, hs[h]] + sa * b_ref[:, hs[h]] + v_col * k_ref[:, hs[h]])
        so_ref[h] = s_new[h]
    yb = _mm2(jnp.concatenate([s_new[h] * r_ref[:, hs[h]] for h in heads], axis=0), ones)
    for h in heads:
        y_blk = yb[h * HEAD_DIM:(h + 1) * HEAD_DIM, 0:HEAD_DIM]
        y_ref[:, hs[h]] = jnp.sum(jnp.where(eye, y_blk, 0.0), axis=0, keepdims=True)


def _rwkv_dec_back_kernel(y_ref, r_ref, k_ref, v_ref, g_ref, rk_ref, lnw_ref, lnb_ref, bd_ref, o_ref):
    back = (rk_ref, lnw_ref, lnb_ref, bd_ref)
    o_ref[...] = _rwkv_back(y_ref[...], r_ref[...], k_ref[...], v_ref[...], g_ref[...], back).astype(BF16)


def _rwkv_decode(rw, shift0, wkv0, front_params, back_params):
    b = rw.shape[0]
    vm = pl.BlockSpec(memory_space=pltpu.VMEM)
    wide = jax.ShapeDtypeStruct((b, RWKV_WIDTH), F32)
    r, w, k, v, kk, bb, g = pl.pallas_call(
        _rwkv_dec_front_kernel,
        in_specs=[vm] * (2 + len(front_params)),
        out_specs=[vm] * 7,
        out_shape=[wide] * 7,
        name="rwkv_dec_front",
    )(rw, shift0, *front_params)
    rowspec = pl.BlockSpec((None, 1, RWKV_WIDTH), lambda i: (i, 0, 0))
    stspec = pl.BlockSpec((None, RWKV_HEADS, HEAD_DIM, HEAD_DIM), lambda i: (i, 0, 0, 0))
    as_rows = lambda t: t.reshape(b, 1, RWKV_WIDTH)
    y, s_new = pl.pallas_call(
        _rwkv_dec_state_kernel,
        grid=(b,),
        in_specs=[rowspec] * 6 + [stspec],
        out_specs=[rowspec, stspec],
        out_shape=[jax.ShapeDtypeStruct((b, 1, RWKV_WIDTH), F32),
                   jax.ShapeDtypeStruct((b, RWKV_HEADS, HEAD_DIM, HEAD_DIM), F32)],
        compiler_params=_cparams(("arbitrary",)),
        name="rwkv_dec_state",
    )(as_rows(r), as_rows(w), as_rows(k), as_rows(v), as_rows(kk), as_rows(bb), wkv0)
    o = pl.pallas_call(
        _rwkv_dec_back_kernel,
        in_specs=[vm] * (5 + len(back_params)),
        out_specs=vm,
        out_shape=jax.ShapeDtypeStruct((b, RWKV_WIDTH), BF16),
        name="rwkv_dec_back",
    )(y.reshape(b, RWKV_WIDTH), r, k, v, g, *back_params)
    return o, s_new


def kernel(x_prompt, x_sample, cache_k, cache_v, state_wkv, state_shift, rel_bias, ffn1_norm, ffn1_w_gate, ffn1_w_up, ffn1_w_down, mix_norm, w_in, attn_sinks, shift_mu, decay_w0, decay_w2, aaa_a0, aaa_a2, gate_g2, key_k, key_a, bonus_r_k, ln_x_w, ln_x_b, w_out, ffn2_norm, ffn2_w_gate, ffn2_w_up, ffn2_w_down, final_norm):
    batch, seq, _ = x_prompt.shape
    dec_b = x_sample.shape[0]
    lw = cache_k.shape[2]
    l = 0
    row = lambda t: t.reshape(1, -1)

    w_in_b = w_in[l].astype(BF16)
    w_out_b = w_out[l].astype(BF16)
    lora_w = jnp.zeros((LANES, 2 * RWKV_WIDTH), F32)
    lora_w = lora_w.at[0:DECAY_LORA, 0:RWKV_WIDTH].set(decay_w2[l])
    lora_w = lora_w.at[DECAY_LORA:LANES, RWKV_WIDTH:].set(aaa_a2[l])
    head_of = np.arange(LANES) // HEAD_DIM
    bd_ones = jnp.asarray(head_of[:, None] == head_of[None, :], BF16)
    front_params = (row(shift_mu[l]), row(decay_w0[l]), row(aaa_a0[l]), row(key_k[l]), row(key_a[l]),
                    lora_w.astype(BF16), gate_g2[l].astype(BF16), bd_ones)
    back_params = (row(bonus_r_k[l]), row(ln_x_w[l]), row(ln_x_b[l]), bd_ones)

    bias = _bias_table(rel_bias)
    sinks = attn_sinks[l]

    xs = x_sample.reshape(dec_b, D_MODEL)
    x1s, h1s, *f1 = _ffn(xs, row(ffn1_norm[l]), ffn1_w_gate[l], ffn1_w_up[l], ffn1_w_down[l], row(mix_norm[l]),
                         tm=dec_b, tf=FF_TILE_CAST, emit_x=True, n_dtype=BF16, emit_w=True)
    qs, kvs, rws = _proj(h1s, w_in_b, tm=dec_b)
    sinks_b = jnp.broadcast_to(sinks[:, None], (N_Q_HEADS, LANES))
    o_att_s = _attn_decode(qs, kvs, cache_k[l], cache_v[l], bias[1, :, 0, :], sinks_b)
    o_rw_s, s_s = _rwkv_decode(rws, state_shift[l], state_wkv[l], front_params, back_params)
    x2s = _outproj(x1s, o_att_s.astype(BF16), o_rw_s, w_out_b, tm=dec_b)
    y_s, *f2 = _ffn(x2s, row(ffn2_norm[l]), ffn2_w_gate[l], ffn2_w_up[l], ffn2_w_down[l], row(final_norm),
                    tm=dec_b, tf=FF_TILE_CAST, emit_x=False, n_dtype=F32, emit_w=True)

    xp = x_prompt.reshape(batch * seq, D_MODEL)
    x1, h1 = _ffn_stream(xp, row(ffn1_norm[l]), *f1, row(mix_norm[l]), tm=512, emit_x=True, n_dtype=BF16)
    q, kv, rw = _proj(h1, w_in_b, tm=256)
    o_att = _attn_prompt(q, kv, bias, sinks, batch=batch, seq=seq)
    o_rw, s_p = _rwkv_prompt(rw, front_params + back_params[:3], batch=batch, seq=seq)
    x2 = _outproj(x1, o_att, o_rw, w_out_b, tm=512)
    (y_p,) = _ffn_stream(x2, row(ffn2_norm[l]), *f2, row(final_norm), tm=512, emit_x=False, n_dtype=F32)

    kv3 = kv.reshape(batch, seq, 2 * KV_WIDTH)
    lp = min(WINDOW, seq)
    new_k_p = kv3[:, seq - lp:, 0:KV_WIDTH].reshape(1, batch, lp, N_KV_HEADS, HEAD_DIM)
    new_v_p = kv3[:, seq - lp:, KV_WIDTH:].reshape(1, batch, lp, N_KV_HEADS, HEAD_DIM)
    new_shift_p = rw.reshape(batch, seq, SHIFT_COLS)[:, seq - 1][None]
    k_new = kvs[:, 0:KV_WIDTH].reshape(dec_b, 1, N_KV_HEADS, HEAD_DIM)
    v_new = kvs[:, KV_WIDTH:].reshape(dec_b, 1, N_KV_HEADS, HEAD_DIM)
    new_k_s = jnp.concatenate([cache_k[l], k_new], axis=1)[:, -lw:][None]
    new_v_s = jnp.concatenate([cache_v[l], v_new], axis=1)[:, -lw:][None]
    return (y_p.reshape(batch, seq, D_MODEL), y_s.reshape(dec_b, 1, D_MODEL),
            new_k_p, new_v_p, s_p[None], new_shift_p,
            new_k_s, new_v_s, s_s[None], rws[None])
```

```python
import functools
import math

import numpy as np
import jax
import jax.numpy as jnp
from jax import lax
from jax.experimental import pallas as pl
from jax.experimental.pallas import tpu as pltpu

F32 = jnp.float32
BF16 = jnp.bfloat16

D_MODEL = 2048
HEAD_DIM = 64
ATT_WIDTH = 1024
N_Q_HEADS = 16
N_KV_HEADS = 4
GQA_GROUP = 4
KV_WIDTH = 256
RWKV_WIDTH = 1024
RWKV_HEADS = 16
WINDOW = 128
BLOCK = 128
N_BUCKETS = 32
MAX_DISTANCE = 128
DECAY_LORA = 64
AAA_LORA = 64
GATE_LORA = 128
D_FF = 5504
ATT_COLS = ATT_WIDTH + 2 * KV_WIDTH
SHIFT_COLS = 3 * RWKV_WIDTH + DECAY_LORA + AAA_LORA + GATE_LORA
IN_COLS = ATT_COLS + SHIFT_COLS
RMS_EPS = 1e-5
GN_EPS = 64e-5
FFN_RES = 0.5

LANES = 128
VMEM_LIMIT_BYTES = 60 * 1024 * 1024

FF_TILE = 1024
FF_TILE_CAST = 512
CHUNK = 64
CHUNKS_PER_STEP = 4
ATT_BLOCKS_PER_STEP = 8
HEAD_PAIRS = RWKV_HEADS // 2
NEG_BIG = -1e30
DECAY_SCALE = math.exp(-0.5)

_NN = (((1,), (0,)), ((), ()))
_NT = (((1,), (1,)), ((), ()))
_TN = (((0,), (0,)), ((), ()))


def _dg(a, b, dims=_NN):
    return lax.dot_general(a, b, dims, preferred_element_type=F32)


def _mm(a, b, dims=_NN):
    return _dg(a.astype(BF16), b.astype(BF16), dims)


def _split2(x):
    hi = x.astype(BF16)
    lo = (x - hi.astype(F32)).astype(BF16)
    return hi, lo


def _mm2(a, b_bf16, dims=_NN):
    ah, al = _split2(a)
    return _dg(ah, b_bf16, dims) + _dg(al, b_bf16, dims)


def _mm2r(a_bf16, b, dims=_NN):
    bh, bl = _split2(b)
    return _dg(a_bf16, bh, dims) + _dg(a_bf16, bl, dims)


def _sigmoid(x):
    return 1.0 / (1.0 + jnp.exp(-x))


def _rms(x, g):
    ms = jnp.mean(x * x, axis=-1, keepdims=True)
    return x * lax.rsqrt(ms + RMS_EPS) * g


def _cparams(sem):
    return pltpu.CompilerParams(dimension_semantics=sem, vmem_limit_bytes=VMEM_LIMIT_BYTES)


def _swiglu_down(xn, wg, wu, wd):
    gate = jnp.dot(xn, wg, preferred_element_type=F32)
    up = jnp.dot(xn, wu, preferred_element_type=F32)
    h = (gate * _sigmoid(gate) * up).astype(BF16)
    return jnp.dot(h, wd, preferred_element_type=F32)


_FF_TAIL_STEP = 2


def _ff_block(j, *, steps):
    return jnp.where(j < _FF_TAIL_STEP, j, jnp.where(j == _FF_TAIL_STEP, steps - 1, j - 1))


def _ffn_kernel(x_ref, g_ref, wg_ref, wu_ref, wd_ref, g2_ref, *rest, emit_x, n_dtype, tf, emit_w):
    rest = list(rest)
    ox_ref = rest.pop(0) if emit_x else None
    on_ref = rest.pop(0)
    wgo_ref, wuo_ref, wdo_ref = (rest.pop(0), rest.pop(0), rest.pop(0)) if emit_w else (None, None, None)
    xn_scr, acc_scr = rest
    j = pl.program_id(1)

    @pl.when(j == 0)
    def _():
        xn_scr[...] = _rms(x_ref[...], g_ref[...]).astype(BF16)
        acc_scr[...] = jnp.zeros_like(acc_scr)

    steps = pl.cdiv(D_FF, tf)
    last = j == steps - 1
    tail = j == _FF_TAIL_STEP

    def accumulate(width):
        wg = wg_ref[:, 0:width].astype(BF16)
        wu = wu_ref[:, 0:width].astype(BF16)
        wd = wd_ref[0:width, :].astype(BF16)
        if emit_w:
            wgo_ref[:, 0:width] = wg
            wuo_ref[:, 0:width] = wu
            wdo_ref[0:width, :] = wd
        acc_scr[...] += _swiglu_down(xn_scr[...], wg, wu, wd)

    @pl.when(jnp.logical_not(tail))
    def _():
        accumulate(tf)

    @pl.when(tail)
    def _():
        accumulate(D_FF - (steps - 1) * tf)

    @pl.when(last)
    def _():
        y = x_ref[...] + FFN_RES * acc_scr[...]
        if emit_x:
            ox_ref[...] = y
        on_ref[...] = _rms(y, g2_ref[...]).astype(n_dtype)


def _ffn(x, g, wg, wu, wd, g2, *, tm, tf, emit_x, n_dtype, emit_w=False):
    m = x.shape[0]
    assert not emit_w or m == tm
    grid = (m // tm, pl.cdiv(D_FF, tf))
    row = pl.BlockSpec((tm, D_MODEL), lambda i, j: (i, 0))
    vec = pl.BlockSpec((1, D_MODEL), lambda i, j: (0, 0))
    steps = pl.cdiv(D_FF, tf)
    blk = functools.partial(_ff_block, steps=steps)
    w_in = pl.BlockSpec((D_MODEL, tf), lambda i, j: (0, blk(j)))
    w_dn = pl.BlockSpec((tf, D_MODEL), lambda i, j: (blk(j), 0))
    out_shape = [jax.ShapeDtypeStruct((m, D_MODEL), n_dtype)]
    out_specs = [row]
    if emit_x:
        out_shape = [jax.ShapeDtypeStruct((m, D_MODEL), F32)] + out_shape
        out_specs = [row, row]
    if emit_w:
        out_shape += [jax.ShapeDtypeStruct(w.shape, BF16) for w in (wg, wu, wd)]
        out_specs += [w_in, w_in, w_dn]
    return pl.pallas_call(
        functools.partial(_ffn_kernel, emit_x=emit_x, n_dtype=n_dtype, tf=tf, emit_w=emit_w),
        grid=grid,
        in_specs=[row, vec, w_in, w_in, w_dn, vec],
        out_specs=out_specs,
        out_shape=out_shape,
        scratch_shapes=[pltpu.VMEM((tm, D_MODEL), BF16), pltpu.VMEM((tm, D_MODEL), F32)],
        compiler_params=_cparams(("arbitrary", "arbitrary")),
        name="ffn_cast" if emit_w else "ffn",
    )(x, g, wg, wu, wd, g2)


def _ff_sweep():
    steps = pl.cdiv(D_FF, FF_TILE)
    order = list(range(_FF_TAIL_STEP)) + [steps - 1] + list(range(_FF_TAIL_STEP, steps - 1))
    return [(b, min(FF_TILE, D_FF - b * FF_TILE)) for b in order]


def _ffn_stream_kernel(x_ref, g_ref, wg_hbm, wu_hbm, wd_hbm, g2_ref, *rest, emit_x, n_dtype):
    rest = list(rest)
    ox_ref = rest.pop(0) if emit_x else None
    on_ref = rest.pop(0)
    wg_buf, wu_buf, wd_buf, acc_scr, sem = rest
    i = pl.program_id(0)
    sweep = _ff_sweep()
    assert len(sweep) % 2 == 0

    def copies(pos):
        blk, width = sweep[pos]
        slot = pos % 2
        cols = pl.ds(blk * FF_TILE, width)
        return (pltpu.make_async_copy(wg_hbm.at[:, cols], wg_buf.at[slot, :, pl.ds(0, width)], sem.at[0, slot]),
                pltpu.make_async_copy(wu_hbm.at[:, cols], wu_buf.at[slot, :, pl.ds(0, width)], sem.at[1, slot]),
                pltpu.make_async_copy(wd_hbm.at[cols, :], wd_buf.at[slot, pl.ds(0, width), :], sem.at[2, slot]))

    def start(pos):
        for cp in copies(pos):
            cp.start()

    @pl.when(i == 0)
    def _():
        start(0)

    for pos, (_, width) in enumerate(sweep):
        slot = pos % 2
        for cp in copies(pos):
            cp.wait()
        if pos + 1 < len(sweep):
            start(pos + 1)
        else:
            @pl.when(i + 1 < pl.num_programs(0))
            def _():
                start(0)
        if pos == 0:
            xn = _rms(x_ref[...], g_ref[...]).astype(BF16)
        part = _swiglu_down(xn, wg_buf[slot, :, 0:width], wu_buf[slot, :, 0:width], wd_buf[slot, 0:width, :])
        if pos == 0:
            acc_scr[...] = part
        else:
            acc_scr[...] += part

    y = x_ref[...] + FFN_RES * acc_scr[...]
    if emit_x:
        ox_ref[...] = y
    on_ref[...] = _rms(y, g2_ref[...]).astype(n_dtype)


def _ffn_stream(x, g, wg, wu, wd, g2, *, tm, emit_x, n_dtype):
    m = x.shape[0]
    row = pl.BlockSpec((tm, D_MODEL), lambda i: (i, 0))
    vec = pl.BlockSpec((1, D_MODEL), lambda i: (0, 0))
    hbm = pl.BlockSpec(memory_space=pl.ANY)
    out_shape = [jax.ShapeDtypeStruct((m, D_MODEL), n_dtype)]
    out_specs = [row]
    if emit_x:
        out_shape = [jax.ShapeDtypeStruct((m, D_MODEL), F32)] + out_shape
        out_specs = [row, row]
    return pl.pallas_call(
        functools.partial(_ffn_stream_kernel, emit_x=emit_x, n_dtype=n_dtype),
        grid=(m // tm,),
        in_specs=[row, vec, hbm, hbm, hbm, vec],
        out_specs=out_specs,
        out_shape=out_shape,
        scratch_shapes=[pltpu.VMEM((2, D_MODEL, FF_TILE), BF16), pltpu.VMEM((2, D_MODEL, FF_TILE), BF16),
                        pltpu.VMEM((2, FF_TILE, D_MODEL), BF16), pltpu.VMEM((tm, D_MODEL), F32),
                        pltpu.SemaphoreType.DMA((3, 2))],
        compiler_params=_cparams(("arbitrary",)),
        name="ffn",
    )(x, g, wg, wu, wd, g2)


def _proj_kernel(h_ref, w_ref, q_ref, kv_ref, rw_ref):
    h = h_ref[...]
    q = jnp.dot(h, w_ref[:, 0:ATT_WIDTH], preferred_element_type=F32)
    q_ref[...] = (q * (HEAD_DIM ** -0.5)).astype(BF16)
    kv_ref[...] = jnp.dot(h, w_ref[:, ATT_WIDTH:ATT_COLS], preferred_element_type=F32)
    rw_ref[...] = jnp.dot(h, w_ref[:, ATT_COLS:IN_COLS], preferred_element_type=F32)


def _proj(h, w, *, tm):
    m = h.shape[0]
    return pl.pallas_call(
        _proj_kernel,
        grid=(m // tm,),
        in_specs=[pl.BlockSpec((tm, D_MODEL), lambda i: (i, 0)),
                  pl.BlockSpec((D_MODEL, IN_COLS), lambda i: (0, 0), pipeline_mode=pl.Buffered(1))],
        out_specs=[pl.BlockSpec((tm, ATT_WIDTH), lambda i: (i, 0)),
                   pl.BlockSpec((tm, 2 * KV_WIDTH), lambda i: (i, 0)),
                   pl.BlockSpec((tm, SHIFT_COLS), lambda i: (i, 0))],
        out_shape=[jax.ShapeDtypeStruct((m, ATT_WIDTH), BF16),
                   jax.ShapeDtypeStruct((m, 2 * KV_WIDTH), F32),
                   jax.ShapeDtypeStruct((m, SHIFT_COLS), F32)],
        compiler_params=_cparams(("arbitrary",)),
        name="in_proj",
    )(h, w)


def _outproj_kernel(x_ref, oa_ref, orw_ref, w_ref, o_ref, *wo_ref):
    w_att = w_ref[0:ATT_WIDTH, :].astype(BF16)
    w_rw = w_ref[ATT_WIDTH:D_MODEL, :].astype(BF16)
    if wo_ref:
        wo_ref[0][0:ATT_WIDTH, :] = w_att
        wo_ref[0][ATT_WIDTH:D_MODEL, :] = w_rw
    acc = jnp.dot(oa_ref[...], w_att, preferred_element_type=F32)
    acc += jnp.dot(orw_ref[...], w_rw, preferred_element_type=F32)
    o_ref[...] = x_ref[...] + acc


def _outproj(x, oa, orw, w, *, tm, emit_w=False):
    m = x.shape[0]
    assert not emit_w or m == tm
    row = pl.BlockSpec((tm, D_MODEL), lambda i: (i, 0))
    half = pl.BlockSpec((tm, ATT_WIDTH), lambda i: (i, 0))
    full = pl.BlockSpec((D_MODEL, D_MODEL), lambda i: (0, 0), pipeline_mode=pl.Buffered(1))
    out_shape = [jax.ShapeDtypeStruct((m, D_MODEL), F32)]
    out_specs = [row]
    if emit_w:
        out_shape.append(jax.ShapeDtypeStruct((D_MODEL, D_MODEL), BF16))
        out_specs.append(pl.BlockSpec((D_MODEL, D_MODEL), lambda i: (0, 0)))
    return pl.pallas_call(
        _outproj_kernel,
        grid=(m // tm,),
        in_specs=[row, half, half, full],
        out_specs=out_specs,
        out_shape=out_shape,
        compiler_params=_cparams(("arbitrary",)),
        name="out_proj_cast" if emit_w else "out_proj",
    )(x, oa, orw, w)


def _bucket_table():
    qi = np.arange(BLOCK)[:, None]
    kj = np.arange(2 * BLOCK)[None, :]
    dist = BLOCK + qi - kj
    n = np.maximum(dist, 0)
    max_exact = N_BUCKETS // 2
    nf = np.maximum(n, 1).astype(np.float32)
    large = max_exact + (np.log(nf / np.float32(max_exact)) / np.float32(math.log(MAX_DISTANCE / max_exact))
                         * np.float32(N_BUCKETS - max_exact)).astype(np.int32)
    large = np.minimum(large, N_BUCKETS - 1)
    bucket = np.where(n < max_exact, n, large).astype(np.int32)
    valid = (dist >= 0) & (dist <= WINDOW)
    return np.where(valid, bucket, -1).astype(np.int32)


def _bias_kernel(bucket_ref, rb_ref, o_ref):
    bucket = bucket_ref[...]
    kj = lax.broadcasted_iota(jnp.int32, (BLOCK, 2 * BLOCK), 1)
    for h in range(N_Q_HEADS):
        acc = jnp.full((BLOCK, 2 * BLOCK), NEG_BIG, F32)
        for n in range(N_BUCKETS):
            acc = jnp.where(bucket == n, rb_ref[n, h], acc)
        o_ref[0, h] = jnp.where(kj >= BLOCK, acc, NEG_BIG)
        o_ref[1, h] = acc


def _bias_table(rel_bias):
    bucket = jnp.asarray(_bucket_table())
    return pl.pallas_call(
        _bias_kernel,
        in_specs=[pl.BlockSpec(memory_space=pltpu.VMEM), pl.BlockSpec(memory_space=pltpu.SMEM)],
        out_specs=pl.BlockSpec(memory_space=pltpu.VMEM),
        out_shape=jax.ShapeDtypeStruct((2, N_Q_HEADS, BLOCK, 2 * BLOCK), F32),
        name="bias_table",
    )(bucket, rel_bias)


def _attn_kernel(sink_ref, q_ref, kvp_ref, kvc_ref, bias_ref, o_ref, kv_scr):
    j = pl.program_id(1)
    kv_scr[0:BLOCK, :] = kvp_ref[...]
    kv_scr[BLOCK:, :] = kvc_ref[...]

    def block(i, carry):
        r0 = pl.multiple_of(i * BLOCK, BLOCK)
        rows = pl.ds(r0, BLOCK)
        table = jnp.where(jnp.logical_and(j == 0, i == 0), 0, 1)
        _attn_block(sink_ref, q_ref.at[rows], kv_scr[rows, :], kv_scr[pl.ds(r0 + BLOCK, BLOCK), :],
                    bias_ref.at[table], o_ref.at[rows])
        return carry

    lax.fori_loop(0, ATT_BLOCKS_PER_STEP, block, 0)


def _attn_block(sink_ref, q_ref, kvp, kvc, bias_ref, o_ref):
    half = N_KV_HEADS // 2
    for g0 in range(0, N_KV_HEADS, half):
        kcat, vcat = {}, {}
        for g in range(g0, g0 + half):
            ks = slice(g * HEAD_DIM, (g + 1) * HEAD_DIM)
            vs = slice(KV_WIDTH + g * HEAD_DIM, KV_WIDTH + (g + 1) * HEAD_DIM)
            kcat[g] = jnp.concatenate([kvp[:, ks], kvc[:, ks]], axis=0).astype(BF16)
            vcat[g] = jnp.concatenate([kvp[:, vs], kvc[:, vs]], axis=0).astype(BF16)
        heads = range(g0 * GQA_GROUP, (g0 + half) * GQA_GROUP)
        hs = {h: slice(h * HEAD_DIM, (h + 1) * HEAD_DIM) for h in heads}
        s = {h: _dg(q_ref[:, hs[h]], kcat[h // GQA_GROUP], _NT) + bias_ref[h] for h in heads}
        m = {h: jnp.maximum(jnp.max(s[h], axis=-1, keepdims=True), sink_ref[h]) for h in heads}
        p = {h: jnp.exp(s[h] - m[h]) for h in heads}
        denom = {h: jnp.sum(p[h], axis=-1, keepdims=True) + jnp.exp(sink_ref[h] - m[h]) for h in heads}
        o = {h: jnp.dot(p[h].astype(BF16), vcat[h // GQA_GROUP], preferred_element_type=F32) for h in heads}
        for h in heads:
            o_ref[:, hs[h]] = (o[h] / denom[h]).astype(BF16)


def _attn_prompt(q, kv, bias, sinks, *, batch, seq):
    n = ATT_BLOCKS_PER_STEP
    rows = n * BLOCK
    q3 = q.reshape(batch, seq, ATT_WIDTH)
    kv3 = kv.reshape(batch, seq, 2 * KV_WIDTH)
    out = pl.pallas_call(
        _attn_kernel,
        grid=(batch, seq // rows),
        in_specs=[pl.BlockSpec(memory_space=pltpu.SMEM),
                  pl.BlockSpec((None, rows, ATT_WIDTH), lambda b, j: (b, j, 0)),
                  pl.BlockSpec((None, BLOCK, 2 * KV_WIDTH), lambda b, j: (b, jnp.maximum(n * j - 1, 0), 0)),
                  pl.BlockSpec((None, rows, 2 * KV_WIDTH), lambda b, j: (b, j, 0)),
                  pl.BlockSpec((2, N_Q_HEADS, BLOCK, 2 * BLOCK), lambda b, j: (0, 0, 0, 0))],
        out_specs=pl.BlockSpec((None, rows, ATT_WIDTH), lambda b, j: (b, j, 0)),
        out_shape=jax.ShapeDtypeStruct((batch, seq, ATT_WIDTH), BF16),
        scratch_shapes=[pltpu.VMEM((rows + BLOCK, 2 * KV_WIDTH), F32)],
        compiler_params=_cparams(("arbitrary", "arbitrary")),
        name="attn_prompt",
    )(sinks, q3, kv3, kv3, bias)
    return out.reshape(batch * seq, ATT_WIDTH)


def _attn_decode_kernel(sink_ref, q_ref, kvn_ref, ck_ref, cv_ref, bias_ref, o_ref):
    q = q_ref[...].astype(F32)
    kvn = kvn_ref[...]
    for g in range(N_KV_HEADS):
        ks = slice(g * HEAD_DIM, (g + 1) * HEAD_DIM)
        vs = slice(KV_WIDTH + g * HEAD_DIM, KV_WIDTH + (g + 1) * HEAD_DIM)
        hs = slice(g * GQA_GROUP, (g + 1) * GQA_GROUP)
        qg = q[:, hs, :]
        kc = ck_ref[:, :, ks]
        vc = cv_ref[:, :, ks]
        kn = kvn[:, ks].astype(BF16).astype(F32)[:, None, :]
        vn = kvn[:, vs].astype(BF16).astype(F32)[:, None, :]
        s = jnp.einsum("bqd,bkd->bqk", qg.astype(BF16), kc.astype(BF16), preferred_element_type=F32)
        s = s + bias_ref[hs, 0:WINDOW][None]
        s_new = jnp.sum(qg * kn, axis=-1, keepdims=True) + bias_ref[hs, WINDOW:WINDOW + 1][None]
        sink = sink_ref[hs, 0:1][None]
        m = jnp.maximum(jnp.maximum(jnp.max(s, axis=-1, keepdims=True), s_new), sink)
        p = jnp.exp(s - m)
        p_new = jnp.exp(s_new - m)
        denom = jnp.sum(p, axis=-1, keepdims=True) + p_new + jnp.exp(sink - m)
        o = jnp.einsum("bqk,bkd->bqd", p.astype(BF16), vc.astype(BF16), preferred_element_type=F32)
        o = o + p_new * vn
        o_ref[:, hs, :] = o / denom


def _attn_decode(q, kvn, cache_k, cache_v, bias_row, sinks_b):
    b = q.shape[0]
    lw = cache_k.shape[1]
    vm = pl.BlockSpec(memory_space=pltpu.VMEM)
    out = pl.pallas_call(
        _attn_decode_kernel,
        in_specs=[vm, vm, vm, vm, vm, vm],
        out_specs=vm,
        out_shape=jax.ShapeDtypeStruct((b, N_Q_HEADS, HEAD_DIM), F32),
        compiler_params=pltpu.CompilerParams(vmem_limit_bytes=VMEM_LIMIT_BYTES),
        name="attn_decode",
    )(sinks_b, q.reshape(b, N_Q_HEADS, HEAD_DIM), kvn,
      cache_k.reshape(b, lw, KV_WIDTH), cache_v.reshape(b, lw, KV_WIDTH), bias_row)
    return out.reshape(b, ATT_WIDTH)


def _seg_expand(x, bd_ref):
    rows = x.shape[0]
    groups = x.shape[1] // LANES
    xs = jnp.concatenate([x[:, i * LANES:(i + 1) * LANES] for i in range(groups)], axis=0)
    s = _mm(xs, bd_ref[...])
    return jnp.concatenate([s[i * rows:(i + 1) * rows] for i in range(groups)], axis=1)


def _rwkv_front(x, prev, p):
    mu_ref, w0_ref, a0_ref, kk_ref, ka_ref, lw_ref, g2_ref, bd_ref = p
    xm = x + mu_ref[...] * (prev - x)
    o3 = 3 * RWKV_WIDTH
    r = xm[:, 0:RWKV_WIDTH]
    k = xm[:, RWKV_WIDTH:2 * RWKV_WIDTH]
    v = xm[:, 2 * RWKV_WIDTH:o3]
    wa = xm[:, o3:o3 + LANES]
    lane = lax.broadcasted_iota(jnp.int32, wa.shape, 1)
    z = jnp.where(lane < DECAY_LORA, jnp.tanh(wa), wa)
    lora = _mm(z, lw_ref[...])
    logw = -DECAY_SCALE * _sigmoid(w0_ref[...] + lora[:, 0:RWKV_WIDTH])
    a = _sigmoid(a0_ref[...] + lora[:, RWKV_WIDTH:2 * RWKV_WIDTH])
    g = _mm(_sigmoid(xm[:, o3 + LANES:SHIFT_COLS]), g2_ref[...])
    kk = k * kk_ref[...]
    kmod = k * (1.0 + (a - 1.0) * ka_ref[...])
    nrm = jnp.sqrt(_seg_expand(kk * kk, bd_ref))
    kk = kk / jnp.maximum(nrm, 1e-12)
    return r, logw, kmod, v, kk, kk * a, g


def _rwkv_back(y, r, kmod, v, g, q):
    rk_ref, lnw_ref, lnb_ref, bd_ref = q
    mean = _seg_expand(y, bd_ref) * (1.0 / HEAD_DIM)
    d = y - mean
    var = _seg_expand(d * d, bd_ref) * (1.0 / HEAD_DIM)
    yn = d * lax.rsqrt(var + GN_EPS) * lnw_ref[...] + lnb_ref[...]
    bonus = _seg_expand(r * kmod * rk_ref[...], bd_ref) * v
    return (yn + bonus) * g


def _rwkv_chunk(rw_ref, mu_ref, w0_ref, a0_ref, kk_ref, ka_ref, lw_ref, g2_ref, bd_ref,
                rk_ref, lnw_ref, lnb_ref,
                o_ref,
                s_scr, prev_scr, y_scr, ea_scr, er_scr, eb_scr, ek_scr, be_scr, ke_scr, v_scr):
    C = CHUNK
    x = rw_ref[...]
    row = lax.broadcasted_iota(jnp.int32, x.shape, 0)
    prev = jnp.where(row == 0, prev_scr[...], pltpu.roll(x, 1, axis=0))
    prev_scr[...] = x[C - 1:C, :]
    front = (mu_ref, w0_ref, a0_ref, kk_ref, ka_ref, lw_ref, g2_ref, bd_ref)
    r, logw, kmod, v, kk, bb, g = _rwkv_front(x, prev, front)

    ti = lax.broadcasted_iota(jnp.int32, (C, C), 0)
    si = lax.broadcasted_iota(jnp.int32, (C, C), 1)
    cum = _mm2r(jnp.where(si <= ti, 1.0, 0.0).astype(BF16), logw)
    cum_end = cum[C - 1:C, :]
    e_in = jnp.exp(cum)
    e_neg = jnp.exp(-cum)
    e_end = jnp.exp(cum_end - cum)
    ea_scr[...] = kk * jnp.exp(cum - logw)
    er_scr[...] = r * e_in
    eb_scr[...] = bb * e_neg
    ek_scr[...] = kmod * e_neg
    be_scr[...] = bb * e_end
    ke_scr[...] = kmod * e_end
    v_scr[...] = v
    w_end = jnp.exp(cum_end)

    lane = lax.broadcasted_iota(jnp.int32, (C, LANES), 1)
    lo_half = lane < HEAD_DIM
    ri = lax.broadcasted_iota(jnp.int32, (LANES, LANES), 0)
    ci = lax.broadcasted_iota(jnp.int32, (LANES, LANES), 1)
    same_head = jnp.right_shift(ri, 6) == jnp.right_shift(ci, 6)
    eye = ri == ci
    t_loc = jnp.bitwise_and(ri, C - 1)
    s_loc = jnp.bitwise_and(ci, C - 1)
    bd_strict = jnp.logical_and(same_head, s_loc < t_loc)
    bd_incl = jnp.logical_and(same_head, s_loc <= t_loc)

    def swap(t):
        return pltpu.roll(t, HEAD_DIM, axis=1)

    def stack2(t):
        return jnp.concatenate([jnp.where(lo_half, t, 0.0), jnp.where(lo_half, 0.0, t)], axis=0)

    def stack2x(t):
        ts = swap(t)
        return jnp.concatenate([jnp.where(lo_half, 0.0, ts), jnp.where(lo_half, ts, 0.0)], axis=0)

    def own(t):
        return jnp.where(lo_half, t[0:C], t[C:2 * C])

    def other(t):
        return swap(jnp.where(lo_half, t[C:2 * C], t[0:C]))

    pairs = range(HEAD_PAIRS)
    sl = [slice(p * LANES, (p + 1) * LANES) for p in pairs]
    a2 = [stack2(ea_scr[:, sl[p]]) for p in pairs]
    m = []
    for p in pairs:
        r2 = stack2(er_scr[:, sl[p]])
        b_p = eb_scr[:, sl[p]]
        k_p = ek_scr[:, sl[p]]
        m.append(_mm(jnp.concatenate([a2[p], r2], axis=0), jnp.concatenate([b_p, b_p, k_p, k_p], axis=0), _NT))
    l_ak = [jnp.where(bd_strict, m[p][0:2 * C, LANES:2 * LANES], 0.0) for p in pairs]
    pw = [-jnp.where(bd_strict, m[p][0:2 * C, 0:LANES], 0.0) for p in pairs]
    lrbk = [jnp.concatenate([jnp.where(bd_incl, m[p][2 * C:4 * C, 0:LANES], 0.0),
                             jnp.where(bd_incl, m[p][2 * C:4 * C, LANES:2 * LANES], 0.0)], axis=1) for p in pairs]
    v2x = [stack2x(v_scr[:, sl[p]]) for p in pairs]
    xc = [a2[p] - _mm(l_ak[p], v2x[p]) for p in pairs]
    for it in range(6):
        for p in pairs:
            pw_b = pw[p].astype(BF16)
            xc_b = xc[p].astype(BF16)
            if it < 5:
                px = _dg(pw_b, jnp.concatenate([xc_b, pw_b], axis=1))
                pw[p] = px[:, LANES:2 * LANES]
                xc[p] = xc[p] + px[:, 0:LANES]
            else:
                xc[p] = xc[p] + _dg(pw_b, xc_b)
    xv = [jnp.concatenate([xc[p], v2x[p]], axis=0).astype(BF16) for p in pairs]
    lb = [_mm(lrbk[p], xv[p]) for p in pairs]
    z = [_mm(xv[p], jnp.concatenate([stack2(be_scr[:, sl[p]]), stack2(ke_scr[:, sl[p]])], axis=0), _TN)
         for p in pairs]
    rt = [er_scr[:, sl[p]] - own(lb[p]) for p in pairs]
    yl = [other(lb[p]) for p in pairs]
    d = [jnp.where(eye, w_end[:, sl[p]], 0.0) - jnp.where(same_head, z[p], 0.0) for p in pairs]
    gg = [jnp.where(same_head, jnp.concatenate([z[p][C:2 * C], z[p][0:C]], axis=0), 0.0) for p in pairs]
    s0 = [s_scr[p].astype(BF16) for p in pairs]
    ys = [_mm(rt[p], s0[p], _NT) for p in pairs]
    for p in pairs:
        y_scr[:, sl[p]] = yl[p] + ys[p]
    sn = [_mm(s0[p], d[p]) for p in pairs]
    for p in pairs:
        s_scr[p] = sn[p] + gg[p]

    back = (rk_ref, lnw_ref, lnb_ref, bd_ref)
    o_ref[...] = _rwkv_back(y_scr[...], r, kmod, v, g, back).astype(BF16)


def _rwkv_chunk_kernel(rw_ref, *rest):
    params, (o_ref, s_out_ref, s_scr, prev_scr), work = rest[:11], rest[11:15], rest[15:]
    c = pl.program_id(1)

    @pl.when(c == 0)
    def _():
        s_scr[...] = jnp.zeros_like(s_scr)
        prev_scr[...] = jnp.zeros_like(prev_scr)

    for i in range(CHUNKS_PER_STEP):
        rows = pl.ds(i * CHUNK, CHUNK)
        _rwkv_chunk(rw_ref.at[rows], *params, o_ref.at[rows], s_scr, prev_scr, *[w.at[i] for w in work])

    @pl.when(c == pl.num_programs(1) - 1)
    def _():
        for p in range(HEAD_PAIRS):
            s_out_ref[2 * p] = s_scr[p, 0:HEAD_DIM, 0:HEAD_DIM]
            s_out_ref[2 * p + 1] = s_scr[p, HEAD_DIM:LANES, HEAD_DIM:LANES]


def _rwkv_prompt(rw, params, *, batch, seq):
    rw3 = rw.reshape(batch, seq, SHIFT_COLS)
    rows = CHUNKS_PER_STEP * CHUNK
    nc = seq // rows
    const = lambda shape: pl.BlockSpec(shape, lambda b, c: tuple(0 for _ in shape))
    in_specs = [pl.BlockSpec((None, rows, SHIFT_COLS), lambda b, c: (b, c, 0))]
    in_specs += [const(p.shape) for p in params]
    wide = pltpu.VMEM((CHUNKS_PER_STEP, CHUNK, RWKV_WIDTH), F32)
    o, s_out = pl.pallas_call(
        _rwkv_chunk_kernel,
        grid=(batch, nc),
        in_specs=in_specs,
        out_specs=[pl.BlockSpec((None, rows, RWKV_WIDTH), lambda b, c: (b, c, 0)),
                   pl.BlockSpec((None, RWKV_HEADS, HEAD_DIM, HEAD_DIM), lambda b, c: (b, 0, 0, 0))],
        out_shape=[jax.ShapeDtypeStruct((batch, seq, RWKV_WIDTH), BF16),
                   jax.ShapeDtypeStruct((batch, RWKV_HEADS, HEAD_DIM, HEAD_DIM), F32)],
        scratch_shapes=[pltpu.VMEM((HEAD_PAIRS, LANES, LANES), F32),
                        pltpu.VMEM((1, SHIFT_COLS), F32),
                        wide, wide, wide, wide, wide, wide, wide, wide],
        compiler_params=_cparams(("arbitrary", "arbitrary")),
        name="rwkv_prompt",
    )(rw3, *params)
    return o.reshape(batch * seq, RWKV_WIDTH), s_out


def _rwkv_dec_front_kernel(rw_ref, prev_ref, mu_ref, w0_ref, a0_ref, kk_ref, ka_ref, lw_ref, g2_ref, bd_ref,
                           r_ref, w_ref, k_ref, v_ref, kkn_ref, b_ref, g_ref):
    front = (mu_ref, w0_ref, a0_ref, kk_ref, ka_ref, lw_ref, g2_ref, bd_ref)
    r, logw, kmod, v, kk, bb, g = _rwkv_front(rw_ref[...], prev_ref[...], front)
    r_ref[...] = r
    w_ref[...] = jnp.exp(logw)
    k_ref[...] = kmod
    v_ref[...] = v
    kkn_ref[...] = kk
    b_ref[...] = bb
    g_ref[...] = g


def _rwkv_dec_state_kernel(r_all, w_all, k_all, v_all, kk_all, b_all, s_ref, y_ref, so_ref):
    row = pl.ds(pl.program_id(0), 1)
    r_ref, w_ref, k_ref, v_ref, kk_ref, b_ref = (t.at[row] for t in (r_all, w_all, k_all, v_all, kk_all, b_all))
    ri = lax.broadcasted_iota(jnp.int32, (HEAD_DIM, HEAD_DIM), 0)
    ci = lax.broadcasted_iota(jnp.int32, (HEAD_DIM, HEAD_DIM), 1)
    eye = ri == ci
    ones = jnp.ones((HEAD_DIM, LANES), BF16)
    heads = range(RWKV_HEADS)
    hs = [slice(h * HEAD_DIM, (h + 1) * HEAD_DIM) for h in heads]
    n = RWKV_HEADS * HEAD_DIM
    s = [s_ref[h] for h in heads]
    lhs = [s[h] * kk_ref[:, hs[h]] for h in heads] + [jnp.where(eye, v_ref[:, hs[h]], 0.0) for h in heads]
    red = _mm2(jnp.concatenate(lhs, axis=0), ones)
    s_new = []
    for h in heads:
        sa = -red[h * HEAD_DIM:(h + 1) * HEAD_DIM, 0:HEAD_DIM]
        v_col = red[n + h * HEAD_DIM:n + (h + 1) * HEAD_DIM, 0:HEAD_DIM]
        s_new.append(s[h] * w_ref[:, hs[h]] + sa * b_ref[:, hs[h]] + v_col * k_ref[:, hs[h]])
        so_ref[h] = s_new[h]
    yb = _mm2(jnp.concatenate([s_new[h] * r_ref[:, hs[h]] for h in heads], axis=0), ones)
    for h in heads:
        y_blk = yb[h * HEAD_DIM:(h + 1) * HEAD_DIM, 0:HEAD_DIM]
        y_ref[:, hs[h]] = jnp.sum(jnp.where(eye, y_blk, 0.0), axis=0, keepdims=True)


def _rwkv_dec_back_kernel(y_ref, r_ref, k_ref, v_ref, g_ref, rk_ref, lnw_ref, lnb_ref, bd_ref, o_ref):
    back = (rk_ref, lnw_ref, lnb_ref, bd_ref)
    o_ref[...] = _rwkv_back(y_ref[...], r_ref[...], k_ref[...], v_ref[...], g_ref[...], back).astype(BF16)


def _rwkv_decode(rw, shift0, wkv0, front_params, back_params):
    b = rw.shape[0]
    vm = pl.BlockSpec(memory_space=pltpu.VMEM)
    wide = jax.ShapeDtypeStruct((b, RWKV_WIDTH), F32)
    r, w, k, v, kk, bb, g = pl.pallas_call(
        _rwkv_dec_front_kernel,
        in_specs=[vm] * (2 + len(front_params)),
        out_specs=[vm] * 7,
        out_shape=[wide] * 7,
        name="rwkv_dec_front",
    )(rw, shift0, *front_params)
    rowspec = pl.BlockSpec((None, 1, RWKV_WIDTH), lambda i: (i, 0, 0))
    stspec = pl.BlockSpec((None, RWKV_HEADS, HEAD_DIM, HEAD_DIM), lambda i: (i, 0, 0, 0))
    allrows = pl.BlockSpec((b, RWKV_WIDTH), lambda i: (0, 0))
    y, s_new = pl.pallas_call(
        _rwkv_dec_state_kernel,
        grid=(b,),
        in_specs=[allrows] * 6 + [stspec],
        out_specs=[rowspec, stspec],
        out_shape=[jax.ShapeDtypeStruct((b, 1, RWKV_WIDTH), F32),
                   jax.ShapeDtypeStruct((b, RWKV_HEADS, HEAD_DIM, HEAD_DIM), F32)],
        compiler_params=_cparams(("arbitrary",)),
        name="rwkv_dec_state",
    )(r, w, k, v, kk, bb, wkv0)
    o = pl.pallas_call(
        _rwkv_dec_back_kernel,
        in_specs=[vm] * (5 + len(back_params)),
        out_specs=vm,
        out_shape=jax.ShapeDtypeStruct((b, RWKV_WIDTH), BF16),
        name="rwkv_dec_back",
    )(y.reshape(b, RWKV_WIDTH), r, k, v, g, *back_params)
    return o, s_new


def kernel(x_prompt, x_sample, cache_k, cache_v, state_wkv, state_shift, rel_bias, ffn1_norm, ffn1_w_gate, ffn1_w_up, ffn1_w_down, mix_norm, w_in, attn_sinks, shift_mu, decay_w0, decay_w2, aaa_a0, aaa_a2, gate_g2, key_k, key_a, bonus_r_k, ln_x_w, ln_x_b, w_out, ffn2_norm, ffn2_w_gate, ffn2_w_up, ffn2_w_down, final_norm):
    batch, seq, _ = x_prompt.shape
    dec_b = x_sample.shape[0]
    lw = cache_k.shape[2]
    l = 0
    row = lambda t: t.reshape(1, -1)

    w_in_b = w_in[l].astype(BF16)
    lora_w = jnp.zeros((LANES, 2 * RWKV_WIDTH), F32)
    lora_w = lora_w.at[0:DECAY_LORA, 0:RWKV_WIDTH].set(decay_w2[l])
    lora_w = lora_w.at[DECAY_LORA:LANES, RWKV_WIDTH:].set(aaa_a2[l])
    head_of = np.arange(LANES) // HEAD_DIM
    bd_ones = jnp.asarray(head_of[:, None] == head_of[None, :], BF16)
    front_params = (row(shift_mu[l]), row(decay_w0[l]), row(aaa_a0[l]), row(key_k[l]), row(key_a[l]),
                    lora_w.astype(BF16), gate_g2[l].astype(BF16), bd_ones)
    back_params = (row(bonus_r_k[l]), row(ln_x_w[l]), row(ln_x_b[l]), bd_ones)

    bias = _bias_table(rel_bias)
    sinks = attn_sinks[l]

    xs = x_sample.reshape(dec_b, D_MODEL)
    x1s, h1s, *f1 = _ffn(xs, row(ffn1_norm[l]), ffn1_w_gate[l], ffn1_w_up[l], ffn1_w_down[l], row(mix_norm[l]),
                         tm=dec_b, tf=FF_TILE_CAST, emit_x=True, n_dtype=BF16, emit_w=True)
    qs, kvs, rws = _proj(h1s, w_in_b, tm=dec_b)
    sinks_b = jnp.broadcast_to(sinks[:, None], (N_Q_HEADS, LANES))
    o_att_s = _attn_decode(qs, kvs, cache_k[l], cache_v[l], bias[1, :, 0, :], sinks_b)
    o_rw_s, s_s = _rwkv_decode(rws, state_shift[l], state_wkv[l], front_params, back_params)
    x2s, w_out_b = _outproj(x1s, o_att_s.astype(BF16), o_rw_s, w_out[l], tm=dec_b, emit_w=True)
    y_s, *f2 = _ffn(x2s, row(ffn2_norm[l]), ffn2_w_gate[l], ffn2_w_up[l], ffn2_w_down[l], row(final_norm),
                    tm=dec_b, tf=FF_TILE_CAST, emit_x=False, n_dtype=F32, emit_w=True)

    xp = x_prompt.reshape(batch * seq, D_MODEL)
    x1, h1 = _ffn_stream(xp, row(ffn1_norm[l]), *f1, row(mix_norm[l]), tm=512, emit_x=True, n_dtype=BF16)
    q, kv, rw = _proj(h1, w_in_b, tm=256)
    o_att = _attn_prompt(q, kv, bias, sinks, batch=batch, seq=seq)
    o_rw, s_p = _rwkv_prompt(rw, front_params + back_params[:3], batch=batch, seq=seq)
    (x2,) = _outproj(x1, o_att, o_rw, w_out_b, tm=512)
    (y_p,) = _ffn_stream(x2, row(ffn2_norm[l]), *f2, row(final_norm), tm=512, emit_x=False, n_dtype=F32)

    kv3 = kv.reshape(batch, seq, 2 * KV_WIDTH)
    lp = min(WINDOW, seq)
    new_k_p = kv3[:, seq - lp:, 0:KV_WIDTH].reshape(1, batch, lp, N_KV_HEADS, HEAD_DIM)
    new_v_p = kv3[:, seq - lp:, KV_WIDTH:].reshape(1, batch, lp, N_KV_HEADS, HEAD_DIM)
    new_shift_p = rw.reshape(batch, seq, SHIFT_COLS)[:, seq - 1][None]
    k_new = kvs[:, 0:KV_WIDTH].reshape(dec_b, 1, N_KV_HEADS, HEAD_DIM)
    v_new = kvs[:, KV_WIDTH:].reshape(dec_b, 1, N_KV_HEADS, HEAD_DIM)
    new_k_s = jnp.concatenate([cache_k[l], k_new], axis=1)[:, -lw:][None]
    new_v_s = jnp.concatenate([cache_v[l], v_new], axis=1)[:, -lw:][None]
    return (y_p.reshape(batch, seq, D_MODEL), y_s.reshape(dec_b, 1, D_MODEL),
            new_k_p, new_v_p, s_p[None], new_shift_p,
            new_k_s, new_v_s, s_s[None], rws[None])
```

```python
import functools
import math

import numpy as np
import jax
import jax.numpy as jnp
from jax import lax
from jax.experimental import pallas as pl
from jax.experimental.pallas import tpu as pltpu

F32 = jnp.float32
BF16 = jnp.bfloat16

D_MODEL = 2048
HEAD_DIM = 64
ATT_WIDTH = 1024
N_Q_HEADS = 16
N_KV_HEADS = 4
GQA_GROUP = 4
KV_WIDTH = 256
RWKV_WIDTH = 1024
RWKV_HEADS = 16
WINDOW = 128
BLOCK = 128
N_BUCKETS = 32
MAX_DISTANCE = 128
DECAY_LORA = 64
AAA_LORA = 64
GATE_LORA = 128
D_FF = 5504
ATT_COLS = ATT_WIDTH + 2 * KV_WIDTH
SHIFT_COLS = 3 * RWKV_WIDTH + DECAY_LORA + AAA_LORA + GATE_LORA
IN_COLS = ATT_COLS + SHIFT_COLS
RMS_EPS = 1e-5
GN_EPS = 64e-5
FFN_RES = 0.5

LANES = 128
VMEM_LIMIT_BYTES = 60 * 1024 * 1024

FF_TILE = 1024
FF_TILE_CAST = 512
CHUNK = 64
CHUNKS_PER_STEP = 4
ATT_BLOCKS_PER_STEP = 8
HEAD_PAIRS = RWKV_HEADS // 2
NEG_BIG = -1e30
DECAY_SCALE = math.exp(-0.5)

_NN = (((1,), (0,)), ((), ()))
_NT = (((1,), (1,)), ((), ()))
_TN = (((0,), (0,)), ((), ()))


def _dg(a, b, dims=_NN):
    return lax.dot_general(a, b, dims, preferred_element_type=F32)


def _mm(a, b, dims=_NN):
    return _dg(a.astype(BF16), b.astype(BF16), dims)


def _split2(x):
    hi = x.astype(BF16)
    lo = (x - hi.astype(F32)).astype(BF16)
    return hi, lo


def _mm2(a, b_bf16, dims=_NN):
    ah, al = _split2(a)
    return _dg(ah, b_bf16, dims) + _dg(al, b_bf16, dims)


def _mm2r(a_bf16, b, dims=_NN):
    bh, bl = _split2(b)
    return _dg(a_bf16, bh, dims) + _dg(a_bf16, bl, dims)


def _sigmoid(x):
    return 1.0 / (1.0 + jnp.exp(-x))


def _rms(x, g):
    ms = jnp.mean(x * x, axis=-1, keepdims=True)
    return x * lax.rsqrt(ms + RMS_EPS) * g


def _cparams(sem):
    return pltpu.CompilerParams(dimension_semantics=sem, vmem_limit_bytes=VMEM_LIMIT_BYTES)


def _swiglu_down(xn, wg, wu, wd):
    gate = jnp.dot(xn, wg, preferred_element_type=F32)
    up = jnp.dot(xn, wu, preferred_element_type=F32)
    h = (gate * _sigmoid(gate) * up).astype(BF16)
    return jnp.dot(h, wd, preferred_element_type=F32)


_FF_TAIL_STEP = 2


def _ff_block(j, *, steps):
    return jnp.where(j < _FF_TAIL_STEP, j, jnp.where(j == _FF_TAIL_STEP, steps - 1, j - 1))


def _ffn_kernel(x_ref, g_ref, wg_ref, wu_ref, wd_ref, g2_ref, *rest, emit_x, n_dtype, tf, emit_w):
    rest = list(rest)
    ox_ref = rest.pop(0) if emit_x else None
    on_ref = rest.pop(0)
    wgo_ref, wuo_ref, wdo_ref = (rest.pop(0), rest.pop(0), rest.pop(0)) if emit_w else (None, None, None)
    xn_scr, acc_scr = rest
    j = pl.program_id(1)

    @pl.when(j == 0)
    def _():
        xn_scr[...] = _rms(x_ref[...], g_ref[...]).astype(BF16)
        acc_scr[...] = jnp.zeros_like(acc_scr)

    steps = pl.cdiv(D_FF, tf)
    last = j == steps - 1
    tail = j == _FF_TAIL_STEP

    def accumulate(width):
        wg = wg_ref[:, 0:width].astype(BF16)
        wu = wu_ref[:, 0:width].astype(BF16)
        wd = wd_ref[0:width, :].astype(BF16)
        if emit_w:
            wgo_ref[:, 0:width] = wg
            wuo_ref[:, 0:width] = wu
            wdo_ref[0:width, :] = wd
        acc_scr[...] += _swiglu_down(xn_scr[...], wg, wu, wd)

    @pl.when(jnp.logical_not(tail))
    def _():
        accumulate(tf)

    @pl.when(tail)
    def _():
        accumulate(D_FF - (steps - 1) * tf)

    @pl.when(last)
    def _():
        y = x_ref[...] + FFN_RES * acc_scr[...]
        if emit_x:
            ox_ref[...] = y
        on_ref[...] = _rms(y, g2_ref[...]).astype(n_dtype)


def _ffn(x, g, wg, wu, wd, g2, *, tm, tf, emit_x, n_dtype, emit_w=False):
    m = x.shape[0]
    assert not emit_w or m == tm
    grid = (m // tm, pl.cdiv(D_FF, tf))
    row = pl.BlockSpec((tm, D_MODEL), lambda i, j: (i, 0))
    vec = pl.BlockSpec((1, D_MODEL), lambda i, j: (0, 0))
    steps = pl.cdiv(D_FF, tf)
    blk = functools.partial(_ff_block, steps=steps)
    w_in = pl.BlockSpec((D_MODEL, tf), lambda i, j: (0, blk(j)))
    w_dn = pl.BlockSpec((tf, D_MODEL), lambda i, j: (blk(j), 0))
    out_shape = [jax.ShapeDtypeStruct((m, D_MODEL), n_dtype)]
    out_specs = [row]
    if emit_x:
        out_shape = [jax.ShapeDtypeStruct((m, D_MODEL), F32)] + out_shape
        out_specs = [row, row]
    if emit_w:
        out_shape += [jax.ShapeDtypeStruct(w.shape, BF16) for w in (wg, wu, wd)]
        out_specs += [w_in, w_in, w_dn]
    return pl.pallas_call(
        functools.partial(_ffn_kernel, emit_x=emit_x, n_dtype=n_dtype, tf=tf, emit_w=emit_w),
        grid=grid,
        in_specs=[row, vec, w_in, w_in, w_dn, vec],
        out_specs=out_specs,
        out_shape=out_shape,
        scratch_shapes=[pltpu.VMEM((tm, D_MODEL), BF16), pltpu.VMEM((tm, D_MODEL), F32)],
        compiler_params=_cparams(("arbitrary", "arbitrary")),
        name="ffn_cast" if emit_w else "ffn",
    )(x, g, wg, wu, wd, g2)


def _ff_sweep():
    steps = pl.cdiv(D_FF, FF_TILE)
    order = list(range(_FF_TAIL_STEP)) + [steps - 1] + list(range(_FF_TAIL_STEP, steps - 1))
    return [(b, min(FF_TILE, D_FF - b * FF_TILE)) for b in order]


def _ffn_stream_kernel(x_ref, g_ref, wg_hbm, wu_hbm, wd_hbm, g2_ref, *rest, emit_x, n_dtype):
    rest = list(rest)
    ox_ref = rest.pop(0) if emit_x else None
    on_ref = rest.pop(0)
    wg_buf, wu_buf, wd_buf, acc_scr, sem = rest
    i = pl.program_id(0)
    sweep = _ff_sweep()
    assert len(sweep) % 2 == 0

    def copies(pos):
        blk, width = sweep[pos]
        slot = pos % 2
        cols = pl.ds(blk * FF_TILE, width)
        return (pltpu.make_async_copy(wg_hbm.at[:, cols], wg_buf.at[slot, :, pl.ds(0, width)], sem.at[0, slot]),
                pltpu.make_async_copy(wu_hbm.at[:, cols], wu_buf.at[slot, :, pl.ds(0, width)], sem.at[1, slot]),
                pltpu.make_async_copy(wd_hbm.at[cols, :], wd_buf.at[slot, pl.ds(0, width), :], sem.at[2, slot]))

    def start(pos):
        for cp in copies(pos):
            cp.start()

    @pl.when(i == 0)
    def _():
        start(0)

    for pos, (_, width) in enumerate(sweep):
        slot = pos % 2
        for cp in copies(pos):
            cp.wait()
        if pos + 1 < len(sweep):
            start(pos + 1)
        else:
            @pl.when(i + 1 < pl.num_programs(0))
            def _():
                start(0)
        if pos == 0:
            xn = _rms(x_ref[...], g_ref[...]).astype(BF16)
        part = _swiglu_down(xn, wg_buf[slot, :, 0:width], wu_buf[slot, :, 0:width], wd_buf[slot, 0:width, :])
        if pos == 0:
            acc_scr[...] = part
        else:
            acc_scr[...] += part

    y = x_ref[...] + FFN_RES * acc_scr[...]
    if emit_x:
        ox_ref[...] = y
    on_ref[...] = _rms(y, g2_ref[...]).astype(n_dtype)


def _ffn_stream(x, g, wg, wu, wd, g2, *, tm, emit_x, n_dtype):
    m = x.shape[0]
    row = pl.BlockSpec((tm, D_MODEL), lambda i: (i, 0))
    vec = pl.BlockSpec((1, D_MODEL), lambda i: (0, 0))
    hbm = pl.BlockSpec(memory_space=pl.ANY)
    out_shape = [jax.ShapeDtypeStruct((m, D_MODEL), n_dtype)]
    out_specs = [row]
    if emit_x:
        out_shape = [jax.ShapeDtypeStruct((m, D_MODEL), F32)] + out_shape
        out_specs = [row, row]
    return pl.pallas_call(
        functools.partial(_ffn_stream_kernel, emit_x=emit_x, n_dtype=n_dtype),
        grid=(m // tm,),
        in_specs=[row, vec, hbm, hbm, hbm, vec],
        out_specs=out_specs,
        out_shape=out_shape,
        scratch_shapes=[pltpu.VMEM((2, D_MODEL, FF_TILE), BF16), pltpu.VMEM((2, D_MODEL, FF_TILE), BF16),
                        pltpu.VMEM((2, FF_TILE, D_MODEL), BF16), pltpu.VMEM((tm, D_MODEL), F32),
                        pltpu.SemaphoreType.DMA((3, 2))],
        compiler_params=_cparams(("arbitrary",)),
        name="ffn",
    )(x, g, wg, wu, wd, g2)


def _proj_kernel(h_ref, w_ref, q_ref, kv_ref, rw_ref):
    h = h_ref[...]
    q = jnp.dot(h, w_ref[:, 0:ATT_WIDTH], preferred_element_type=F32)
    q_ref[...] = (q * (HEAD_DIM ** -0.5)).astype(BF16)
    kv_ref[...] = jnp.dot(h, w_ref[:, ATT_WIDTH:ATT_COLS], preferred_element_type=F32)
    rw_ref[...] = jnp.dot(h, w_ref[:, ATT_COLS:IN_COLS], preferred_element_type=F32)


def _proj(h, w, *, tm):
    m = h.shape[0]
    return pl.pallas_call(
        _proj_kernel,
        grid=(m // tm,),
        in_specs=[pl.BlockSpec((tm, D_MODEL), lambda i: (i, 0)),
                  pl.BlockSpec((D_MODEL, IN_COLS), lambda i: (0, 0), pipeline_mode=pl.Buffered(1))],
        out_specs=[pl.BlockSpec((tm, ATT_WIDTH), lambda i: (i, 0)),
                   pl.BlockSpec((tm, 2 * KV_WIDTH), lambda i: (i, 0)),
                   pl.BlockSpec((tm, SHIFT_COLS), lambda i: (i, 0))],
        out_shape=[jax.ShapeDtypeStruct((m, ATT_WIDTH), BF16),
                   jax.ShapeDtypeStruct((m, 2 * KV_WIDTH), F32),
                   jax.ShapeDtypeStruct((m, SHIFT_COLS), F32)],
        compiler_params=_cparams(("arbitrary",)),
        name="in_proj",
    )(h, w)


def _outproj_kernel(x_ref, oa_ref, orw_ref, w_ref, o_ref, *wo_ref):
    w_att = w_ref[0:ATT_WIDTH, :].astype(BF16)
    w_rw = w_ref[ATT_WIDTH:D_MODEL, :].astype(BF16)
    if wo_ref:
        wo_ref[0][0:ATT_WIDTH, :] = w_att
        wo_ref[0][ATT_WIDTH:D_MODEL, :] = w_rw
    acc = jnp.dot(oa_ref[...], w_att, preferred_element_type=F32)
    acc += jnp.dot(orw_ref[...], w_rw, preferred_element_type=F32)
    o_ref[...] = x_ref[...] + acc


def _outproj(x, oa, orw, w, *, tm, emit_w=False):
    m = x.shape[0]
    assert not emit_w or m == tm
    row = pl.BlockSpec((tm, D_MODEL), lambda i: (i, 0))
    half = pl.BlockSpec((tm, ATT_WIDTH), lambda i: (i, 0))
    full = pl.BlockSpec((D_MODEL, D_MODEL), lambda i: (0, 0), pipeline_mode=pl.Buffered(1))
    out_shape = [jax.ShapeDtypeStruct((m, D_MODEL), F32)]
    out_specs = [row]
    if emit_w:
        out_shape.append(jax.ShapeDtypeStruct((D_MODEL, D_MODEL), BF16))
        out_specs.append(pl.BlockSpec((D_MODEL, D_MODEL), lambda i: (0, 0)))
    return pl.pallas_call(
        _outproj_kernel,
        grid=(m // tm,),
        in_specs=[row, half, half, full],
        out_specs=out_specs,
        out_shape=out_shape,
        compiler_params=_cparams(("arbitrary",)),
        name="out_proj_cast" if emit_w else "out_proj",
    )(x, oa, orw, w)


def _bucket_table():
    qi = np.arange(BLOCK)[:, None]
    kj = np.arange(2 * BLOCK)[None, :]
    dist = BLOCK + qi - kj
    n = np.maximum(dist, 0)
    max_exact = N_BUCKETS // 2
    nf = np.maximum(n, 1).astype(np.float32)
    large = max_exact + (np.log(nf / np.float32(max_exact)) / np.float32(math.log(MAX_DISTANCE / max_exact))
                         * np.float32(N_BUCKETS - max_exact)).astype(np.int32)
    large = np.minimum(large, N_BUCKETS - 1)
    bucket = np.where(n < max_exact, n, large).astype(np.int32)
    valid = (dist >= 0) & (dist <= WINDOW)
    return np.where(valid, bucket, -1).astype(np.int32)


def _bias_kernel(bucket_ref, rb_ref, o_ref):
    bucket = bucket_ref[...]
    kj = lax.broadcasted_iota(jnp.int32, (BLOCK, 2 * BLOCK), 1)
    for h in range(N_Q_HEADS):
        acc = jnp.full((BLOCK, 2 * BLOCK), NEG_BIG, F32)
        for n in range(N_BUCKETS):
            acc = jnp.where(bucket == n, rb_ref[n, h], acc)
        o_ref[0, h] = jnp.where(kj >= BLOCK, acc, NEG_BIG)
        o_ref[1, h] = acc


def _bias_table(rel_bias):
    bucket = jnp.asarray(_bucket_table())
    return pl.pallas_call(
        _bias_kernel,
        in_specs=[pl.BlockSpec(memory_space=pltpu.VMEM), pl.BlockSpec(memory_space=pltpu.SMEM)],
        out_specs=pl.BlockSpec(memory_space=pltpu.VMEM),
        out_shape=jax.ShapeDtypeStruct((2, N_Q_HEADS, BLOCK, 2 * BLOCK), F32),
        name="bias_table",
    )(bucket, rel_bias)


def _attn_kernel(sink_ref, q_ref, kvp_ref, kvc_ref, bias_ref, o_ref, kv_scr):
    j = pl.program_id(1)
    kv_scr[0:BLOCK, :] = kvp_ref[...]
    kv_scr[BLOCK:, :] = kvc_ref[...]

    def block(i, carry):
        r0 = pl.multiple_of(i * BLOCK, BLOCK)
        rows = pl.ds(r0, BLOCK)
        table = jnp.where(jnp.logical_and(j == 0, i == 0), 0, 1)
        _attn_block(sink_ref, q_ref.at[rows], kv_scr[rows, :], kv_scr[pl.ds(r0 + BLOCK, BLOCK), :],
                    bias_ref.at[table], o_ref.at[rows])
        return carry

    lax.fori_loop(0, ATT_BLOCKS_PER_STEP, block, 0)


def _attn_block(sink_ref, q_ref, kvp, kvc, bias_ref, o_ref):
    half = N_KV_HEADS // 2
    for g0 in range(0, N_KV_HEADS, half):
        kcat, vcat = {}, {}
        for g in range(g0, g0 + half):
            ks = slice(g * HEAD_DIM, (g + 1) * HEAD_DIM)
            vs = slice(KV_WIDTH + g * HEAD_DIM, KV_WIDTH + (g + 1) * HEAD_DIM)
            kcat[g] = jnp.concatenate([kvp[:, ks], kvc[:, ks]], axis=0).astype(BF16)
            vcat[g] = jnp.concatenate([kvp[:, vs], kvc[:, vs]], axis=0).astype(BF16)
        heads = range(g0 * GQA_GROUP, (g0 + half) * GQA_GROUP)
        hs = {h: slice(h * HEAD_DIM, (h + 1) * HEAD_DIM) for h in heads}
        s = {h: _dg(q_ref[:, hs[h]], kcat[h // GQA_GROUP], _NT) + bias_ref[h] for h in heads}
        m = {h: jnp.maximum(jnp.max(s[h], axis=-1, keepdims=True), sink_ref[h]) for h in heads}
        p = {h: jnp.exp(s[h] - m[h]) for h in heads}
        denom = {h: jnp.sum(p[h], axis=-1, keepdims=True) + jnp.exp(sink_ref[h] - m[h]) for h in heads}
        o = {h: jnp.dot(p[h].astype(BF16), vcat[h // GQA_GROUP], preferred_element_type=F32) for h in heads}
        for h in heads:
            o_ref[:, hs[h]] = (o[h] / denom[h]).astype(BF16)


def _attn_prompt(q, kv, bias, sinks, *, batch, seq):
    n = ATT_BLOCKS_PER_STEP
    rows = n * BLOCK
    q3 = q.reshape(batch, seq, ATT_WIDTH)
    kv3 = kv.reshape(batch, seq, 2 * KV_WIDTH)
    out = pl.pallas_call(
        _attn_kernel,
        grid=(batch, seq // rows),
        in_specs=[pl.BlockSpec(memory_space=pltpu.SMEM),
                  pl.BlockSpec((None, rows, ATT_WIDTH), lambda b, j: (b, j, 0)),
                  pl.BlockSpec((None, BLOCK, 2 * KV_WIDTH), lambda b, j: (b, jnp.maximum(n * j - 1, 0), 0)),
                  pl.BlockSpec((None, rows, 2 * KV_WIDTH), lambda b, j: (b, j, 0)),
                  pl.BlockSpec((2, N_Q_HEADS, BLOCK, 2 * BLOCK), lambda b, j: (0, 0, 0, 0))],
        out_specs=pl.BlockSpec((None, rows, ATT_WIDTH), lambda b, j: (b, j, 0)),
        out_shape=jax.ShapeDtypeStruct((batch, seq, ATT_WIDTH), BF16),
        scratch_shapes=[pltpu.VMEM((rows + BLOCK, 2 * KV_WIDTH), F32)],
        compiler_params=_cparams(("arbitrary", "arbitrary")),
        name="attn_prompt",
    )(sinks, q3, kv3, kv3, bias)
    return out.reshape(batch * seq, ATT_WIDTH)


def _attn_decode_kernel(sink_ref, q_ref, kvn_ref, ck_ref, cv_ref, bias_ref, o_ref):
    q = q_ref[...].astype(F32)
    kvn = kvn_ref[...]
    for g in range(N_KV_HEADS):
        ks = slice(g * HEAD_DIM, (g + 1) * HEAD_DIM)
        vs = slice(KV_WIDTH + g * HEAD_DIM, KV_WIDTH + (g + 1) * HEAD_DIM)
        hs = slice(g * GQA_GROUP, (g + 1) * GQA_GROUP)
        qg = q[:, hs, :]
        kc = ck_ref[:, :, ks]
        vc = cv_ref[:, :, ks]
        kn = kvn[:, ks].astype(BF16).astype(F32)[:, None, :]
        vn = kvn[:, vs].astype(BF16).astype(F32)[:, None, :]
        s = jnp.einsum("bqd,bkd->bqk", qg.astype(BF16), kc.astype(BF16), preferred_element_type=F32)
        s = s + bias_ref[hs, 0:WINDOW][None]
        s_new = jnp.sum(qg * kn, axis=-1, keepdims=True) + bias_ref[hs, WINDOW:WINDOW + 1][None]
        sink = sink_ref[hs, 0:1][None]
        m = jnp.maximum(jnp.maximum(jnp.max(s, axis=-1, keepdims=True), s_new), sink)
        p = jnp.exp(s - m)
        p_new = jnp.exp(s_new - m)
        denom = jnp.sum(p, axis=-1, keepdims=True) + p_new + jnp.exp(sink - m)
        o = jnp.einsum("bqk,bkd->bqd", p.astype(BF16), vc.astype(BF16), preferred_element_type=F32)
        o = o + p_new * vn
        o_ref[:, hs, :] = o / denom


def _attn_decode(q, kvn, cache_k, cache_v, bias_row, sinks_b):
    b = q.shape[0]
    lw = cache_k.shape[1]
    vm = pl.BlockSpec(memory_space=pltpu.VMEM)
    out = pl.pallas_call(
        _attn_decode_kernel,
        in_specs=[vm, vm, vm, vm, vm, vm],
        out_specs=vm,
        out_shape=jax.ShapeDtypeStruct((b, N_Q_HEADS, HEAD_DIM), F32),
        compiler_params=pltpu.CompilerParams(vmem_limit_bytes=VMEM_LIMIT_BYTES),
        name="attn_decode",
    )(sinks_b, q.reshape(b, N_Q_HEADS, HEAD_DIM), kvn,
      cache_k.reshape(b, lw, KV_WIDTH), cache_v.reshape(b, lw, KV_WIDTH), bias_row)
    return out.reshape(b, ATT_WIDTH)


def _seg_expand(x, bd_ref):
    rows = x.shape[0]
    groups = x.shape[1] // LANES
    xs = jnp.concatenate([x[:, i * LANES:(i + 1) * LANES] for i in range(groups)], axis=0)
    s = _mm(xs, bd_ref[...])
    return jnp.concatenate([s[i * rows:(i + 1) * rows] for i in range(groups)], axis=1)


def _rwkv_front(x, prev, p):
    mu_ref, w0_ref, a0_ref, kk_ref, ka_ref, lw_ref, g2_ref, bd_ref = p
    xm = x + mu_ref[...] * (prev - x)
    o3 = 3 * RWKV_WIDTH
    r = xm[:, 0:RWKV_WIDTH]
    k = xm[:, RWKV_WIDTH:2 * RWKV_WIDTH]
    v = xm[:, 2 * RWKV_WIDTH:o3]
    wa = xm[:, o3:o3 + LANES]
    lane = lax.broadcasted_iota(jnp.int32, wa.shape, 1)
    z = jnp.where(lane < DECAY_LORA, jnp.tanh(wa), wa)
    lora = _mm(z, lw_ref[...])
    logw = -DECAY_SCALE * _sigmoid(w0_ref[...] + lora[:, 0:RWKV_WIDTH])
    a = _sigmoid(a0_ref[...] + lora[:, RWKV_WIDTH:2 * RWKV_WIDTH])
    g = _mm(_sigmoid(xm[:, o3 + LANES:SHIFT_COLS]), g2_ref[...])
    kk = k * kk_ref[...]
    kmod = k * (1.0 + (a - 1.0) * ka_ref[...])
    nrm = jnp.sqrt(_seg_expand(kk * kk, bd_ref))
    kk = kk / jnp.maximum(nrm, 1e-12)
    return r, logw, kmod, v, kk, kk * a, g


def _rwkv_back(y, r, kmod, v, g, q):
    rk_ref, lnw_ref, lnb_ref, bd_ref = q
    mean = _seg_expand(y, bd_ref) * (1.0 / HEAD_DIM)
    d = y - mean
    var = _seg_expand(d * d, bd_ref) * (1.0 / HEAD_DIM)
    yn = d * lax.rsqrt(var + GN_EPS) * lnw_ref[...] + lnb_ref[...]
    bonus = _seg_expand(r * kmod * rk_ref[...], bd_ref) * v
    return (yn + bonus) * g


def _rwkv_chunk(rw_ref, mu_ref, w0_ref, a0_ref, kk_ref, ka_ref, lw_ref, g2_ref, bd_ref,
                rk_ref, lnw_ref, lnb_ref,
                o_ref,
                s_scr, prev_scr, y_scr, ea_scr, er_scr, eb_scr, ek_scr, be_scr, ke_scr, v_scr):
    C = CHUNK
    x = rw_ref[...]
    row = lax.broadcasted_iota(jnp.int32, x.shape, 0)
    prev = jnp.where(row == 0, prev_scr[...], pltpu.roll(x, 1, axis=0))
    prev_scr[...] = x[C - 1:C, :]
    front = (mu_ref, w0_ref, a0_ref, kk_ref, ka_ref, lw_ref, g2_ref, bd_ref)
    r, logw, kmod, v, kk, bb, g = _rwkv_front(x, prev, front)

    ti = lax.broadcasted_iota(jnp.int32, (C, C), 0)
    si = lax.broadcasted_iota(jnp.int32, (C, C), 1)
    cum = _mm2r(jnp.where(si <= ti, 1.0, 0.0).astype(BF16), logw)
    cum_end = cum[C - 1:C, :]
    e_in = jnp.exp(cum)
    e_neg = jnp.exp(-cum)
    e_end = jnp.exp(cum_end - cum)
    ea_scr[...] = kk * jnp.exp(cum - logw)
    er_scr[...] = r * e_in
    eb_scr[...] = bb * e_neg
    ek_scr[...] = kmod * e_neg
    be_scr[...] = bb * e_end
    ke_scr[...] = kmod * e_end
    v_scr[...] = v
    w_end = jnp.exp(cum_end)

    lane = lax.broadcasted_iota(jnp.int32, (C, LANES), 1)
    lo_half = lane < HEAD_DIM
    ri = lax.broadcasted_iota(jnp.int32, (LANES, LANES), 0)
    ci = lax.broadcasted_iota(jnp.int32, (LANES, LANES), 1)
    same_head = jnp.right_shift(ri, 6) == jnp.right_shift(ci, 6)
    eye = ri == ci
    t_loc = jnp.bitwise_and(ri, C - 1)
    s_loc = jnp.bitwise_and(ci, C - 1)
    bd_strict = jnp.logical_and(same_head, s_loc < t_loc)
    bd_incl = jnp.logical_and(same_head, s_loc <= t_loc)

    def swap(t):
        return pltpu.roll(t, HEAD_DIM, axis=1)

    def stack2(t):
        return jnp.concatenate([jnp.where(lo_half, t, 0.0), jnp.where(lo_half, 0.0, t)], axis=0)

    def stack2x(t):
        ts = swap(t)
        return jnp.concatenate([jnp.where(lo_half, 0.0, ts), jnp.where(lo_half, ts, 0.0)], axis=0)

    def own(t):
        return jnp.where(lo_half, t[0:C], t[C:2 * C])

    def other(t):
        return swap(jnp.where(lo_half, t[C:2 * C], t[0:C]))

    pairs = range(HEAD_PAIRS)
    sl = [slice(p * LANES, (p + 1) * LANES) for p in pairs]
    a2 = [stack2(ea_scr[:, sl[p]]) for p in pairs]
    m = []
    for p in pairs:
        r2 = stack2(er_scr[:, sl[p]])
        b_p = eb_scr[:, sl[p]]
        k_p = ek_scr[:, sl[p]]
        m.append(_mm(jnp.concatenate([a2[p], r2], axis=0), jnp.concatenate([b_p, b_p, k_p, k_p], axis=0), _NT))
    l_ak = [jnp.where(bd_strict, m[p][0:2 * C, LANES:2 * LANES], 0.0) for p in pairs]
    pw = [-jnp.where(bd_strict, m[p][0:2 * C, 0:LANES], 0.0) for p in pairs]
    lrbk = [jnp.concatenate([jnp.where(bd_incl, m[p][2 * C:4 * C, 0:LANES], 0.0),
                             jnp.where(bd_incl, m[p][2 * C:4 * C, LANES:2 * LANES], 0.0)], axis=1) for p in pairs]
    v2x = [stack2x(v_scr[:, sl[p]]) for p in pairs]
    xc = [a2[p] - _mm(l_ak[p], v2x[p]) for p in pairs]
    for it in range(6):
        for p in pairs:
            pw_b = pw[p].astype(BF16)
            xc_b = xc[p].astype(BF16)
            if it < 5:
                px = _dg(pw_b, jnp.concatenate([xc_b, pw_b], axis=1))
                pw[p] = px[:, LANES:2 * LANES]
                xc[p] = xc[p] + px[:, 0:LANES]
            else:
                xc[p] = xc[p] + _dg(pw_b, xc_b)
    xv = [jnp.concatenate([xc[p], v2x[p]], axis=0).astype(BF16) for p in pairs]
    lb = [_mm(lrbk[p], xv[p]) for p in pairs]
    z = [_mm(xv[p], jnp.concatenate([stack2(be_scr[:, sl[p]]), stack2(ke_scr[:, sl[p]])], axis=0), _TN)
         for p in pairs]
    rt = [er_scr[:, sl[p]] - own(lb[p]) for p in pairs]
    yl = [other(lb[p]) for p in pairs]
    d = [jnp.where(eye, w_end[:, sl[p]], 0.0) - jnp.where(same_head, z[p], 0.0) for p in pairs]
    gg = [jnp.where(same_head, jnp.concatenate([z[p][C:2 * C], z[p][0:C]], axis=0), 0.0) for p in pairs]
    s0 = [s_scr[p].astype(BF16) for p in pairs]
    ys = [_mm(rt[p], s0[p], _NT) for p in pairs]
    for p in pairs:
        y_scr[:, sl[p]] = yl[p] + ys[p]
    sn = [_mm(s0[p], d[p]) for p in pairs]
    for p in pairs:
        s_scr[p] = sn[p] + gg[p]

    back = (rk_ref, lnw_ref, lnb_ref, bd_ref)
    o_ref[...] = _rwkv_back(y_scr[...], r, kmod, v, g, back).astype(BF16)


def _rwkv_chunk_kernel(rw_ref, *rest):
    params, (o_ref, s_out_ref, s_scr, prev_scr), work = rest[:11], rest[11:15], rest[15:]
    c = pl.program_id(1)

    @pl.when(c == 0)
    def _():
        s_scr[...] = jnp.zeros_like(s_scr)
        prev_scr[...] = jnp.zeros_like(prev_scr)

    for i in range(CHUNKS_PER_STEP):
        rows = pl.ds(i * CHUNK, CHUNK)
        _rwkv_chunk(rw_ref.at[rows], *params, o_ref.at[rows], s_scr, prev_scr, *[w.at[i] for w in work])

    @pl.when(c == pl.num_programs(1) - 1)
    def _():
        for p in range(HEAD_PAIRS):
            s_out_ref[2 * p] = s_scr[p, 0:HEAD_DIM, 0:HEAD_DIM]
            s_out_ref[2 * p + 1] = s_scr[p, HEAD_DIM:LANES, HEAD_DIM:LANES]


def _rwkv_prompt(rw, params, *, batch, seq):
    rw3 = rw.reshape(batch, seq, SHIFT_COLS)
    rows = CHUNKS_PER_STEP * CHUNK
    nc = seq // rows
    const = lambda shape: pl.BlockSpec(shape, lambda b, c: tuple(0 for _ in shape))
    in_specs = [pl.BlockSpec((None, rows, SHIFT_COLS), lambda b, c: (b, c, 0))]
    in_specs += [const(p.shape) for p in params]
    wide = pltpu.VMEM((CHUNKS_PER_STEP, CHUNK, RWKV_WIDTH), F32)
    o, s_out = pl.pallas_call(
        _rwkv_chunk_kernel,
        grid=(batch, nc),
        in_specs=in_specs,
        out_specs=[pl.BlockSpec((None, rows, RWKV_WIDTH), lambda b, c: (b, c, 0)),
                   pl.BlockSpec((None, RWKV_HEADS, HEAD_DIM, HEAD_DIM), lambda b, c: (b, 0, 0, 0))],
        out_shape=[jax.ShapeDtypeStruct((batch, seq, RWKV_WIDTH), BF16),
                   jax.ShapeDtypeStruct((batch, RWKV_HEADS, HEAD_DIM, HEAD_DIM), F32)],
        scratch_shapes=[pltpu.VMEM((HEAD_PAIRS, LANES, LANES), F32),
                        pltpu.VMEM((1, SHIFT_COLS), F32),
                        wide, wide, wide, wide, wide, wide, wide, wide],
        compiler_params=_cparams(("arbitrary", "arbitrary")),
        name="rwkv_prompt",
    )(rw3, *params)
    return o.reshape(batch * seq, RWKV_WIDTH), s_out


def _rwkv_dec_front_kernel(rw_ref, prev_ref, mu_ref, w0_ref, a0_ref, kk_ref, ka_ref, lw_ref, g2_ref, bd_ref,
                           r_ref, w_ref, k_ref, v_ref, kkn_ref, b_ref, g_ref):
    front = (mu_ref, w0_ref, a0_ref, kk_ref, ka_ref, lw_ref, g2_ref, bd_ref)
    r, logw, kmod, v, kk, bb, g = _rwkv_front(rw_ref[...], prev_ref[...], front)
    r_ref[...] = r
    w_ref[...] = jnp.exp(logw)
    k_ref[...] = kmod
    v_ref[...] = v
    kkn_ref[...] = kk
    b_ref[...] = bb
    g_ref[...] = g


def _rwkv_dec_state_kernel(r_all, w_all, k_all, v_all, kk_all, b_all, s_ref, y_ref, so_ref):
    row = pl.ds(pl.program_id(0), 1)
    r_ref, w_ref, k_ref, v_ref, kk_ref, b_ref = (t.at[row] for t in (r_all, w_all, k_all, v_all, kk_all, b_all))
    ri = lax.broadcasted_iota(jnp.int32, (HEAD_DIM, HEAD_DIM), 0)
    ci = lax.broadcasted_iota(jnp.int32, (HEAD_DIM, HEAD_DIM), 1)
    eye = ri == ci
    ones = jnp.ones((HEAD_DIM, LANES), BF16)
    heads = range(RWKV_HEADS)
    hs = [slice(h * HEAD_DIM, (h + 1) * HEAD_DIM) for h in heads]
    n = RWKV_HEADS * HEAD_DIM
    s = [s_ref[h] for h in heads]
    lhs = [s[h] * kk_ref[:, hs[h]] for h in heads] + [jnp.where(eye, v_ref[:, hs[h]], 0.0) for h in heads]
    red = _mm(jnp.concatenate(lhs, axis=0), ones)
    s_new = []
    for h in heads:
        sa = -red[h * HEAD_DIM:(h + 1) * HEAD_DIM, 0:HEAD_DIM]
        v_col = red[n + h * HEAD_DIM:n + (h + 1) * HEAD_DIM, 0:HEAD_DIM]
        s_new.append(s[h] * w_ref[:, hs[h]] + sa * b_ref[:, hs[h]] + v_col * k_ref[:, hs[h]])
        so_ref[h] = s_new[h]
    yb = _mm(jnp.concatenate([s_new[h] * r_ref[:, hs[h]] for h in heads], axis=0), ones)
    for h in heads:
        y_blk = yb[h * HEAD_DIM:(h + 1) * HEAD_DIM, 0:HEAD_DIM]
        y_ref[:, hs[h]] = jnp.sum(jnp.where(eye, y_blk, 0.0), axis=0, keepdims=True)


def _rwkv_dec_back_kernel(y_ref, r_ref, k_ref, v_ref, g_ref, rk_ref, lnw_ref, lnb_ref, bd_ref, o_ref):
    back = (rk_ref, lnw_ref, lnb_ref, bd_ref)
    o_ref[...] = _rwkv_back(y_ref[...], r_ref[...], k_ref[...], v_ref[...], g_ref[...], back).astype(BF16)


def _rwkv_decode(rw, shift0, wkv0, front_params, back_params):
    b = rw.shape[0]
    vm = pl.BlockSpec(memory_space=pltpu.VMEM)
    wide = jax.ShapeDtypeStruct((b, RWKV_WIDTH), F32)
    r, w, k, v, kk, bb, g = pl.pallas_call(
        _rwkv_dec_front_kernel,
        in_specs=[vm] * (2 + len(front_params)),
        out_specs=[vm] * 7,
        out_shape=[wide] * 7,
        name="rwkv_dec_front",
    )(rw, shift0, *front_params)
    rowspec = pl.BlockSpec((None, 1, RWKV_WIDTH), lambda i: (i, 0, 0))
    stspec = pl.BlockSpec((None, RWKV_HEADS, HEAD_DIM, HEAD_DIM), lambda i: (i, 0, 0, 0))
    allrows = pl.BlockSpec((b, RWKV_WIDTH), lambda i: (0, 0))
    y, s_new = pl.pallas_call(
        _rwkv_dec_state_kernel,
        grid=(b,),
        in_specs=[allrows] * 6 + [stspec],
        out_specs=[rowspec, stspec],
        out_shape=[jax.ShapeDtypeStruct((b, 1, RWKV_WIDTH), F32),
                   jax.ShapeDtypeStruct((b, RWKV_HEADS, HEAD_DIM, HEAD_DIM), F32)],
        compiler_params=_cparams(("arbitrary",)),
        name="rwkv_dec_state",
    )(r, w, k, v, kk, bb, wkv0)
    o = pl.pallas_call(
        _rwkv_dec_back_kernel,
        in_specs=[vm] * (5 + len(back_params)),
        out_specs=vm,
        out_shape=jax.ShapeDtypeStruct((b, RWKV_WIDTH), BF16),
        name="rwkv_dec_back",
    )(y.reshape(b, RWKV_WIDTH), r, k, v, g, *back_params)
    return o, s_new


def kernel(x_prompt, x_sample, cache_k, cache_v, state_wkv, state_shift, rel_bias, ffn1_norm, ffn1_w_gate, ffn1_w_up, ffn1_w_down, mix_norm, w_in, attn_sinks, shift_mu, decay_w0, decay_w2, aaa_a0, aaa_a2, gate_g2, key_k, key_a, bonus_r_k, ln_x_w, ln_x_b, w_out, ffn2_norm, ffn2_w_gate, ffn2_w_up, ffn2_w_down, final_norm):
    batch, seq, _ = x_prompt.shape
    dec_b = x_sample.shape[0]
    lw = cache_k.shape[2]
    l = 0
    row = lambda t: t.reshape(1, -1)

    w_in_b = w_in[l].astype(BF16)
    lora_w = jnp.zeros((LANES, 2 * RWKV_WIDTH), F32)
    lora_w = lora_w.at[0:DECAY_LORA, 0:RWKV_WIDTH].set(decay_w2[l])
    lora_w = lora_w.at[DECAY_LORA:LANES, RWKV_WIDTH:].set(aaa_a2[l])
    head_of = np.arange(LANES) // HEAD_DIM
    bd_ones = jnp.asarray(head_of[:, None] == head_of[None, :], BF16)
    front_params = (row(shift_mu[l]), row(decay_w0[l]), row(aaa_a0[l]), row(key_k[l]), row(key_a[l]),
                    lora_w.astype(BF16), gate_g2[l].astype(BF16), bd_ones)
    back_params = (row(bonus_r_k[l]), row(ln_x_w[l]), row(ln_x_b[l]), bd_ones)

    bias = _bias_table(rel_bias)
    sinks = attn_sinks[l]

    xs = x_sample.reshape(dec_b, D_MODEL)
    x1s, h1s, *f1 = _ffn(xs, row(ffn1_norm[l]), ffn1_w_gate[l], ffn1_w_up[l], ffn1_w_down[l], row(mix_norm[l]),
                         tm=dec_b, tf=FF_TILE_CAST, emit_x=True, n_dtype=BF16, emit_w=True)
    qs, kvs, rws = _proj(h1s, w_in_b, tm=dec_b)
    sinks_b = jnp.broadcast_to(sinks[:, None], (N_Q_HEADS, LANES))
    o_att_s = _attn_decode(qs, kvs, cache_k[l], cache_v[l], bias[1, :, 0, :], sinks_b)
    o_rw_s, s_s = _rwkv_decode(rws, state_shift[l], state_wkv[l], front_params, back_params)
    x2s, w_out_b = _outproj(x1s, o_att_s.astype(BF16), o_rw_s, w_out[l], tm=dec_b, emit_w=True)
    y_s, *f2 = _ffn(x2s, row(ffn2_norm[l]), ffn2_w_gate[l], ffn2_w_up[l], ffn2_w_down[l], row(final_norm),
                    tm=dec_b, tf=FF_TILE_CAST, emit_x=False, n_dtype=F32, emit_w=True)

    xp = x_prompt.reshape(batch * seq, D_MODEL)
    x1, h1 = _ffn_stream(xp, row(ffn1_norm[l]), *f1, row(mix_norm[l]), tm=512, emit_x=True, n_dtype=BF16)
    q, kv, rw = _proj(h1, w_in_b, tm=256)
    o_att = _attn_prompt(q, kv, bias, sinks, batch=batch, seq=seq)
    o_rw, s_p = _rwkv_prompt(rw, front_params + back_params[:3], batch=batch, seq=seq)
    (x2,) = _outproj(x1, o_att, o_rw, w_out_b, tm=512)
    (y_p,) = _ffn_stream(x2, row(ffn2_norm[l]), *f2, row(final_norm), tm=512, emit_x=False, n_dtype=F32)

    kv3 = kv.reshape(batch, seq, 2 * KV_WIDTH)
    lp = min(WINDOW, seq)
    new_k_p = kv3[:, seq - lp:, 0:KV_WIDTH].reshape(1, batch, lp, N_KV_HEADS, HEAD_DIM)
    new_v_p = kv3[:, seq - lp:, KV_WIDTH:].reshape(1, batch, lp, N_KV_HEADS, HEAD_DIM)
    new_shift_p = rw.reshape(batch, seq, SHIFT_COLS)[:, seq - 1][None]
    k_new = kvs[:, 0:KV_WIDTH].reshape(dec_b, 1, N_KV_HEADS, HEAD_DIM)
    v_new = kvs[:, KV_WIDTH:].reshape(dec_b, 1, N_KV_HEADS, HEAD_DIM)
    new_k_s = jnp.concatenate([cache_k[l], k_new], axis=1)[:, -lw:][None]
    new_v_s = jnp.concatenate([cache_v[l], v_new], axis=1)[:, -lw:][None]
    return (y_p.reshape(batch, seq, D_MODEL), y_s.reshape(dec_b, 1, D_MODEL),
            new_k_p, new_v_p, s_p[None], new_shift_p,
            new_k_s, new_v_s, s_s[None], rws[None])
```

```python
import functools
import math

import numpy as np
import jax
import jax.numpy as jnp
from jax import lax
from jax.experimental import pallas as pl
from jax.experimental.pallas import tpu as pltpu

F32 = jnp.float32
BF16 = jnp.bfloat16

D_MODEL = 2048
HEAD_DIM = 64
ATT_WIDTH = 1024
N_Q_HEADS = 16
N_KV_HEADS = 4
GQA_GROUP = 4
KV_WIDTH = 256
RWKV_WIDTH = 1024
RWKV_HEADS = 16
WINDOW = 128
BLOCK = 128
N_BUCKETS = 32
MAX_DISTANCE = 128
DECAY_LORA = 64
AAA_LORA = 64
GATE_LORA = 128
D_FF = 5504
ATT_COLS = ATT_WIDTH + 2 * KV_WIDTH
SHIFT_COLS = 3 * RWKV_WIDTH + DECAY_LORA + AAA_LORA + GATE_LORA
IN_COLS = ATT_COLS + SHIFT_COLS
RMS_EPS = 1e-5
GN_EPS = 64e-5
FFN_RES = 0.5

LANES = 128
VMEM_LIMIT_BYTES = 60 * 1024 * 1024

FF_TILE = 1024
FF_TILE_CAST = 512
CHUNK = 64
CHUNKS_PER_STEP = 4
ATT_BLOCKS_PER_STEP = 8
HEAD_PAIRS = RWKV_HEADS // 2
NEG_BIG = -1e30
DECAY_SCALE = math.exp(-0.5)

_NN = (((1,), (0,)), ((), ()))
_NT = (((1,), (1,)), ((), ()))
_TN = (((0,), (0,)), ((), ()))


def _dg(a, b, dims=_NN):
    return lax.dot_general(a, b, dims, preferred_element_type=F32)


def _mm(a, b, dims=_NN):
    return _dg(a.astype(BF16), b.astype(BF16), dims)


def _split2(x):
    hi = x.astype(BF16)
    lo = (x - hi.astype(F32)).astype(BF16)
    return hi, lo


def _mm2r(a_bf16, b, dims=_NN):
    bh, bl = _split2(b)
    return _dg(a_bf16, bh, dims) + _dg(a_bf16, bl, dims)


def _sigmoid(x):
    return 1.0 / (1.0 + jnp.exp(-x))


def _rms(x, g):
    ms = jnp.mean(x * x, axis=-1, keepdims=True)
    return x * lax.rsqrt(ms + RMS_EPS) * g


def _cparams(sem):
    return pltpu.CompilerParams(dimension_semantics=sem, vmem_limit_bytes=VMEM_LIMIT_BYTES)


def _swiglu_down(xn, wg, wu, wd):
    gate = jnp.dot(xn, wg, preferred_element_type=F32)
    up = jnp.dot(xn, wu, preferred_element_type=F32)
    h = (gate * _sigmoid(gate) * up).astype(BF16)
    return jnp.dot(h, wd, preferred_element_type=F32)


_FF_TAIL_STEP = 2


def _ff_block(j, *, steps):
    return jnp.where(j < _FF_TAIL_STEP, j, jnp.where(j == _FF_TAIL_STEP, steps - 1, j - 1))


def _ffn_kernel(x_ref, g_ref, wg_ref, wu_ref, wd_ref, g2_ref, *rest, emit_x, n_dtype, tf, emit_w):
    rest = list(rest)
    ox_ref = rest.pop(0) if emit_x else None
    on_ref = rest.pop(0)
    wgo_ref, wuo_ref, wdo_ref = (rest.pop(0), rest.pop(0), rest.pop(0)) if emit_w else (None, None, None)
    xn_scr, acc_scr = rest
    j = pl.program_id(1)

    @pl.when(j == 0)
    def _():
        xn_scr[...] = _rms(x_ref[...], g_ref[...]).astype(BF16)
        acc_scr[...] = jnp.zeros_like(acc_scr)

    steps = pl.cdiv(D_FF, tf)
    last = j == steps - 1
    tail = j == _FF_TAIL_STEP

    def accumulate(width):
        wg = wg_ref[:, 0:width].astype(BF16)
        wu = wu_ref[:, 0:width].astype(BF16)
        wd = wd_ref[0:width, :].astype(BF16)
        if emit_w:
            wgo_ref[:, 0:width] = wg
            wuo_ref[:, 0:width] = wu
            wdo_ref[0:width, :] = wd
        acc_scr[...] += _swiglu_down(xn_scr[...], wg, wu, wd)

    @pl.when(jnp.logical_not(tail))
    def _():
        accumulate(tf)

    @pl.when(tail)
    def _():
        accumulate(D_FF - (steps - 1) * tf)

    @pl.when(last)
    def _():
        y = x_ref[...] + FFN_RES * acc_scr[...]
        if emit_x:
            ox_ref[...] = y
        on_ref[...] = _rms(y, g2_ref[...]).astype(n_dtype)


def _ffn(x, g, wg, wu, wd, g2, *, tm, tf, emit_x, n_dtype, emit_w=False):
    m = x.shape[0]
    assert not emit_w or m == tm
    grid = (m // tm, pl.cdiv(D_FF, tf))
    row = pl.BlockSpec((tm, D_MODEL), lambda i, j: (i, 0))
    vec = pl.BlockSpec((1, D_MODEL), lambda i, j: (0, 0))
    steps = pl.cdiv(D_FF, tf)
    blk = functools.partial(_ff_block, steps=steps)
    w_in = pl.BlockSpec((D_MODEL, tf), lambda i, j: (0, blk(j)))
    w_dn = pl.BlockSpec((tf, D_MODEL), lambda i, j: (blk(j), 0))
    out_shape = [jax.ShapeDtypeStruct((m, D_MODEL), n_dtype)]
    out_specs = [row]
    if emit_x:
        out_shape = [jax.ShapeDtypeStruct((m, D_MODEL), F32)] + out_shape
        out_specs = [row, row]
    if emit_w:
        out_shape += [jax.ShapeDtypeStruct(w.shape, BF16) for w in (wg, wu, wd)]
        out_specs += [w_in, w_in, w_dn]
    return pl.pallas_call(
        functools.partial(_ffn_kernel, emit_x=emit_x, n_dtype=n_dtype, tf=tf, emit_w=emit_w),
        grid=grid,
        in_specs=[row, vec, w_in, w_in, w_dn, vec],
        out_specs=out_specs,
        out_shape=out_shape,
        scratch_shapes=[pltpu.VMEM((tm, D_MODEL), BF16), pltpu.VMEM((tm, D_MODEL), F32)],
        compiler_params=_cparams(("arbitrary", "arbitrary")),
        name="ffn_cast" if emit_w else "ffn",
    )(x, g, wg, wu, wd, g2)


def _ff_sweep():
    steps = pl.cdiv(D_FF, FF_TILE)
    order = list(range(_FF_TAIL_STEP)) + [steps - 1] + list(range(_FF_TAIL_STEP, steps - 1))
    return [(b, min(FF_TILE, D_FF - b * FF_TILE)) for b in order]


def _ffn_stream_kernel(x_ref, g_ref, wg_hbm, wu_hbm, wd_hbm, g2_ref, *rest, emit_x, n_dtype):
    rest = list(rest)
    ox_ref = rest.pop(0) if emit_x else None
    on_ref = rest.pop(0)
    wg_buf, wu_buf, wd_buf, acc_scr, sem = rest
    i = pl.program_id(0)
    sweep = _ff_sweep()
    assert len(sweep) % 2 == 0

    def copies(pos):
        blk, width = sweep[pos]
        slot = pos % 2
        cols = pl.ds(blk * FF_TILE, width)
        return (pltpu.make_async_copy(wg_hbm.at[:, cols], wg_buf.at[slot, :, pl.ds(0, width)], sem.at[0, slot]),
                pltpu.make_async_copy(wu_hbm.at[:, cols], wu_buf.at[slot, :, pl.ds(0, width)], sem.at[1, slot]),
                pltpu.make_async_copy(wd_hbm.at[cols, :], wd_buf.at[slot, pl.ds(0, width), :], sem.at[2, slot]))

    def start(pos):
        for cp in copies(pos):
            cp.start()

    @pl.when(i == 0)
    def _():
        start(0)

    for pos, (_, width) in enumerate(sweep):
        slot = pos % 2
        for cp in copies(pos):
            cp.wait()
        if pos + 1 < len(sweep):
            start(pos + 1)
        else:
            @pl.when(i + 1 < pl.num_programs(0))
            def _():
                start(0)
        if pos == 0:
            xn = _rms(x_ref[...], g_ref[...]).astype(BF16)
        part = _swiglu_down(xn, wg_buf[slot, :, 0:width], wu_buf[slot, :, 0:width], wd_buf[slot, 0:width, :])
        if pos == 0:
            acc_scr[...] = part
        else:
            acc_scr[...] += part

    y = x_ref[...] + FFN_RES * acc_scr[...]
    if emit_x:
        ox_ref[...] = y
    on_ref[...] = _rms(y, g2_ref[...]).astype(n_dtype)


def _ffn_stream(x, g, wg, wu, wd, g2, *, tm, emit_x, n_dtype):
    m = x.shape[0]
    row = pl.BlockSpec((tm, D_MODEL), lambda i: (i, 0))
    vec = pl.BlockSpec((1, D_MODEL), lambda i: (0, 0))
    hbm = pl.BlockSpec(memory_space=pl.ANY)
    out_shape = [jax.ShapeDtypeStruct((m, D_MODEL), n_dtype)]
    out_specs = [row]
    if emit_x:
        out_shape = [jax.ShapeDtypeStruct((m, D_MODEL), F32)] + out_shape
        out_specs = [row, row]
    return pl.pallas_call(
        functools.partial(_ffn_stream_kernel, emit_x=emit_x, n_dtype=n_dtype),
        grid=(m // tm,),
        in_specs=[row, vec, hbm, hbm, hbm, vec],
        out_specs=out_specs,
        out_shape=out_shape,
        scratch_shapes=[pltpu.VMEM((2, D_MODEL, FF_TILE), BF16), pltpu.VMEM((2, D_MODEL, FF_TILE), BF16),
                        pltpu.VMEM((2, FF_TILE, D_MODEL), BF16), pltpu.VMEM((tm, D_MODEL), F32),
                        pltpu.SemaphoreType.DMA((3, 2))],
        compiler_params=_cparams(("arbitrary",)),
        name="ffn",
    )(x, g, wg, wu, wd, g2)


def _proj_kernel(h_ref, w_ref, q_ref, kv_ref, rw_ref):
    h = h_ref[...]
    q = jnp.dot(h, w_ref[:, 0:ATT_WIDTH], preferred_element_type=F32)
    q_ref[...] = (q * (HEAD_DIM ** -0.5)).astype(BF16)
    kv_ref[...] = jnp.dot(h, w_ref[:, ATT_WIDTH:ATT_COLS], preferred_element_type=F32)
    rw_ref[...] = jnp.dot(h, w_ref[:, ATT_COLS:IN_COLS], preferred_element_type=F32)


def _proj(h, w, *, tm):
    m = h.shape[0]
    return pl.pallas_call(
        _proj_kernel,
        grid=(m // tm,),
        in_specs=[pl.BlockSpec((tm, D_MODEL), lambda i: (i, 0)),
                  pl.BlockSpec((D_MODEL, IN_COLS), lambda i: (0, 0), pipeline_mode=pl.Buffered(1))],
        out_specs=[pl.BlockSpec((tm, ATT_WIDTH), lambda i: (i, 0)),
                   pl.BlockSpec((tm, 2 * KV_WIDTH), lambda i: (i, 0)),
                   pl.BlockSpec((tm, SHIFT_COLS), lambda i: (i, 0))],
        out_shape=[jax.ShapeDtypeStruct((m, ATT_WIDTH), BF16),
                   jax.ShapeDtypeStruct((m, 2 * KV_WIDTH), F32),
                   jax.ShapeDtypeStruct((m, SHIFT_COLS), F32)],
        compiler_params=_cparams(("arbitrary",)),
        name="in_proj",
    )(h, w)


def _outproj_kernel(x_ref, oa_ref, orw_ref, w_ref, o_ref, *wo_ref):
    w_att = w_ref[0:ATT_WIDTH, :].astype(BF16)
    w_rw = w_ref[ATT_WIDTH:D_MODEL, :].astype(BF16)
    if wo_ref:
        wo_ref[0][0:ATT_WIDTH, :] = w_att
        wo_ref[0][ATT_WIDTH:D_MODEL, :] = w_rw
    acc = jnp.dot(oa_ref[...], w_att, preferred_element_type=F32)
    acc += jnp.dot(orw_ref[...], w_rw, preferred_element_type=F32)
    o_ref[...] = x_ref[...] + acc


def _outproj(x, oa, orw, w, *, tm, emit_w=False):
    m = x.shape[0]
    assert not emit_w or m == tm
    row = pl.BlockSpec((tm, D_MODEL), lambda i: (i, 0))
    half = pl.BlockSpec((tm, ATT_WIDTH), lambda i: (i, 0))
    full = pl.BlockSpec((D_MODEL, D_MODEL), lambda i: (0, 0), pipeline_mode=pl.Buffered(1))
    out_shape = [jax.ShapeDtypeStruct((m, D_MODEL), F32)]
    out_specs = [row]
    if emit_w:
        out_shape.append(jax.ShapeDtypeStruct((D_MODEL, D_MODEL), BF16))
        out_specs.append(pl.BlockSpec((D_MODEL, D_MODEL), lambda i: (0, 0)))
    return pl.pallas_call(
        _outproj_kernel,
        grid=(m // tm,),
        in_specs=[row, half, half, full],
        out_specs=out_specs,
        out_shape=out_shape,
        compiler_params=_cparams(("arbitrary",)),
        name="out_proj_cast" if emit_w else "out_proj",
    )(x, oa, orw, w)


def _bucket_table():
    qi = np.arange(BLOCK)[:, None]
    kj = np.arange(2 * BLOCK)[None, :]
    dist = BLOCK + qi - kj
    n = np.maximum(dist, 0)
    max_exact = N_BUCKETS // 2
    nf = np.maximum(n, 1).astype(np.float32)
    large = max_exact + (np.log(nf / np.float32(max_exact)) / np.float32(math.log(MAX_DISTANCE / max_exact))
                         * np.float32(N_BUCKETS - max_exact)).astype(np.int32)
    large = np.minimum(large, N_BUCKETS - 1)
    bucket = np.where(n < max_exact, n, large).astype(np.int32)
    valid = (dist >= 0) & (dist <= WINDOW)
    return np.where(valid, bucket, -1).astype(np.int32)


def _bias_kernel(bucket_ref, rb_ref, o_ref):
    bucket = bucket_ref[...]
    kj = lax.broadcasted_iota(jnp.int32, (BLOCK, 2 * BLOCK), 1)
    for h in range(N_Q_HEADS):
        acc = jnp.full((BLOCK, 2 * BLOCK), NEG_BIG, F32)
        for n in range(N_BUCKETS):
            acc = jnp.where(bucket == n, rb_ref[n, h], acc)
        o_ref[0, h] = jnp.where(kj >= BLOCK, acc, NEG_BIG)
        o_ref[1, h] = acc


def _bias_table(rel_bias):
    bucket = jnp.asarray(_bucket_table())
    return pl.pallas_call(
        _bias_kernel,
        in_specs=[pl.BlockSpec(memory_space=pltpu.VMEM), pl.BlockSpec(memory_space=pltpu.SMEM)],
        out_specs=pl.BlockSpec(memory_space=pltpu.VMEM),
        out_shape=jax.ShapeDtypeStruct((2, N_Q_HEADS, BLOCK, 2 * BLOCK), F32),
        name="bias_table",
    )(bucket, rel_bias)


def _attn_kernel(sink_ref, q_ref, kvp_ref, kvc_ref, bias_ref, o_ref, kv_scr):
    j = pl.program_id(1)
    kv_scr[0:BLOCK, :] = kvp_ref[...].astype(BF16)
    kv_scr[BLOCK:, :] = kvc_ref[...].astype(BF16)

    def block(i, carry):
        r0 = pl.multiple_of(i * BLOCK, BLOCK)
        rows = pl.ds(r0, BLOCK)
        table = jnp.where(jnp.logical_and(j == 0, i == 0), 0, 1)
        _attn_block(sink_ref, q_ref.at[rows], kv_scr.at[pl.ds(r0, 2 * BLOCK)], bias_ref.at[table], o_ref.at[rows])
        return carry

    lax.fori_loop(0, ATT_BLOCKS_PER_STEP, block, 0)


def _attn_block(sink_ref, q_ref, kv_ref, bias_ref, o_ref):
    half = N_KV_HEADS // 2
    for g0 in range(0, N_KV_HEADS, half):
        kcat, vcat = {}, {}
        for g in range(g0, g0 + half):
            kcat[g] = kv_ref[:, g * HEAD_DIM:(g + 1) * HEAD_DIM]
            vcat[g] = kv_ref[:, KV_WIDTH + g * HEAD_DIM:KV_WIDTH + (g + 1) * HEAD_DIM]
        heads = range(g0 * GQA_GROUP, (g0 + half) * GQA_GROUP)
        hs = {h: slice(h * HEAD_DIM, (h + 1) * HEAD_DIM) for h in heads}
        s = {h: _dg(q_ref[:, hs[h]], kcat[h // GQA_GROUP], _NT) + bias_ref[h] for h in heads}
        m = {h: jnp.maximum(jnp.max(s[h], axis=-1, keepdims=True), sink_ref[h]) for h in heads}
        p = {h: jnp.exp(s[h] - m[h]) for h in heads}
        denom = {h: jnp.sum(p[h], axis=-1, keepdims=True) + jnp.exp(sink_ref[h] - m[h]) for h in heads}
        o = {h: jnp.dot(p[h].astype(BF16), vcat[h // GQA_GROUP], preferred_element_type=F32) for h in heads}
        for h in heads:
            o_ref[:, hs[h]] = (o[h] / denom[h]).astype(BF16)


def _attn_prompt(q, kv, bias, sinks, *, batch, seq):
    n = ATT_BLOCKS_PER_STEP
    rows = n * BLOCK
    q3 = q.reshape(batch, seq, ATT_WIDTH)
    kv3 = kv.reshape(batch, seq, 2 * KV_WIDTH)
    out = pl.pallas_call(
        _attn_kernel,
        grid=(batch, seq // rows),
        in_specs=[pl.BlockSpec(memory_space=pltpu.SMEM),
                  pl.BlockSpec((None, rows, ATT_WIDTH), lambda b, j: (b, j, 0)),
                  pl.BlockSpec((None, BLOCK, 2 * KV_WIDTH), lambda b, j: (b, jnp.maximum(n * j - 1, 0), 0)),
                  pl.BlockSpec((None, rows, 2 * KV_WIDTH), lambda b, j: (b, j, 0)),
                  pl.BlockSpec((2, N_Q_HEADS, BLOCK, 2 * BLOCK), lambda b, j: (0, 0, 0, 0))],
        out_specs=pl.BlockSpec((None, rows, ATT_WIDTH), lambda b, j: (b, j, 0)),
        out_shape=jax.ShapeDtypeStruct((batch, seq, ATT_WIDTH), BF16),
        scratch_shapes=[pltpu.VMEM((rows + BLOCK, 2 * KV_WIDTH), BF16)],
        compiler_params=_cparams(("arbitrary", "arbitrary")),
        name="attn_prompt",
    )(sinks, q3, kv3, kv3, bias)
    return out.reshape(batch * seq, ATT_WIDTH)


def _attn_decode_kernel(sink_ref, q_ref, kvn_ref, ck_ref, cv_ref, bias_ref, o_ref):
    q = q_ref[...].astype(F32)
    kvn = kvn_ref[...]
    for g in range(N_KV_HEADS):
        ks = slice(g * HEAD_DIM, (g + 1) * HEAD_DIM)
        vs = slice(KV_WIDTH + g * HEAD_DIM, KV_WIDTH + (g + 1) * HEAD_DIM)
        hs = slice(g * GQA_GROUP, (g + 1) * GQA_GROUP)
        qg = q[:, hs, :]
        kc = ck_ref[:, :, ks]
        vc = cv_ref[:, :, ks]
        kn = kvn[:, ks].astype(BF16).astype(F32)[:, None, :]
        vn = kvn[:, vs].astype(BF16).astype(F32)[:, None, :]
        s = jnp.einsum("bqd,bkd->bqk", qg.astype(BF16), kc.astype(BF16), preferred_element_type=F32)
        s = s + bias_ref[hs, 0:WINDOW][None]
        s_new = jnp.sum(qg * kn, axis=-1, keepdims=True) + bias_ref[hs, WINDOW:WINDOW + 1][None]
        sink = sink_ref[hs, 0:1][None]
        m = jnp.maximum(jnp.maximum(jnp.max(s, axis=-1, keepdims=True), s_new), sink)
        p = jnp.exp(s - m)
        p_new = jnp.exp(s_new - m)
        denom = jnp.sum(p, axis=-1, keepdims=True) + p_new + jnp.exp(sink - m)
        o = jnp.einsum("bqk,bkd->bqd", p.astype(BF16), vc.astype(BF16), preferred_element_type=F32)
        o = o + p_new * vn
        o_ref[:, hs, :] = o / denom


def _attn_decode(q, kvn, cache_k, cache_v, bias_row, sinks_b):
    b = q.shape[0]
    lw = cache_k.shape[1]
    vm = pl.BlockSpec(memory_space=pltpu.VMEM)
    out = pl.pallas_call(
        _attn_decode_kernel,
        in_specs=[vm, vm, vm, vm, vm, vm],
        out_specs=vm,
        out_shape=jax.ShapeDtypeStruct((b, N_Q_HEADS, HEAD_DIM), F32),
        compiler_params=pltpu.CompilerParams(vmem_limit_bytes=VMEM_LIMIT_BYTES),
        name="attn_decode",
    )(sinks_b, q.reshape(b, N_Q_HEADS, HEAD_DIM), kvn,
      cache_k.reshape(b, lw, KV_WIDTH), cache_v.reshape(b, lw, KV_WIDTH), bias_row)
    return out.reshape(b, ATT_WIDTH)


def _seg_expand(x, bd_ref):
    rows = x.shape[0]
    groups = x.shape[1] // LANES
    xs = jnp.concatenate([x[:, i * LANES:(i + 1) * LANES] for i in range(groups)], axis=0)
    s = _mm(xs, bd_ref[...])
    return jnp.concatenate([s[i * rows:(i + 1) * rows] for i in range(groups)], axis=1)


def _rwkv_front(x, prev, p):
    mu_ref, w0_ref, a0_ref, kk_ref, ka_ref, lw_ref, g2_ref, bd_ref = p
    xm = x + mu_ref[...] * (prev - x)
    o3 = 3 * RWKV_WIDTH
    r = xm[:, 0:RWKV_WIDTH]
    k = xm[:, RWKV_WIDTH:2 * RWKV_WIDTH]
    v = xm[:, 2 * RWKV_WIDTH:o3]
    wa = xm[:, o3:o3 + LANES]
    lane = lax.broadcasted_iota(jnp.int32, wa.shape, 1)
    z = jnp.where(lane < DECAY_LORA, jnp.tanh(wa), wa)
    lora = _mm(z, lw_ref[...])
    logw = -DECAY_SCALE * _sigmoid(w0_ref[...] + lora[:, 0:RWKV_WIDTH])
    a = _sigmoid(a0_ref[...] + lora[:, RWKV_WIDTH:2 * RWKV_WIDTH])
    g = _mm(_sigmoid(xm[:, o3 + LANES:SHIFT_COLS]), g2_ref[...])
    kk = k * kk_ref[...]
    kmod = k * (1.0 + (a - 1.0) * ka_ref[...])
    nrm = jnp.sqrt(_seg_expand(kk * kk, bd_ref))
    kk = kk / jnp.maximum(nrm, 1e-12)
    return r, logw, kmod, v, kk, kk * a, g


def _rwkv_back(y, r, kmod, v, g, q):
    rk_ref, lnw_ref, lnb_ref, bd_ref = q
    mean = _seg_expand(y, bd_ref) * (1.0 / HEAD_DIM)
    d = y - mean
    var = _seg_expand(d * d, bd_ref) * (1.0 / HEAD_DIM)
    yn = d * lax.rsqrt(var + GN_EPS) * lnw_ref[...] + lnb_ref[...]
    bonus = _seg_expand(r * kmod * rk_ref[...], bd_ref) * v
    return (yn + bonus) * g


def _rwkv_chunk(rw_ref, mu_ref, w0_ref, a0_ref, kk_ref, ka_ref, lw_ref, g2_ref, bd_ref,
                rk_ref, lnw_ref, lnb_ref,
                o_ref,
                s_scr, prev_scr, y_scr, ea_scr, er_scr, eb_scr, ek_scr, be_scr, ke_scr, v_scr):
    C = CHUNK
    x = rw_ref[...]
    row = lax.broadcasted_iota(jnp.int32, x.shape, 0)
    prev = jnp.where(row == 0, prev_scr[...], pltpu.roll(x, 1, axis=0))
    prev_scr[...] = x[C - 1:C, :]
    front = (mu_ref, w0_ref, a0_ref, kk_ref, ka_ref, lw_ref, g2_ref, bd_ref)
    r, logw, kmod, v, kk, bb, g = _rwkv_front(x, prev, front)

    ti = lax.broadcasted_iota(jnp.int32, (C, C), 0)
    si = lax.broadcasted_iota(jnp.int32, (C, C), 1)
    cum = _mm2r(jnp.where(si <= ti, 1.0, 0.0).astype(BF16), logw)
    cum_end = cum[C - 1:C, :]
    e_in = jnp.exp(cum)
    e_neg = jnp.exp(-cum)
    e_end = jnp.exp(cum_end - cum)
    ea_scr[...] = kk * jnp.exp(cum - logw)
    er_scr[...] = r * e_in
    eb_scr[...] = bb * e_neg
    ek_scr[...] = kmod * e_neg
    be_scr[...] = bb * e_end
    ke_scr[...] = kmod * e_end
    v_scr[...] = v
    w_end = jnp.exp(cum_end)

    lane = lax.broadcasted_iota(jnp.int32, (C, LANES), 1)
    lo_half = lane < HEAD_DIM
    ri = lax.broadcasted_iota(jnp.int32, (LANES, LANES), 0)
    ci = lax.broadcasted_iota(jnp.int32, (LANES, LANES), 1)
    same_head = jnp.right_shift(ri, 6) == jnp.right_shift(ci, 6)
    eye = ri == ci
    t_loc = jnp.bitwise_and(ri, C - 1)
    s_loc = jnp.bitwise_and(ci, C - 1)
    bd_strict = jnp.logical_and(same_head, s_loc < t_loc)
    bd_incl = jnp.logical_and(same_head, s_loc <= t_loc)

    def swap(t):
        return pltpu.roll(t, HEAD_DIM, axis=1)

    def stack2(t):
        return jnp.concatenate([jnp.where(lo_half, t, 0.0), jnp.where(lo_half, 0.0, t)], axis=0)

    def stack2x(t):
        ts = swap(t)
        return jnp.concatenate([jnp.where(lo_half, 0.0, ts), jnp.where(lo_half, ts, 0.0)], axis=0)

    def own(t):
        return jnp.where(lo_half, t[0:C], t[C:2 * C])

    def other(t):
        return swap(jnp.where(lo_half, t[C:2 * C], t[0:C]))

    pairs = range(HEAD_PAIRS)
    sl = [slice(p * LANES, (p + 1) * LANES) for p in pairs]
    a2 = [stack2(ea_scr[:, sl[p]]) for p in pairs]
    m = []
    for p in pairs:
        r2 = stack2(er_scr[:, sl[p]])
        b_p = eb_scr[:, sl[p]]
        k_p = ek_scr[:, sl[p]]
        m.append(_mm(jnp.concatenate([a2[p], r2], axis=0), jnp.concatenate([b_p, b_p, k_p, k_p], axis=0), _NT))
    l_ak = [jnp.where(bd_strict, m[p][0:2 * C, LANES:2 * LANES], 0.0) for p in pairs]
    pw = [-jnp.where(bd_strict, m[p][0:2 * C, 0:LANES], 0.0) for p in pairs]
    lrbk = [jnp.concatenate([jnp.where(bd_incl, m[p][2 * C:4 * C, 0:LANES], 0.0),
                             jnp.where(bd_incl, m[p][2 * C:4 * C, LANES:2 * LANES], 0.0)], axis=1) for p in pairs]
    v2x = [stack2x(v_scr[:, sl[p]]) for p in pairs]
    xc = [a2[p] - _mm(l_ak[p], v2x[p]) for p in pairs]
    for it in range(6):
        for p in pairs:
            pw_b = pw[p].astype(BF16)
            xc_b = xc[p].astype(BF16)
            if it < 5:
                px = _dg(pw_b, jnp.concatenate([xc_b, pw_b], axis=1))
                pw[p] = px[:, LANES:2 * LANES]
                xc[p] = xc[p] + px[:, 0:LANES]
            else:
                xc[p] = xc[p] + _dg(pw_b, xc_b)
    xv = [jnp.concatenate([xc[p], v2x[p]], axis=0).astype(BF16) for p in pairs]
    lb = [_mm(lrbk[p], xv[p]) for p in pairs]
    z = [_mm(xv[p], jnp.concatenate([stack2(be_scr[:, sl[p]]), stack2(ke_scr[:, sl[p]])], axis=0), _TN)
         for p in pairs]
    rt = [er_scr[:, sl[p]] - own(lb[p]) for p in pairs]
    yl = [other(lb[p]) for p in pairs]
    d = [jnp.where(eye, w_end[:, sl[p]], 0.0) - jnp.where(same_head, z[p], 0.0) for p in pairs]
    gg = [jnp.where(same_head, jnp.concatenate([z[p][C:2 * C], z[p][0:C]], axis=0), 0.0) for p in pairs]
    s0 = [s_scr[p].astype(BF16) for p in pairs]
    ys = [_mm(rt[p], s0[p], _NT) for p in pairs]
    for p in pairs:
        y_scr[:, sl[p]] = yl[p] + ys[p]
    sn = [_mm(s0[p], d[p]) for p in pairs]
    for p in pairs:
        s_scr[p] = sn[p] + gg[p]

    back = (rk_ref, lnw_ref, lnb_ref, bd_ref)
    o_ref[...] = _rwkv_back(y_scr[...], r, kmod, v, g, back).astype(BF16)


def _rwkv_chunk_kernel(rw_ref, *rest):
    params, (o_ref, s_out_ref, s_scr, prev_scr), work = rest[:11], rest[11:15], rest[15:]
    c = pl.program_id(1)

    @pl.when(c == 0)
    def _():
        s_scr[...] = jnp.zeros_like(s_scr)
        prev_scr[...] = jnp.zeros_like(prev_scr)

    for i in range(CHUNKS_PER_STEP):
        rows = pl.ds(i * CHUNK, CHUNK)
        _rwkv_chunk(rw_ref.at[rows], *params, o_ref.at[rows], s_scr, prev_scr, *[w.at[i] for w in work])

    @pl.when(c == pl.num_programs(1) - 1)
    def _():
        for p in range(HEAD_PAIRS):
            s_out_ref[2 * p] = s_scr[p, 0:HEAD_DIM, 0:HEAD_DIM]
            s_out_ref[2 * p + 1] = s_scr[p, HEAD_DIM:LANES, HEAD_DIM:LANES]


def _rwkv_prompt(rw, params, *, batch, seq):
    rw3 = rw.reshape(batch, seq, SHIFT_COLS)
    rows = CHUNKS_PER_STEP * CHUNK
    nc = seq // rows
    const = lambda shape: pl.BlockSpec(shape, lambda b, c: tuple(0 for _ in shape))
    in_specs = [pl.BlockSpec((None, rows, SHIFT_COLS), lambda b, c: (b, c, 0))]
    in_specs += [const(p.shape) for p in params]
    wide = pltpu.VMEM((CHUNKS_PER_STEP, CHUNK, RWKV_WIDTH), F32)
    o, s_out = pl.pallas_call(
        _rwkv_chunk_kernel,
        grid=(batch, nc),
        in_specs=in_specs,
        out_specs=[pl.BlockSpec((None, rows, RWKV_WIDTH), lambda b, c: (b, c, 0)),
                   pl.BlockSpec((None, RWKV_HEADS, HEAD_DIM, HEAD_DIM), lambda b, c: (b, 0, 0, 0))],
        out_shape=[jax.ShapeDtypeStruct((batch, seq, RWKV_WIDTH), BF16),
                   jax.ShapeDtypeStruct((batch, RWKV_HEADS, HEAD_DIM, HEAD_DIM), F32)],
        scratch_shapes=[pltpu.VMEM((HEAD_PAIRS, LANES, LANES), F32),
                        pltpu.VMEM((1, SHIFT_COLS), F32),
                        wide, wide, wide, wide, wide, wide, wide, wide],
        compiler_params=_cparams(("arbitrary", "arbitrary")),
        name="rwkv_prompt",
    )(rw3, *params)
    return o.reshape(batch * seq, RWKV_WIDTH), s_out


def _rwkv_dec_front_kernel(rw_ref, prev_ref, mu_ref, w0_ref, a0_ref, kk_ref, ka_ref, lw_ref, g2_ref, bd_ref,
                           r_ref, w_ref, k_ref, v_ref, kkn_ref, b_ref, g_ref):
    front = (mu_ref, w0_ref, a0_ref, kk_ref, ka_ref, lw_ref, g2_ref, bd_ref)
    r, logw, kmod, v, kk, bb, g = _rwkv_front(rw_ref[...], prev_ref[...], front)
    r_ref[...] = r
    w_ref[...] = jnp.exp(logw)
    k_ref[...] = kmod
    v_ref[...] = v
    kkn_ref[...] = kk
    b_ref[...] = bb
    g_ref[...] = g


def _rwkv_dec_state_kernel(r_all, w_all, k_all, v_all, kk_all, b_all, s_ref, y_ref, so_ref):
    row = pl.ds(pl.program_id(0), 1)
    r_ref, w_ref, k_ref, v_ref, kk_ref, b_ref = (t.at[row] for t in (r_all, w_all, k_all, v_all, kk_all, b_all))
    ri = lax.broadcasted_iota(jnp.int32, (HEAD_DIM, HEAD_DIM), 0)
    ci = lax.broadcasted_iota(jnp.int32, (HEAD_DIM, HEAD_DIM), 1)
    eye = ri == ci
    ones = jnp.ones((HEAD_DIM, LANES), BF16)
    heads = range(RWKV_HEADS)
    hs = [slice(h * HEAD_DIM, (h + 1) * HEAD_DIM) for h in heads]
    n = RWKV_HEADS * HEAD_DIM
    s = [s_ref[h] for h in heads]
    lhs = [s[h] * kk_ref[:, hs[h]] for h in heads] + [jnp.where(eye, v_ref[:, hs[h]], 0.0) for h in heads]
    red = _mm(jnp.concatenate(lhs, axis=0), ones)
    s_new = []
    for h in heads:
        sa = -red[h * HEAD_DIM:(h + 1) * HEAD_DIM, 0:HEAD_DIM]
        v_col = red[n + h * HEAD_DIM:n + (h + 1) * HEAD_DIM, 0:HEAD_DIM]
        s_new.append(s[h] * w_ref[:, hs[h]] + sa * b_ref[:, hs[h]] + v_col * k_ref[:, hs[h]])
        so_ref[h] = s_new[h]
    yb = _mm(jnp.concatenate([s_new[h] * r_ref[:, hs[h]] for h in heads], axis=0), ones)
    for h in heads:
        y_blk = yb[h * HEAD_DIM:(h + 1) * HEAD_DIM, 0:HEAD_DIM]
        y_ref[:, hs[h]] = jnp.sum(jnp.where(eye, y_blk, 0.0), axis=0, keepdims=True)


def _rwkv_dec_back_kernel(y_ref, r_ref, k_ref, v_ref, g_ref, rk_ref, lnw_ref, lnb_ref, bd_ref, o_ref):
    back = (rk_ref, lnw_ref, lnb_ref, bd_ref)
    o_ref[...] = _rwkv_back(y_ref[...], r_ref[...], k_ref[...], v_ref[...], g_ref[...], back).astype(BF16)


def _rwkv_decode(rw, shift0, wkv0, front_params, back_params):
    b = rw.shape[0]
    vm = pl.BlockSpec(memory_space=pltpu.VMEM)
    wide = jax.ShapeDtypeStruct((b, RWKV_WIDTH), F32)
    r, w, k, v, kk, bb, g = pl.pallas_call(
        _rwkv_dec_front_kernel,
        in_specs=[vm] * (2 + len(front_params)),
        out_specs=[vm] * 7,
        out_shape=[wide] * 7,
        name="rwkv_dec_front",
    )(rw, shift0, *front_params)
    rowspec = pl.BlockSpec((None, 1, RWKV_WIDTH), lambda i: (i, 0, 0))
    stspec = pl.BlockSpec((None, RWKV_HEADS, HEAD_DIM, HEAD_DIM), lambda i: (i, 0, 0, 0))
    allrows = pl.BlockSpec((b, RWKV_WIDTH), lambda i: (0, 0))
    y, s_new = pl.pallas_call(
        _rwkv_dec_state_kernel,
        grid=(b,),
        in_specs=[allrows] * 6 + [stspec],
        out_specs=[rowspec, stspec],
        out_shape=[jax.ShapeDtypeStruct((b, 1, RWKV_WIDTH), F32),
                   jax.ShapeDtypeStruct((b, RWKV_HEADS, HEAD_DIM, HEAD_DIM), F32)],
        compiler_params=_cparams(("arbitrary",)),
        name="rwkv_dec_state",
    )(r, w, k, v, kk, bb, wkv0)
    o = pl.pallas_call(
        _rwkv_dec_back_kernel,
        in_specs=[vm] * (5 + len(back_params)),
        out_specs=vm,
        out_shape=jax.ShapeDtypeStruct((b, RWKV_WIDTH), BF16),
        name="rwkv_dec_back",
    )(y.reshape(b, RWKV_WIDTH), r, k, v, g, *back_params)
    return o, s_new


def kernel(x_prompt, x_sample, cache_k, cache_v, state_wkv, state_shift, rel_bias, ffn1_norm, ffn1_w_gate, ffn1_w_up, ffn1_w_down, mix_norm, w_in, attn_sinks, shift_mu, decay_w0, decay_w2, aaa_a0, aaa_a2, gate_g2, key_k, key_a, bonus_r_k, ln_x_w, ln_x_b, w_out, ffn2_norm, ffn2_w_gate, ffn2_w_up, ffn2_w_down, final_norm):
    batch, seq, _ = x_prompt.shape
    dec_b = x_sample.shape[0]
    lw = cache_k.shape[2]
    l = 0
    row = lambda t: t.reshape(1, -1)

    w_in_b = w_in[l].astype(BF16)
    lora_w = jnp.zeros((LANES, 2 * RWKV_WIDTH), F32)
    lora_w = lora_w.at[0:DECAY_LORA, 0:RWKV_WIDTH].set(decay_w2[l])
    lora_w = lora_w.at[DECAY_LORA:LANES, RWKV_WIDTH:].set(aaa_a2[l])
    head_of = np.arange(LANES) // HEAD_DIM
    bd_ones = jnp.asarray(head_of[:, None] == head_of[None, :], BF16)
    front_params = (row(shift_mu[l]), row(decay_w0[l]), row(aaa_a0[l]), row(key_k[l]), row(key_a[l]),
                    lora_w.astype(BF16), gate_g2[l].astype(BF16), bd_ones)
    back_params = (row(bonus_r_k[l]), row(ln_x_w[l]), row(ln_x_b[l]), bd_ones)

    bias = _bias_table(rel_bias)
    sinks = attn_sinks[l]

    xs = x_sample.reshape(dec_b, D_MODEL)
    x1s, h1s, *f1 = _ffn(xs, row(ffn1_norm[l]), ffn1_w_gate[l], ffn1_w_up[l], ffn1_w_down[l], row(mix_norm[l]),
                         tm=dec_b, tf=FF_TILE_CAST, emit_x=True, n_dtype=BF16, emit_w=True)
    qs, kvs, rws = _proj(h1s, w_in_b, tm=dec_b)
    sinks_b = jnp.broadcast_to(sinks[:, None], (N_Q_HEADS, LANES))
    o_att_s = _attn_decode(qs, kvs, cache_k[l], cache_v[l], bias[1, :, 0, :], sinks_b)
    o_rw_s, s_s = _rwkv_decode(rws, state_shift[l], state_wkv[l], front_params, back_params)
    x2s, w_out_b = _outproj(x1s, o_att_s.astype(BF16), o_rw_s, w_out[l], tm=dec_b, emit_w=True)
    y_s, *f2 = _ffn(x2s, row(ffn2_norm[l]), ffn2_w_gate[l], ffn2_w_up[l], ffn2_w_down[l], row(final_norm),
                    tm=dec_b, tf=FF_TILE_CAST, emit_x=False, n_dtype=F32, emit_w=True)

    xp = x_prompt.reshape(batch * seq, D_MODEL)
    x1, h1 = _ffn_stream(xp, row(ffn1_norm[l]), *f1, row(mix_norm[l]), tm=512, emit_x=True, n_dtype=BF16)
    q, kv, rw = _proj(h1, w_in_b, tm=256)
    o_att = _attn_prompt(q, kv, bias, sinks, batch=batch, seq=seq)
    o_rw, s_p = _rwkv_prompt(rw, front_params + back_params[:3], batch=batch, seq=seq)
    (x2,) = _outproj(x1, o_att, o_rw, w_out_b, tm=512)
    (y_p,) = _ffn_stream(x2, row(ffn2_norm[l]), *f2, row(final_norm), tm=512, emit_x=False, n_dtype=F32)

    kv3 = kv.reshape(batch, seq, 2 * KV_WIDTH)
    lp = min(WINDOW, seq)
    new_k_p = kv3[:, seq - lp:, 0:KV_WIDTH].reshape(1, batch, lp, N_KV_HEADS, HEAD_DIM)
    new_v_p = kv3[:, seq - lp:, KV_WIDTH:].reshape(1, batch, lp, N_KV_HEADS, HEAD_DIM)
    new_shift_p = rw.reshape(batch, seq, SHIFT_COLS)[:, seq - 1][None]
    k_new = kvs[:, 0:KV_WIDTH].reshape(dec_b, 1, N_KV_HEADS, HEAD_DIM)
    v_new = kvs[:, KV_WIDTH:].reshape(dec_b, 1, N_KV_HEADS, HEAD_DIM)
    new_k_s = jnp.concatenate([cache_k[l], k_new], axis=1)[:, -lw:][None]
    new_v_s = jnp.concatenate([cache_v[l], v_new], axis=1)[:, -lw:][None]
    return (y_p.reshape(batch, seq, D_MODEL), y_s.reshape(dec_b, 1, D_MODEL),
            new_k_p, new_v_p, s_p[None], new_shift_p,
            new_k_s, new_v_s, s_s[None], rws[None])
```

```python
import functools
import math

import numpy as np
import jax
import jax.numpy as jnp
from jax import lax
from jax.experimental import pallas as pl
from jax.experimental.pallas import tpu as pltpu

F32 = jnp.float32
BF16 = jnp.bfloat16

D_MODEL = 2048
HEAD_DIM = 64
ATT_WIDTH = 1024
N_Q_HEADS = 16
N_KV_HEADS = 4
GQA_GROUP = 4
KV_WIDTH = 256
RWKV_WIDTH = 1024
RWKV_HEADS = 16
WINDOW = 128
BLOCK = 128
N_BUCKETS = 32
MAX_DISTANCE = 128
DECAY_LORA = 64
AAA_LORA = 64
GATE_LORA = 128
D_FF = 5504
ATT_COLS = ATT_WIDTH + 2 * KV_WIDTH
SHIFT_COLS = 3 * RWKV_WIDTH + DECAY_LORA + AAA_LORA + GATE_LORA
IN_COLS = ATT_COLS + SHIFT_COLS
RMS_EPS = 1e-5
GN_EPS = 64e-5
FFN_RES = 0.5

LANES = 128
VMEM_LIMIT_BYTES = 60 * 1024 * 1024

FF_TILE = 1024
FF_TILE_CAST = 512
CHUNK = 64
CHUNKS_PER_STEP = 4
ATT_BLOCKS_PER_STEP = 8
HEAD_PAIRS = RWKV_HEADS // 2
CHUNK_GROUP = 2
NEG_BIG = -1e30
DECAY_SCALE = math.exp(-0.5)

_NN = (((1,), (0,)), ((), ()))
_NT = (((1,), (1,)), ((), ()))
_TN = (((0,), (0,)), ((), ()))


def _dg(a, b, dims=_NN):
    return lax.dot_general(a, b, dims, preferred_element_type=F32)


def _mm(a, b, dims=_NN):
    return _dg(a.astype(BF16), b.astype(BF16), dims)


def _split2(x):
    hi = x.astype(BF16)
    lo = (x - hi.astype(F32)).astype(BF16)
    return hi, lo


def _mm2r(a_bf16, b, dims=_NN):
    bh, bl = _split2(b)
    return _dg(a_bf16, bh, dims) + _dg(a_bf16, bl, dims)


def _sigmoid(x):
    return 1.0 / (1.0 + jnp.exp(-x))


def _rms(x, g):
    ms = jnp.mean(x * x, axis=-1, keepdims=True)
    return x * lax.rsqrt(ms + RMS_EPS) * g


def _cparams(sem):
    return pltpu.CompilerParams(dimension_semantics=sem, vmem_limit_bytes=VMEM_LIMIT_BYTES)


def _swiglu_down(xn, wg, wu, wd):
    gate = jnp.dot(xn, wg, preferred_element_type=F32)
    up = jnp.dot(xn, wu, preferred_element_type=F32)
    h = (gate * _sigmoid(gate) * up).astype(BF16)
    return jnp.dot(h, wd, preferred_element_type=F32)


_FF_TAIL_STEP = 2


def _ff_block(j, *, steps):
    return jnp.where(j < _FF_TAIL_STEP, j, jnp.where(j == _FF_TAIL_STEP, steps - 1, j - 1))


def _ffn_kernel(x_ref, g_ref, wg_ref, wu_ref, wd_ref, g2_ref, *rest, emit_x, n_dtype, tf, emit_w):
    rest = list(rest)
    ox_ref = rest.pop(0) if emit_x else None
    on_ref = rest.pop(0)
    wgo_ref, wuo_ref, wdo_ref = (rest.pop(0), rest.pop(0), rest.pop(0)) if emit_w else (None, None, None)
    xn_scr, acc_scr = rest
    j = pl.program_id(1)

    @pl.when(j == 0)
    def _():
        xn_scr[...] = _rms(x_ref[...], g_ref[...]).astype(BF16)
        acc_scr[...] = jnp.zeros_like(acc_scr)

    steps = pl.cdiv(D_FF, tf)
    last = j == steps - 1
    tail = j == _FF_TAIL_STEP

    def accumulate(width):
        wg = wg_ref[:, 0:width].astype(BF16)
        wu = wu_ref[:, 0:width].astype(BF16)
        wd = wd_ref[0:width, :].astype(BF16)
        if emit_w:
            wgo_ref[:, 0:width] = wg
            wuo_ref[:, 0:width] = wu
            wdo_ref[0:width, :] = wd
        acc_scr[...] += _swiglu_down(xn_scr[...], wg, wu, wd)

    @pl.when(jnp.logical_not(tail))
    def _():
        accumulate(tf)

    @pl.when(tail)
    def _():
        accumulate(D_FF - (steps - 1) * tf)

    @pl.when(last)
    def _():
        y = x_ref[...] + FFN_RES * acc_scr[...]
        if emit_x:
            ox_ref[...] = y
        on_ref[...] = _rms(y, g2_ref[...]).astype(n_dtype)


def _ffn(x, g, wg, wu, wd, g2, *, tm, tf, emit_x, n_dtype, emit_w=False):
    m = x.shape[0]
    assert not emit_w or m == tm
    grid = (m // tm, pl.cdiv(D_FF, tf))
    row = pl.BlockSpec((tm, D_MODEL), lambda i, j: (i, 0))
    vec = pl.BlockSpec((1, D_MODEL), lambda i, j: (0, 0))
    steps = pl.cdiv(D_FF, tf)
    blk = functools.partial(_ff_block, steps=steps)
    w_in = pl.BlockSpec((D_MODEL, tf), lambda i, j: (0, blk(j)))
    w_dn = pl.BlockSpec((tf, D_MODEL), lambda i, j: (blk(j), 0))
    out_shape = [jax.ShapeDtypeStruct((m, D_MODEL), n_dtype)]
    out_specs = [row]
    if emit_x:
        out_shape = [jax.ShapeDtypeStruct((m, D_MODEL), F32)] + out_shape
        out_specs = [row, row]
    if emit_w:
        out_shape += [jax.ShapeDtypeStruct(w.shape, BF16) for w in (wg, wu, wd)]
        out_specs += [w_in, w_in, w_dn]
    return pl.pallas_call(
        functools.partial(_ffn_kernel, emit_x=emit_x, n_dtype=n_dtype, tf=tf, emit_w=emit_w),
        grid=grid,
        in_specs=[row, vec, w_in, w_in, w_dn, vec],
        out_specs=out_specs,
        out_shape=out_shape,
        scratch_shapes=[pltpu.VMEM((tm, D_MODEL), BF16), pltpu.VMEM((tm, D_MODEL), F32)],
        compiler_params=_cparams(("arbitrary", "arbitrary")),
        name="ffn_cast" if emit_w else "ffn",
    )(x, g, wg, wu, wd, g2)


def _ff_sweep():
    steps = pl.cdiv(D_FF, FF_TILE)
    order = list(range(_FF_TAIL_STEP)) + [steps - 1] + list(range(_FF_TAIL_STEP, steps - 1))
    return [(b, min(FF_TILE, D_FF - b * FF_TILE)) for b in order]


def _ffn_stream_kernel(x_ref, g_ref, wg_hbm, wu_hbm, wd_hbm, g2_ref, *rest, emit_x, n_dtype):
    rest = list(rest)
    ox_ref = rest.pop(0) if emit_x else None
    on_ref = rest.pop(0)
    wg_buf, wu_buf, wd_buf, acc_scr, sem = rest
    i = pl.program_id(0)
    sweep = _ff_sweep()
    assert len(sweep) % 2 == 0

    def copies(pos):
        blk, width = sweep[pos]
        slot = pos % 2
        cols = pl.ds(blk * FF_TILE, width)
        return (pltpu.make_async_copy(wg_hbm.at[:, cols], wg_buf.at[slot, :, pl.ds(0, width)], sem.at[0, slot]),
                pltpu.make_async_copy(wu_hbm.at[:, cols], wu_buf.at[slot, :, pl.ds(0, width)], sem.at[1, slot]),
                pltpu.make_async_copy(wd_hbm.at[cols, :], wd_buf.at[slot, pl.ds(0, width), :], sem.at[2, slot]))

    def start(pos):
        for cp in copies(pos):
            cp.start()

    @pl.when(i == 0)
    def _():
        start(0)

    for pos, (_, width) in enumerate(sweep):
        slot = pos % 2
        for cp in copies(pos):
            cp.wait()
        if pos + 1 < len(sweep):
            start(pos + 1)
        else:
            @pl.when(i + 1 < pl.num_programs(0))
            def _():
                start(0)
        if pos == 0:
            xn = _rms(x_ref[...], g_ref[...]).astype(BF16)
        part = _swiglu_down(xn, wg_buf[slot, :, 0:width], wu_buf[slot, :, 0:width], wd_buf[slot, 0:width, :])
        if pos == 0:
            acc_scr[...] = part
        else:
            acc_scr[...] += part

    y = x_ref[...] + FFN_RES * acc_scr[...]
    if emit_x:
        ox_ref[...] = y
    on_ref[...] = _rms(y, g2_ref[...]).astype(n_dtype)


def _ffn_stream(x, g, wg, wu, wd, g2, *, tm, emit_x, n_dtype):
    m = x.shape[0]
    row = pl.BlockSpec((tm, D_MODEL), lambda i: (i, 0))
    vec = pl.BlockSpec((1, D_MODEL), lambda i: (0, 0))
    hbm = pl.BlockSpec(memory_space=pl.ANY)
    out_shape = [jax.ShapeDtypeStruct((m, D_MODEL), n_dtype)]
    out_specs = [row]
    if emit_x:
        out_shape = [jax.ShapeDtypeStruct((m, D_MODEL), F32)] + out_shape
        out_specs = [row, row]
    return pl.pallas_call(
        functools.partial(_ffn_stream_kernel, emit_x=emit_x, n_dtype=n_dtype),
        grid=(m // tm,),
        in_specs=[row, vec, hbm, hbm, hbm, vec],
        out_specs=out_specs,
        out_shape=out_shape,
        scratch_shapes=[pltpu.VMEM((2, D_MODEL, FF_TILE), BF16), pltpu.VMEM((2, D_MODEL, FF_TILE), BF16),
                        pltpu.VMEM((2, FF_TILE, D_MODEL), BF16), pltpu.VMEM((tm, D_MODEL), F32),
                        pltpu.SemaphoreType.DMA((3, 2))],
        compiler_params=_cparams(("arbitrary",)),
        name="ffn",
    )(x, g, wg, wu, wd, g2)


def _proj_kernel(h_ref, w_ref, q_ref, kv_ref, rw_ref):
    h = h_ref[...]
    q = jnp.dot(h, w_ref[:, 0:ATT_WIDTH], preferred_element_type=F32)
    q_ref[...] = (q * (HEAD_DIM ** -0.5)).astype(BF16)
    kv_ref[...] = jnp.dot(h, w_ref[:, ATT_WIDTH:ATT_COLS], preferred_element_type=F32)
    rw_ref[...] = jnp.dot(h, w_ref[:, ATT_COLS:IN_COLS], preferred_element_type=F32)


def _proj(h, w, *, tm):
    m = h.shape[0]
    return pl.pallas_call(
        _proj_kernel,
        grid=(m // tm,),
        in_specs=[pl.BlockSpec((tm, D_MODEL), lambda i: (i, 0)),
                  pl.BlockSpec((D_MODEL, IN_COLS), lambda i: (0, 0), pipeline_mode=pl.Buffered(1))],
        out_specs=[pl.BlockSpec((tm, ATT_WIDTH), lambda i: (i, 0)),
                   pl.BlockSpec((tm, 2 * KV_WIDTH), lambda i: (i, 0)),
                   pl.BlockSpec((tm, SHIFT_COLS), lambda i: (i, 0))],
        out_shape=[jax.ShapeDtypeStruct((m, ATT_WIDTH), BF16),
                   jax.ShapeDtypeStruct((m, 2 * KV_WIDTH), F32),
                   jax.ShapeDtypeStruct((m, SHIFT_COLS), F32)],
        compiler_params=_cparams(("arbitrary",)),
        name="in_proj",
    )(h, w)


def _outproj_kernel(x_ref, oa_ref, orw_ref, w_ref, o_ref, *wo_ref):
    w_att = w_ref[0:ATT_WIDTH, :].astype(BF16)
    w_rw = w_ref[ATT_WIDTH:D_MODEL, :].astype(BF16)
    if wo_ref:
        wo_ref[0][0:ATT_WIDTH, :] = w_att
        wo_ref[0][ATT_WIDTH:D_MODEL, :] = w_rw
    acc = jnp.dot(oa_ref[...], w_att, preferred_element_type=F32)
    acc += jnp.dot(orw_ref[...], w_rw, preferred_element_type=F32)
    o_ref[...] = x_ref[...] + acc


def _outproj(x, oa, orw, w, *, tm, emit_w=False):
    m = x.shape[0]
    assert not emit_w or m == tm
    row = pl.BlockSpec((tm, D_MODEL), lambda i: (i, 0))
    half = pl.BlockSpec((tm, ATT_WIDTH), lambda i: (i, 0))
    full = pl.BlockSpec((D_MODEL, D_MODEL), lambda i: (0, 0), pipeline_mode=pl.Buffered(1))
    out_shape = [jax.ShapeDtypeStruct((m, D_MODEL), F32)]
    out_specs = [row]
    if emit_w:
        out_shape.append(jax.ShapeDtypeStruct((D_MODEL, D_MODEL), BF16))
        out_specs.append(pl.BlockSpec((D_MODEL, D_MODEL), lambda i: (0, 0)))
    return pl.pallas_call(
        _outproj_kernel,
        grid=(m // tm,),
        in_specs=[row, half, half, full],
        out_specs=out_specs,
        out_shape=out_shape,
        compiler_params=_cparams(("arbitrary",)),
        name="out_proj_cast" if emit_w else "out_proj",
    )(x, oa, orw, w)


def _bucket_table():
    qi = np.arange(BLOCK)[:, None]
    kj = np.arange(2 * BLOCK)[None, :]
    dist = BLOCK + qi - kj
    n = np.maximum(dist, 0)
    max_exact = N_BUCKETS // 2
    nf = np.maximum(n, 1).astype(np.float32)
    large = max_exact + (np.log(nf / np.float32(max_exact)) / np.float32(math.log(MAX_DISTANCE / max_exact))
                         * np.float32(N_BUCKETS - max_exact)).astype(np.int32)
    large = np.minimum(large, N_BUCKETS - 1)
    bucket = np.where(n < max_exact, n, large).astype(np.int32)
    valid = (dist >= 0) & (dist <= WINDOW)
    return np.where(valid, bucket, -1).astype(np.int32)


def _bias_kernel(bucket_ref, rb_ref, o_ref):
    bucket = bucket_ref[...]
    kj = lax.broadcasted_iota(jnp.int32, (BLOCK, 2 * BLOCK), 1)
    for h in range(N_Q_HEADS):
        acc = jnp.full((BLOCK, 2 * BLOCK), NEG_BIG, F32)
        for n in range(N_BUCKETS):
            acc = jnp.where(bucket == n, rb_ref[n, h], acc)
        o_ref[0, h] = jnp.where(kj >= BLOCK, acc, NEG_BIG)
        o_ref[1, h] = acc


def _bias_table(rel_bias):
    bucket = jnp.asarray(_bucket_table())
    return pl.pallas_call(
        _bias_kernel,
        in_specs=[pl.BlockSpec(memory_space=pltpu.VMEM), pl.BlockSpec(memory_space=pltpu.SMEM)],
        out_specs=pl.BlockSpec(memory_space=pltpu.VMEM),
        out_shape=jax.ShapeDtypeStruct((2, N_Q_HEADS, BLOCK, 2 * BLOCK), F32),
        name="bias_table",
    )(bucket, rel_bias)


def _attn_kernel(sink_ref, q_ref, kvp_ref, kvc_ref, bias_ref, o_ref, kv_scr):
    j = pl.program_id(1)
    kv_scr[0:BLOCK, :] = kvp_ref[...].astype(BF16)
    kv_scr[BLOCK:, :] = kvc_ref[...].astype(BF16)

    def block(i, carry):
        r0 = pl.multiple_of(i * BLOCK, BLOCK)
        rows = pl.ds(r0, BLOCK)
        table = jnp.where(jnp.logical_and(j == 0, i == 0), 0, 1)
        _attn_block(sink_ref, q_ref.at[rows], kv_scr.at[pl.ds(r0, 2 * BLOCK)], bias_ref.at[table], o_ref.at[rows])
        return carry

    lax.fori_loop(0, ATT_BLOCKS_PER_STEP, block, 0)


def _attn_block(sink_ref, q_ref, kv_ref, bias_ref, o_ref):
    half = N_KV_HEADS // 2
    for g0 in range(0, N_KV_HEADS, half):
        kcat, vcat = {}, {}
        for g in range(g0, g0 + half):
            kcat[g] = kv_ref[:, g * HEAD_DIM:(g + 1) * HEAD_DIM]
            vcat[g] = kv_ref[:, KV_WIDTH + g * HEAD_DIM:KV_WIDTH + (g + 1) * HEAD_DIM]
        heads = range(g0 * GQA_GROUP, (g0 + half) * GQA_GROUP)
        hs = {h: slice(h * HEAD_DIM, (h + 1) * HEAD_DIM) for h in heads}
        s = {h: _dg(q_ref[:, hs[h]], kcat[h // GQA_GROUP], _NT) + bias_ref[h] for h in heads}
        m = {h: jnp.maximum(jnp.max(s[h], axis=-1, keepdims=True), sink_ref[h]) for h in heads}
        p = {h: jnp.exp(s[h] - m[h]) for h in heads}
        denom = {h: jnp.sum(p[h], axis=-1, keepdims=True) + jnp.exp(sink_ref[h] - m[h]) for h in heads}
        o = {h: jnp.dot(p[h].astype(BF16), vcat[h // GQA_GROUP], preferred_element_type=F32) for h in heads}
        for h in heads:
            o_ref[:, hs[h]] = (o[h] / denom[h]).astype(BF16)


def _attn_prompt(q, kv, bias, sinks, *, batch, seq):
    n = ATT_BLOCKS_PER_STEP
    rows = n * BLOCK
    q3 = q.reshape(batch, seq, ATT_WIDTH)
    kv3 = kv.reshape(batch, seq, 2 * KV_WIDTH)
    out = pl.pallas_call(
        _attn_kernel,
        grid=(batch, seq // rows),
        in_specs=[pl.BlockSpec(memory_space=pltpu.SMEM),
                  pl.BlockSpec((None, rows, ATT_WIDTH), lambda b, j: (b, j, 0)),
                  pl.BlockSpec((None, BLOCK, 2 * KV_WIDTH), lambda b, j: (b, jnp.maximum(n * j - 1, 0), 0)),
                  pl.BlockSpec((None, rows, 2 * KV_WIDTH), lambda b, j: (b, j, 0)),
                  pl.BlockSpec((2, N_Q_HEADS, BLOCK, 2 * BLOCK), lambda b, j: (0, 0, 0, 0))],
        out_specs=pl.BlockSpec((None, rows, ATT_WIDTH), lambda b, j: (b, j, 0)),
        out_shape=jax.ShapeDtypeStruct((batch, seq, ATT_WIDTH), BF16),
        scratch_shapes=[pltpu.VMEM((rows + BLOCK, 2 * KV_WIDTH), BF16)],
        compiler_params=_cparams(("arbitrary", "arbitrary")),
        name="attn_prompt",
    )(sinks, q3, kv3, kv3, bias)
    return out.reshape(batch * seq, ATT_WIDTH)


def _attn_decode_kernel(sink_ref, q_ref, kvn_ref, ck_ref, cv_ref, bias_ref, o_ref):
    q = q_ref[...].astype(F32)
    kvn = kvn_ref[...]
    for g in range(N_KV_HEADS):
        ks = slice(g * HEAD_DIM, (g + 1) * HEAD_DIM)
        vs = slice(KV_WIDTH + g * HEAD_DIM, KV_WIDTH + (g + 1) * HEAD_DIM)
        hs = slice(g * GQA_GROUP, (g + 1) * GQA_GROUP)
        qg = q[:, hs, :]
        kc = ck_ref[:, :, ks]
        vc = cv_ref[:, :, ks]
        kn = kvn[:, ks].astype(BF16).astype(F32)[:, None, :]
        vn = kvn[:, vs].astype(BF16).astype(F32)[:, None, :]
        s = jnp.einsum("bqd,bkd->bqk", qg.astype(BF16), kc.astype(BF16), preferred_element_type=F32)
        s = s + bias_ref[hs, 0:WINDOW][None]
        s_new = jnp.sum(qg * kn, axis=-1, keepdims=True) + bias_ref[hs, WINDOW:WINDOW + 1][None]
        sink = sink_ref[hs, 0:1][None]
        m = jnp.maximum(jnp.maximum(jnp.max(s, axis=-1, keepdims=True), s_new), sink)
        p = jnp.exp(s - m)
        p_new = jnp.exp(s_new - m)
        denom = jnp.sum(p, axis=-1, keepdims=True) + p_new + jnp.exp(sink - m)
        o = jnp.einsum("bqk,bkd->bqd", p.astype(BF16), vc.astype(BF16), preferred_element_type=F32)
        o = o + p_new * vn
        o_ref[:, hs, :] = o / denom


def _attn_decode(q, kvn, cache_k, cache_v, bias_row, sinks_b):
    b = q.shape[0]
    lw = cache_k.shape[1]
    vm = pl.BlockSpec(memory_space=pltpu.VMEM)
    out = pl.pallas_call(
        _attn_decode_kernel,
        in_specs=[vm, vm, vm, vm, vm, vm],
        out_specs=vm,
        out_shape=jax.ShapeDtypeStruct((b, N_Q_HEADS, HEAD_DIM), F32),
        compiler_params=pltpu.CompilerParams(vmem_limit_bytes=VMEM_LIMIT_BYTES),
        name="attn_decode",
    )(sinks_b, q.reshape(b, N_Q_HEADS, HEAD_DIM), kvn,
      cache_k.reshape(b, lw, KV_WIDTH), cache_v.reshape(b, lw, KV_WIDTH), bias_row)
    return out.reshape(b, ATT_WIDTH)


def _seg_expand(x, bd_ref):
    rows = x.shape[0]
    groups = x.shape[1] // LANES
    xs = jnp.concatenate([x[:, i * LANES:(i + 1) * LANES] for i in range(groups)], axis=0)
    s = _mm(xs, bd_ref[...])
    return jnp.concatenate([s[i * rows:(i + 1) * rows] for i in range(groups)], axis=1)


def _rwkv_front(x, prev, p):
    mu_ref, w0_ref, a0_ref, kk_ref, ka_ref, lw_ref, g2_ref, bd_ref = p
    xm = x + mu_ref[...] * (prev - x)
    o3 = 3 * RWKV_WIDTH
    r = xm[:, 0:RWKV_WIDTH]
    k = xm[:, RWKV_WIDTH:2 * RWKV_WIDTH]
    v = xm[:, 2 * RWKV_WIDTH:o3]
    wa = xm[:, o3:o3 + LANES]
    lane = lax.broadcasted_iota(jnp.int32, wa.shape, 1)
    z = jnp.where(lane < DECAY_LORA, jnp.tanh(wa), wa)
    lora = _mm(z, lw_ref[...])
    logw = -DECAY_SCALE * _sigmoid(w0_ref[...] + lora[:, 0:RWKV_WIDTH])
    a = _sigmoid(a0_ref[...] + lora[:, RWKV_WIDTH:2 * RWKV_WIDTH])
    g = _mm(_sigmoid(xm[:, o3 + LANES:SHIFT_COLS]), g2_ref[...])
    kk = k * kk_ref[...]
    kmod = k * (1.0 + (a - 1.0) * ka_ref[...])
    nrm = jnp.sqrt(_seg_expand(kk * kk, bd_ref))
    kk = kk / jnp.maximum(nrm, 1e-12)
    return r, logw, kmod, v, kk, kk * a, g


def _rwkv_back(y, r, kmod, v, g, q):
    rk_ref, lnw_ref, lnb_ref, bd_ref = q
    mean = _seg_expand(y, bd_ref) * (1.0 / HEAD_DIM)
    d = y - mean
    var = _seg_expand(d * d, bd_ref) * (1.0 / HEAD_DIM)
    yn = d * lax.rsqrt(var + GN_EPS) * lnw_ref[...] + lnb_ref[...]
    bonus = _seg_expand(r * kmod * rk_ref[...], bd_ref) * v
    return (yn + bonus) * g


def _rwkv_chunks(rw_refs, mu_ref, w0_ref, a0_ref, kk_ref, ka_ref, lw_ref, g2_ref, bd_ref,
                 rk_ref, lnw_ref, lnb_ref,
                 o_refs, s_scr, prev_scr, works):
    C = CHUNK
    chunks = range(len(rw_refs))
    front = (mu_ref, w0_ref, a0_ref, kk_ref, ka_ref, lw_ref, g2_ref, bd_ref)
    ti = lax.broadcasted_iota(jnp.int32, (C, C), 0)
    si = lax.broadcasted_iota(jnp.int32, (C, C), 1)
    tri = jnp.where(si <= ti, 1.0, 0.0).astype(BF16)
    tail, w_end = {}, {}
    for c in chunks:
        y_scr, ea_scr, er_scr, eb_scr, ek_scr, be_scr, ke_scr, v_scr = works[c]
        x = rw_refs[c][...]
        row = lax.broadcasted_iota(jnp.int32, x.shape, 0)
        before = prev_scr[...] if c == 0 else rw_refs[c - 1][C - 1:C, :]
        prev = jnp.where(row == 0, before, pltpu.roll(x, 1, axis=0))
        r, logw, kmod, v, kk, bb, g = _rwkv_front(x, prev, front)
        cum = _mm2r(tri, logw)
        cum_end = cum[C - 1:C, :]
        e_in = jnp.exp(cum)
        e_neg = jnp.exp(-cum)
        e_end = jnp.exp(cum_end - cum)
        ea_scr[...] = kk * jnp.exp(cum - logw)
        er_scr[...] = r * e_in
        eb_scr[...] = bb * e_neg
        ek_scr[...] = kmod * e_neg
        be_scr[...] = bb * e_end
        ke_scr[...] = kmod * e_end
        v_scr[...] = v
        w_end[c] = jnp.exp(cum_end)
        tail[c] = (r, kmod, v, g)
    prev_scr[...] = rw_refs[-1][C - 1:C, :]

    lane = lax.broadcasted_iota(jnp.int32, (C, LANES), 1)
    lo_half = lane < HEAD_DIM
    ri = lax.broadcasted_iota(jnp.int32, (LANES, LANES), 0)
    ci = lax.broadcasted_iota(jnp.int32, (LANES, LANES), 1)
    same_head = jnp.right_shift(ri, 6) == jnp.right_shift(ci, 6)
    eye = ri == ci
    t_loc = jnp.bitwise_and(ri, C - 1)
    s_loc = jnp.bitwise_and(ci, C - 1)
    bd_strict = jnp.logical_and(same_head, s_loc < t_loc)
    bd_incl = jnp.logical_and(same_head, s_loc <= t_loc)

    def swap(t):
        return pltpu.roll(t, HEAD_DIM, axis=1)

    def stack2(t):
        return jnp.concatenate([jnp.where(lo_half, t, 0.0), jnp.where(lo_half, 0.0, t)], axis=0)

    def stack2x(t):
        ts = swap(t)
        return jnp.concatenate([jnp.where(lo_half, 0.0, ts), jnp.where(lo_half, ts, 0.0)], axis=0)

    def own(t):
        return jnp.where(lo_half, t[0:C], t[C:2 * C])

    def other(t):
        return swap(jnp.where(lo_half, t[C:2 * C], t[0:C]))

    sl = [slice(p * LANES, (p + 1) * LANES) for p in range(HEAD_PAIRS)]
    pairs = range(HEAD_PAIRS)
    items = [(c, p) for c in chunks for p in pairs]
    col = lambda c, k, p: works[c][k][:, sl[p]]
    a2 = {(c, p): stack2(col(c, 1, p)) for c, p in items}
    m = {}
    for c, p in items:
        r2 = stack2(col(c, 2, p))
        b_p = col(c, 3, p)
        k_p = col(c, 4, p)
        m[c, p] = _mm(jnp.concatenate([a2[c, p], r2], axis=0), jnp.concatenate([b_p, b_p, k_p, k_p], axis=0), _NT)
    l_ak = {i: jnp.where(bd_strict, m[i][0:2 * C, LANES:2 * LANES], 0.0) for i in items}
    pw = {i: -jnp.where(bd_strict, m[i][0:2 * C, 0:LANES], 0.0) for i in items}
    lrbk = {i: jnp.concatenate([jnp.where(bd_incl, m[i][2 * C:4 * C, 0:LANES], 0.0),
                                jnp.where(bd_incl, m[i][2 * C:4 * C, LANES:2 * LANES], 0.0)], axis=1) for i in items}
    v2x = {(c, p): stack2x(col(c, 7, p)) for c, p in items}
    xc = {i: a2[i] - _mm(l_ak[i], v2x[i]) for i in items}
    for it in range(6):
        for i in items:
            pw_b = pw[i].astype(BF16)
            xc_b = xc[i].astype(BF16)
            if it < 5:
                px = _dg(pw_b, jnp.concatenate([xc_b, pw_b], axis=1))
                pw[i] = px[:, LANES:2 * LANES]
                xc[i] = xc[i] + px[:, 0:LANES]
            else:
                xc[i] = xc[i] + _dg(pw_b, xc_b)
    xv = {i: jnp.concatenate([xc[i], v2x[i]], axis=0).astype(BF16) for i in items}
    lb = {i: _mm(lrbk[i], xv[i]) for i in items}
    z = {(c, p): _mm(xv[c, p], jnp.concatenate([stack2(col(c, 5, p)), stack2(col(c, 6, p))], axis=0), _TN)
         for c, p in items}
    rt = {(c, p): col(c, 2, p) - own(lb[c, p]) for c, p in items}
    yl = {i: other(lb[i]) for i in items}
    d = {(c, p): jnp.where(eye, w_end[c][:, sl[p]], 0.0) - jnp.where(same_head, z[c, p], 0.0) for c, p in items}
    gg = {i: jnp.where(same_head, jnp.concatenate([z[i][C:2 * C], z[i][0:C]], axis=0), 0.0) for i in items}
    back = (rk_ref, lnw_ref, lnb_ref, bd_ref)
    for c in chunks:
        y_scr = works[c][0]
        s0 = {p: s_scr[p].astype(BF16) for p in pairs}
        ys = {p: _mm(rt[c, p], s0[p], _NT) for p in pairs}
        for p in pairs:
            y_scr[:, sl[p]] = yl[c, p] + ys[p]
        sn = {p: _mm(s0[p], d[c, p]) for p in pairs}
        for p in pairs:
            s_scr[p] = sn[p] + gg[c, p]
    for c in chunks:
        o_refs[c][...] = _rwkv_back(works[c][0][...], *tail[c], back).astype(BF16)


def _rwkv_chunk_kernel(rw_ref, *rest):
    params, (o_ref, s_out_ref, s_scr, prev_scr), work = rest[:11], rest[11:15], rest[15:]
    c = pl.program_id(1)

    @pl.when(c == 0)
    def _():
        s_scr[...] = jnp.zeros_like(s_scr)
        prev_scr[...] = jnp.zeros_like(prev_scr)

    for g in range(0, CHUNKS_PER_STEP, CHUNK_GROUP):
        ids = range(g, g + CHUNK_GROUP)
        rows = [pl.ds(i * CHUNK, CHUNK) for i in ids]
        _rwkv_chunks([rw_ref.at[r] for r in rows], *params, [o_ref.at[r] for r in rows], s_scr, prev_scr,
                     [[w.at[i] for w in work] for i in ids])

    @pl.when(c == pl.num_programs(1) - 1)
    def _():
        for p in range(HEAD_PAIRS):
            s_out_ref[2 * p] = s_scr[p, 0:HEAD_DIM, 0:HEAD_DIM]
            s_out_ref[2 * p + 1] = s_scr[p, HEAD_DIM:LANES, HEAD_DIM:LANES]


def _rwkv_prompt(rw, params, *, batch, seq):
    rw3 = rw.reshape(batch, seq, SHIFT_COLS)
    rows = CHUNKS_PER_STEP * CHUNK
    nc = seq // rows
    const = lambda shape: pl.BlockSpec(shape, lambda b, c: tuple(0 for _ in shape))
    in_specs = [pl.BlockSpec((None, rows, SHIFT_COLS), lambda b, c: (b, c, 0))]
    in_specs += [const(p.shape) for p in params]
    wide = pltpu.VMEM((CHUNKS_PER_STEP, CHUNK, RWKV_WIDTH), F32)
    o, s_out = pl.pallas_call(
        _rwkv_chunk_kernel,
        grid=(batch, nc),
        in_specs=in_specs,
        out_specs=[pl.BlockSpec((None, rows, RWKV_WIDTH), lambda b, c: (b, c, 0)),
                   pl.BlockSpec((None, RWKV_HEADS, HEAD_DIM, HEAD_DIM), lambda b, c: (b, 0, 0, 0))],
        out_shape=[jax.ShapeDtypeStruct((batch, seq, RWKV_WIDTH), BF16),
                   jax.ShapeDtypeStruct((batch, RWKV_HEADS, HEAD_DIM, HEAD_DIM), F32)],
        scratch_shapes=[pltpu.VMEM((HEAD_PAIRS, LANES, LANES), F32),
                        pltpu.VMEM((1, SHIFT_COLS), F32),
                        wide, wide, wide, wide, wide, wide, wide, wide],
        compiler_params=_cparams(("arbitrary", "arbitrary")),
        name="rwkv_prompt",
    )(rw3, *params)
    return o.reshape(batch * seq, RWKV_WIDTH), s_out


def _rwkv_dec_front_kernel(rw_ref, prev_ref, mu_ref, w0_ref, a0_ref, kk_ref, ka_ref, lw_ref, g2_ref, bd_ref,
                           r_ref, w_ref, k_ref, v_ref, kkn_ref, b_ref, g_ref):
    front = (mu_ref, w0_ref, a0_ref, kk_ref, ka_ref, lw_ref, g2_ref, bd_ref)
    r, logw, kmod, v, kk, bb, g = _rwkv_front(rw_ref[...], prev_ref[...], front)
    r_ref[...] = r
    w_ref[...] = jnp.exp(logw)
    k_ref[...] = kmod
    v_ref[...] = v
    kkn_ref[...] = kk
    b_ref[...] = bb
    g_ref[...] = g


def _rwkv_dec_state_kernel(r_all, w_all, k_all, v_all, kk_all, b_all, s_ref, y_ref, so_ref):
    row = pl.ds(pl.program_id(0), 1)
    r_ref, w_ref, k_ref, v_ref, kk_ref, b_ref = (t.at[row] for t in (r_all, w_all, k_all, v_all, kk_all, b_all))
    ri = lax.broadcasted_iota(jnp.int32, (HEAD_DIM, HEAD_DIM), 0)
    ci = lax.broadcasted_iota(jnp.int32, (HEAD_DIM, HEAD_DIM), 1)
    eye = ri == ci
    ones = jnp.ones((HEAD_DIM, LANES), BF16)
    heads = range(RWKV_HEADS)
    hs = [slice(h * HEAD_DIM, (h + 1) * HEAD_DIM) for h in heads]
    n = RWKV_HEADS * HEAD_DIM
    s = [s_ref[h] for h in heads]
    lhs = [s[h] * kk_ref[:, hs[h]] for h in heads] + [jnp.where(eye, v_ref[:, hs[h]], 0.0) for h in heads]
    red = _mm(jnp.concatenate(lhs, axis=0), ones)
    s_new = []
    for h in heads:
        sa = -red[h * HEAD_DIM:(h + 1) * HEAD_DIM, 0:HEAD_DIM]
        v_col = red[n + h * HEAD_DIM:n + (h + 1) * HEAD_DIM, 0:HEAD_DIM]
        s_new.append(s[h] * w_ref[:, hs[h]] + sa * b_ref[:, hs[h]] + v_col * k_ref[:, hs[h]])
        so_ref[h] = s_new[h]
    yb = _mm(jnp.concatenate([s_new[h] * r_ref[:, hs[h]] for h in heads], axis=0), ones)
    for h in heads:
        y_blk = yb[h * HEAD_DIM:(h + 1) * HEAD_DIM, 0:HEAD_DIM]
        y_ref[:, hs[h]] = jnp.sum(jnp.where(eye, y_blk, 0.0), axis=0, keepdims=True)


def _rwkv_dec_back_kernel(y_ref, r_ref, k_ref, v_ref, g_ref, rk_ref, lnw_ref, lnb_ref, bd_ref, o_ref):
    back = (rk_ref, lnw_ref, lnb_ref, bd_ref)
    o_ref[...] = _rwkv_back(y_ref[...], r_ref[...], k_ref[...], v_ref[...], g_ref[...], back).astype(BF16)


def _rwkv_decode(rw, shift0, wkv0, front_params, back_params):
    b = rw.shape[0]
    vm = pl.BlockSpec(memory_space=pltpu.VMEM)
    wide = jax.ShapeDtypeStruct((b, RWKV_WIDTH), F32)
    r, w, k, v, kk, bb, g = pl.pallas_call(
        _rwkv_dec_front_kernel,
        in_specs=[vm] * (2 + len(front_params)),
        out_specs=[vm] * 7,
        out_shape=[wide] * 7,
        name="rwkv_dec_front",
    )(rw, shift0, *front_params)
    rowspec = pl.BlockSpec((None, 1, RWKV_WIDTH), lambda i: (i, 0, 0))
    stspec = pl.BlockSpec((None, RWKV_HEADS, HEAD_DIM, HEAD_DIM), lambda i: (i, 0, 0, 0))
    allrows = pl.BlockSpec((b, RWKV_WIDTH), lambda i: (0, 0))
    y, s_new = pl.pallas_call(
        _rwkv_dec_state_kernel,
        grid=(b,),
        in_specs=[allrows] * 6 + [stspec],
        out_specs=[rowspec, stspec],
        out_shape=[jax.ShapeDtypeStruct((b, 1, RWKV_WIDTH), F32),
                   jax.ShapeDtypeStruct((b, RWKV_HEADS, HEAD_DIM, HEAD_DIM), F32)],
        compiler_params=_cparams(("arbitrary",)),
        name="rwkv_dec_state",
    )(r, w, k, v, kk, bb, wkv0)
    o = pl.pallas_call(
        _rwkv_dec_back_kernel,
        in_specs=[vm] * (5 + len(back_params)),
        out_specs=vm,
        out_shape=jax.ShapeDtypeStruct((b, RWKV_WIDTH), BF16),
        name="rwkv_dec_back",
    )(y.reshape(b, RWKV_WIDTH), r, k, v, g, *back_params)
    return o, s_new


def kernel(x_prompt, x_sample, cache_k, cache_v, state_wkv, state_shift, rel_bias, ffn1_norm, ffn1_w_gate, ffn1_w_up, ffn1_w_down, mix_norm, w_in, attn_sinks, shift_mu, decay_w0, decay_w2, aaa_a0, aaa_a2, gate_g2, key_k, key_a, bonus_r_k, ln_x_w, ln_x_b, w_out, ffn2_norm, ffn2_w_gate, ffn2_w_up, ffn2_w_down, final_norm):
    batch, seq, _ = x_prompt.shape
    dec_b = x_sample.shape[0]
    lw = cache_k.shape[2]
    l = 0
    row = lambda t: t.reshape(1, -1)

    w_in_b = w_in[l].astype(BF16)
    lora_w = jnp.zeros((LANES, 2 * RWKV_WIDTH), F32)
    lora_w = lora_w.at[0:DECAY_LORA, 0:RWKV_WIDTH].set(decay_w2[l])
    lora_w = lora_w.at[DECAY_LORA:LANES, RWKV_WIDTH:].set(aaa_a2[l])
    head_of = np.arange(LANES) // HEAD_DIM
    bd_ones = jnp.asarray(head_of[:, None] == head_of[None, :], BF16)
    front_params = (row(shift_mu[l]), row(decay_w0[l]), row(aaa_a0[l]), row(key_k[l]), row(key_a[l]),
                    lora_w.astype(BF16), gate_g2[l].astype(BF16), bd_ones)
    back_params = (row(bonus_r_k[l]), row(ln_x_w[l]), row(ln_x_b[l]), bd_ones)

    bias = _bias_table(rel_bias)
    sinks = attn_sinks[l]

    xs = x_sample.reshape(dec_b, D_MODEL)
    x1s, h1s, *f1 = _ffn(xs, row(ffn1_norm[l]), ffn1_w_gate[l], ffn1_w_up[l], ffn1_w_down[l], row(mix_norm[l]),
                         tm=dec_b, tf=FF_TILE_CAST, emit_x=True, n_dtype=BF16, emit_w=True)
    qs, kvs, rws = _proj(h1s, w_in_b, tm=dec_b)
    sinks_b = jnp.broadcast_to(sinks[:, None], (N_Q_HEADS, LANES))
    o_att_s = _attn_decode(qs, kvs, cache_k[l], cache_v[l], bias[1, :, 0, :], sinks_b)
    o_rw_s, s_s = _rwkv_decode(rws, state_shift[l], state_wkv[l], front_params, back_params)
    x2s, w_out_b = _outproj(x1s, o_att_s.astype(BF16), o_rw_s, w_out[l], tm=dec_b, emit_w=True)
    y_s, *f2 = _ffn(x2s, row(ffn2_norm[l]), ffn2_w_gate[l], ffn2_w_up[l], ffn2_w_down[l], row(final_norm),
                    tm=dec_b, tf=FF_TILE_CAST, emit_x=False, n_dtype=F32, emit_w=True)

    xp = x_prompt.reshape(batch * seq, D_MODEL)
    x1, h1 = _ffn_stream(xp, row(ffn1_norm[l]), *f1, row(mix_norm[l]), tm=512, emit_x=True, n_dtype=BF16)
    q, kv, rw = _proj(h1, w_in_b, tm=256)
    o_att = _attn_prompt(q, kv, bias, sinks, batch=batch, seq=seq)
    o_rw, s_p = _rwkv_prompt(rw, front_params + back_params[:3], batch=batch, seq=seq)
    (x2,) = _outproj(x1, o_att, o_rw, w_out_b, tm=512)
    (y_p,) = _ffn_stream(x2, row(ffn2_norm[l]), *f2, row(final_norm), tm=512, emit_x=False, n_dtype=F32)

    kv3 = kv.reshape(batch, seq, 2 * KV_WIDTH)
    lp = min(WINDOW, seq)
    new_k_p = kv3[:, seq - lp:, 0:KV_WIDTH].reshape(1, batch, lp, N_KV_HEADS, HEAD_DIM)
    new_v_p = kv3[:, seq - lp:, KV_WIDTH:].reshape(1, batch, lp, N_KV_HEADS, HEAD_DIM)
    new_shift_p = rw.reshape(batch, seq, SHIFT_COLS)[:, seq - 1][None]
    k_new = kvs[:, 0:KV_WIDTH].reshape(dec_b, 1, N_KV_HEADS, HEAD_DIM)
    v_new = kvs[:, KV_WIDTH:].reshape(dec_b, 1, N_KV_HEADS, HEAD_DIM)
    new_k_s = jnp.concatenate([cache_k[l], k_new], axis=1)[:, -lw:][None]
    new_v_s = jnp.concatenate([cache_v[l], v_new], axis=1)[:, -lw:][None]
    return (y_p.reshape(batch, seq, D_MODEL), y_s.reshape(dec_b, 1, D_MODEL),
            new_k_p, new_v_p, s_p[None], new_shift_p,
            new_k_s, new_v_s, s_s[None], rws[None])
```

```python
import functools
import math

import numpy as np
import jax
import jax.numpy as jnp
from jax import lax
from jax.experimental import pallas as pl
from jax.experimental.pallas import tpu as pltpu

F32 = jnp.float32
BF16 = jnp.bfloat16

D_MODEL = 2048
HEAD_DIM = 64
ATT_WIDTH = 1024
N_Q_HEADS = 16
N_KV_HEADS = 4
GQA_GROUP = 4
KV_WIDTH = 256
RWKV_WIDTH = 1024
RWKV_HEADS = 16
WINDOW = 128
BLOCK = 128
N_BUCKETS = 32
MAX_DISTANCE = 128
DECAY_LORA = 64
AAA_LORA = 64
GATE_LORA = 128
D_FF = 5504
ATT_COLS = ATT_WIDTH + 2 * KV_WIDTH
SHIFT_COLS = 3 * RWKV_WIDTH + DECAY_LORA + AAA_LORA + GATE_LORA
IN_COLS = ATT_COLS + SHIFT_COLS
RMS_EPS = 1e-5
GN_EPS = 64e-5
FFN_RES = 0.5

LANES = 128
VMEM_LIMIT_BYTES = 60 * 1024 * 1024

FF_TILE = 1024
FF_TILE_CAST = 512
CHUNK = 64
CHUNKS_PER_STEP = 4
ATT_BLOCKS_PER_STEP = 8
HEAD_PAIRS = RWKV_HEADS // 2
DEC_ROWS = 4
CHUNK_GROUP = 2
NEG_BIG = -1e30
DECAY_SCALE = math.exp(-0.5)

_NN = (((1,), (0,)), ((), ()))
_NT = (((1,), (1,)), ((), ()))
_TN = (((0,), (0,)), ((), ()))


def _dg(a, b, dims=_NN):
    return lax.dot_general(a, b, dims, preferred_element_type=F32)


def _mm(a, b, dims=_NN):
    return _dg(a.astype(BF16), b.astype(BF16), dims)


def _split2(x):
    hi = x.astype(BF16)
    lo = (x - hi.astype(F32)).astype(BF16)
    return hi, lo


def _mm2r(a_bf16, b, dims=_NN):
    bh, bl = _split2(b)
    return _dg(a_bf16, bh, dims) + _dg(a_bf16, bl, dims)


def _sigmoid(x):
    return 1.0 / (1.0 + jnp.exp(-x))


def _rms(x, g):
    ms = jnp.mean(x * x, axis=-1, keepdims=True)
    return x * lax.rsqrt(ms + RMS_EPS) * g


def _cparams(sem):
    return pltpu.CompilerParams(dimension_semantics=sem, vmem_limit_bytes=VMEM_LIMIT_BYTES)


def _swiglu_down(xn, wg, wu, wd):
    gate = jnp.dot(xn, wg, preferred_element_type=F32)
    up = jnp.dot(xn, wu, preferred_element_type=F32)
    h = (gate * _sigmoid(gate) * up).astype(BF16)
    return jnp.dot(h, wd, preferred_element_type=F32)


_FF_TAIL_STEP = 2


def _ff_block(j, *, steps):
    return jnp.where(j < _FF_TAIL_STEP, j, jnp.where(j == _FF_TAIL_STEP, steps - 1, j - 1))


def _ffn_kernel(x_ref, g_ref, wg_ref, wu_ref, wd_ref, g2_ref, *rest, emit_x, n_dtype, tf, emit_w):
    rest = list(rest)
    ox_ref = rest.pop(0) if emit_x else None
    on_ref = rest.pop(0)
    wgo_ref, wuo_ref, wdo_ref = (rest.pop(0), rest.pop(0), rest.pop(0)) if emit_w else (None, None, None)
    xn_scr, acc_scr = rest
    j = pl.program_id(1)

    @pl.when(j == 0)
    def _():
        xn_scr[...] = _rms(x_ref[...], g_ref[...]).astype(BF16)
        acc_scr[...] = jnp.zeros_like(acc_scr)

    steps = pl.cdiv(D_FF, tf)
    last = j == steps - 1
    tail = j == _FF_TAIL_STEP

    def accumulate(width):
        wg = wg_ref[:, 0:width].astype(BF16)
        wu = wu_ref[:, 0:width].astype(BF16)
        wd = wd_ref[0:width, :].astype(BF16)
        if emit_w:
            wgo_ref[:, 0:width] = wg
            wuo_ref[:, 0:width] = wu
            wdo_ref[0:width, :] = wd
        acc_scr[...] += _swiglu_down(xn_scr[...], wg, wu, wd)

    @pl.when(jnp.logical_not(tail))
    def _():
        accumulate(tf)

    @pl.when(tail)
    def _():
        accumulate(D_FF - (steps - 1) * tf)

    @pl.when(last)
    def _():
        y = x_ref[...] + FFN_RES * acc_scr[...]
        if emit_x:
            ox_ref[...] = y
        on_ref[...] = _rms(y, g2_ref[...]).astype(n_dtype)


def _ffn(x, g, wg, wu, wd, g2, *, tm, tf, emit_x, n_dtype, emit_w=False):
    m = x.shape[0]
    assert not emit_w or m == tm
    grid = (m // tm, pl.cdiv(D_FF, tf))
    row = pl.BlockSpec((tm, D_MODEL), lambda i, j: (i, 0))
    vec = pl.BlockSpec((1, D_MODEL), lambda i, j: (0, 0))
    steps = pl.cdiv(D_FF, tf)
    blk = functools.partial(_ff_block, steps=steps)
    w_in = pl.BlockSpec((D_MODEL, tf), lambda i, j: (0, blk(j)))
    w_dn = pl.BlockSpec((tf, D_MODEL), lambda i, j: (blk(j), 0))
    out_shape = [jax.ShapeDtypeStruct((m, D_MODEL), n_dtype)]
    out_specs = [row]
    if emit_x:
        out_shape = [jax.ShapeDtypeStruct((m, D_MODEL), F32)] + out_shape
        out_specs = [row, row]
    if emit_w:
        out_shape += [jax.ShapeDtypeStruct(w.shape, BF16) for w in (wg, wu, wd)]
        out_specs += [w_in, w_in, w_dn]
    return pl.pallas_call(
        functools.partial(_ffn_kernel, emit_x=emit_x, n_dtype=n_dtype, tf=tf, emit_w=emit_w),
        grid=grid,
        in_specs=[row, vec, w_in, w_in, w_dn, vec],
        out_specs=out_specs,
        out_shape=out_shape,
        scratch_shapes=[pltpu.VMEM((tm, D_MODEL), BF16), pltpu.VMEM((tm, D_MODEL), F32)],
        compiler_params=_cparams(("arbitrary", "arbitrary")),
        name="ffn_cast" if emit_w else "ffn",
    )(x, g, wg, wu, wd, g2)


def _ff_sweep():
    steps = pl.cdiv(D_FF, FF_TILE)
    order = list(range(_FF_TAIL_STEP)) + [steps - 1] + list(range(_FF_TAIL_STEP, steps - 1))
    return [(b, min(FF_TILE, D_FF - b * FF_TILE)) for b in order]


def _ffn_stream_kernel(x_ref, g_ref, wg_hbm, wu_hbm, wd_hbm, g2_ref, *rest, emit_x, n_dtype):
    rest = list(rest)
    ox_ref = rest.pop(0) if emit_x else None
    on_ref = rest.pop(0)
    wg_buf, wu_buf, wd_buf, acc_scr, sem = rest
    i = pl.program_id(0)
    sweep = _ff_sweep()
    assert len(sweep) % 2 == 0

    def copies(pos):
        blk, width = sweep[pos]
        slot = pos % 2
        cols = pl.ds(blk * FF_TILE, width)
        return (pltpu.make_async_copy(wg_hbm.at[:, cols], wg_buf.at[slot, :, pl.ds(0, width)], sem.at[0, slot]),
                pltpu.make_async_copy(wu_hbm.at[:, cols], wu_buf.at[slot, :, pl.ds(0, width)], sem.at[1, slot]),
                pltpu.make_async_copy(wd_hbm.at[cols, :], wd_buf.at[slot, pl.ds(0, width), :], sem.at[2, slot]))

    def start(pos):
        for cp in copies(pos):
            cp.start()

    @pl.when(i == 0)
    def _():
        start(0)

    for pos, (_, width) in enumerate(sweep):
        slot = pos % 2
        for cp in copies(pos):
            cp.wait()
        if pos + 1 < len(sweep):
            start(pos + 1)
        else:
            @pl.when(i + 1 < pl.num_programs(0))
            def _():
                start(0)
        if pos == 0:
            xn = _rms(x_ref[...], g_ref[...]).astype(BF16)
        part = _swiglu_down(xn, wg_buf[slot, :, 0:width], wu_buf[slot, :, 0:width], wd_buf[slot, 0:width, :])
        if pos == 0:
            acc_scr[...] = part
        else:
            acc_scr[...] += part

    y = x_ref[...] + FFN_RES * acc_scr[...]
    if emit_x:
        ox_ref[...] = y
    on_ref[...] = _rms(y, g2_ref[...]).astype(n_dtype)


def _ffn_stream(x, g, wg, wu, wd, g2, *, tm, emit_x, n_dtype):
    m = x.shape[0]
    row = pl.BlockSpec((tm, D_MODEL), lambda i: (i, 0))
    vec = pl.BlockSpec((1, D_MODEL), lambda i: (0, 0))
    hbm = pl.BlockSpec(memory_space=pl.ANY)
    out_shape = [jax.ShapeDtypeStruct((m, D_MODEL), n_dtype)]
    out_specs = [row]
    if emit_x:
        out_shape = [jax.ShapeDtypeStruct((m, D_MODEL), F32)] + out_shape
        out_specs = [row, row]
    return pl.pallas_call(
        functools.partial(_ffn_stream_kernel, emit_x=emit_x, n_dtype=n_dtype),
        grid=(m // tm,),
        in_specs=[row, vec, hbm, hbm, hbm, vec],
        out_specs=out_specs,
        out_shape=out_shape,
        scratch_shapes=[pltpu.VMEM((2, D_MODEL, FF_TILE), BF16), pltpu.VMEM((2, D_MODEL, FF_TILE), BF16),
                        pltpu.VMEM((2, FF_TILE, D_MODEL), BF16), pltpu.VMEM((tm, D_MODEL), F32),
                        pltpu.SemaphoreType.DMA((3, 2))],
        compiler_params=_cparams(("arbitrary",)),
        name="ffn",
    )(x, g, wg, wu, wd, g2)


def _proj_kernel(h_ref, w_ref, q_ref, kv_ref, rw_ref):
    h = h_ref[...]
    q = jnp.dot(h, w_ref[:, 0:ATT_WIDTH], preferred_element_type=F32)
    q_ref[...] = (q * (HEAD_DIM ** -0.5)).astype(BF16)
    kv_ref[...] = jnp.dot(h, w_ref[:, ATT_WIDTH:ATT_COLS], preferred_element_type=F32)
    rw_ref[...] = jnp.dot(h, w_ref[:, ATT_COLS:IN_COLS], preferred_element_type=F32)


def _proj(h, w, *, tm):
    m = h.shape[0]
    return pl.pallas_call(
        _proj_kernel,
        grid=(m // tm,),
        in_specs=[pl.BlockSpec((tm, D_MODEL), lambda i: (i, 0)),
                  pl.BlockSpec((D_MODEL, IN_COLS), lambda i: (0, 0), pipeline_mode=pl.Buffered(1))],
        out_specs=[pl.BlockSpec((tm, ATT_WIDTH), lambda i: (i, 0)),
                   pl.BlockSpec((tm, 2 * KV_WIDTH), lambda i: (i, 0)),
                   pl.BlockSpec((tm, SHIFT_COLS), lambda i: (i, 0))],
        out_shape=[jax.ShapeDtypeStruct((m, ATT_WIDTH), BF16),
                   jax.ShapeDtypeStruct((m, 2 * KV_WIDTH), F32),
                   jax.ShapeDtypeStruct((m, SHIFT_COLS), F32)],
        compiler_params=_cparams(("arbitrary",)),
        name="in_proj",
    )(h, w)


def _outproj_kernel(x_ref, oa_ref, orw_ref, w_ref, o_ref, *wo_ref):
    w_att = w_ref[0:ATT_WIDTH, :].astype(BF16)
    w_rw = w_ref[ATT_WIDTH:D_MODEL, :].astype(BF16)
    if wo_ref:
        wo_ref[0][0:ATT_WIDTH, :] = w_att
        wo_ref[0][ATT_WIDTH:D_MODEL, :] = w_rw
    acc = jnp.dot(oa_ref[...], w_att, preferred_element_type=F32)
    acc += jnp.dot(orw_ref[...], w_rw, preferred_element_type=F32)
    o_ref[...] = x_ref[...] + acc


def _outproj(x, oa, orw, w, *, tm, emit_w=False):
    m = x.shape[0]
    assert not emit_w or m == tm
    row = pl.BlockSpec((tm, D_MODEL), lambda i: (i, 0))
    half = pl.BlockSpec((tm, ATT_WIDTH), lambda i: (i, 0))
    full = pl.BlockSpec((D_MODEL, D_MODEL), lambda i: (0, 0), pipeline_mode=pl.Buffered(1))
    out_shape = [jax.ShapeDtypeStruct((m, D_MODEL), F32)]
    out_specs = [row]
    if emit_w:
        out_shape.append(jax.ShapeDtypeStruct((D_MODEL, D_MODEL), BF16))
        out_specs.append(pl.BlockSpec((D_MODEL, D_MODEL), lambda i: (0, 0)))
    return pl.pallas_call(
        _outproj_kernel,
        grid=(m // tm,),
        in_specs=[row, half, half, full],
        out_specs=out_specs,
        out_shape=out_shape,
        compiler_params=_cparams(("arbitrary",)),
        name="out_proj_cast" if emit_w else "out_proj",
    )(x, oa, orw, w)


def _bucket_table():
    qi = np.arange(BLOCK)[:, None]
    kj = np.arange(2 * BLOCK)[None, :]
    dist = BLOCK + qi - kj
    n = np.maximum(dist, 0)
    max_exact = N_BUCKETS // 2
    nf = np.maximum(n, 1).astype(np.float32)
    large = max_exact + (np.log(nf / np.float32(max_exact)) / np.float32(math.log(MAX_DISTANCE / max_exact))
                         * np.float32(N_BUCKETS - max_exact)).astype(np.int32)
    large = np.minimum(large, N_BUCKETS - 1)
    bucket = np.where(n < max_exact, n, large).astype(np.int32)
    valid = (dist >= 0) & (dist <= WINDOW)
    return np.where(valid, bucket, -1).astype(np.int32)


def _bias_kernel(bucket_ref, rb_ref, o_ref):
    bucket = bucket_ref[...]
    kj = lax.broadcasted_iota(jnp.int32, (BLOCK, 2 * BLOCK), 1)
    for h in range(N_Q_HEADS):
        acc = jnp.full((BLOCK, 2 * BLOCK), NEG_BIG, F32)
        for n in range(N_BUCKETS):
            acc = jnp.where(bucket == n, rb_ref[n, h], acc)
        o_ref[0, h] = jnp.where(kj >= BLOCK, acc, NEG_BIG)
        o_ref[1, h] = acc


def _bias_table(rel_bias):
    bucket = jnp.asarray(_bucket_table())
    return pl.pallas_call(
        _bias_kernel,
        in_specs=[pl.BlockSpec(memory_space=pltpu.VMEM), pl.BlockSpec(memory_space=pltpu.SMEM)],
        out_specs=pl.BlockSpec(memory_space=pltpu.VMEM),
        out_shape=jax.ShapeDtypeStruct((2, N_Q_HEADS, BLOCK, 2 * BLOCK), F32),
        name="bias_table",
    )(bucket, rel_bias)


def _attn_kernel(sink_ref, q_ref, kvp_ref, kvc_ref, bias_ref, o_ref, kv_scr):
    j = pl.program_id(1)
    kv_scr[0:BLOCK, :] = kvp_ref[...].astype(BF16)
    kv_scr[BLOCK:, :] = kvc_ref[...].astype(BF16)

    def block(i, carry):
        r0 = pl.multiple_of(i * BLOCK, BLOCK)
        rows = pl.ds(r0, BLOCK)
        table = jnp.where(jnp.logical_and(j == 0, i == 0), 0, 1)
        _attn_block(sink_ref, q_ref.at[rows], kv_scr.at[pl.ds(r0, 2 * BLOCK)], bias_ref.at[table], o_ref.at[rows])
        return carry

    lax.fori_loop(0, ATT_BLOCKS_PER_STEP, block, 0)


def _attn_block(sink_ref, q_ref, kv_ref, bias_ref, o_ref):
    half = N_KV_HEADS // 2
    for g0 in range(0, N_KV_HEADS, half):
        kcat, vcat = {}, {}
        for g in range(g0, g0 + half):
            kcat[g] = kv_ref[:, g * HEAD_DIM:(g + 1) * HEAD_DIM]
            vcat[g] = kv_ref[:, KV_WIDTH + g * HEAD_DIM:KV_WIDTH + (g + 1) * HEAD_DIM]
        heads = range(g0 * GQA_GROUP, (g0 + half) * GQA_GROUP)
        hs = {h: slice(h * HEAD_DIM, (h + 1) * HEAD_DIM) for h in heads}
        s = {h: _dg(q_ref[:, hs[h]], kcat[h // GQA_GROUP], _NT) + bias_ref[h] for h in heads}
        m = {h: jnp.maximum(jnp.max(s[h], axis=-1, keepdims=True), sink_ref[h]) for h in heads}
        p = {h: jnp.exp(s[h] - m[h]) for h in heads}
        denom = {h: jnp.sum(p[h], axis=-1, keepdims=True) + jnp.exp(sink_ref[h] - m[h]) for h in heads}
        o = {h: jnp.dot(p[h].astype(BF16), vcat[h // GQA_GROUP], preferred_element_type=F32) for h in heads}
        for h in heads:
            o_ref[:, hs[h]] = (o[h] / denom[h]).astype(BF16)


def _attn_prompt(q, kv, bias, sinks, *, batch, seq):
    n = ATT_BLOCKS_PER_STEP
    rows = n * BLOCK
    q3 = q.reshape(batch, seq, ATT_WIDTH)
    kv3 = kv.reshape(batch, seq, 2 * KV_WIDTH)
    out = pl.pallas_call(
        _attn_kernel,
        grid=(batch, seq // rows),
        in_specs=[pl.BlockSpec(memory_space=pltpu.SMEM),
                  pl.BlockSpec((None, rows, ATT_WIDTH), lambda b, j: (b, j, 0)),
                  pl.BlockSpec((None, BLOCK, 2 * KV_WIDTH), lambda b, j: (b, jnp.maximum(n * j - 1, 0), 0)),
                  pl.BlockSpec((None, rows, 2 * KV_WIDTH), lambda b, j: (b, j, 0)),
                  pl.BlockSpec((2, N_Q_HEADS, BLOCK, 2 * BLOCK), lambda b, j: (0, 0, 0, 0))],
        out_specs=pl.BlockSpec((None, rows, ATT_WIDTH), lambda b, j: (b, j, 0)),
        out_shape=jax.ShapeDtypeStruct((batch, seq, ATT_WIDTH), BF16),
        scratch_shapes=[pltpu.VMEM((rows + BLOCK, 2 * KV_WIDTH), BF16)],
        compiler_params=_cparams(("arbitrary", "arbitrary")),
        name="attn_prompt",
    )(sinks, q3, kv3, kv3, bias)
    return out.reshape(batch * seq, ATT_WIDTH)


def _attn_decode_kernel(sink_ref, q_ref, kvn_ref, ck_ref, cv_ref, bias_ref, o_ref):
    q = q_ref[...].astype(F32)
    kvn = kvn_ref[...]
    for g in range(N_KV_HEADS):
        ks = slice(g * HEAD_DIM, (g + 1) * HEAD_DIM)
        vs = slice(KV_WIDTH + g * HEAD_DIM, KV_WIDTH + (g + 1) * HEAD_DIM)
        hs = slice(g * GQA_GROUP, (g + 1) * GQA_GROUP)
        qg = q[:, hs, :]
        kc = ck_ref[:, :, ks]
        vc = cv_ref[:, :, ks]
        kn = kvn[:, ks].astype(BF16).astype(F32)[:, None, :]
        vn = kvn[:, vs].astype(BF16).astype(F32)[:, None, :]
        s = jnp.einsum("bqd,bkd->bqk", qg.astype(BF16), kc.astype(BF16), preferred_element_type=F32)
        s = s + bias_ref[hs, 0:WINDOW][None]
        s_new = jnp.sum(qg * kn, axis=-1, keepdims=True) + bias_ref[hs, WINDOW:WINDOW + 1][None]
        sink = sink_ref[hs, 0:1][None]
        m = jnp.maximum(jnp.maximum(jnp.max(s, axis=-1, keepdims=True), s_new), sink)
        p = jnp.exp(s - m)
        p_new = jnp.exp(s_new - m)
        denom = jnp.sum(p, axis=-1, keepdims=True) + p_new + jnp.exp(sink - m)
        o = jnp.einsum("bqk,bkd->bqd", p.astype(BF16), vc.astype(BF16), preferred_element_type=F32)
        o = o + p_new * vn
        o_ref[:, hs, :] = o / denom


def _attn_decode(q, kvn, cache_k, cache_v, bias_row, sinks_b):
    b = q.shape[0]
    lw = cache_k.shape[1]
    vm = pl.BlockSpec(memory_space=pltpu.VMEM)
    out = pl.pallas_call(
        _attn_decode_kernel,
        in_specs=[vm, vm, vm, vm, vm, vm],
        out_specs=vm,
        out_shape=jax.ShapeDtypeStruct((b, N_Q_HEADS, HEAD_DIM), F32),
        compiler_params=pltpu.CompilerParams(vmem_limit_bytes=VMEM_LIMIT_BYTES),
        name="attn_decode",
    )(sinks_b, q.reshape(b, N_Q_HEADS, HEAD_DIM), kvn,
      cache_k.reshape(b, lw, KV_WIDTH), cache_v.reshape(b, lw, KV_WIDTH), bias_row)
    return out.reshape(b, ATT_WIDTH)


def _seg_expand(x, bd_ref):
    rows = x.shape[0]
    groups = x.shape[1] // LANES
    xs = jnp.concatenate([x[:, i * LANES:(i + 1) * LANES] for i in range(groups)], axis=0)
    s = _mm(xs, bd_ref[...])
    return jnp.concatenate([s[i * rows:(i + 1) * rows] for i in range(groups)], axis=1)


def _rwkv_front(x, prev, p):
    mu_ref, w0_ref, a0_ref, kk_ref, ka_ref, lw_ref, g2_ref, bd_ref = p
    xm = x + mu_ref[...] * (prev - x)
    o3 = 3 * RWKV_WIDTH
    r = xm[:, 0:RWKV_WIDTH]
    k = xm[:, RWKV_WIDTH:2 * RWKV_WIDTH]
    v = xm[:, 2 * RWKV_WIDTH:o3]
    wa = xm[:, o3:o3 + LANES]
    lane = lax.broadcasted_iota(jnp.int32, wa.shape, 1)
    z = jnp.where(lane < DECAY_LORA, jnp.tanh(wa), wa)
    lora = _mm(z, lw_ref[...])
    logw = -DECAY_SCALE * _sigmoid(w0_ref[...] + lora[:, 0:RWKV_WIDTH])
    a = _sigmoid(a0_ref[...] + lora[:, RWKV_WIDTH:2 * RWKV_WIDTH])
    g = _mm(_sigmoid(xm[:, o3 + LANES:SHIFT_COLS]), g2_ref[...])
    kk = k * kk_ref[...]
    kmod = k * (1.0 + (a - 1.0) * ka_ref[...])
    nrm = jnp.sqrt(_seg_expand(kk * kk, bd_ref))
    kk = kk / jnp.maximum(nrm, 1e-12)
    return r, logw, kmod, v, kk, kk * a, g


def _rwkv_back(y, r, kmod, v, g, q):
    rk_ref, lnw_ref, lnb_ref, bd_ref = q
    mean = _seg_expand(y, bd_ref) * (1.0 / HEAD_DIM)
    d = y - mean
    var = _seg_expand(d * d, bd_ref) * (1.0 / HEAD_DIM)
    yn = d * lax.rsqrt(var + GN_EPS) * lnw_ref[...] + lnb_ref[...]
    bonus = _seg_expand(r * kmod * rk_ref[...], bd_ref) * v
    return (yn + bonus) * g


def _rwkv_chunks(rw_refs, mu_ref, w0_ref, a0_ref, kk_ref, ka_ref, lw_ref, g2_ref, bd_ref,
                 rk_ref, lnw_ref, lnb_ref,
                 o_refs, s_scr, prev_scr, works):
    C = CHUNK
    chunks = range(len(rw_refs))
    front = (mu_ref, w0_ref, a0_ref, kk_ref, ka_ref, lw_ref, g2_ref, bd_ref)
    ti = lax.broadcasted_iota(jnp.int32, (C, C), 0)
    si = lax.broadcasted_iota(jnp.int32, (C, C), 1)
    tri = jnp.where(si <= ti, 1.0, 0.0).astype(BF16)
    tail, w_end = {}, {}
    for c in chunks:
        y_scr, ea_scr, er_scr, eb_scr, ek_scr, be_scr, ke_scr, v_scr = works[c]
        x = rw_refs[c][...]
        row = lax.broadcasted_iota(jnp.int32, x.shape, 0)
        before = prev_scr[...] if c == 0 else rw_refs[c - 1][C - 1:C, :]
        prev = jnp.where(row == 0, before, pltpu.roll(x, 1, axis=0))
        r, logw, kmod, v, kk, bb, g = _rwkv_front(x, prev, front)
        cum = _mm2r(tri, logw)
        cum_end = cum[C - 1:C, :]
        e_in = jnp.exp(cum)
        e_neg = jnp.exp(-cum)
        e_end = jnp.exp(cum_end - cum)
        ea_scr[...] = kk * jnp.exp(cum - logw)
        er_scr[...] = r * e_in
        eb_scr[...] = bb * e_neg
        ek_scr[...] = kmod * e_neg
        be_scr[...] = bb * e_end
        ke_scr[...] = kmod * e_end
        v_scr[...] = v
        w_end[c] = jnp.exp(cum_end)
        tail[c] = (r, kmod, v, g)
    prev_scr[...] = rw_refs[-1][C - 1:C, :]

    lane = lax.broadcasted_iota(jnp.int32, (C, LANES), 1)
    lo_half = lane < HEAD_DIM
    ri = lax.broadcasted_iota(jnp.int32, (LANES, LANES), 0)
    ci = lax.broadcasted_iota(jnp.int32, (LANES, LANES), 1)
    same_head = jnp.right_shift(ri, 6) == jnp.right_shift(ci, 6)
    eye = ri == ci
    t_loc = jnp.bitwise_and(ri, C - 1)
    s_loc = jnp.bitwise_and(ci, C - 1)
    bd_strict = jnp.logical_and(same_head, s_loc < t_loc)
    bd_incl = jnp.logical_and(same_head, s_loc <= t_loc)

    def swap(t):
        return pltpu.roll(t, HEAD_DIM, axis=1)

    def stack2(t):
        return jnp.concatenate([jnp.where(lo_half, t, 0.0), jnp.where(lo_half, 0.0, t)], axis=0)

    def stack2x(t):
        ts = swap(t)
        return jnp.concatenate([jnp.where(lo_half, 0.0, ts), jnp.where(lo_half, ts, 0.0)], axis=0)

    def own(t):
        return jnp.where(lo_half, t[0:C], t[C:2 * C])

    def other(t):
        return swap(jnp.where(lo_half, t[C:2 * C], t[0:C]))

    sl = [slice(p * LANES, (p + 1) * LANES) for p in range(HEAD_PAIRS)]
    pairs = range(HEAD_PAIRS)
    items = [(c, p) for c in chunks for p in pairs]
    col = lambda c, k, p: works[c][k][:, sl[p]]
    a2 = {(c, p): stack2(col(c, 1, p)) for c, p in items}
    m = {}
    for c, p in items:
        r2 = stack2(col(c, 2, p))
        b_p = col(c, 3, p)
        k_p = col(c, 4, p)
        m[c, p] = _mm(jnp.concatenate([a2[c, p], r2], axis=0), jnp.concatenate([b_p, b_p, k_p, k_p], axis=0), _NT)
    l_ak = {i: jnp.where(bd_strict, m[i][0:2 * C, LANES:2 * LANES], 0.0) for i in items}
    pw = {i: -jnp.where(bd_strict, m[i][0:2 * C, 0:LANES], 0.0) for i in items}
    lrbk = {i: jnp.concatenate([jnp.where(bd_incl, m[i][2 * C:4 * C, 0:LANES], 0.0),
                                jnp.where(bd_incl, m[i][2 * C:4 * C, LANES:2 * LANES], 0.0)], axis=1) for i in items}
    v2x = {(c, p): stack2x(col(c, 7, p)) for c, p in items}
    xc = {i: a2[i] - _mm(l_ak[i], v2x[i]) for i in items}
    for it in range(6):
        for i in items:
            pw_b = pw[i].astype(BF16)
            xc_b = xc[i].astype(BF16)
            if it < 5:
                px = _dg(pw_b, jnp.concatenate([xc_b, pw_b], axis=1))
                pw[i] = px[:, LANES:2 * LANES]
                xc[i] = xc[i] + px[:, 0:LANES]
            else:
                xc[i] = xc[i] + _dg(pw_b, xc_b)
    xv = {i: jnp.concatenate([xc[i], v2x[i]], axis=0).astype(BF16) for i in items}
    lb = {i: _mm(lrbk[i], xv[i]) for i in items}
    z = {(c, p): _mm(xv[c, p], jnp.concatenate([stack2(col(c, 5, p)), stack2(col(c, 6, p))], axis=0), _TN)
         for c, p in items}
    rt = {(c, p): col(c, 2, p) - own(lb[c, p]) for c, p in items}
    yl = {i: other(lb[i]) for i in items}
    d = {(c, p): jnp.where(eye, w_end[c][:, sl[p]], 0.0) - jnp.where(same_head, z[c, p], 0.0) for c, p in items}
    gg = {i: jnp.where(same_head, jnp.concatenate([z[i][C:2 * C], z[i][0:C]], axis=0), 0.0) for i in items}
    back = (rk_ref, lnw_ref, lnb_ref, bd_ref)
    for c in chunks:
        y_scr = works[c][0]
        s0 = {p: s_scr[p].astype(BF16) for p in pairs}
        ys = {p: _mm(rt[c, p], s0[p], _NT) for p in pairs}
        for p in pairs:
            y_scr[:, sl[p]] = yl[c, p] + ys[p]
        sn = {p: _mm(s0[p], d[c, p]) for p in pairs}
        for p in pairs:
            s_scr[p] = sn[p] + gg[c, p]
    for c in chunks:
        o_refs[c][...] = _rwkv_back(works[c][0][...], *tail[c], back).astype(BF16)


def _rwkv_chunk_kernel(rw_ref, *rest):
    params, (o_ref, s_out_ref, s_scr, prev_scr), work = rest[:11], rest[11:15], rest[15:]
    c = pl.program_id(1)

    @pl.when(c == 0)
    def _():
        s_scr[...] = jnp.zeros_like(s_scr)
        prev_scr[...] = jnp.zeros_like(prev_scr)

    for g in range(0, CHUNKS_PER_STEP, CHUNK_GROUP):
        ids = range(g, g + CHUNK_GROUP)
        rows = [pl.ds(i * CHUNK, CHUNK) for i in ids]
        _rwkv_chunks([rw_ref.at[r] for r in rows], *params, [o_ref.at[r] for r in rows], s_scr, prev_scr,
                     [[w.at[i] for w in work] for i in ids])

    @pl.when(c == pl.num_programs(1) - 1)
    def _():
        for p in range(HEAD_PAIRS):
            s_out_ref[2 * p] = s_scr[p, 0:HEAD_DIM, 0:HEAD_DIM]
            s_out_ref[2 * p + 1] = s_scr[p, HEAD_DIM:LANES, HEAD_DIM:LANES]


def _rwkv_prompt(rw, params, *, batch, seq):
    rw3 = rw.reshape(batch, seq, SHIFT_COLS)
    rows = CHUNKS_PER_STEP * CHUNK
    nc = seq // rows
    const = lambda shape: pl.BlockSpec(shape, lambda b, c: tuple(0 for _ in shape))
    in_specs = [pl.BlockSpec((None, rows, SHIFT_COLS), lambda b, c: (b, c, 0))]
    in_specs += [const(p.shape) for p in params]
    wide = pltpu.VMEM((CHUNKS_PER_STEP, CHUNK, RWKV_WIDTH), F32)
    o, s_out = pl.pallas_call(
        _rwkv_chunk_kernel,
        grid=(batch, nc),
        in_specs=in_specs,
        out_specs=[pl.BlockSpec((None, rows, RWKV_WIDTH), lambda b, c: (b, c, 0)),
                   pl.BlockSpec((None, RWKV_HEADS, HEAD_DIM, HEAD_DIM), lambda b, c: (b, 0, 0, 0))],
        out_shape=[jax.ShapeDtypeStruct((batch, seq, RWKV_WIDTH), BF16),
                   jax.ShapeDtypeStruct((batch, RWKV_HEADS, HEAD_DIM, HEAD_DIM), F32)],
        scratch_shapes=[pltpu.VMEM((HEAD_PAIRS, LANES, LANES), F32),
                        pltpu.VMEM((1, SHIFT_COLS), F32),
                        wide, wide, wide, wide, wide, wide, wide, wide],
        compiler_params=_cparams(("arbitrary", "arbitrary")),
        name="rwkv_prompt",
    )(rw3, *params)
    return o.reshape(batch * seq, RWKV_WIDTH), s_out


def _rwkv_dec_front_kernel(rw_ref, prev_ref, mu_ref, w0_ref, a0_ref, kk_ref, ka_ref, lw_ref, g2_ref, bd_ref,
                           r_ref, w_ref, k_ref, v_ref, kkn_ref, b_ref, g_ref):
    front = (mu_ref, w0_ref, a0_ref, kk_ref, ka_ref, lw_ref, g2_ref, bd_ref)
    r, logw, kmod, v, kk, bb, g = _rwkv_front(rw_ref[...], prev_ref[...], front)
    r_ref[...] = r
    w_ref[...] = jnp.exp(logw)
    k_ref[...] = kmod
    v_ref[...] = v
    kkn_ref[...] = kk
    b_ref[...] = bb
    g_ref[...] = g


def _rwkv_dec_state_kernel(r_all, w_all, k_all, v_all, kk_all, b_all, s_blk, y_blk_ref, so_blk):
    def one(j, carry):
        row = pl.ds(pl.program_id(0) * DEC_ROWS + j, 1)
        vecs = [t.at[row] for t in (r_all, w_all, k_all, v_all, kk_all, b_all)]
        _rwkv_dec_state_one(*vecs, s_blk.at[j], y_blk_ref.at[j], so_blk.at[j])
        return carry

    lax.fori_loop(0, DEC_ROWS, one, 0)


def _rwkv_dec_state_one(r_ref, w_ref, k_ref, v_ref, kk_ref, b_ref, s_ref, y_ref, so_ref):
    ri = lax.broadcasted_iota(jnp.int32, (HEAD_DIM, HEAD_DIM), 0)
    ci = lax.broadcasted_iota(jnp.int32, (HEAD_DIM, HEAD_DIM), 1)
    eye = ri == ci
    ones = jnp.ones((HEAD_DIM, LANES), BF16)
    heads = range(RWKV_HEADS)
    hs = [slice(h * HEAD_DIM, (h + 1) * HEAD_DIM) for h in heads]
    n = RWKV_HEADS * HEAD_DIM
    s = [s_ref[h] for h in heads]
    lhs = [s[h] * kk_ref[:, hs[h]] for h in heads] + [jnp.where(eye, v_ref[:, hs[h]], 0.0) for h in heads]
    red = _mm(jnp.concatenate(lhs, axis=0), ones)
    s_new = []
    for h in heads:
        sa = -red[h * HEAD_DIM:(h + 1) * HEAD_DIM, 0:HEAD_DIM]
        v_col = red[n + h * HEAD_DIM:n + (h + 1) * HEAD_DIM, 0:HEAD_DIM]
        s_new.append(s[h] * w_ref[:, hs[h]] + sa * b_ref[:, hs[h]] + v_col * k_ref[:, hs[h]])
        so_ref[h] = s_new[h]
    yb = _mm(jnp.concatenate([s_new[h] * r_ref[:, hs[h]] for h in heads], axis=0), ones)
    for h in heads:
        y_blk = yb[h * HEAD_DIM:(h + 1) * HEAD_DIM, 0:HEAD_DIM]
        y_ref[:, hs[h]] = jnp.sum(jnp.where(eye, y_blk, 0.0), axis=0, keepdims=True)


def _rwkv_dec_back_kernel(y_ref, r_ref, k_ref, v_ref, g_ref, rk_ref, lnw_ref, lnb_ref, bd_ref, o_ref):
    back = (rk_ref, lnw_ref, lnb_ref, bd_ref)
    o_ref[...] = _rwkv_back(y_ref[...], r_ref[...], k_ref[...], v_ref[...], g_ref[...], back).astype(BF16)


def _rwkv_decode(rw, shift0, wkv0, front_params, back_params):
    b = rw.shape[0]
    vm = pl.BlockSpec(memory_space=pltpu.VMEM)
    wide = jax.ShapeDtypeStruct((b, RWKV_WIDTH), F32)
    r, w, k, v, kk, bb, g = pl.pallas_call(
        _rwkv_dec_front_kernel,
        in_specs=[vm] * (2 + len(front_params)),
        out_specs=[vm] * 7,
        out_shape=[wide] * 7,
        name="rwkv_dec_front",
    )(rw, shift0, *front_params)
    assert b % DEC_ROWS == 0
    rowspec = pl.BlockSpec((DEC_ROWS, 1, RWKV_WIDTH), lambda i: (i, 0, 0))
    stspec = pl.BlockSpec((DEC_ROWS, RWKV_HEADS, HEAD_DIM, HEAD_DIM), lambda i: (i, 0, 0, 0))
    allrows = pl.BlockSpec((b, RWKV_WIDTH), lambda i: (0, 0))
    y, s_new = pl.pallas_call(
        _rwkv_dec_state_kernel,
        grid=(b // DEC_ROWS,),
        in_specs=[allrows] * 6 + [stspec],
        out_specs=[rowspec, stspec],
        out_shape=[jax.ShapeDtypeStruct((b, 1, RWKV_WIDTH), F32),
                   jax.ShapeDtypeStruct((b, RWKV_HEADS, HEAD_DIM, HEAD_DIM), F32)],
        compiler_params=_cparams(("arbitrary",)),
        name="rwkv_dec_state",
    )(r, w, k, v, kk, bb, wkv0)
    o = pl.pallas_call(
        _rwkv_dec_back_kernel,
        in_specs=[vm] * (5 + len(back_params)),
        out_specs=vm,
        out_shape=jax.ShapeDtypeStruct((b, RWKV_WIDTH), BF16),
        name="rwkv_dec_back",
    )(y.reshape(b, RWKV_WIDTH), r, k, v, g, *back_params)
    return o, s_new


def kernel(x_prompt, x_sample, cache_k, cache_v, state_wkv, state_shift, rel_bias, ffn1_norm, ffn1_w_gate, ffn1_w_up, ffn1_w_down, mix_norm, w_in, attn_sinks, shift_mu, decay_w0, decay_w2, aaa_a0, aaa_a2, gate_g2, key_k, key_a, bonus_r_k, ln_x_w, ln_x_b, w_out, ffn2_norm, ffn2_w_gate, ffn2_w_up, ffn2_w_down, final_norm):
    batch, seq, _ = x_prompt.shape
    dec_b = x_sample.shape[0]
    lw = cache_k.shape[2]
    l = 0
    row = lambda t: t.reshape(1, -1)

    w_in_b = w_in[l].astype(BF16)
    lora_w = jnp.zeros((LANES, 2 * RWKV_WIDTH), F32)
    lora_w = lora_w.at[0:DECAY_LORA, 0:RWKV_WIDTH].set(decay_w2[l])
    lora_w = lora_w.at[DECAY_LORA:LANES, RWKV_WIDTH:].set(aaa_a2[l])
    head_of = np.arange(LANES) // HEAD_DIM
    bd_ones = jnp.asarray(head_of[:, None] == head_of[None, :], BF16)
    front_params = (row(shift_mu[l]), row(decay_w0[l]), row(aaa_a0[l]), row(key_k[l]), row(key_a[l]),
                    lora_w.astype(BF16), gate_g2[l].astype(BF16), bd_ones)
    back_params = (row(bonus_r_k[l]), row(ln_x_w[l]), row(ln_x_b[l]), bd_ones)

    bias = _bias_table(rel_bias)
    sinks = attn_sinks[l]

    xs = x_sample.reshape(dec_b, D_MODEL)
    x1s, h1s, *f1 = _ffn(xs, row(ffn1_norm[l]), ffn1_w_gate[l], ffn1_w_up[l], ffn1_w_down[l], row(mix_norm[l]),
                         tm=dec_b, tf=FF_TILE_CAST, emit_x=True, n_dtype=BF16, emit_w=True)
    qs, kvs, rws = _proj(h1s, w_in_b, tm=dec_b)
    sinks_b = jnp.broadcast_to(sinks[:, None], (N_Q_HEADS, LANES))
    o_att_s = _attn_decode(qs, kvs, cache_k[l], cache_v[l], bias[1, :, 0, :], sinks_b)
    o_rw_s, s_s = _rwkv_decode(rws, state_shift[l], state_wkv[l], front_params, back_params)
    x2s, w_out_b = _outproj(x1s, o_att_s.astype(BF16), o_rw_s, w_out[l], tm=dec_b, emit_w=True)
    y_s, *f2 = _ffn(x2s, row(ffn2_norm[l]), ffn2_w_gate[l], ffn2_w_up[l], ffn2_w_down[l], row(final_norm),
                    tm=dec_b, tf=FF_TILE_CAST, emit_x=False, n_dtype=F32, emit_w=True)

    xp = x_prompt.reshape(batch * seq, D_MODEL)
    x1, h1 = _ffn_stream(xp, row(ffn1_norm[l]), *f1, row(mix_norm[l]), tm=512, emit_x=True, n_dtype=BF16)
    q, kv, rw = _proj(h1, w_in_b, tm=256)
    o_att = _attn_prompt(q, kv, bias, sinks, batch=batch, seq=seq)
    o_rw, s_p = _rwkv_prompt(rw, front_params + back_params[:3], batch=batch, seq=seq)
    (x2,) = _outproj(x1, o_att, o_rw, w_out_b, tm=512)
    (y_p,) = _ffn_stream(x2, row(ffn2_norm[l]), *f2, row(final_norm), tm=512, emit_x=False, n_dtype=F32)

    kv3 = kv.reshape(batch, seq, 2 * KV_WIDTH)
    lp = min(WINDOW, seq)
    new_k_p = kv3[:, seq - lp:, 0:KV_WIDTH].reshape(1, batch, lp, N_KV_HEADS, HEAD_DIM)
    new_v_p = kv3[:, seq - lp:, KV_WIDTH:].reshape(1, batch, lp, N_KV_HEADS, HEAD_DIM)
    new_shift_p = rw.reshape(batch, seq, SHIFT_COLS)[:, seq - 1][None]
    k_new = kvs[:, 0:KV_WIDTH].reshape(dec_b, 1, N_KV_HEADS, HEAD_DIM)
    v_new = kvs[:, KV_WIDTH:].reshape(dec_b, 1, N_KV_HEADS, HEAD_DIM)
    new_k_s = jnp.concatenate([cache_k[l], k_new], axis=1)[:, -lw:][None]
    new_v_s = jnp.concatenate([cache_v[l], v_new], axis=1)[:, -lw:][None]
    return (y_p.reshape(batch, seq, D_MODEL), y_s.reshape(dec_b, 1, D_MODEL),
            new_k_p, new_v_p, s_p[None], new_shift_p,
            new_k_s, new_v_s, s_s[None], rws[None])
```

```python
import functools
import math

import numpy as np
import jax
import jax.numpy as jnp
from jax import lax
from jax.experimental import pallas as pl
from jax.experimental.pallas import tpu as pltpu

F32 = jnp.float32
BF16 = jnp.bfloat16

D_MODEL = 2048
HEAD_DIM = 64
ATT_WIDTH = 1024
N_Q_HEADS = 16
N_KV_HEADS = 4
GQA_GROUP = 4
KV_WIDTH = 256
RWKV_WIDTH = 1024
RWKV_HEADS = 16
WINDOW = 128
BLOCK = 128
N_BUCKETS = 32
MAX_DISTANCE = 128
DECAY_LORA = 64
AAA_LORA = 64
GATE_LORA = 128
D_FF = 5504
ATT_COLS = ATT_WIDTH + 2 * KV_WIDTH
SHIFT_COLS = 3 * RWKV_WIDTH + DECAY_LORA + AAA_LORA + GATE_LORA
IN_COLS = ATT_COLS + SHIFT_COLS
RMS_EPS = 1e-5
GN_EPS = 64e-5
FFN_RES = 0.5

LANES = 128
VMEM_LIMIT_BYTES = 60 * 1024 * 1024

FF_TILE = 1024
FF_TILE_CAST = 512
CHUNK = 64
CHUNKS_PER_STEP = 4
ATT_BLOCKS_PER_STEP = 8
HEAD_PAIRS = RWKV_HEADS // 2
DEC_ROWS = 4
CHUNK_GROUP = 2
NEG_BIG = -1e30
DECAY_SCALE = math.exp(-0.5)

_NN = (((1,), (0,)), ((), ()))
_NT = (((1,), (1,)), ((), ()))
_TN = (((0,), (0,)), ((), ()))


def _dg(a, b, dims=_NN):
    return lax.dot_general(a, b, dims, preferred_element_type=F32)


def _mm(a, b, dims=_NN):
    return _dg(a.astype(BF16), b.astype(BF16), dims)


def _split2(x):
    hi = x.astype(BF16)
    lo = (x - hi.astype(F32)).astype(BF16)
    return hi, lo


def _mm2r(a_bf16, b, dims=_NN):
    bh, bl = _split2(b)
    return _dg(a_bf16, bh, dims) + _dg(a_bf16, bl, dims)


def _sigmoid(x):
    return 1.0 / (1.0 + jnp.exp(-x))


def _rms(x, g):
    ms = jnp.mean(x * x, axis=-1, keepdims=True)
    return x * lax.rsqrt(ms + RMS_EPS) * g


def _cparams(sem):
    return pltpu.CompilerParams(dimension_semantics=sem, vmem_limit_bytes=VMEM_LIMIT_BYTES)


def _swiglu_down(xn, wg, wu, wd):
    gate = jnp.dot(xn, wg, preferred_element_type=F32)
    up = jnp.dot(xn, wu, preferred_element_type=F32)
    h = (gate * _sigmoid(gate) * up).astype(BF16)
    return jnp.dot(h, wd, preferred_element_type=F32)


_FF_TAIL_STEP = 2


def _ff_block(j, *, steps):
    return jnp.where(j < _FF_TAIL_STEP, j, jnp.where(j == _FF_TAIL_STEP, steps - 1, j - 1))


def _ffn_kernel(x_ref, g_ref, wg_ref, wu_ref, wd_ref, g2_ref, *rest, emit_x, n_dtype, tf, emit_w):
    rest = list(rest)
    ox_ref = rest.pop(0) if emit_x else None
    on_ref = rest.pop(0)
    wgo_ref, wuo_ref, wdo_ref = (rest.pop(0), rest.pop(0), rest.pop(0)) if emit_w else (None, None, None)
    xn_scr, acc_scr = rest
    j = pl.program_id(1)

    @pl.when(j == 0)
    def _():
        xn_scr[...] = _rms(x_ref[...], g_ref[...]).astype(BF16)
        acc_scr[...] = jnp.zeros_like(acc_scr)

    steps = pl.cdiv(D_FF, tf)
    last = j == steps - 1
    tail = j == _FF_TAIL_STEP

    def accumulate(width):
        wg = wg_ref[:, 0:width].astype(BF16)
        wu = wu_ref[:, 0:width].astype(BF16)
        wd = wd_ref[0:width, :].astype(BF16)
        if emit_w:
            wgo_ref[:, 0:width] = wg
            wuo_ref[:, 0:width] = wu
            wdo_ref[0:width, :] = wd
        acc_scr[...] += _swiglu_down(xn_scr[...], wg, wu, wd)

    @pl.when(jnp.logical_not(tail))
    def _():
        accumulate(tf)

    @pl.when(tail)
    def _():
        accumulate(D_FF - (steps - 1) * tf)

    @pl.when(last)
    def _():
        y = x_ref[...] + FFN_RES * acc_scr[...]
        if emit_x:
            ox_ref[...] = y
        on_ref[...] = _rms(y, g2_ref[...]).astype(n_dtype)


def _ffn(x, g, wg, wu, wd, g2, *, tm, tf, emit_x, n_dtype, emit_w=False):
    m = x.shape[0]
    assert not emit_w or m == tm
    grid = (m // tm, pl.cdiv(D_FF, tf))
    row = pl.BlockSpec((tm, D_MODEL), lambda i, j: (i, 0))
    vec = pl.BlockSpec((1, D_MODEL), lambda i, j: (0, 0))
    steps = pl.cdiv(D_FF, tf)
    blk = functools.partial(_ff_block, steps=steps)
    w_in = pl.BlockSpec((D_MODEL, tf), lambda i, j: (0, blk(j)))
    w_dn = pl.BlockSpec((tf, D_MODEL), lambda i, j: (blk(j), 0))
    out_shape = [jax.ShapeDtypeStruct((m, D_MODEL), n_dtype)]
    out_specs = [row]
    if emit_x:
        out_shape = [jax.ShapeDtypeStruct((m, D_MODEL), F32)] + out_shape
        out_specs = [row, row]
    if emit_w:
        out_shape += [jax.ShapeDtypeStruct(w.shape, BF16) for w in (wg, wu, wd)]
        out_specs += [w_in, w_in, w_dn]
    return pl.pallas_call(
        functools.partial(_ffn_kernel, emit_x=emit_x, n_dtype=n_dtype, tf=tf, emit_w=emit_w),
        grid=grid,
        in_specs=[row, vec, w_in, w_in, w_dn, vec],
        out_specs=out_specs,
        out_shape=out_shape,
        scratch_shapes=[pltpu.VMEM((tm, D_MODEL), BF16), pltpu.VMEM((tm, D_MODEL), F32)],
        compiler_params=_cparams(("arbitrary", "arbitrary")),
        name="ffn_cast" if emit_w else "ffn",
    )(x, g, wg, wu, wd, g2)


def _ff_sweep():
    steps = pl.cdiv(D_FF, FF_TILE)
    order = list(range(_FF_TAIL_STEP)) + [steps - 1] + list(range(_FF_TAIL_STEP, steps - 1))
    return [(b, min(FF_TILE, D_FF - b * FF_TILE)) for b in order]


def _ffn_stream_kernel(x_ref, g_ref, wg_hbm, wu_hbm, wd_hbm, g2_ref, *rest, emit_x, n_dtype):
    rest = list(rest)
    ox_ref = rest.pop(0) if emit_x else None
    on_ref = rest.pop(0)
    wg_buf, wu_buf, wd_buf, acc_scr, sem = rest
    i = pl.program_id(0)
    sweep = _ff_sweep()
    assert len(sweep) % 2 == 0

    def copies(pos):
        blk, width = sweep[pos]
        slot = pos % 2
        cols = pl.ds(blk * FF_TILE, width)
        return (pltpu.make_async_copy(wg_hbm.at[:, cols], wg_buf.at[slot, :, pl.ds(0, width)], sem.at[0, slot]),
                pltpu.make_async_copy(wu_hbm.at[:, cols], wu_buf.at[slot, :, pl.ds(0, width)], sem.at[1, slot]),
                pltpu.make_async_copy(wd_hbm.at[cols, :], wd_buf.at[slot, pl.ds(0, width), :], sem.at[2, slot]))

    def start(pos):
        for cp in copies(pos):
            cp.start()

    @pl.when(i == 0)
    def _():
        start(0)

    for pos, (_, width) in enumerate(sweep):
        slot = pos % 2
        for cp in copies(pos):
            cp.wait()
        if pos + 1 < len(sweep):
            start(pos + 1)
        else:
            @pl.when(i + 1 < pl.num_programs(0))
            def _():
                start(0)
        if pos == 0:
            xn = _rms(x_ref[...], g_ref[...]).astype(BF16)
        part = _swiglu_down(xn, wg_buf[slot, :, 0:width], wu_buf[slot, :, 0:width], wd_buf[slot, 0:width, :])
        if pos == 0:
            acc_scr[...] = part
        else:
            acc_scr[...] += part

    y = x_ref[...] + FFN_RES * acc_scr[...]
    if emit_x:
        ox_ref[...] = y
    on_ref[...] = _rms(y, g2_ref[...]).astype(n_dtype)


def _ffn_stream(x, g, wg, wu, wd, g2, *, tm, emit_x, n_dtype):
    m = x.shape[0]
    row = pl.BlockSpec((tm, D_MODEL), lambda i: (i, 0))
    vec = pl.BlockSpec((1, D_MODEL), lambda i: (0, 0))
    hbm = pl.BlockSpec(memory_space=pl.ANY)
    out_shape = [jax.ShapeDtypeStruct((m, D_MODEL), n_dtype)]
    out_specs = [row]
    if emit_x:
        out_shape = [jax.ShapeDtypeStruct((m, D_MODEL), F32)] + out_shape
        out_specs = [row, row]
    return pl.pallas_call(
        functools.partial(_ffn_stream_kernel, emit_x=emit_x, n_dtype=n_dtype),
        grid=(m // tm,),
        in_specs=[row, vec, hbm, hbm, hbm, vec],
        out_specs=out_specs,
        out_shape=out_shape,
        scratch_shapes=[pltpu.VMEM((2, D_MODEL, FF_TILE), BF16), pltpu.VMEM((2, D_MODEL, FF_TILE), BF16),
                        pltpu.VMEM((2, FF_TILE, D_MODEL), BF16), pltpu.VMEM((tm, D_MODEL), F32),
                        pltpu.SemaphoreType.DMA((3, 2))],
        compiler_params=_cparams(("arbitrary",)),
        name="ffn",
    )(x, g, wg, wu, wd, g2)


def _proj_kernel(h_ref, w_ref, q_ref, kv_ref, rw_ref):
    h = h_ref[...]
    q = jnp.dot(h, w_ref[:, 0:ATT_WIDTH], preferred_element_type=F32)
    q_ref[...] = (q * (HEAD_DIM ** -0.5)).astype(BF16)
    kv_ref[...] = jnp.dot(h, w_ref[:, ATT_WIDTH:ATT_COLS], preferred_element_type=F32)
    rw_ref[...] = jnp.dot(h, w_ref[:, ATT_COLS:IN_COLS], preferred_element_type=F32)


def _proj(h, w, *, tm):
    m = h.shape[0]
    return pl.pallas_call(
        _proj_kernel,
        grid=(m // tm,),
        in_specs=[pl.BlockSpec((tm, D_MODEL), lambda i: (i, 0)),
                  pl.BlockSpec((D_MODEL, IN_COLS), lambda i: (0, 0), pipeline_mode=pl.Buffered(1))],
        out_specs=[pl.BlockSpec((tm, ATT_WIDTH), lambda i: (i, 0)),
                   pl.BlockSpec((tm, 2 * KV_WIDTH), lambda i: (i, 0)),
                   pl.BlockSpec((tm, SHIFT_COLS), lambda i: (i, 0))],
        out_shape=[jax.ShapeDtypeStruct((m, ATT_WIDTH), BF16),
                   jax.ShapeDtypeStruct((m, 2 * KV_WIDTH), F32),
                   jax.ShapeDtypeStruct((m, SHIFT_COLS), F32)],
        compiler_params=_cparams(("arbitrary",)),
        name="in_proj",
    )(h, w)


def _outproj_kernel(x_ref, oa_ref, orw_ref, w_ref, o_ref, *wo_ref):
    w_att = w_ref[0:ATT_WIDTH, :].astype(BF16)
    w_rw = w_ref[ATT_WIDTH:D_MODEL, :].astype(BF16)
    if wo_ref:
        wo_ref[0][0:ATT_WIDTH, :] = w_att
        wo_ref[0][ATT_WIDTH:D_MODEL, :] = w_rw
    acc = jnp.dot(oa_ref[...], w_att, preferred_element_type=F32)
    acc += jnp.dot(orw_ref[...], w_rw, preferred_element_type=F32)
    o_ref[...] = x_ref[...] + acc


def _outproj(x, oa, orw, w, *, tm, emit_w=False):
    m = x.shape[0]
    assert not emit_w or m == tm
    row = pl.BlockSpec((tm, D_MODEL), lambda i: (i, 0))
    half = pl.BlockSpec((tm, ATT_WIDTH), lambda i: (i, 0))
    full = pl.BlockSpec((D_MODEL, D_MODEL), lambda i: (0, 0), pipeline_mode=pl.Buffered(1))
    out_shape = [jax.ShapeDtypeStruct((m, D_MODEL), F32)]
    out_specs = [row]
    if emit_w:
        out_shape.append(jax.ShapeDtypeStruct((D_MODEL, D_MODEL), BF16))
        out_specs.append(pl.BlockSpec((D_MODEL, D_MODEL), lambda i: (0, 0)))
    return pl.pallas_call(
        _outproj_kernel,
        grid=(m // tm,),
        in_specs=[row, half, half, full],
        out_specs=out_specs,
        out_shape=out_shape,
        compiler_params=_cparams(("arbitrary",)),
        name="out_proj_cast" if emit_w else "out_proj",
    )(x, oa, orw, w)


def _bucket_table():
    qi = np.arange(BLOCK)[:, None]
    kj = np.arange(2 * BLOCK)[None, :]
    dist = BLOCK + qi - kj
    n = np.maximum(dist, 0)
    max_exact = N_BUCKETS // 2
    nf = np.maximum(n, 1).astype(np.float32)
    large = max_exact + (np.log(nf / np.float32(max_exact)) / np.float32(math.log(MAX_DISTANCE / max_exact))
                         * np.float32(N_BUCKETS - max_exact)).astype(np.int32)
    large = np.minimum(large, N_BUCKETS - 1)
    bucket = np.where(n < max_exact, n, large).astype(np.int32)
    valid = (dist >= 0) & (dist <= WINDOW)
    return np.where(valid, bucket, -1).astype(np.int32)


def _bias_kernel(bucket_ref, rb_ref, o_ref):
    bucket = bucket_ref[...]
    kj = lax.broadcasted_iota(jnp.int32, (BLOCK, 2 * BLOCK), 1)
    for h in range(N_Q_HEADS):
        acc = jnp.full((BLOCK, 2 * BLOCK), NEG_BIG, F32)
        for n in range(N_BUCKETS):
            acc = jnp.where(bucket == n, rb_ref[n, h], acc)
        o_ref[0, h] = jnp.where(kj >= BLOCK, acc, NEG_BIG)
        o_ref[1, h] = acc


def _bias_table(rel_bias):
    bucket = jnp.asarray(_bucket_table())
    return pl.pallas_call(
        _bias_kernel,
        in_specs=[pl.BlockSpec(memory_space=pltpu.VMEM), pl.BlockSpec(memory_space=pltpu.SMEM)],
        out_specs=pl.BlockSpec(memory_space=pltpu.VMEM),
        out_shape=jax.ShapeDtypeStruct((2, N_Q_HEADS, BLOCK, 2 * BLOCK), F32),
        name="bias_table",
    )(bucket, rel_bias)


def _attn_kernel(sink_ref, q_ref, kvp_ref, kvc_ref, bias_ref, o_ref, kv_scr):
    j = pl.program_id(1)
    kv_scr[0:BLOCK, :] = kvp_ref[...].astype(BF16)
    kv_scr[BLOCK:, :] = kvc_ref[...].astype(BF16)

    def block(i, carry):
        r0 = pl.multiple_of(i * BLOCK, BLOCK)
        rows = pl.ds(r0, BLOCK)
        table = jnp.where(jnp.logical_and(j == 0, i == 0), 0, 1)
        _attn_block(sink_ref, q_ref.at[rows], kv_scr.at[pl.ds(r0, 2 * BLOCK)], bias_ref.at[table], o_ref.at[rows])
        return carry

    lax.fori_loop(0, ATT_BLOCKS_PER_STEP, block, 0)


def _attn_block(sink_ref, q_ref, kv_ref, bias_ref, o_ref):
    half = N_KV_HEADS // 2
    for g0 in range(0, N_KV_HEADS, half):
        kcat, vcat = {}, {}
        for g in range(g0, g0 + half):
            kcat[g] = kv_ref[:, g * HEAD_DIM:(g + 1) * HEAD_DIM]
            vcat[g] = kv_ref[:, KV_WIDTH + g * HEAD_DIM:KV_WIDTH + (g + 1) * HEAD_DIM]
        heads = range(g0 * GQA_GROUP, (g0 + half) * GQA_GROUP)
        hs = {h: slice(h * HEAD_DIM, (h + 1) * HEAD_DIM) for h in heads}
        s = {h: _dg(q_ref[:, hs[h]], kcat[h // GQA_GROUP], _NT) + bias_ref[h] for h in heads}
        m = {h: jnp.maximum(jnp.max(s[h], axis=-1, keepdims=True), sink_ref[h]) for h in heads}
        p = {h: jnp.exp(s[h] - m[h]) for h in heads}
        denom = {h: jnp.sum(p[h], axis=-1, keepdims=True) + jnp.exp(sink_ref[h] - m[h]) for h in heads}
        o = {h: jnp.dot(p[h].astype(BF16), vcat[h // GQA_GROUP], preferred_element_type=F32) for h in heads}
        for h in heads:
            o_ref[:, hs[h]] = (o[h] / denom[h]).astype(BF16)


def _attn_prompt(q, kv, bias, sinks, *, batch, seq):
    n = ATT_BLOCKS_PER_STEP
    rows = n * BLOCK
    q3 = q.reshape(batch, seq, ATT_WIDTH)
    kv3 = kv.reshape(batch, seq, 2 * KV_WIDTH)
    out = pl.pallas_call(
        _attn_kernel,
        grid=(batch, seq // rows),
        in_specs=[pl.BlockSpec(memory_space=pltpu.SMEM),
                  pl.BlockSpec((None, rows, ATT_WIDTH), lambda b, j: (b, j, 0)),
                  pl.BlockSpec((None, BLOCK, 2 * KV_WIDTH), lambda b, j: (b, jnp.maximum(n * j - 1, 0), 0)),
                  pl.BlockSpec((None, rows, 2 * KV_WIDTH), lambda b, j: (b, j, 0)),
                  pl.BlockSpec((2, N_Q_HEADS, BLOCK, 2 * BLOCK), lambda b, j: (0, 0, 0, 0))],
        out_specs=pl.BlockSpec((None, rows, ATT_WIDTH), lambda b, j: (b, j, 0)),
        out_shape=jax.ShapeDtypeStruct((batch, seq, ATT_WIDTH), BF16),
        scratch_shapes=[pltpu.VMEM((rows + BLOCK, 2 * KV_WIDTH), BF16)],
        compiler_params=_cparams(("arbitrary", "arbitrary")),
        name="attn_prompt",
    )(sinks, q3, kv3, kv3, bias)
    return out.reshape(batch * seq, ATT_WIDTH)


def _attn_decode_kernel(sink_ref, q_ref, kvn_ref, ck_ref, cv_ref, bias_ref, o_ref):
    q = q_ref[...].astype(F32)
    kvn = kvn_ref[...]
    for g in range(N_KV_HEADS):
        ks = slice(g * HEAD_DIM, (g + 1) * HEAD_DIM)
        vs = slice(KV_WIDTH + g * HEAD_DIM, KV_WIDTH + (g + 1) * HEAD_DIM)
        hs = slice(g * GQA_GROUP, (g + 1) * GQA_GROUP)
        qg = q[:, hs, :]
        kc = ck_ref[:, :, ks]
        vc = cv_ref[:, :, ks]
        kn = kvn[:, ks].astype(BF16).astype(F32)[:, None, :]
        vn = kvn[:, vs].astype(BF16).astype(F32)[:, None, :]
        s = jnp.einsum("bqd,bkd->bqk", qg.astype(BF16), kc.astype(BF16), preferred_element_type=F32)
        s = s + bias_ref[hs, 0:WINDOW][None]
        s_new = jnp.sum(qg * kn, axis=-1, keepdims=True) + bias_ref[hs, WINDOW:WINDOW + 1][None]
        sink = sink_ref[hs, 0:1][None]
        m = jnp.maximum(jnp.maximum(jnp.max(s, axis=-1, keepdims=True), s_new), sink)
        p = jnp.exp(s - m)
        p_new = jnp.exp(s_new - m)
        denom = jnp.sum(p, axis=-1, keepdims=True) + p_new + jnp.exp(sink - m)
        o = jnp.einsum("bqk,bkd->bqd", p.astype(BF16), vc.astype(BF16), preferred_element_type=F32)
        o = o + p_new * vn
        o_ref[:, hs, :] = o / denom


def _attn_decode(q, kvn, cache_k, cache_v, bias_row, sinks_b):
    b = q.shape[0]
    lw = cache_k.shape[1]
    vm = pl.BlockSpec(memory_space=pltpu.VMEM)
    out = pl.pallas_call(
        _attn_decode_kernel,
        in_specs=[vm, vm, vm, vm, vm, vm],
        out_specs=vm,
        out_shape=jax.ShapeDtypeStruct((b, N_Q_HEADS, HEAD_DIM), F32),
        compiler_params=pltpu.CompilerParams(vmem_limit_bytes=VMEM_LIMIT_BYTES),
        name="attn_decode",
    )(sinks_b, q.reshape(b, N_Q_HEADS, HEAD_DIM), kvn,
      cache_k.reshape(b, lw, KV_WIDTH), cache_v.reshape(b, lw, KV_WIDTH), bias_row)
    return out.reshape(b, ATT_WIDTH)


def _seg_expand(x, bd_ref):
    rows = x.shape[0]
    groups = x.shape[1] // LANES
    xs = jnp.concatenate([x[:, i * LANES:(i + 1) * LANES] for i in range(groups)], axis=0)
    s = _mm(xs, bd_ref[...])
    return jnp.concatenate([s[i * rows:(i + 1) * rows] for i in range(groups)], axis=1)


def _rwkv_front(x, prev, p):
    mu_ref, w0_ref, a0_ref, kk_ref, ka_ref, lw_ref, g2_ref, bd_ref = p
    xm = x + mu_ref[...] * (prev - x)
    o3 = 3 * RWKV_WIDTH
    r = xm[:, 0:RWKV_WIDTH]
    k = xm[:, RWKV_WIDTH:2 * RWKV_WIDTH]
    v = xm[:, 2 * RWKV_WIDTH:o3]
    wa = xm[:, o3:o3 + LANES]
    lane = lax.broadcasted_iota(jnp.int32, wa.shape, 1)
    z = jnp.where(lane < DECAY_LORA, jnp.tanh(wa), wa)
    lora = _mm(z, lw_ref[...])
    logw = -DECAY_SCALE * _sigmoid(w0_ref[...] + lora[:, 0:RWKV_WIDTH])
    a = _sigmoid(a0_ref[...] + lora[:, RWKV_WIDTH:2 * RWKV_WIDTH])
    g = _mm(_sigmoid(xm[:, o3 + LANES:SHIFT_COLS]), g2_ref[...])
    kk = k * kk_ref[...]
    kmod = k * (1.0 + (a - 1.0) * ka_ref[...])
    nrm = jnp.sqrt(_seg_expand(kk * kk, bd_ref))
    kk = kk / jnp.maximum(nrm, 1e-12)
    return r, logw, kmod, v, kk, kk * a, g


def _rwkv_back(y, r, kmod, v, g, q):
    rk_ref, lnw_ref, lnb_ref, bd_ref = q
    mean = _seg_expand(y, bd_ref) * (1.0 / HEAD_DIM)
    d = y - mean
    var = _seg_expand(d * d, bd_ref) * (1.0 / HEAD_DIM)
    yn = d * lax.rsqrt(var + GN_EPS) * lnw_ref[...] + lnb_ref[...]
    bonus = _seg_expand(r * kmod * rk_ref[...], bd_ref) * v
    return (yn + bonus) * g


def _rwkv_chunks(rw_refs, mu_ref, w0_ref, a0_ref, kk_ref, ka_ref, lw_ref, g2_ref, bd_ref,
                 rk_ref, lnw_ref, lnb_ref,
                 o_refs, s_scr, prev_scr, works):
    C = CHUNK
    chunks = range(len(rw_refs))
    front = (mu_ref, w0_ref, a0_ref, kk_ref, ka_ref, lw_ref, g2_ref, bd_ref)
    ti = lax.broadcasted_iota(jnp.int32, (C, C), 0)
    si = lax.broadcasted_iota(jnp.int32, (C, C), 1)
    tri = jnp.where(si <= ti, 1.0, 0.0).astype(BF16)
    tail, w_end = {}, {}
    for c in chunks:
        y_scr, ea_scr, er_scr, eb_scr, ek_scr, be_scr, ke_scr, v_scr = works[c]
        x = rw_refs[c][...]
        row = lax.broadcasted_iota(jnp.int32, x.shape, 0)
        before = prev_scr[...] if c == 0 else rw_refs[c - 1][C - 1:C, :]
        prev = jnp.where(row == 0, before, pltpu.roll(x, 1, axis=0))
        r, logw, kmod, v, kk, bb, g = _rwkv_front(x, prev, front)
        cum = _mm2r(tri, logw)
        cum_end = cum[C - 1:C, :]
        e_in = jnp.exp(cum)
        e_neg = jnp.exp(-cum)
        e_end = jnp.exp(cum_end - cum)
        ea_scr[...] = kk * jnp.exp(cum - logw)
        er_scr[...] = r * e_in
        eb_scr[...] = bb * e_neg
        ek_scr[...] = kmod * e_neg
        be_scr[...] = bb * e_end
        ke_scr[...] = kmod * e_end
        v_scr[...] = v
        w_end[c] = jnp.exp(cum_end)
        tail[c] = (r, kmod, v, g)
    prev_scr[...] = rw_refs[-1][C - 1:C, :]

    lane = lax.broadcasted_iota(jnp.int32, (C, LANES), 1)
    lo_half = lane < HEAD_DIM
    ri = lax.broadcasted_iota(jnp.int32, (LANES, LANES), 0)
    ci = lax.broadcasted_iota(jnp.int32, (LANES, LANES), 1)
    same_head = jnp.right_shift(ri, 6) == jnp.right_shift(ci, 6)
    eye = ri == ci
    t_loc = jnp.bitwise_and(ri, C - 1)
    s_loc = jnp.bitwise_and(ci, C - 1)
    bd_strict = jnp.logical_and(same_head, s_loc < t_loc)
    bd_incl = jnp.logical_and(same_head, s_loc <= t_loc)

    def swap(t):
        return pltpu.roll(t, HEAD_DIM, axis=1)

    def stack2(t):
        return jnp.concatenate([jnp.where(lo_half, t, 0.0), jnp.where(lo_half, 0.0, t)], axis=0)

    def stack2x(t):
        ts = swap(t)
        return jnp.concatenate([jnp.where(lo_half, 0.0, ts), jnp.where(lo_half, ts, 0.0)], axis=0)

    def own(t):
        return jnp.where(lo_half, t[0:C], t[C:2 * C])

    def other(t):
        return swap(jnp.where(lo_half, t[C:2 * C], t[0:C]))

    sl = [slice(p * LANES, (p + 1) * LANES) for p in range(HEAD_PAIRS)]
    pairs = range(HEAD_PAIRS)
    items = [(c, p) for c in chunks for p in pairs]
    col = lambda c, k, p: works[c][k][:, sl[p]]
    a2 = {(c, p): stack2(col(c, 1, p)) for c, p in items}
    m = {}
    for c, p in items:
        r2 = stack2(col(c, 2, p))
        b_p = col(c, 3, p)
        k_p = col(c, 4, p)
        m[c, p] = _mm(jnp.concatenate([a2[c, p], r2], axis=0), jnp.concatenate([b_p, k_p], axis=0), _NT)
    ms = {i: swap(m[i]) for i in items}
    first = ri < HEAD_DIM
    l_ak = {i: jnp.where(bd_strict, jnp.where(first, ms[i][0:2 * C], m[i][0:2 * C]), 0.0) for i in items}
    pw = {i: -jnp.where(bd_strict, jnp.where(first, m[i][0:2 * C], ms[i][0:2 * C]), 0.0) for i in items}
    lrbk = {i: jnp.concatenate([jnp.where(bd_incl, jnp.where(first, m[i][2 * C:4 * C], ms[i][2 * C:4 * C]), 0.0),
                                jnp.where(bd_incl, jnp.where(first, ms[i][2 * C:4 * C], m[i][2 * C:4 * C]), 0.0)],
                               axis=1) for i in items}
    v2x = {(c, p): stack2x(col(c, 7, p)) for c, p in items}
    xc = {i: a2[i] - _mm(l_ak[i], v2x[i]) for i in items}
    for it in range(6):
        for i in items:
            pw_b = pw[i].astype(BF16)
            xc_b = xc[i].astype(BF16)
            if it < 5:
                px = _dg(pw_b, jnp.concatenate([xc_b, pw_b], axis=1))
                pw[i] = px[:, LANES:2 * LANES]
                xc[i] = xc[i] + px[:, 0:LANES]
            else:
                xc[i] = xc[i] + _dg(pw_b, xc_b)
    xv = {i: jnp.concatenate([xc[i], v2x[i]], axis=0).astype(BF16) for i in items}
    lb = {i: _mm(lrbk[i], xv[i]) for i in items}
    z = {(c, p): _mm(xv[c, p], jnp.concatenate([stack2(col(c, 5, p)), stack2(col(c, 6, p))], axis=0), _TN)
         for c, p in items}
    rt = {(c, p): col(c, 2, p) - own(lb[c, p]) for c, p in items}
    yl = {i: other(lb[i]) for i in items}
    d = {(c, p): jnp.where(eye, w_end[c][:, sl[p]], 0.0) - jnp.where(same_head, z[c, p], 0.0) for c, p in items}
    gg = {i: jnp.where(same_head, jnp.concatenate([z[i][C:2 * C], z[i][0:C]], axis=0), 0.0) for i in items}
    back = (rk_ref, lnw_ref, lnb_ref, bd_ref)
    for c in chunks:
        y_scr = works[c][0]
        s0 = {p: s_scr[p].astype(BF16) for p in pairs}
        ys = {p: _mm(rt[c, p], s0[p], _NT) for p in pairs}
        for p in pairs:
            y_scr[:, sl[p]] = yl[c, p] + ys[p]
        sn = {p: _mm(s0[p], d[c, p]) for p in pairs}
        for p in pairs:
            s_scr[p] = sn[p] + gg[c, p]
    for c in chunks:
        o_refs[c][...] = _rwkv_back(works[c][0][...], *tail[c], back).astype(BF16)


def _rwkv_chunk_kernel(rw_ref, *rest):
    params, (o_ref, s_out_ref, s_scr, prev_scr), work = rest[:11], rest[11:15], rest[15:]
    c = pl.program_id(1)

    @pl.when(c == 0)
    def _():
        s_scr[...] = jnp.zeros_like(s_scr)
        prev_scr[...] = jnp.zeros_like(prev_scr)

    for g in range(0, CHUNKS_PER_STEP, CHUNK_GROUP):
        ids = range(g, g + CHUNK_GROUP)
        rows = [pl.ds(i * CHUNK, CHUNK) for i in ids]
        _rwkv_chunks([rw_ref.at[r] for r in rows], *params, [o_ref.at[r] for r in rows], s_scr, prev_scr,
                     [[w.at[i] for w in work] for i in ids])

    @pl.when(c == pl.num_programs(1) - 1)
    def _():
        for p in range(HEAD_PAIRS):
            s_out_ref[2 * p] = s_scr[p, 0:HEAD_DIM, 0:HEAD_DIM]
            s_out_ref[2 * p + 1] = s_scr[p, HEAD_DIM:LANES, HEAD_DIM:LANES]


def _rwkv_prompt(rw, params, *, batch, seq):
    rw3 = rw.reshape(batch, seq, SHIFT_COLS)
    rows = CHUNKS_PER_STEP * CHUNK
    nc = seq // rows
    const = lambda shape: pl.BlockSpec(shape, lambda b, c: tuple(0 for _ in shape))
    in_specs = [pl.BlockSpec((None, rows, SHIFT_COLS), lambda b, c: (b, c, 0))]
    in_specs += [const(p.shape) for p in params]
    wide = pltpu.VMEM((CHUNKS_PER_STEP, CHUNK, RWKV_WIDTH), F32)
    o, s_out = pl.pallas_call(
        _rwkv_chunk_kernel,
        grid=(batch, nc),
        in_specs=in_specs,
        out_specs=[pl.BlockSpec((None, rows, RWKV_WIDTH), lambda b, c: (b, c, 0)),
                   pl.BlockSpec((None, RWKV_HEADS, HEAD_DIM, HEAD_DIM), lambda b, c: (b, 0, 0, 0))],
        out_shape=[jax.ShapeDtypeStruct((batch, seq, RWKV_WIDTH), BF16),
                   jax.ShapeDtypeStruct((batch, RWKV_HEADS, HEAD_DIM, HEAD_DIM), F32)],
        scratch_shapes=[pltpu.VMEM((HEAD_PAIRS, LANES, LANES), F32),
                        pltpu.VMEM((1, SHIFT_COLS), F32),
                        wide, wide, wide, wide, wide, wide, wide, wide],
        compiler_params=_cparams(("arbitrary", "arbitrary")),
        name="rwkv_prompt",
    )(rw3, *params)
    return o.reshape(batch * seq, RWKV_WIDTH), s_out


def _rwkv_dec_front_kernel(rw_ref, prev_ref, mu_ref, w0_ref, a0_ref, kk_ref, ka_ref, lw_ref, g2_ref, bd_ref,
                           r_ref, w_ref, k_ref, v_ref, kkn_ref, b_ref, g_ref):
    front = (mu_ref, w0_ref, a0_ref, kk_ref, ka_ref, lw_ref, g2_ref, bd_ref)
    r, logw, kmod, v, kk, bb, g = _rwkv_front(rw_ref[...], prev_ref[...], front)
    r_ref[...] = r
    w_ref[...] = jnp.exp(logw)
    k_ref[...] = kmod
    v_ref[...] = v
    kkn_ref[...] = kk
    b_ref[...] = bb
    g_ref[...] = g


def _rwkv_dec_state_kernel(r_all, w_all, k_all, v_all, kk_all, b_all, s_blk, y_blk_ref, so_blk):
    def one(j, carry):
        row = pl.ds(pl.program_id(0) * DEC_ROWS + j, 1)
        vecs = [t.at[row] for t in (r_all, w_all, k_all, v_all, kk_all, b_all)]
        _rwkv_dec_state_one(*vecs, s_blk.at[j], y_blk_ref.at[j], so_blk.at[j])
        return carry

    lax.fori_loop(0, DEC_ROWS, one, 0)


def _rwkv_dec_state_one(r_ref, w_ref, k_ref, v_ref, kk_ref, b_ref, s_ref, y_ref, so_ref):
    ri = lax.broadcasted_iota(jnp.int32, (HEAD_DIM, HEAD_DIM), 0)
    ci = lax.broadcasted_iota(jnp.int32, (HEAD_DIM, HEAD_DIM), 1)
    eye = ri == ci
    ones = jnp.ones((HEAD_DIM, LANES), BF16)
    heads = range(RWKV_HEADS)
    hs = [slice(h * HEAD_DIM, (h + 1) * HEAD_DIM) for h in heads]
    n = RWKV_HEADS * HEAD_DIM
    s = [s_ref[h] for h in heads]
    lhs = [s[h] * kk_ref[:, hs[h]] for h in heads] + [jnp.where(eye, v_ref[:, hs[h]], 0.0) for h in heads]
    red = _mm(jnp.concatenate(lhs, axis=0), ones)
    s_new = []
    for h in heads:
        sa = -red[h * HEAD_DIM:(h + 1) * HEAD_DIM, 0:HEAD_DIM]
        v_col = red[n + h * HEAD_DIM:n + (h + 1) * HEAD_DIM, 0:HEAD_DIM]
        s_new.append(s[h] * w_ref[:, hs[h]] + sa * b_ref[:, hs[h]] + v_col * k_ref[:, hs[h]])
        so_ref[h] = s_new[h]
    yb = _mm(jnp.concatenate([s_new[h] * r_ref[:, hs[h]] for h in heads], axis=0), ones)
    for h in heads:
        y_blk = yb[h * HEAD_DIM:(h + 1) * HEAD_DIM, 0:HEAD_DIM]
        y_ref[:, hs[h]] = jnp.sum(jnp.where(eye, y_blk, 0.0), axis=0, keepdims=True)


def _rwkv_dec_back_kernel(y_ref, r_ref, k_ref, v_ref, g_ref, rk_ref, lnw_ref, lnb_ref, bd_ref, o_ref):
    back = (rk_ref, lnw_ref, lnb_ref, bd_ref)
    o_ref[...] = _rwkv_back(y_ref[...], r_ref[...], k_ref[...], v_ref[...], g_ref[...], back).astype(BF16)


def _rwkv_decode(rw, shift0, wkv0, front_params, back_params):
    b = rw.shape[0]
    vm = pl.BlockSpec(memory_space=pltpu.VMEM)
    wide = jax.ShapeDtypeStruct((b, RWKV_WIDTH), F32)
    r, w, k, v, kk, bb, g = pl.pallas_call(
        _rwkv_dec_front_kernel,
        in_specs=[vm] * (2 + len(front_params)),
        out_specs=[vm] * 7,
        out_shape=[wide] * 7,
        name="rwkv_dec_front",
    )(rw, shift0, *front_params)
    assert b % DEC_ROWS == 0
    rowspec = pl.BlockSpec((DEC_ROWS, 1, RWKV_WIDTH), lambda i: (i, 0, 0))
    stspec = pl.BlockSpec((DEC_ROWS, RWKV_HEADS, HEAD_DIM, HEAD_DIM), lambda i: (i, 0, 0, 0))
    allrows = pl.BlockSpec((b, RWKV_WIDTH), lambda i: (0, 0))
    y, s_new = pl.pallas_call(
        _rwkv_dec_state_kernel,
        grid=(b // DEC_ROWS,),
        in_specs=[allrows] * 6 + [stspec],
        out_specs=[rowspec, stspec],
        out_shape=[jax.ShapeDtypeStruct((b, 1, RWKV_WIDTH), F32),
                   jax.ShapeDtypeStruct((b, RWKV_HEADS, HEAD_DIM, HEAD_DIM), F32)],
        compiler_params=_cparams(("arbitrary",)),
        name="rwkv_dec_state",
    )(r, w, k, v, kk, bb, wkv0)
    o = pl.pallas_call(
        _rwkv_dec_back_kernel,
        in_specs=[vm] * (5 + len(back_params)),
        out_specs=vm,
        out_shape=jax.ShapeDtypeStruct((b, RWKV_WIDTH), BF16),
        name="rwkv_dec_back",
    )(y.reshape(b, RWKV_WIDTH), r, k, v, g, *back_params)
    return o, s_new


def kernel(x_prompt, x_sample, cache_k, cache_v, state_wkv, state_shift, rel_bias, ffn1_norm, ffn1_w_gate, ffn1_w_up, ffn1_w_down, mix_norm, w_in, attn_sinks, shift_mu, decay_w0, decay_w2, aaa_a0, aaa_a2, gate_g2, key_k, key_a, bonus_r_k, ln_x_w, ln_x_b, w_out, ffn2_norm, ffn2_w_gate, ffn2_w_up, ffn2_w_down, final_norm):
    batch, seq, _ = x_prompt.shape
    dec_b = x_sample.shape[0]
    lw = cache_k.shape[2]
    l = 0
    row = lambda t: t.reshape(1, -1)

    w_in_b = w_in[l].astype(BF16)
    lora_w = jnp.zeros((LANES, 2 * RWKV_WIDTH), F32)
    lora_w = lora_w.at[0:DECAY_LORA, 0:RWKV_WIDTH].set(decay_w2[l])
    lora_w = lora_w.at[DECAY_LORA:LANES, RWKV_WIDTH:].set(aaa_a2[l])
    head_of = np.arange(LANES) // HEAD_DIM
    bd_ones = jnp.asarray(head_of[:, None] == head_of[None, :], BF16)
    front_params = (row(shift_mu[l]), row(decay_w0[l]), row(aaa_a0[l]), row(key_k[l]), row(key_a[l]),
                    lora_w.astype(BF16), gate_g2[l].astype(BF16), bd_ones)
    back_params = (row(bonus_r_k[l]), row(ln_x_w[l]), row(ln_x_b[l]), bd_ones)

    bias = _bias_table(rel_bias)
    sinks = attn_sinks[l]

    xs = x_sample.reshape(dec_b, D_MODEL)
    x1s, h1s, *f1 = _ffn(xs, row(ffn1_norm[l]), ffn1_w_gate[l], ffn1_w_up[l], ffn1_w_down[l], row(mix_norm[l]),
                         tm=dec_b, tf=FF_TILE_CAST, emit_x=True, n_dtype=BF16, emit_w=True)
    qs, kvs, rws = _proj(h1s, w_in_b, tm=dec_b)
    sinks_b = jnp.broadcast_to(sinks[:, None], (N_Q_HEADS, LANES))
    o_att_s = _attn_decode(qs, kvs, cache_k[l], cache_v[l], bias[1, :, 0, :], sinks_b)
    o_rw_s, s_s = _rwkv_decode(rws, state_shift[l], state_wkv[l], front_params, back_params)
    x2s, w_out_b = _outproj(x1s, o_att_s.astype(BF16), o_rw_s, w_out[l], tm=dec_b, emit_w=True)
    y_s, *f2 = _ffn(x2s, row(ffn2_norm[l]), ffn2_w_gate[l], ffn2_w_up[l], ffn2_w_down[l], row(final_norm),
                    tm=dec_b, tf=FF_TILE_CAST, emit_x=False, n_dtype=F32, emit_w=True)

    xp = x_prompt.reshape(batch * seq, D_MODEL)
    x1, h1 = _ffn_stream(xp, row(ffn1_norm[l]), *f1, row(mix_norm[l]), tm=512, emit_x=True, n_dtype=BF16)
    q, kv, rw = _proj(h1, w_in_b, tm=256)
    o_att = _attn_prompt(q, kv, bias, sinks, batch=batch, seq=seq)
    o_rw, s_p = _rwkv_prompt(rw, front_params + back_params[:3], batch=batch, seq=seq)
    (x2,) = _outproj(x1, o_att, o_rw, w_out_b, tm=512)
    (y_p,) = _ffn_stream(x2, row(ffn2_norm[l]), *f2, row(final_norm), tm=512, emit_x=False, n_dtype=F32)

    kv3 = kv.reshape(batch, seq, 2 * KV_WIDTH)
    lp = min(WINDOW, seq)
    new_k_p = kv3[:, seq - lp:, 0:KV_WIDTH].reshape(1, batch, lp, N_KV_HEADS, HEAD_DIM)
    new_v_p = kv3[:, seq - lp:, KV_WIDTH:].reshape(1, batch, lp, N_KV_HEADS, HEAD_DIM)
    new_shift_p = rw.reshape(batch, seq, SHIFT_COLS)[:, seq - 1][None]
    k_new = kvs[:, 0:KV_WIDTH].reshape(dec_b, 1, N_KV_HEADS, HEAD_DIM)
    v_new = kvs[:, KV_WIDTH:].reshape(dec_b, 1, N_KV_HEADS, HEAD_DIM)
    new_k_s = jnp.concatenate([cache_k[l], k_new], axis=1)[:, -lw:][None]
    new_v_s = jnp.concatenate([cache_v[l], v_new], axis=1)[:, -lw:][None]
    return (y_p.reshape(batch, seq, D_MODEL), y_s.reshape(dec_b, 1, D_MODEL),
            new_k_p, new_v_p, s_p[None], new_shift_p,
            new_k_s, new_v_s, s_s[None], rws[None])
```

```python
import functools
import math

import numpy as np
import jax
import jax.numpy as jnp
from jax import lax
from jax.experimental import pallas as pl
from jax.experimental.pallas import tpu as pltpu

F32 = jnp.float32
BF16 = jnp.bfloat16

D_MODEL = 2048
HEAD_DIM = 64
ATT_WIDTH = 1024
N_Q_HEADS = 16
N_KV_HEADS = 4
GQA_GROUP = 4
KV_WIDTH = 256
RWKV_WIDTH = 1024
RWKV_HEADS = 16
WINDOW = 128
BLOCK = 128
N_BUCKETS = 32
MAX_DISTANCE = 128
DECAY_LORA = 64
AAA_LORA = 64
GATE_LORA = 128
D_FF = 5504
ATT_COLS = ATT_WIDTH + 2 * KV_WIDTH
SHIFT_COLS = 3 * RWKV_WIDTH + DECAY_LORA + AAA_LORA + GATE_LORA
IN_COLS = ATT_COLS + SHIFT_COLS
RMS_EPS = 1e-5
GN_EPS = 64e-5
FFN_RES = 0.5

LANES = 128
VMEM_LIMIT_BYTES = 60 * 1024 * 1024

FF_TILE = 1024
FF_TILE_CAST = 512
CHUNK = 64
CHUNKS_PER_STEP = 4
ATT_BLOCKS_PER_STEP = 16
HEAD_PAIRS = RWKV_HEADS // 2
DEC_ROWS = 8
CHUNK_GROUP = 2
NEG_BIG = -1e30
DECAY_SCALE = math.exp(-0.5)

_NN = (((1,), (0,)), ((), ()))
_NT = (((1,), (1,)), ((), ()))
_TN = (((0,), (0,)), ((), ()))


def _dg(a, b, dims=_NN):
    return lax.dot_general(a, b, dims, preferred_element_type=F32)


def _mm(a, b, dims=_NN):
    return _dg(a.astype(BF16), b.astype(BF16), dims)


def _split2(x):
    hi = x.astype(BF16)
    lo = (x - hi.astype(F32)).astype(BF16)
    return hi, lo


def _mm2r(a_bf16, b, dims=_NN):
    bh, bl = _split2(b)
    return _dg(a_bf16, bh, dims) + _dg(a_bf16, bl, dims)


def _sigmoid(x):
    return 1.0 / (1.0 + jnp.exp(-x))


def _rms(x, g):
    ms = jnp.mean(x * x, axis=-1, keepdims=True)
    return x * lax.rsqrt(ms + RMS_EPS) * g


def _cparams(sem):
    return pltpu.CompilerParams(dimension_semantics=sem, vmem_limit_bytes=VMEM_LIMIT_BYTES)


def _swiglu_down(xn, wg, wu, wd):
    gate = jnp.dot(xn, wg, preferred_element_type=F32)
    up = jnp.dot(xn, wu, preferred_element_type=F32)
    h = (gate * _sigmoid(gate) * up).astype(BF16)
    return jnp.dot(h, wd, preferred_element_type=F32)


_FF_TAIL_STEP = 2


def _ff_block(j, *, steps):
    return jnp.where(j < _FF_TAIL_STEP, j, jnp.where(j == _FF_TAIL_STEP, steps - 1, j - 1))


def _ffn_kernel(x_ref, g_ref, wg_ref, wu_ref, wd_ref, g2_ref, *rest, emit_x, n_dtype, tf, emit_w):
    rest = list(rest)
    ox_ref = rest.pop(0) if emit_x else None
    on_ref = rest.pop(0)
    wgo_ref, wuo_ref, wdo_ref = (rest.pop(0), rest.pop(0), rest.pop(0)) if emit_w else (None, None, None)
    xn_scr, acc_scr = rest
    j = pl.program_id(1)

    @pl.when(j == 0)
    def _():
        xn_scr[...] = _rms(x_ref[...], g_ref[...]).astype(BF16)
        acc_scr[...] = jnp.zeros_like(acc_scr)

    steps = pl.cdiv(D_FF, tf)
    last = j == steps - 1
    tail = j == _FF_TAIL_STEP

    def accumulate(width):
        wg = wg_ref[:, 0:width].astype(BF16)
        wu = wu_ref[:, 0:width].astype(BF16)
        wd = wd_ref[0:width, :].astype(BF16)
        if emit_w:
            wgo_ref[:, 0:width] = wg
            wuo_ref[:, 0:width] = wu
            wdo_ref[0:width, :] = wd
        acc_scr[...] += _swiglu_down(xn_scr[...], wg, wu, wd)

    @pl.when(jnp.logical_not(tail))
    def _():
        accumulate(tf)

    @pl.when(tail)
    def _():
        accumulate(D_FF - (steps - 1) * tf)

    @pl.when(last)
    def _():
        y = x_ref[...] + FFN_RES * acc_scr[...]
        if emit_x:
            ox_ref[...] = y
        on_ref[...] = _rms(y, g2_ref[...]).astype(n_dtype)


def _ffn(x, g, wg, wu, wd, g2, *, tm, tf, emit_x, n_dtype, emit_w=False):
    m = x.shape[0]
    assert not emit_w or m == tm
    grid = (m // tm, pl.cdiv(D_FF, tf))
    row = pl.BlockSpec((tm, D_MODEL), lambda i, j: (i, 0))
    vec = pl.BlockSpec((1, D_MODEL), lambda i, j: (0, 0))
    steps = pl.cdiv(D_FF, tf)
    blk = functools.partial(_ff_block, steps=steps)
    w_in = pl.BlockSpec((D_MODEL, tf), lambda i, j: (0, blk(j)))
    w_dn = pl.BlockSpec((tf, D_MODEL), lambda i, j: (blk(j), 0))
    out_shape = [jax.ShapeDtypeStruct((m, D_MODEL), n_dtype)]
    out_specs = [row]
    if emit_x:
        out_shape = [jax.ShapeDtypeStruct((m, D_MODEL), F32)] + out_shape
        out_specs = [row, row]
    if emit_w:
        out_shape += [jax.ShapeDtypeStruct(w.shape, BF16) for w in (wg, wu, wd)]
        out_specs += [w_in, w_in, w_dn]
    return pl.pallas_call(
        functools.partial(_ffn_kernel, emit_x=emit_x, n_dtype=n_dtype, tf=tf, emit_w=emit_w),
        grid=grid,
        in_specs=[row, vec, w_in, w_in, w_dn, vec],
        out_specs=out_specs,
        out_shape=out_shape,
        scratch_shapes=[pltpu.VMEM((tm, D_MODEL), BF16), pltpu.VMEM((tm, D_MODEL), F32)],
        compiler_params=_cparams(("arbitrary", "arbitrary")),
        name="ffn_cast" if emit_w else "ffn",
    )(x, g, wg, wu, wd, g2)


def _ff_sweep():
    steps = pl.cdiv(D_FF, FF_TILE)
    order = list(range(_FF_TAIL_STEP)) + [steps - 1] + list(range(_FF_TAIL_STEP, steps - 1))
    return [(b, min(FF_TILE, D_FF - b * FF_TILE)) for b in order]


def _ffn_stream_kernel(x_ref, g_ref, wg_hbm, wu_hbm, wd_hbm, g2_ref, *rest, emit_x, n_dtype):
    rest = list(rest)
    ox_ref = rest.pop(0) if emit_x else None
    on_ref = rest.pop(0)
    wg_buf, wu_buf, wd_buf, acc_scr, sem = rest
    i = pl.program_id(0)
    sweep = _ff_sweep()
    assert len(sweep) % 2 == 0

    def copies(pos):
        blk, width = sweep[pos]
        slot = pos % 2
        cols = pl.ds(blk * FF_TILE, width)
        return (pltpu.make_async_copy(wg_hbm.at[:, cols], wg_buf.at[slot, :, pl.ds(0, width)], sem.at[0, slot]),
                pltpu.make_async_copy(wu_hbm.at[:, cols], wu_buf.at[slot, :, pl.ds(0, width)], sem.at[1, slot]),
                pltpu.make_async_copy(wd_hbm.at[cols, :], wd_buf.at[slot, pl.ds(0, width), :], sem.at[2, slot]))

    def start(pos):
        for cp in copies(pos):
            cp.start()

    @pl.when(i == 0)
    def _():
        start(0)

    for pos, (_, width) in enumerate(sweep):
        slot = pos % 2
        for cp in copies(pos):
            cp.wait()
        if pos + 1 < len(sweep):
            start(pos + 1)
        else:
            @pl.when(i + 1 < pl.num_programs(0))
            def _():
                start(0)
        if pos == 0:
            xn = _rms(x_ref[...], g_ref[...]).astype(BF16)
        part = _swiglu_down(xn, wg_buf[slot, :, 0:width], wu_buf[slot, :, 0:width], wd_buf[slot, 0:width, :])
        if pos == 0:
            acc_scr[...] = part
        else:
            acc_scr[...] += part

    y = x_ref[...] + FFN_RES * acc_scr[...]
    if emit_x:
        ox_ref[...] = y
    on_ref[...] = _rms(y, g2_ref[...]).astype(n_dtype)


def _ffn_stream(x, g, wg, wu, wd, g2, *, tm, emit_x, n_dtype):
    m = x.shape[0]
    row = pl.BlockSpec((tm, D_MODEL), lambda i: (i, 0))
    vec = pl.BlockSpec((1, D_MODEL), lambda i: (0, 0))
    hbm = pl.BlockSpec(memory_space=pl.ANY)
    out_shape = [jax.ShapeDtypeStruct((m, D_MODEL), n_dtype)]
    out_specs = [row]
    if emit_x:
        out_shape = [jax.ShapeDtypeStruct((m, D_MODEL), F32)] + out_shape
        out_specs = [row, row]
    return pl.pallas_call(
        functools.partial(_ffn_stream_kernel, emit_x=emit_x, n_dtype=n_dtype),
        grid=(m // tm,),
        in_specs=[row, vec, hbm, hbm, hbm, vec],
        out_specs=out_specs,
        out_shape=out_shape,
        scratch_shapes=[pltpu.VMEM((2, D_MODEL, FF_TILE), BF16), pltpu.VMEM((2, D_MODEL, FF_TILE), BF16),
                        pltpu.VMEM((2, FF_TILE, D_MODEL), BF16), pltpu.VMEM((tm, D_MODEL), F32),
                        pltpu.SemaphoreType.DMA((3, 2))],
        compiler_params=_cparams(("arbitrary",)),
        name="ffn",
    )(x, g, wg, wu, wd, g2)


def _proj_kernel(h_ref, w_ref, q_ref, kv_ref, rw_ref):
    h = h_ref[...]
    q = jnp.dot(h, w_ref[:, 0:ATT_WIDTH], preferred_element_type=F32)
    q_ref[...] = (q * (HEAD_DIM ** -0.5)).astype(BF16)
    kv_ref[...] = jnp.dot(h, w_ref[:, ATT_WIDTH:ATT_COLS], preferred_element_type=F32)
    rw_ref[...] = jnp.dot(h, w_ref[:, ATT_COLS:IN_COLS], preferred_element_type=F32)


def _proj(h, w, *, tm):
    m = h.shape[0]
    return pl.pallas_call(
        _proj_kernel,
        grid=(m // tm,),
        in_specs=[pl.BlockSpec((tm, D_MODEL), lambda i: (i, 0)),
                  pl.BlockSpec((D_MODEL, IN_COLS), lambda i: (0, 0), pipeline_mode=pl.Buffered(1))],
        out_specs=[pl.BlockSpec((tm, ATT_WIDTH), lambda i: (i, 0)),
                   pl.BlockSpec((tm, 2 * KV_WIDTH), lambda i: (i, 0)),
                   pl.BlockSpec((tm, SHIFT_COLS), lambda i: (i, 0))],
        out_shape=[jax.ShapeDtypeStruct((m, ATT_WIDTH), BF16),
                   jax.ShapeDtypeStruct((m, 2 * KV_WIDTH), F32),
                   jax.ShapeDtypeStruct((m, SHIFT_COLS), F32)],
        compiler_params=_cparams(("arbitrary",)),
        name="in_proj",
    )(h, w)


def _outproj_kernel(x_ref, oa_ref, orw_ref, w_ref, o_ref, *wo_ref):
    w_att = w_ref[0:ATT_WIDTH, :].astype(BF16)
    w_rw = w_ref[ATT_WIDTH:D_MODEL, :].astype(BF16)
    if wo_ref:
        wo_ref[0][0:ATT_WIDTH, :] = w_att
        wo_ref[0][ATT_WIDTH:D_MODEL, :] = w_rw
    acc = jnp.dot(oa_ref[...], w_att, preferred_element_type=F32)
    acc += jnp.dot(orw_ref[...], w_rw, preferred_element_type=F32)
    o_ref[...] = x_ref[...] + acc


def _outproj(x, oa, orw, w, *, tm, emit_w=False):
    m = x.shape[0]
    assert not emit_w or m == tm
    row = pl.BlockSpec((tm, D_MODEL), lambda i: (i, 0))
    half = pl.BlockSpec((tm, ATT_WIDTH), lambda i: (i, 0))
    full = pl.BlockSpec((D_MODEL, D_MODEL), lambda i: (0, 0), pipeline_mode=pl.Buffered(1))
    out_shape = [jax.ShapeDtypeStruct((m, D_MODEL), F32)]
    out_specs = [row]
    if emit_w:
        out_shape.append(jax.ShapeDtypeStruct((D_MODEL, D_MODEL), BF16))
        out_specs.append(pl.BlockSpec((D_MODEL, D_MODEL), lambda i: (0, 0)))
    return pl.pallas_call(
        _outproj_kernel,
        grid=(m // tm,),
        in_specs=[row, half, half, full],
        out_specs=out_specs,
        out_shape=out_shape,
        compiler_params=_cparams(("arbitrary",)),
        name="out_proj_cast" if emit_w else "out_proj",
    )(x, oa, orw, w)


def _bucket_table():
    qi = np.arange(BLOCK)[:, None]
    kj = np.arange(2 * BLOCK)[None, :]
    dist = BLOCK + qi - kj
    n = np.maximum(dist, 0)
    max_exact = N_BUCKETS // 2
    nf = np.maximum(n, 1).astype(np.float32)
    large = max_exact + (np.log(nf / np.float32(max_exact)) / np.float32(math.log(MAX_DISTANCE / max_exact))
                         * np.float32(N_BUCKETS - max_exact)).astype(np.int32)
    large = np.minimum(large, N_BUCKETS - 1)
    bucket = np.where(n < max_exact, n, large).astype(np.int32)
    valid = (dist >= 0) & (dist <= WINDOW)
    return np.where(valid, bucket, -1).astype(np.int32)


def _bias_kernel(bucket_ref, rb_ref, o_ref):
    bucket = bucket_ref[...]
    kj = lax.broadcasted_iota(jnp.int32, (BLOCK, 2 * BLOCK), 1)
    for h in range(N_Q_HEADS):
        acc = jnp.full((BLOCK, 2 * BLOCK), NEG_BIG, F32)
        for n in range(N_BUCKETS):
            acc = jnp.where(bucket == n, rb_ref[n, h], acc)
        o_ref[0, h] = jnp.where(kj >= BLOCK, acc, NEG_BIG)
        o_ref[1, h] = acc


def _bias_table(rel_bias):
    bucket = jnp.asarray(_bucket_table())
    return pl.pallas_call(
        _bias_kernel,
        in_specs=[pl.BlockSpec(memory_space=pltpu.VMEM), pl.BlockSpec(memory_space=pltpu.SMEM)],
        out_specs=pl.BlockSpec(memory_space=pltpu.VMEM),
        out_shape=jax.ShapeDtypeStruct((2, N_Q_HEADS, BLOCK, 2 * BLOCK), F32),
        name="bias_table",
    )(bucket, rel_bias)


def _attn_kernel(sink_ref, q_ref, kvp_ref, kvc_ref, bias_ref, o_ref, kv_scr):
    j = pl.program_id(1)
    kv_scr[0:BLOCK, :] = kvp_ref[...].astype(BF16)
    kv_scr[BLOCK:, :] = kvc_ref[...].astype(BF16)

    def block(i, carry):
        r0 = pl.multiple_of(i * BLOCK, BLOCK)
        rows = pl.ds(r0, BLOCK)
        table = jnp.where(jnp.logical_and(j == 0, i == 0), 0, 1)
        _attn_block(sink_ref, q_ref.at[rows], kv_scr.at[pl.ds(r0, 2 * BLOCK)], bias_ref.at[table], o_ref.at[rows])
        return carry

    lax.fori_loop(0, ATT_BLOCKS_PER_STEP, block, 0)


def _attn_block(sink_ref, q_ref, kv_ref, bias_ref, o_ref):
    half = N_KV_HEADS // 2
    for g0 in range(0, N_KV_HEADS, half):
        kcat, vcat = {}, {}
        for g in range(g0, g0 + half):
            kcat[g] = kv_ref[:, g * HEAD_DIM:(g + 1) * HEAD_DIM]
            vcat[g] = kv_ref[:, KV_WIDTH + g * HEAD_DIM:KV_WIDTH + (g + 1) * HEAD_DIM]
        heads = range(g0 * GQA_GROUP, (g0 + half) * GQA_GROUP)
        hs = {h: slice(h * HEAD_DIM, (h + 1) * HEAD_DIM) for h in heads}
        s = {h: _dg(q_ref[:, hs[h]], kcat[h // GQA_GROUP], _NT) + bias_ref[h] for h in heads}
        m = {h: jnp.maximum(jnp.max(s[h], axis=-1, keepdims=True), sink_ref[h]) for h in heads}
        p = {h: jnp.exp(s[h] - m[h]) for h in heads}
        denom = {h: jnp.sum(p[h], axis=-1, keepdims=True) + jnp.exp(sink_ref[h] - m[h]) for h in heads}
        o = {h: jnp.dot(p[h].astype(BF16), vcat[h // GQA_GROUP], preferred_element_type=F32) for h in heads}
        for h in heads:
            o_ref[:, hs[h]] = (o[h] / denom[h]).astype(BF16)


def _attn_prompt(q, kv, bias, sinks, *, batch, seq):
    n = ATT_BLOCKS_PER_STEP
    rows = n * BLOCK
    q3 = q.reshape(batch, seq, ATT_WIDTH)
    kv3 = kv.reshape(batch, seq, 2 * KV_WIDTH)
    out = pl.pallas_call(
        _attn_kernel,
        grid=(batch, seq // rows),
        in_specs=[pl.BlockSpec(memory_space=pltpu.SMEM),
                  pl.BlockSpec((None, rows, ATT_WIDTH), lambda b, j: (b, j, 0)),
                  pl.BlockSpec((None, BLOCK, 2 * KV_WIDTH), lambda b, j: (b, jnp.maximum(n * j - 1, 0), 0)),
                  pl.BlockSpec((None, rows, 2 * KV_WIDTH), lambda b, j: (b, j, 0)),
                  pl.BlockSpec((2, N_Q_HEADS, BLOCK, 2 * BLOCK), lambda b, j: (0, 0, 0, 0))],
        out_specs=pl.BlockSpec((None, rows, ATT_WIDTH), lambda b, j: (b, j, 0)),
        out_shape=jax.ShapeDtypeStruct((batch, seq, ATT_WIDTH), BF16),
        scratch_shapes=[pltpu.VMEM((rows + BLOCK, 2 * KV_WIDTH), BF16)],
        compiler_params=_cparams(("arbitrary", "arbitrary")),
        name="attn_prompt",
    )(sinks, q3, kv3, kv3, bias)
    return out.reshape(batch * seq, ATT_WIDTH)


def _attn_decode_kernel(sink_ref, q_ref, kvn_ref, ck_ref, cv_ref, bias_ref, o_ref):
    q = q_ref[...].astype(F32)
    kvn = kvn_ref[...]
    for g in range(N_KV_HEADS):
        ks = slice(g * HEAD_DIM, (g + 1) * HEAD_DIM)
        vs = slice(KV_WIDTH + g * HEAD_DIM, KV_WIDTH + (g + 1) * HEAD_DIM)
        hs = slice(g * GQA_GROUP, (g + 1) * GQA_GROUP)
        qg = q[:, hs, :]
        kc = ck_ref[:, :, ks]
        vc = cv_ref[:, :, ks]
        kn = kvn[:, ks].astype(BF16).astype(F32)[:, None, :]
        vn = kvn[:, vs].astype(BF16).astype(F32)[:, None, :]
        s = jnp.einsum("bqd,bkd->bqk", qg.astype(BF16), kc.astype(BF16), preferred_element_type=F32)
        s = s + bias_ref[hs, 0:WINDOW][None]
        s_new = jnp.sum(qg * kn, axis=-1, keepdims=True) + bias_ref[hs, WINDOW:WINDOW + 1][None]
        sink = sink_ref[hs, 0:1][None]
        m = jnp.maximum(jnp.maximum(jnp.max(s, axis=-1, keepdims=True), s_new), sink)
        p = jnp.exp(s - m)
        p_new = jnp.exp(s_new - m)
        denom = jnp.sum(p, axis=-1, keepdims=True) + p_new + jnp.exp(sink - m)
        o = jnp.einsum("bqk,bkd->bqd", p.astype(BF16), vc.astype(BF16), preferred_element_type=F32)
        o = o + p_new * vn
        o_ref[:, hs, :] = o / denom


def _attn_decode(q, kvn, cache_k, cache_v, bias_row, sinks_b):
    b = q.shape[0]
    lw = cache_k.shape[1]
    vm = pl.BlockSpec(memory_space=pltpu.VMEM)
    out = pl.pallas_call(
        _attn_decode_kernel,
        in_specs=[vm, vm, vm, vm, vm, vm],
        out_specs=vm,
        out_shape=jax.ShapeDtypeStruct((b, N_Q_HEADS, HEAD_DIM), F32),
        compiler_params=pltpu.CompilerParams(vmem_limit_bytes=VMEM_LIMIT_BYTES),
        name="attn_decode",
    )(sinks_b, q.reshape(b, N_Q_HEADS, HEAD_DIM), kvn,
      cache_k.reshape(b, lw, KV_WIDTH), cache_v.reshape(b, lw, KV_WIDTH), bias_row)
    return out.reshape(b, ATT_WIDTH)


def _seg_expand(x, bd_ref):
    rows = x.shape[0]
    groups = x.shape[1] // LANES
    xs = jnp.concatenate([x[:, i * LANES:(i + 1) * LANES] for i in range(groups)], axis=0)
    s = _mm(xs, bd_ref[...])
    return jnp.concatenate([s[i * rows:(i + 1) * rows] for i in range(groups)], axis=1)


def _rwkv_front(x, prev, p):
    mu_ref, w0_ref, a0_ref, kk_ref, ka_ref, lw_ref, g2_ref, bd_ref = p
    xm = x + mu_ref[...] * (prev - x)
    o3 = 3 * RWKV_WIDTH
    r = xm[:, 0:RWKV_WIDTH]
    k = xm[:, RWKV_WIDTH:2 * RWKV_WIDTH]
    v = xm[:, 2 * RWKV_WIDTH:o3]
    wa = xm[:, o3:o3 + LANES]
    lane = lax.broadcasted_iota(jnp.int32, wa.shape, 1)
    z = jnp.where(lane < DECAY_LORA, jnp.tanh(wa), wa)
    lora = _mm(z, lw_ref[...])
    logw = -DECAY_SCALE * _sigmoid(w0_ref[...] + lora[:, 0:RWKV_WIDTH])
    a = _sigmoid(a0_ref[...] + lora[:, RWKV_WIDTH:2 * RWKV_WIDTH])
    g = _mm(_sigmoid(xm[:, o3 + LANES:SHIFT_COLS]), g2_ref[...])
    kk = k * kk_ref[...]
    kmod = k * (1.0 + (a - 1.0) * ka_ref[...])
    nrm = jnp.sqrt(_seg_expand(kk * kk, bd_ref))
    kk = kk / jnp.maximum(nrm, 1e-12)
    return r, logw, kmod, v, kk, kk * a, g


def _rwkv_back(y, r, kmod, v, g, q):
    rk_ref, lnw_ref, lnb_ref, bd_ref = q
    mean = _seg_expand(y, bd_ref) * (1.0 / HEAD_DIM)
    d = y - mean
    var = _seg_expand(d * d, bd_ref) * (1.0 / HEAD_DIM)
    yn = d * lax.rsqrt(var + GN_EPS) * lnw_ref[...] + lnb_ref[...]
    bonus = _seg_expand(r * kmod * rk_ref[...], bd_ref) * v
    return (yn + bonus) * g


def _rwkv_chunks(rw_refs, mu_ref, w0_ref, a0_ref, kk_ref, ka_ref, lw_ref, g2_ref, bd_ref,
                 rk_ref, lnw_ref, lnb_ref,
                 o_refs, s_scr, prev_scr, works):
    C = CHUNK
    chunks = range(len(rw_refs))
    front = (mu_ref, w0_ref, a0_ref, kk_ref, ka_ref, lw_ref, g2_ref, bd_ref)
    ti = lax.broadcasted_iota(jnp.int32, (C, C), 0)
    si = lax.broadcasted_iota(jnp.int32, (C, C), 1)
    tri = jnp.where(si <= ti, 1.0, 0.0).astype(BF16)
    tail, w_end = {}, {}
    for c in chunks:
        y_scr, ea_scr, er_scr, eb_scr, ek_scr, be_scr, ke_scr, v_scr = works[c]
        x = rw_refs[c][...]
        row = lax.broadcasted_iota(jnp.int32, x.shape, 0)
        before = prev_scr[...] if c == 0 else rw_refs[c - 1][C - 1:C, :]
        prev = jnp.where(row == 0, before, pltpu.roll(x, 1, axis=0))
        r, logw, kmod, v, kk, bb, g = _rwkv_front(x, prev, front)
        cum = _mm2r(tri, logw)
        cum_end = cum[C - 1:C, :]
        e_in = jnp.exp(cum)
        e_neg = jnp.exp(-cum)
        e_end = jnp.exp(cum_end - cum)
        ea_scr[...] = kk * jnp.exp(cum - logw)
        er_scr[...] = r * e_in
        eb_scr[...] = bb * e_neg
        ek_scr[...] = kmod * e_neg
        be_scr[...] = bb * e_end
        ke_scr[...] = kmod * e_end
        v_scr[...] = v
        w_end[c] = jnp.exp(cum_end)
        tail[c] = (r, kmod, v, g)
    prev_scr[...] = rw_refs[-1][C - 1:C, :]

    lane = lax.broadcasted_iota(jnp.int32, (C, LANES), 1)
    lo_half = lane < HEAD_DIM
    ri = lax.broadcasted_iota(jnp.int32, (LANES, LANES), 0)
    ci = lax.broadcasted_iota(jnp.int32, (LANES, LANES), 1)
    same_head = jnp.right_shift(ri, 6) == jnp.right_shift(ci, 6)
    eye = ri == ci
    t_loc = jnp.bitwise_and(ri, C - 1)
    s_loc = jnp.bitwise_and(ci, C - 1)
    bd_strict = jnp.logical_and(same_head, s_loc < t_loc)
    bd_incl = jnp.logical_and(same_head, s_loc <= t_loc)

    def swap(t):
        return pltpu.roll(t, HEAD_DIM, axis=1)

    def stack2(t):
        return jnp.concatenate([jnp.where(lo_half, t, 0.0), jnp.where(lo_half, 0.0, t)], axis=0)

    def stack2x(t):
        ts = swap(t)
        return jnp.concatenate([jnp.where(lo_half, 0.0, ts), jnp.where(lo_half, ts, 0.0)], axis=0)

    def own(t):
        return jnp.where(lo_half, t[0:C], t[C:2 * C])

    def other(t):
        return swap(jnp.where(lo_half, t[C:2 * C], t[0:C]))

    sl = [slice(p * LANES, (p + 1) * LANES) for p in range(HEAD_PAIRS)]
    pairs = range(HEAD_PAIRS)
    items = [(c, p) for c in chunks for p in pairs]
    col = lambda c, k, p: works[c][k][:, sl[p]]
    a2 = {(c, p): stack2(col(c, 1, p)) for c, p in items}
    m = {}
    for c, p in items:
        r2 = stack2(col(c, 2, p))
        b_p = col(c, 3, p)
        k_p = col(c, 4, p)
        m[c, p] = _mm(jnp.concatenate([a2[c, p], r2], axis=0), jnp.concatenate([b_p, k_p], axis=0), _NT)
    ms = {i: swap(m[i]) for i in items}
    first = ri < HEAD_DIM
    l_ak = {i: jnp.where(bd_strict, jnp.where(first, ms[i][0:2 * C], m[i][0:2 * C]), 0.0) for i in items}
    pw = {i: -jnp.where(bd_strict, jnp.where(first, m[i][0:2 * C], ms[i][0:2 * C]), 0.0) for i in items}
    lrbk = {i: jnp.concatenate([jnp.where(bd_incl, jnp.where(first, m[i][2 * C:4 * C], ms[i][2 * C:4 * C]), 0.0),
                                jnp.where(bd_incl, jnp.where(first, ms[i][2 * C:4 * C], m[i][2 * C:4 * C]), 0.0)],
                               axis=1) for i in items}
    v2x = {(c, p): stack2x(col(c, 7, p)) for c, p in items}
    xc = {i: a2[i] - _mm(l_ak[i], v2x[i]) for i in items}
    for it in range(6):
        for i in items:
            pw_b = pw[i].astype(BF16)
            xc_b = xc[i].astype(BF16)
            if it < 5:
                px = _dg(pw_b, jnp.concatenate([xc_b, pw_b], axis=1))
                pw[i] = px[:, LANES:2 * LANES]
                xc[i] = xc[i] + px[:, 0:LANES]
            else:
                xc[i] = xc[i] + _dg(pw_b, xc_b)
    xv = {i: jnp.concatenate([xc[i], v2x[i]], axis=0).astype(BF16) for i in items}
    lb = {i: _mm(lrbk[i], xv[i]) for i in items}
    z = {(c, p): _mm(xv[c, p], jnp.concatenate([stack2(col(c, 5, p)), stack2(col(c, 6, p))], axis=0), _TN)
         for c, p in items}
    rt = {(c, p): col(c, 2, p) - own(lb[c, p]) for c, p in items}
    yl = {i: other(lb[i]) for i in items}
    d = {(c, p): jnp.where(eye, w_end[c][:, sl[p]], 0.0) - jnp.where(same_head, z[c, p], 0.0) for c, p in items}
    gg = {i: jnp.where(same_head, jnp.concatenate([z[i][C:2 * C], z[i][0:C]], axis=0), 0.0) for i in items}
    back = (rk_ref, lnw_ref, lnb_ref, bd_ref)
    for c in chunks:
        y_scr = works[c][0]
        s0 = {p: s_scr[p].astype(BF16) for p in pairs}
        ys = {p: _mm(rt[c, p], s0[p], _NT) for p in pairs}
        for p in pairs:
            y_scr[:, sl[p]] = yl[c, p] + ys[p]
        sn = {p: _mm(s0[p], d[c, p]) for p in pairs}
        for p in pairs:
            s_scr[p] = sn[p] + gg[c, p]
    for c in chunks:
        o_refs[c][...] = _rwkv_back(works[c][0][...], *tail[c], back).astype(BF16)


def _rwkv_chunk_kernel(rw_ref, *rest):
    params, (o_ref, s_out_ref, s_scr, prev_scr), work = rest[:11], rest[11:15], rest[15:]
    c = pl.program_id(1)

    @pl.when(c == 0)
    def _():
        s_scr[...] = jnp.zeros_like(s_scr)
        prev_scr[...] = jnp.zeros_like(prev_scr)

    for g in range(0, CHUNKS_PER_STEP, CHUNK_GROUP):
        ids = range(g, g + CHUNK_GROUP)
        rows = [pl.ds(i * CHUNK, CHUNK) for i in ids]
        _rwkv_chunks([rw_ref.at[r] for r in rows], *params, [o_ref.at[r] for r in rows], s_scr, prev_scr,
                     [[w.at[i] for w in work] for i in ids])

    @pl.when(c == pl.num_programs(1) - 1)
    def _():
        for p in range(HEAD_PAIRS):
            s_out_ref[2 * p] = s_scr[p, 0:HEAD_DIM, 0:HEAD_DIM]
            s_out_ref[2 * p + 1] = s_scr[p, HEAD_DIM:LANES, HEAD_DIM:LANES]


def _rwkv_prompt(rw, params, *, batch, seq):
    rw3 = rw.reshape(batch, seq, SHIFT_COLS)
    rows = CHUNKS_PER_STEP * CHUNK
    nc = seq // rows
    const = lambda shape: pl.BlockSpec(shape, lambda b, c: tuple(0 for _ in shape))
    in_specs = [pl.BlockSpec((None, rows, SHIFT_COLS), lambda b, c: (b, c, 0))]
    in_specs += [const(p.shape) for p in params]
    wide = pltpu.VMEM((CHUNKS_PER_STEP, CHUNK, RWKV_WIDTH), F32)
    o, s_out = pl.pallas_call(
        _rwkv_chunk_kernel,
        grid=(batch, nc),
        in_specs=in_specs,
        out_specs=[pl.BlockSpec((None, rows, RWKV_WIDTH), lambda b, c: (b, c, 0)),
                   pl.BlockSpec((None, RWKV_HEADS, HEAD_DIM, HEAD_DIM), lambda b, c: (b, 0, 0, 0))],
        out_shape=[jax.ShapeDtypeStruct((batch, seq, RWKV_WIDTH), BF16),
                   jax.ShapeDtypeStruct((batch, RWKV_HEADS, HEAD_DIM, HEAD_DIM), F32)],
        scratch_shapes=[pltpu.VMEM((HEAD_PAIRS, LANES, LANES), F32),
                        pltpu.VMEM((1, SHIFT_COLS), F32),
                        wide, wide, wide, wide, wide, wide, wide, wide],
        compiler_params=_cparams(("arbitrary", "arbitrary")),
        name="rwkv_prompt",
    )(rw3, *params)
    return o.reshape(batch * seq, RWKV_WIDTH), s_out


def _rwkv_dec_front_kernel(rw_ref, prev_ref, mu_ref, w0_ref, a0_ref, kk_ref, ka_ref, lw_ref, g2_ref, bd_ref,
                           r_ref, w_ref, k_ref, v_ref, kkn_ref, b_ref, g_ref):
    front = (mu_ref, w0_ref, a0_ref, kk_ref, ka_ref, lw_ref, g2_ref, bd_ref)
    r, logw, kmod, v, kk, bb, g = _rwkv_front(rw_ref[...], prev_ref[...], front)
    r_ref[...] = r
    w_ref[...] = jnp.exp(logw)
    k_ref[...] = kmod
    v_ref[...] = v
    kkn_ref[...] = kk
    b_ref[...] = bb
    g_ref[...] = g


def _rwkv_dec_state_kernel(r_all, w_all, k_all, v_all, kk_all, b_all, s_blk, y_blk_ref, so_blk):
    def one(j, carry):
        row = pl.ds(pl.program_id(0) * DEC_ROWS + j, 1)
        vecs = [t.at[row] for t in (r_all, w_all, k_all, v_all, kk_all, b_all)]
        _rwkv_dec_state_one(*vecs, s_blk.at[j], y_blk_ref.at[j], so_blk.at[j])
        return carry

    lax.fori_loop(0, DEC_ROWS, one, 0)


def _rwkv_dec_state_one(r_ref, w_ref, k_ref, v_ref, kk_ref, b_ref, s_ref, y_ref, so_ref):
    ri = lax.broadcasted_iota(jnp.int32, (HEAD_DIM, HEAD_DIM), 0)
    ci = lax.broadcasted_iota(jnp.int32, (HEAD_DIM, HEAD_DIM), 1)
    eye = ri == ci
    ones = jnp.ones((HEAD_DIM, LANES), BF16)
    heads = range(RWKV_HEADS)
    hs = [slice(h * HEAD_DIM, (h + 1) * HEAD_DIM) for h in heads]
    n = RWKV_HEADS * HEAD_DIM
    s = [s_ref[h] for h in heads]
    lhs = [s[h] * kk_ref[:, hs[h]] for h in heads] + [jnp.where(eye, v_ref[:, hs[h]], 0.0) for h in heads]
    red = _mm(jnp.concatenate(lhs, axis=0), ones)
    s_new = []
    for h in heads:
        sa = -red[h * HEAD_DIM:(h + 1) * HEAD_DIM, 0:HEAD_DIM]
        v_col = red[n + h * HEAD_DIM:n + (h + 1) * HEAD_DIM, 0:HEAD_DIM]
        s_new.append(s[h] * w_ref[:, hs[h]] + sa * b_ref[:, hs[h]] + v_col * k_ref[:, hs[h]])
        so_ref[h] = s_new[h]
    yb = _mm(jnp.concatenate([s_new[h] * r_ref[:, hs[h]] for h in heads], axis=0), ones)
    for h in heads:
        y_blk = yb[h * HEAD_DIM:(h + 1) * HEAD_DIM, 0:HEAD_DIM]
        y_ref[:, hs[h]] = jnp.sum(jnp.where(eye, y_blk, 0.0), axis=0, keepdims=True)


def _rwkv_dec_back_kernel(y_ref, r_ref, k_ref, v_ref, g_ref, rk_ref, lnw_ref, lnb_ref, bd_ref, o_ref):
    back = (rk_ref, lnw_ref, lnb_ref, bd_ref)
    o_ref[...] = _rwkv_back(y_ref[...], r_ref[...], k_ref[...], v_ref[...], g_ref[...], back).astype(BF16)


def _rwkv_decode(rw, shift0, wkv0, front_params, back_params):
    b = rw.shape[0]
    vm = pl.BlockSpec(memory_space=pltpu.VMEM)
    wide = jax.ShapeDtypeStruct((b, RWKV_WIDTH), F32)
    r, w, k, v, kk, bb, g = pl.pallas_call(
        _rwkv_dec_front_kernel,
        in_specs=[vm] * (2 + len(front_params)),
        out_specs=[vm] * 7,
        out_shape=[wide] * 7,
        name="rwkv_dec_front",
    )(rw, shift0, *front_params)
    assert b % DEC_ROWS == 0
    rowspec = pl.BlockSpec((DEC_ROWS, 1, RWKV_WIDTH), lambda i: (i, 0, 0))
    stspec = pl.BlockSpec((DEC_ROWS, RWKV_HEADS, HEAD_DIM, HEAD_DIM), lambda i: (i, 0, 0, 0))
    allrows = pl.BlockSpec((b, RWKV_WIDTH), lambda i: (0, 0))
    y, s_new = pl.pallas_call(
        _rwkv_dec_state_kernel,
        grid=(b // DEC_ROWS,),
        in_specs=[allrows] * 6 + [stspec],
        out_specs=[rowspec, stspec],
        out_shape=[jax.ShapeDtypeStruct((b, 1, RWKV_WIDTH), F32),
                   jax.ShapeDtypeStruct((b, RWKV_HEADS, HEAD_DIM, HEAD_DIM), F32)],
        compiler_params=_cparams(("arbitrary",)),
        name="rwkv_dec_state",
    )(r, w, k, v, kk, bb, wkv0)
    o = pl.pallas_call(
        _rwkv_dec_back_kernel,
        in_specs=[vm] * (5 + len(back_params)),
        out_specs=vm,
        out_shape=jax.ShapeDtypeStruct((b, RWKV_WIDTH), BF16),
        name="rwkv_dec_back",
    )(y.reshape(b, RWKV_WIDTH), r, k, v, g, *back_params)
    return o, s_new


def kernel(x_prompt, x_sample, cache_k, cache_v, state_wkv, state_shift, rel_bias, ffn1_norm, ffn1_w_gate, ffn1_w_up, ffn1_w_down, mix_norm, w_in, attn_sinks, shift_mu, decay_w0, decay_w2, aaa_a0, aaa_a2, gate_g2, key_k, key_a, bonus_r_k, ln_x_w, ln_x_b, w_out, ffn2_norm, ffn2_w_gate, ffn2_w_up, ffn2_w_down, final_norm):
    batch, seq, _ = x_prompt.shape
    dec_b = x_sample.shape[0]
    lw = cache_k.shape[2]
    l = 0
    row = lambda t: t.reshape(1, -1)

    w_in_b = w_in[l].astype(BF16)
    lora_w = jnp.zeros((LANES, 2 * RWKV_WIDTH), F32)
    lora_w = lora_w.at[0:DECAY_LORA, 0:RWKV_WIDTH].set(decay_w2[l])
    lora_w = lora_w.at[DECAY_LORA:LANES, RWKV_WIDTH:].set(aaa_a2[l])
    head_of = np.arange(LANES) // HEAD_DIM
    bd_ones = jnp.asarray(head_of[:, None] == head_of[None, :], BF16)
    front_params = (row(shift_mu[l]), row(decay_w0[l]), row(aaa_a0[l]), row(key_k[l]), row(key_a[l]),
                    lora_w.astype(BF16), gate_g2[l].astype(BF16), bd_ones)
    back_params = (row(bonus_r_k[l]), row(ln_x_w[l]), row(ln_x_b[l]), bd_ones)

    bias = _bias_table(rel_bias)
    sinks = attn_sinks[l]

    xs = x_sample.reshape(dec_b, D_MODEL)
    x1s, h1s, *f1 = _ffn(xs, row(ffn1_norm[l]), ffn1_w_gate[l], ffn1_w_up[l], ffn1_w_down[l], row(mix_norm[l]),
                         tm=dec_b, tf=FF_TILE_CAST, emit_x=True, n_dtype=BF16, emit_w=True)
    qs, kvs, rws = _proj(h1s, w_in_b, tm=dec_b)
    sinks_b = jnp.broadcast_to(sinks[:, None], (N_Q_HEADS, LANES))
    o_att_s = _attn_decode(qs, kvs, cache_k[l], cache_v[l], bias[1, :, 0, :], sinks_b)
    o_rw_s, s_s = _rwkv_decode(rws, state_shift[l], state_wkv[l], front_params, back_params)
    x2s, w_out_b = _outproj(x1s, o_att_s.astype(BF16), o_rw_s, w_out[l], tm=dec_b, emit_w=True)
    y_s, *f2 = _ffn(x2s, row(ffn2_norm[l]), ffn2_w_gate[l], ffn2_w_up[l], ffn2_w_down[l], row(final_norm),
                    tm=dec_b, tf=FF_TILE_CAST, emit_x=False, n_dtype=F32, emit_w=True)

    xp = x_prompt.reshape(batch * seq, D_MODEL)
    x1, h1 = _ffn_stream(xp, row(ffn1_norm[l]), *f1, row(mix_norm[l]), tm=512, emit_x=True, n_dtype=BF16)
    q, kv, rw = _proj(h1, w_in_b, tm=256)
    o_att = _attn_prompt(q, kv, bias, sinks, batch=batch, seq=seq)
    o_rw, s_p = _rwkv_prompt(rw, front_params + back_params[:3], batch=batch, seq=seq)
    (x2,) = _outproj(x1, o_att, o_rw, w_out_b, tm=512)
    (y_p,) = _ffn_stream(x2, row(ffn2_norm[l]), *f2, row(final_norm), tm=512, emit_x=False, n_dtype=F32)

    kv3 = kv.reshape(batch, seq, 2 * KV_WIDTH)
    lp = min(WINDOW, seq)
    new_k_p = kv3[:, seq - lp:, 0:KV_WIDTH].reshape(1, batch, lp, N_KV_HEADS, HEAD_DIM)
    new_v_p = kv3[:, seq - lp:, KV_WIDTH:].reshape(1, batch, lp, N_KV_HEADS, HEAD_DIM)
    new_shift_p = rw.reshape(batch, seq, SHIFT_COLS)[:, seq - 1][None]
    k_new = kvs[:, 0:KV_WIDTH].reshape(dec_b, 1, N_KV_HEADS, HEAD_DIM)
    v_new = kvs[:, KV_WIDTH:].reshape(dec_b, 1, N_KV_HEADS, HEAD_DIM)
    new_k_s = jnp.concatenate([cache_k[l], k_new], axis=1)[:, -lw:][None]
    new_v_s = jnp.concatenate([cache_v[l], v_new], axis=1)[:, -lw:][None]
    return (y_p.reshape(batch, seq, D_MODEL), y_s.reshape(dec_b, 1, D_MODEL),
            new_k_p, new_v_p, s_p[None], new_shift_p,
            new_k_s, new_v_s, s_s[None], rws[None])
```

```python
import functools
import math

import numpy as np
import jax
import jax.numpy as jnp
from jax import lax
from jax.experimental import pallas as pl
from jax.experimental.pallas import tpu as pltpu

F32 = jnp.float32
BF16 = jnp.bfloat16

D_MODEL = 2048
HEAD_DIM = 64
ATT_WIDTH = 1024
N_Q_HEADS = 16
N_KV_HEADS = 4
GQA_GROUP = 4
KV_WIDTH = 256
RWKV_WIDTH = 1024
RWKV_HEADS = 16
WINDOW = 128
BLOCK = 128
N_BUCKETS = 32
MAX_DISTANCE = 128
DECAY_LORA = 64
AAA_LORA = 64
GATE_LORA = 128
D_FF = 5504
ATT_COLS = ATT_WIDTH + 2 * KV_WIDTH
SHIFT_COLS = 3 * RWKV_WIDTH + DECAY_LORA + AAA_LORA + GATE_LORA
IN_COLS = ATT_COLS + SHIFT_COLS
RMS_EPS = 1e-5
GN_EPS = 64e-5
FFN_RES = 0.5

LANES = 128
VMEM_LIMIT_BYTES = 60 * 1024 * 1024

FF_TILE = 1024
FF_TILE_CAST = 512
CHUNK = 64
CHUNKS_PER_STEP = 4
ATT_BLOCKS_PER_STEP = 8
HEAD_PAIRS = RWKV_HEADS // 2
DEC_ROWS = 4
CHUNK_GROUP = 2
NEG_BIG = -1e30
DECAY_SCALE = math.exp(-0.5)

_NN = (((1,), (0,)), ((), ()))
_NT = (((1,), (1,)), ((), ()))
_TN = (((0,), (0,)), ((), ()))


def _dg(a, b, dims=_NN):
    return lax.dot_general(a, b, dims, preferred_element_type=F32)


def _mm(a, b, dims=_NN):
    return _dg(a.astype(BF16), b.astype(BF16), dims)


def _split2(x):
    hi = x.astype(BF16)
    lo = (x - hi.astype(F32)).astype(BF16)
    return hi, lo


def _mm2r(a_bf16, b, dims=_NN):
    bh, bl = _split2(b)
    return _dg(a_bf16, bh, dims) + _dg(a_bf16, bl, dims)


def _sigmoid(x):
    return 1.0 / (1.0 + jnp.exp(-x))


def _rms(x, g):
    ms = jnp.mean(x * x, axis=-1, keepdims=True)
    return x * lax.rsqrt(ms + RMS_EPS) * g


def _cparams(sem):
    return pltpu.CompilerParams(dimension_semantics=sem, vmem_limit_bytes=VMEM_LIMIT_BYTES)


def _swiglu_down(xn, wg, wu, wd):
    gate = jnp.dot(xn, wg, preferred_element_type=F32)
    up = jnp.dot(xn, wu, preferred_element_type=F32)
    h = (gate * _sigmoid(gate) * up).astype(BF16)
    return jnp.dot(h, wd, preferred_element_type=F32)


_FF_TAIL_STEP = 2


def _ff_block(j, *, steps):
    return jnp.where(j < _FF_TAIL_STEP, j, jnp.where(j == _FF_TAIL_STEP, steps - 1, j - 1))


def _ffn_kernel(x_ref, g_ref, wg_ref, wu_ref, wd_ref, g2_ref, *rest, emit_x, n_dtype, tf, emit_w):
    rest = list(rest)
    ox_ref = rest.pop(0) if emit_x else None
    on_ref = rest.pop(0)
    wgo_ref, wuo_ref, wdo_ref = (rest.pop(0), rest.pop(0), rest.pop(0)) if emit_w else (None, None, None)
    xn_scr, acc_scr = rest
    j = pl.program_id(1)

    @pl.when(j == 0)
    def _():
        xn_scr[...] = _rms(x_ref[...], g_ref[...]).astype(BF16)
        acc_scr[...] = jnp.zeros_like(acc_scr)

    steps = pl.cdiv(D_FF, tf)
    last = j == steps - 1
    tail = j == _FF_TAIL_STEP

    def accumulate(width):
        wg = wg_ref[:, 0:width].astype(BF16)
        wu = wu_ref[:, 0:width].astype(BF16)
        wd = wd_ref[0:width, :].astype(BF16)
        if emit_w:
            wgo_ref[:, 0:width] = wg
            wuo_ref[:, 0:width] = wu
            wdo_ref[0:width, :] = wd
        acc_scr[...] += _swiglu_down(xn_scr[...], wg, wu, wd)

    @pl.when(jnp.logical_not(tail))
    def _():
        accumulate(tf)

    @pl.when(tail)
    def _():
        accumulate(D_FF - (steps - 1) * tf)

    @pl.when(last)
    def _():
        y = x_ref[...] + FFN_RES * acc_scr[...]
        if emit_x:
            ox_ref[...] = y
        on_ref[...] = _rms(y, g2_ref[...]).astype(n_dtype)


def _ffn(x, g, wg, wu, wd, g2, *, tm, tf, emit_x, n_dtype, emit_w=False):
    m = x.shape[0]
    assert not emit_w or m == tm
    grid = (m // tm, pl.cdiv(D_FF, tf))
    row = pl.BlockSpec((tm, D_MODEL), lambda i, j: (i, 0))
    vec = pl.BlockSpec((1, D_MODEL), lambda i, j: (0, 0))
    steps = pl.cdiv(D_FF, tf)
    blk = functools.partial(_ff_block, steps=steps)
    w_in = pl.BlockSpec((D_MODEL, tf), lambda i, j: (0, blk(j)))
    w_dn = pl.BlockSpec((tf, D_MODEL), lambda i, j: (blk(j), 0))
    out_shape = [jax.ShapeDtypeStruct((m, D_MODEL), n_dtype)]
    out_specs = [row]
    if emit_x:
        out_shape = [jax.ShapeDtypeStruct((m, D_MODEL), F32)] + out_shape
        out_specs = [row, row]
    if emit_w:
        out_shape += [jax.ShapeDtypeStruct(w.shape, BF16) for w in (wg, wu, wd)]
        out_specs += [w_in, w_in, w_dn]
    return pl.pallas_call(
        functools.partial(_ffn_kernel, emit_x=emit_x, n_dtype=n_dtype, tf=tf, emit_w=emit_w),
        grid=grid,
        in_specs=[row, vec, w_in, w_in, w_dn, vec],
        out_specs=out_specs,
        out_shape=out_shape,
        scratch_shapes=[pltpu.VMEM((tm, D_MODEL), BF16), pltpu.VMEM((tm, D_MODEL), F32)],
        compiler_params=_cparams(("arbitrary", "arbitrary")),
        name="ffn_cast" if emit_w else "ffn",
    )(x, g, wg, wu, wd, g2)


def _ff_sweep():
    steps = pl.cdiv(D_FF, FF_TILE)
    order = list(range(_FF_TAIL_STEP)) + [steps - 1] + list(range(_FF_TAIL_STEP, steps - 1))
    return [(b, min(FF_TILE, D_FF - b * FF_TILE)) for b in order]


def _ffn_stream_kernel(x_ref, g_ref, wg_hbm, wu_hbm, wd_hbm, g2_ref, *rest, emit_x, n_dtype):
    rest = list(rest)
    ox_ref = rest.pop(0) if emit_x else None
    on_ref = rest.pop(0)
    wg_buf, wu_buf, wd_buf, acc_scr, sem = rest
    i = pl.program_id(0)
    sweep = _ff_sweep()
    assert len(sweep) % 2 == 0

    def copies(pos):
        blk, width = sweep[pos]
        slot = pos % 2
        cols = pl.ds(blk * FF_TILE, width)
        return (pltpu.make_async_copy(wg_hbm.at[:, cols], wg_buf.at[slot, :, pl.ds(0, width)], sem.at[0, slot]),
                pltpu.make_async_copy(wu_hbm.at[:, cols], wu_buf.at[slot, :, pl.ds(0, width)], sem.at[1, slot]),
                pltpu.make_async_copy(wd_hbm.at[cols, :], wd_buf.at[slot, pl.ds(0, width), :], sem.at[2, slot]))

    def start(pos):
        for cp in copies(pos):
            cp.start()

    @pl.when(i == 0)
    def _():
        start(0)

    for pos, (_, width) in enumerate(sweep):
        slot = pos % 2
        if pos + 1 < len(sweep):
            start(pos + 1)
        else:
            @pl.when(i + 1 < pl.num_programs(0))
            def _():
                start(0)
        for cp in copies(pos):
            cp.wait()
        if pos == 0:
            xn = _rms(x_ref[...], g_ref[...]).astype(BF16)
        part = _swiglu_down(xn, wg_buf[slot, :, 0:width], wu_buf[slot, :, 0:width], wd_buf[slot, 0:width, :])
        if pos == 0:
            acc_scr[...] = part
        else:
            acc_scr[...] += part

    y = x_ref[...] + FFN_RES * acc_scr[...]
    if emit_x:
        ox_ref[...] = y
    on_ref[...] = _rms(y, g2_ref[...]).astype(n_dtype)


def _ffn_stream(x, g, wg, wu, wd, g2, *, tm, emit_x, n_dtype):
    m = x.shape[0]
    row = pl.BlockSpec((tm, D_MODEL), lambda i: (i, 0))
    vec = pl.BlockSpec((1, D_MODEL), lambda i: (0, 0))
    hbm = pl.BlockSpec(memory_space=pl.ANY)
    out_shape = [jax.ShapeDtypeStruct((m, D_MODEL), n_dtype)]
    out_specs = [row]
    if emit_x:
        out_shape = [jax.ShapeDtypeStruct((m, D_MODEL), F32)] + out_shape
        out_specs = [row, row]
    return pl.pallas_call(
        functools.partial(_ffn_stream_kernel, emit_x=emit_x, n_dtype=n_dtype),
        grid=(m // tm,),
        in_specs=[row, vec, hbm, hbm, hbm, vec],
        out_specs=out_specs,
        out_shape=out_shape,
        scratch_shapes=[pltpu.VMEM((2, D_MODEL, FF_TILE), BF16), pltpu.VMEM((2, D_MODEL, FF_TILE), BF16),
                        pltpu.VMEM((2, FF_TILE, D_MODEL), BF16), pltpu.VMEM((tm, D_MODEL), F32),
                        pltpu.SemaphoreType.DMA((3, 2))],
        compiler_params=_cparams(("arbitrary",)),
        name="ffn",
    )(x, g, wg, wu, wd, g2)


def _proj_kernel(h_ref, w_ref, q_ref, kv_ref, rw_ref):
    h = h_ref[...]
    q = jnp.dot(h, w_ref[:, 0:ATT_WIDTH], preferred_element_type=F32)
    q_ref[...] = (q * (HEAD_DIM ** -0.5)).astype(BF16)
    kv_ref[...] = jnp.dot(h, w_ref[:, ATT_WIDTH:ATT_COLS], preferred_element_type=F32)
    rw_ref[...] = jnp.dot(h, w_ref[:, ATT_COLS:IN_COLS], preferred_element_type=F32)


def _proj(h, w, *, tm):
    m = h.shape[0]
    return pl.pallas_call(
        _proj_kernel,
        grid=(m // tm,),
        in_specs=[pl.BlockSpec((tm, D_MODEL), lambda i: (i, 0)),
                  pl.BlockSpec((D_MODEL, IN_COLS), lambda i: (0, 0), pipeline_mode=pl.Buffered(1))],
        out_specs=[pl.BlockSpec((tm, ATT_WIDTH), lambda i: (i, 0)),
                   pl.BlockSpec((tm, 2 * KV_WIDTH), lambda i: (i, 0)),
                   pl.BlockSpec((tm, SHIFT_COLS), lambda i: (i, 0))],
        out_shape=[jax.ShapeDtypeStruct((m, ATT_WIDTH), BF16),
                   jax.ShapeDtypeStruct((m, 2 * KV_WIDTH), F32),
                   jax.ShapeDtypeStruct((m, SHIFT_COLS), F32)],
        compiler_params=_cparams(("arbitrary",)),
        name="in_proj",
    )(h, w)


def _outproj_kernel(x_ref, oa_ref, orw_ref, w_ref, o_ref, *wo_ref):
    w_att = w_ref[0:ATT_WIDTH, :].astype(BF16)
    w_rw = w_ref[ATT_WIDTH:D_MODEL, :].astype(BF16)
    if wo_ref:
        wo_ref[0][0:ATT_WIDTH, :] = w_att
        wo_ref[0][ATT_WIDTH:D_MODEL, :] = w_rw
    acc = jnp.dot(oa_ref[...], w_att, preferred_element_type=F32)
    acc += jnp.dot(orw_ref[...], w_rw, preferred_element_type=F32)
    o_ref[...] = x_ref[...] + acc


def _outproj(x, oa, orw, w, *, tm, emit_w=False):
    m = x.shape[0]
    assert not emit_w or m == tm
    row = pl.BlockSpec((tm, D_MODEL), lambda i: (i, 0))
    half = pl.BlockSpec((tm, ATT_WIDTH), lambda i: (i, 0))
    full = pl.BlockSpec((D_MODEL, D_MODEL), lambda i: (0, 0), pipeline_mode=pl.Buffered(1))
    out_shape = [jax.ShapeDtypeStruct((m, D_MODEL), F32)]
    out_specs = [row]
    if emit_w:
        out_shape.append(jax.ShapeDtypeStruct((D_MODEL, D_MODEL), BF16))
        out_specs.append(pl.BlockSpec((D_MODEL, D_MODEL), lambda i: (0, 0)))
    return pl.pallas_call(
        _outproj_kernel,
        grid=(m // tm,),
        in_specs=[row, half, half, full],
        out_specs=out_specs,
        out_shape=out_shape,
        compiler_params=_cparams(("arbitrary",)),
        name="out_proj_cast" if emit_w else "out_proj",
    )(x, oa, orw, w)


def _bucket_table():
    qi = np.arange(BLOCK)[:, None]
    kj = np.arange(2 * BLOCK)[None, :]
    dist = BLOCK + qi - kj
    n = np.maximum(dist, 0)
    max_exact = N_BUCKETS // 2
    nf = np.maximum(n, 1).astype(np.float32)
    large = max_exact + (np.log(nf / np.float32(max_exact)) / np.float32(math.log(MAX_DISTANCE / max_exact))
                         * np.float32(N_BUCKETS - max_exact)).astype(np.int32)
    large = np.minimum(large, N_BUCKETS - 1)
    bucket = np.where(n < max_exact, n, large).astype(np.int32)
    valid = (dist >= 0) & (dist <= WINDOW)
    return np.where(valid, bucket, -1).astype(np.int32)


def _bias_kernel(bucket_ref, rb_ref, o_ref):
    bucket = bucket_ref[...]
    kj = lax.broadcasted_iota(jnp.int32, (BLOCK, 2 * BLOCK), 1)
    for h in range(N_Q_HEADS):
        acc = jnp.full((BLOCK, 2 * BLOCK), NEG_BIG, F32)
        for n in range(N_BUCKETS):
            acc = jnp.where(bucket == n, rb_ref[n, h], acc)
        o_ref[0, h] = jnp.where(kj >= BLOCK, acc, NEG_BIG)
        o_ref[1, h] = acc


def _bias_table(rel_bias):
    bucket = jnp.asarray(_bucket_table())
    return pl.pallas_call(
        _bias_kernel,
        in_specs=[pl.BlockSpec(memory_space=pltpu.VMEM), pl.BlockSpec(memory_space=pltpu.SMEM)],
        out_specs=pl.BlockSpec(memory_space=pltpu.VMEM),
        out_shape=jax.ShapeDtypeStruct((2, N_Q_HEADS, BLOCK, 2 * BLOCK), F32),
        name="bias_table",
    )(bucket, rel_bias)


def _attn_kernel(sink_ref, q_ref, kvp_ref, kvc_ref, bias_ref, o_ref, kv_scr):
    j = pl.program_id(1)
    kv_scr[0:BLOCK, :] = kvp_ref[...].astype(BF16)
    kv_scr[BLOCK:, :] = kvc_ref[...].astype(BF16)

    def block(i, carry):
        r0 = pl.multiple_of(i * BLOCK, BLOCK)
        rows = pl.ds(r0, BLOCK)
        table = jnp.where(jnp.logical_and(j == 0, i == 0), 0, 1)
        _attn_block(sink_ref, q_ref.at[rows], kv_scr.at[pl.ds(r0, 2 * BLOCK)], bias_ref.at[table], o_ref.at[rows])
        return carry

    lax.fori_loop(0, ATT_BLOCKS_PER_STEP, block, 0)


def _attn_block(sink_ref, q_ref, kv_ref, bias_ref, o_ref):
    half = N_KV_HEADS // 2
    for g0 in range(0, N_KV_HEADS, half):
        kcat, vcat = {}, {}
        for g in range(g0, g0 + half):
            kcat[g] = kv_ref[:, g * HEAD_DIM:(g + 1) * HEAD_DIM]
            vcat[g] = kv_ref[:, KV_WIDTH + g * HEAD_DIM:KV_WIDTH + (g + 1) * HEAD_DIM]
        heads = range(g0 * GQA_GROUP, (g0 + half) * GQA_GROUP)
        hs = {h: slice(h * HEAD_DIM, (h + 1) * HEAD_DIM) for h in heads}
        s = {h: _dg(q_ref[:, hs[h]], kcat[h // GQA_GROUP], _NT) + bias_ref[h] for h in heads}
        m = {h: jnp.maximum(jnp.max(s[h], axis=-1, keepdims=True), sink_ref[h]) for h in heads}
        p = {h: jnp.exp(s[h] - m[h]) for h in heads}
        denom = {h: jnp.sum(p[h], axis=-1, keepdims=True) + jnp.exp(sink_ref[h] - m[h]) for h in heads}
        o = {h: jnp.dot(p[h].astype(BF16), vcat[h // GQA_GROUP], preferred_element_type=F32) for h in heads}
        for h in heads:
            o_ref[:, hs[h]] = (o[h] / denom[h]).astype(BF16)


def _attn_prompt(q, kv, bias, sinks, *, batch, seq):
    n = ATT_BLOCKS_PER_STEP
    rows = n * BLOCK
    q3 = q.reshape(batch, seq, ATT_WIDTH)
    kv3 = kv.reshape(batch, seq, 2 * KV_WIDTH)
    out = pl.pallas_call(
        _attn_kernel,
        grid=(batch, seq // rows),
        in_specs=[pl.BlockSpec(memory_space=pltpu.SMEM),
                  pl.BlockSpec((None, rows, ATT_WIDTH), lambda b, j: (b, j, 0)),
                  pl.BlockSpec((None, BLOCK, 2 * KV_WIDTH), lambda b, j: (b, jnp.maximum(n * j - 1, 0), 0)),
                  pl.BlockSpec((None, rows, 2 * KV_WIDTH), lambda b, j: (b, j, 0)),
                  pl.BlockSpec((2, N_Q_HEADS, BLOCK, 2 * BLOCK), lambda b, j: (0, 0, 0, 0))],
        out_specs=pl.BlockSpec((None, rows, ATT_WIDTH), lambda b, j: (b, j, 0)),
        out_shape=jax.ShapeDtypeStruct((batch, seq, ATT_WIDTH), BF16),
        scratch_shapes=[pltpu.VMEM((rows + BLOCK, 2 * KV_WIDTH), BF16)],
        compiler_params=_cparams(("arbitrary", "arbitrary")),
        name="attn_prompt",
    )(sinks, q3, kv3, kv3, bias)
    return out.reshape(batch * seq, ATT_WIDTH)


def _attn_decode_kernel(sink_ref, q_ref, kvn_ref, ck_ref, cv_ref, bias_ref, o_ref):
    q = q_ref[...].astype(F32)
    kvn = kvn_ref[...]
    for g in range(N_KV_HEADS):
        ks = slice(g * HEAD_DIM, (g + 1) * HEAD_DIM)
        vs = slice(KV_WIDTH + g * HEAD_DIM, KV_WIDTH + (g + 1) * HEAD_DIM)
        hs = slice(g * GQA_GROUP, (g + 1) * GQA_GROUP)
        qg = q[:, hs, :]
        kc = ck_ref[:, :, ks]
        vc = cv_ref[:, :, ks]
        kn = kvn[:, ks].astype(BF16).astype(F32)[:, None, :]
        vn = kvn[:, vs].astype(BF16).astype(F32)[:, None, :]
        s = jnp.einsum("bqd,bkd->bqk", qg.astype(BF16), kc.astype(BF16), preferred_element_type=F32)
        s = s + bias_ref[hs, 0:WINDOW][None]
        s_new = jnp.sum(qg * kn, axis=-1, keepdims=True) + bias_ref[hs, WINDOW:WINDOW + 1][None]
        sink = sink_ref[hs, 0:1][None]
        m = jnp.maximum(jnp.maximum(jnp.max(s, axis=-1, keepdims=True), s_new), sink)
        p = jnp.exp(s - m)
        p_new = jnp.exp(s_new - m)
        denom = jnp.sum(p, axis=-1, keepdims=True) + p_new + jnp.exp(sink - m)
        o = jnp.einsum("bqk,bkd->bqd", p.astype(BF16), vc.astype(BF16), preferred_element_type=F32)
        o = o + p_new * vn
        o_ref[:, hs, :] = o / denom


def _attn_decode(q, kvn, cache_k, cache_v, bias_row, sinks_b):
    b = q.shape[0]
    lw = cache_k.shape[1]
    vm = pl.BlockSpec(memory_space=pltpu.VMEM)
    out = pl.pallas_call(
        _attn_decode_kernel,
        in_specs=[vm, vm, vm, vm, vm, vm],
        out_specs=vm,
        out_shape=jax.ShapeDtypeStruct((b, N_Q_HEADS, HEAD_DIM), F32),
        compiler_params=pltpu.CompilerParams(vmem_limit_bytes=VMEM_LIMIT_BYTES),
        name="attn_decode",
    )(sinks_b, q.reshape(b, N_Q_HEADS, HEAD_DIM), kvn,
      cache_k.reshape(b, lw, KV_WIDTH), cache_v.reshape(b, lw, KV_WIDTH), bias_row)
    return out.reshape(b, ATT_WIDTH)


def _seg_expand(x, bd_ref):
    rows = x.shape[0]
    groups = x.shape[1] // LANES
    xs = jnp.concatenate([x[:, i * LANES:(i + 1) * LANES] for i in range(groups)], axis=0)
    s = _mm(xs, bd_ref[...])
    return jnp.concatenate([s[i * rows:(i + 1) * rows] for i in range(groups)], axis=1)


def _rwkv_front(x, prev, p):
    mu_ref, w0_ref, a0_ref, kk_ref, ka_ref, lw_ref, g2_ref, bd_ref = p
    xm = x + mu_ref[...] * (prev - x)
    o3 = 3 * RWKV_WIDTH
    r = xm[:, 0:RWKV_WIDTH]
    k = xm[:, RWKV_WIDTH:2 * RWKV_WIDTH]
    v = xm[:, 2 * RWKV_WIDTH:o3]
    wa = xm[:, o3:o3 + LANES]
    lane = lax.broadcasted_iota(jnp.int32, wa.shape, 1)
    z = jnp.where(lane < DECAY_LORA, jnp.tanh(wa), wa)
    lora = _mm(z, lw_ref[...])
    logw = -DECAY_SCALE * _sigmoid(w0_ref[...] + lora[:, 0:RWKV_WIDTH])
    a = _sigmoid(a0_ref[...] + lora[:, RWKV_WIDTH:2 * RWKV_WIDTH])
    g = _mm(_sigmoid(xm[:, o3 + LANES:SHIFT_COLS]), g2_ref[...])
    kk = k * kk_ref[...]
    kmod = k * (1.0 + (a - 1.0) * ka_ref[...])
    nrm = jnp.sqrt(_seg_expand(kk * kk, bd_ref))
    kk = kk / jnp.maximum(nrm, 1e-12)
    return r, logw, kmod, v, kk, kk * a, g


def _rwkv_back(y, r, kmod, v, g, q):
    rk_ref, lnw_ref, lnb_ref, bd_ref = q
    mean = _seg_expand(y, bd_ref) * (1.0 / HEAD_DIM)
    d = y - mean
    var = _seg_expand(d * d, bd_ref) * (1.0 / HEAD_DIM)
    yn = d * lax.rsqrt(var + GN_EPS) * lnw_ref[...] + lnb_ref[...]
    bonus = _seg_expand(r * kmod * rk_ref[...], bd_ref) * v
    return (yn + bonus) * g


def _rwkv_chunks(rw_refs, mu_ref, w0_ref, a0_ref, kk_ref, ka_ref, lw_ref, g2_ref, bd_ref,
                 rk_ref, lnw_ref, lnb_ref,
                 o_refs, s_scr, prev_scr, works):
    C = CHUNK
    chunks = range(len(rw_refs))
    front = (mu_ref, w0_ref, a0_ref, kk_ref, ka_ref, lw_ref, g2_ref, bd_ref)
    ti = lax.broadcasted_iota(jnp.int32, (C, C), 0)
    si = lax.broadcasted_iota(jnp.int32, (C, C), 1)
    tri = jnp.where(si <= ti, 1.0, 0.0).astype(BF16)
    tail, w_end = {}, {}
    for c in chunks:
        y_scr, ea_scr, er_scr, eb_scr, ek_scr, be_scr, ke_scr, v_scr = works[c]
        x = rw_refs[c][...]
        row = lax.broadcasted_iota(jnp.int32, x.shape, 0)
        before = prev_scr[...] if c == 0 else rw_refs[c - 1][C - 1:C, :]
        prev = jnp.where(row == 0, before, pltpu.roll(x, 1, axis=0))
        r, logw, kmod, v, kk, bb, g = _rwkv_front(x, prev, front)
        cum = _mm2r(tri, logw)
        cum_end = cum[C - 1:C, :]
        e_in = jnp.exp(cum)
        e_neg = jnp.exp(-cum)
        e_end = jnp.exp(cum_end - cum)
        ea_scr[...] = kk * jnp.exp(cum - logw)
        er_scr[...] = r * e_in
        eb_scr[...] = bb * e_neg
        ek_scr[...] = kmod * e_neg
        be_scr[...] = bb * e_end
        ke_scr[...] = kmod * e_end
        v_scr[...] = v
        w_end[c] = jnp.exp(cum_end)
        tail[c] = (r, kmod, v, g)
    prev_scr[...] = rw_refs[-1][C - 1:C, :]

    lane = lax.broadcasted_iota(jnp.int32, (C, LANES), 1)
    lo_half = lane < HEAD_DIM
    ri = lax.broadcasted_iota(jnp.int32, (LANES, LANES), 0)
    ci = lax.broadcasted_iota(jnp.int32, (LANES, LANES), 1)
    same_head = jnp.right_shift(ri, 6) == jnp.right_shift(ci, 6)
    eye = ri == ci
    t_loc = jnp.bitwise_and(ri, C - 1)
    s_loc = jnp.bitwise_and(ci, C - 1)
    bd_strict = jnp.logical_and(same_head, s_loc < t_loc)
    bd_incl = jnp.logical_and(same_head, s_loc <= t_loc)

    def swap(t):
        return pltpu.roll(t, HEAD_DIM, axis=1)

    def stack2(t):
        return jnp.concatenate([jnp.where(lo_half, t, 0.0), jnp.where(lo_half, 0.0, t)], axis=0)

    def stack2x(t):
        ts = swap(t)
        return jnp.concatenate([jnp.where(lo_half, 0.0, ts), jnp.where(lo_half, ts, 0.0)], axis=0)

    def own(t):
        return jnp.where(lo_half, t[0:C], t[C:2 * C])

    def other(t):
        return swap(jnp.where(lo_half, t[C:2 * C], t[0:C]))

    sl = [slice(p * LANES, (p + 1) * LANES) for p in range(HEAD_PAIRS)]
    pairs = range(HEAD_PAIRS)
    items = [(c, p) for c in chunks for p in pairs]
    col = lambda c, k, p: works[c][k][:, sl[p]]
    a2 = {(c, p): stack2(col(c, 1, p)) for c, p in items}
    m = {}
    for c, p in items:
        r2 = stack2(col(c, 2, p))
        b_p = col(c, 3, p)
        k_p = col(c, 4, p)
        m[c, p] = _mm(jnp.concatenate([a2[c, p], r2], axis=0), jnp.concatenate([b_p, k_p], axis=0), _NT)
    ms = {i: swap(m[i]) for i in items}
    first = ri < HEAD_DIM
    l_ak = {i: jnp.where(bd_strict, jnp.where(first, ms[i][0:2 * C], m[i][0:2 * C]), 0.0) for i in items}
    pw = {i: -jnp.where(bd_strict, jnp.where(first, m[i][0:2 * C], ms[i][0:2 * C]), 0.0) for i in items}
    lrbk = {i: jnp.concatenate([jnp.where(bd_incl, jnp.where(first, m[i][2 * C:4 * C], ms[i][2 * C:4 * C]), 0.0),
                                jnp.where(bd_incl, jnp.where(first, ms[i][2 * C:4 * C], m[i][2 * C:4 * C]), 0.0)],
                               axis=1) for i in items}
    v2x = {(c, p): stack2x(col(c, 7, p)) for c, p in items}
    xc = {i: a2[i] - _mm(l_ak[i], v2x[i]) for i in items}
    for it in range(6):
        for i in items:
            pw_b = pw[i].astype(BF16)
            xc_b = xc[i].astype(BF16)
            if it < 5:
                px = _dg(pw_b, jnp.concatenate([xc_b, pw_b], axis=1))
                pw[i] = px[:, LANES:2 * LANES]
                xc[i] = xc[i] + px[:, 0:LANES]
            else:
                xc[i] = xc[i] + _dg(pw_b, xc_b)
    xv = {i: jnp.concatenate([xc[i], v2x[i]], axis=0).astype(BF16) for i in items}
    lb = {i: _mm(lrbk[i], xv[i]) for i in items}
    z = {(c, p): _mm(xv[c, p], jnp.concatenate([stack2(col(c, 5, p)), stack2(col(c, 6, p))], axis=0), _TN)
         for c, p in items}
    rt = {(c, p): col(c, 2, p) - own(lb[c, p]) for c, p in items}
    yl = {i: other(lb[i]) for i in items}
    d = {(c, p): jnp.where(eye, w_end[c][:, sl[p]], 0.0) - jnp.where(same_head, z[c, p], 0.0) for c, p in items}
    gg = {i: jnp.where(same_head, jnp.concatenate([z[i][C:2 * C], z[i][0:C]], axis=0), 0.0) for i in items}
    back = (rk_ref, lnw_ref, lnb_ref, bd_ref)
    for c in chunks:
        y_scr = works[c][0]
        s0 = {p: s_scr[p].astype(BF16) for p in pairs}
        ys = {p: _mm(rt[c, p], s0[p], _NT) for p in pairs}
        for p in pairs:
            y_scr[:, sl[p]] = yl[c, p] + ys[p]
        sn = {p: _mm(s0[p], d[c, p]) for p in pairs}
        for p in pairs:
            s_scr[p] = sn[p] + gg[c, p]
    for c in chunks:
        o_refs[c][...] = _rwkv_back(works[c][0][...], *tail[c], back).astype(BF16)


def _rwkv_chunk_kernel(rw_ref, *rest):
    params, (o_ref, s_out_ref, s_scr, prev_scr), work = rest[:11], rest[11:15], rest[15:]
    c = pl.program_id(1)

    @pl.when(c == 0)
    def _():
        s_scr[...] = jnp.zeros_like(s_scr)
        prev_scr[...] = jnp.zeros_like(prev_scr)

    for g in range(0, CHUNKS_PER_STEP, CHUNK_GROUP):
        ids = range(g, g + CHUNK_GROUP)
        rows = [pl.ds(i * CHUNK, CHUNK) for i in ids]
        _rwkv_chunks([rw_ref.at[r] for r in rows], *params, [o_ref.at[r] for r in rows], s_scr, prev_scr,
                     [[w.at[i] for w in work] for i in ids])

    @pl.when(c == pl.num_programs(1) - 1)
    def _():
        for p in range(HEAD_PAIRS):
            s_out_ref[2 * p] = s_scr[p, 0:HEAD_DIM, 0:HEAD_DIM]
            s_out_ref[2 * p + 1] = s_scr[p, HEAD_DIM:LANES, HEAD_DIM:LANES]


def _rwkv_prompt(rw, params, *, batch, seq):
    rw3 = rw.reshape(batch, seq, SHIFT_COLS)
    rows = CHUNKS_PER_STEP * CHUNK
    nc = seq // rows
    const = lambda shape: pl.BlockSpec(shape, lambda b, c: tuple(0 for _ in shape))
    in_specs = [pl.BlockSpec((None, rows, SHIFT_COLS), lambda b, c: (b, c, 0))]
    in_specs += [const(p.shape) for p in params]
    wide = pltpu.VMEM((CHUNKS_PER_STEP, CHUNK, RWKV_WIDTH), F32)
    o, s_out = pl.pallas_call(
        _rwkv_chunk_kernel,
        grid=(batch, nc),
        in_specs=in_specs,
        out_specs=[pl.BlockSpec((None, rows, RWKV_WIDTH), lambda b, c: (b, c, 0)),
                   pl.BlockSpec((None, RWKV_HEADS, HEAD_DIM, HEAD_DIM), lambda b, c: (b, 0, 0, 0))],
        out_shape=[jax.ShapeDtypeStruct((batch, seq, RWKV_WIDTH), BF16),
                   jax.ShapeDtypeStruct((batch, RWKV_HEADS, HEAD_DIM, HEAD_DIM), F32)],
        scratch_shapes=[pltpu.VMEM((HEAD_PAIRS, LANES, LANES), F32),
                        pltpu.VMEM((1, SHIFT_COLS), F32),
                        wide, wide, wide, wide, wide, wide, wide, wide],
        compiler_params=_cparams(("arbitrary", "arbitrary")),
        name="rwkv_prompt",
    )(rw3, *params)
    return o.reshape(batch * seq, RWKV_WIDTH), s_out


def _rwkv_dec_front_kernel(rw_ref, prev_ref, mu_ref, w0_ref, a0_ref, kk_ref, ka_ref, lw_ref, g2_ref, bd_ref,
                           r_ref, w_ref, k_ref, v_ref, kkn_ref, b_ref, g_ref):
    front = (mu_ref, w0_ref, a0_ref, kk_ref, ka_ref, lw_ref, g2_ref, bd_ref)
    r, logw, kmod, v, kk, bb, g = _rwkv_front(rw_ref[...], prev_ref[...], front)
    r_ref[...] = r
    w_ref[...] = jnp.exp(logw)
    k_ref[...] = kmod
    v_ref[...] = v
    kkn_ref[...] = kk
    b_ref[...] = bb
    g_ref[...] = g


def _rwkv_dec_state_kernel(r_all, w_all, k_all, v_all, kk_all, b_all, s_blk, y_blk_ref, so_blk):
    def one(j, carry):
        row = pl.ds(pl.program_id(0) * DEC_ROWS + j, 1)
        vecs = [t.at[row] for t in (r_all, w_all, k_all, v_all, kk_all, b_all)]
        _rwkv_dec_state_one(*vecs, s_blk.at[j], y_blk_ref.at[j], so_blk.at[j])
        return carry

    lax.fori_loop(0, DEC_ROWS, one, 0)


def _rwkv_dec_state_one(r_ref, w_ref, k_ref, v_ref, kk_ref, b_ref, s_ref, y_ref, so_ref):
    ri = lax.broadcasted_iota(jnp.int32, (HEAD_DIM, HEAD_DIM), 0)
    ci = lax.broadcasted_iota(jnp.int32, (HEAD_DIM, HEAD_DIM), 1)
    eye = ri == ci
    ones = jnp.ones((HEAD_DIM, LANES), BF16)
    heads = range(RWKV_HEADS)
    hs = [slice(h * HEAD_DIM, (h + 1) * HEAD_DIM) for h in heads]
    n = RWKV_HEADS * HEAD_DIM
    s = [s_ref[h] for h in heads]
    lhs = [s[h] * kk_ref[:, hs[h]] for h in heads] + [jnp.where(eye, v_ref[:, hs[h]], 0.0) for h in heads]
    red = _mm(jnp.concatenate(lhs, axis=0), ones)
    s_new = []
    for h in heads:
        sa = -red[h * HEAD_DIM:(h + 1) * HEAD_DIM, 0:HEAD_DIM]
        v_col = red[n + h * HEAD_DIM:n + (h + 1) * HEAD_DIM, 0:HEAD_DIM]
        s_new.append(s[h] * w_ref[:, hs[h]] + sa * b_ref[:, hs[h]] + v_col * k_ref[:, hs[h]])
        so_ref[h] = s_new[h]
    yb = _mm(jnp.concatenate([s_new[h] * r_ref[:, hs[h]] for h in heads], axis=0), ones)
    for h in heads:
        y_blk = yb[h * HEAD_DIM:(h + 1) * HEAD_DIM, 0:HEAD_DIM]
        y_ref[:, hs[h]] = jnp.sum(jnp.where(eye, y_blk, 0.0), axis=0, keepdims=True)


def _rwkv_dec_back_kernel(y_ref, r_ref, k_ref, v_ref, g_ref, rk_ref, lnw_ref, lnb_ref, bd_ref, o_ref):
    back = (rk_ref, lnw_ref, lnb_ref, bd_ref)
    o_ref[...] = _rwkv_back(y_ref[...], r_ref[...], k_ref[...], v_ref[...], g_ref[...], back).astype(BF16)


def _rwkv_decode(rw, shift0, wkv0, front_params, back_params):
    b = rw.shape[0]
    vm = pl.BlockSpec(memory_space=pltpu.VMEM)
    wide = jax.ShapeDtypeStruct((b, RWKV_WIDTH), F32)
    r, w, k, v, kk, bb, g = pl.pallas_call(
        _rwkv_dec_front_kernel,
        in_specs=[vm] * (2 + len(front_params)),
        out_specs=[vm] * 7,
        out_shape=[wide] * 7,
        name="rwkv_dec_front",
    )(rw, shift0, *front_params)
    assert b % DEC_ROWS == 0
    rowspec = pl.BlockSpec((DEC_ROWS, 1, RWKV_WIDTH), lambda i: (i, 0, 0))
    stspec = pl.BlockSpec((DEC_ROWS, RWKV_HEADS, HEAD_DIM, HEAD_DIM), lambda i: (i, 0, 0, 0))
    allrows = pl.BlockSpec((b, RWKV_WIDTH), lambda i: (0, 0))
    y, s_new = pl.pallas_call(
        _rwkv_dec_state_kernel,
        grid=(b // DEC_ROWS,),
        in_specs=[allrows] * 6 + [stspec],
        out_specs=[rowspec, stspec],
        out_shape=[jax.ShapeDtypeStruct((b, 1, RWKV_WIDTH), F32),
                   jax.ShapeDtypeStruct((b, RWKV_HEADS, HEAD_DIM, HEAD_DIM), F32)],
        compiler_params=_cparams(("arbitrary",)),
        name="rwkv_dec_state",
    )(r, w, k, v, kk, bb, wkv0)
    o = pl.pallas_call(
        _rwkv_dec_back_kernel,
        in_specs=[vm] * (5 + len(back_params)),
        out_specs=vm,
        out_shape=jax.ShapeDtypeStruct((b, RWKV_WIDTH), BF16),
        name="rwkv_dec_back",
    )(y.reshape(b, RWKV_WIDTH), r, k, v, g, *back_params)
    return o, s_new


def kernel(x_prompt, x_sample, cache_k, cache_v, state_wkv, state_shift, rel_bias, ffn1_norm, ffn1_w_gate, ffn1_w_up, ffn1_w_down, mix_norm, w_in, attn_sinks, shift_mu, decay_w0, decay_w2, aaa_a0, aaa_a2, gate_g2, key_k, key_a, bonus_r_k, ln_x_w, ln_x_b, w_out, ffn2_norm, ffn2_w_gate, ffn2_w_up, ffn2_w_down, final_norm):
    batch, seq, _ = x_prompt.shape
    dec_b = x_sample.shape[0]
    lw = cache_k.shape[2]
    l = 0
    row = lambda t: t.reshape(1, -1)

    w_in_b = w_in[l].astype(BF16)
    lora_w = jnp.zeros((LANES, 2 * RWKV_WIDTH), F32)
    lora_w = lora_w.at[0:DECAY_LORA, 0:RWKV_WIDTH].set(decay_w2[l])
    lora_w = lora_w.at[DECAY_LORA:LANES, RWKV_WIDTH:].set(aaa_a2[l])
    head_of = np.arange(LANES) // HEAD_DIM
    bd_ones = jnp.asarray(head_of[:, None] == head_of[None, :], BF16)
    front_params = (row(shift_mu[l]), row(decay_w0[l]), row(aaa_a0[l]), row(key_k[l]), row(key_a[l]),
                    lora_w.astype(BF16), gate_g2[l].astype(BF16), bd_ones)
    back_params = (row(bonus_r_k[l]), row(ln_x_w[l]), row(ln_x_b[l]), bd_ones)

    bias = _bias_table(rel_bias)
    sinks = attn_sinks[l]

    xs = x_sample.reshape(dec_b, D_MODEL)
    x1s, h1s, *f1 = _ffn(xs, row(ffn1_norm[l]), ffn1_w_gate[l], ffn1_w_up[l], ffn1_w_down[l], row(mix_norm[l]),
                         tm=dec_b, tf=FF_TILE_CAST, emit_x=True, n_dtype=BF16, emit_w=True)
    qs, kvs, rws = _proj(h1s, w_in_b, tm=dec_b)
    sinks_b = jnp.broadcast_to(sinks[:, None], (N_Q_HEADS, LANES))
    o_att_s = _attn_decode(qs, kvs, cache_k[l], cache_v[l], bias[1, :, 0, :], sinks_b)
    o_rw_s, s_s = _rwkv_decode(rws, state_shift[l], state_wkv[l], front_params, back_params)
    x2s, w_out_b = _outproj(x1s, o_att_s.astype(BF16), o_rw_s, w_out[l], tm=dec_b, emit_w=True)
    y_s, *f2 = _ffn(x2s, row(ffn2_norm[l]), ffn2_w_gate[l], ffn2_w_up[l], ffn2_w_down[l], row(final_norm),
                    tm=dec_b, tf=FF_TILE_CAST, emit_x=False, n_dtype=F32, emit_w=True)

    xp = x_prompt.reshape(batch * seq, D_MODEL)
    x1, h1 = _ffn_stream(xp, row(ffn1_norm[l]), *f1, row(mix_norm[l]), tm=512, emit_x=True, n_dtype=BF16)
    q, kv, rw = _proj(h1, w_in_b, tm=256)
    o_att = _attn_prompt(q, kv, bias, sinks, batch=batch, seq=seq)
    o_rw, s_p = _rwkv_prompt(rw, front_params + back_params[:3], batch=batch, seq=seq)
    (x2,) = _outproj(x1, o_att, o_rw, w_out_b, tm=512)
    (y_p,) = _ffn_stream(x2, row(ffn2_norm[l]), *f2, row(final_norm), tm=512, emit_x=False, n_dtype=F32)

    kv3 = kv.reshape(batch, seq, 2 * KV_WIDTH)
    lp = min(WINDOW, seq)
    new_k_p = kv3[:, seq - lp:, 0:KV_WIDTH].reshape(1, batch, lp, N_KV_HEADS, HEAD_DIM)
    new_v_p = kv3[:, seq - lp:, KV_WIDTH:].reshape(1, batch, lp, N_KV_HEADS, HEAD_DIM)
    new_shift_p = rw.reshape(batch, seq, SHIFT_COLS)[:, seq - 1][None]
    k_new = kvs[:, 0:KV_WIDTH].reshape(dec_b, 1, N_KV_HEADS, HEAD_DIM)
    v_new = kvs[:, KV_WIDTH:].reshape(dec_b, 1, N_KV_HEADS, HEAD_DIM)
    new_k_s = jnp.concatenate([cache_k[l], k_new], axis=1)[:, -lw:][None]
    new_v_s = jnp.concatenate([cache_v[l], v_new], axis=1)[:, -lw:][None]
    return (y_p.reshape(batch, seq, D_MODEL), y_s.reshape(dec_b, 1, D_MODEL),
            new_k_p, new_v_p, s_p[None], new_shift_p,
            new_k_s, new_v_s, s_s[None], rws[None])
```
